```python
import jax, jax.numpy as jnp
from jax import lax
import numpy as np

D_MODEL = 1024
BATCH = 4
SEQ = 4096
DEPTH = 1
DEC_BATCH = 32
DEC_SEQ = 1
PAST_LEN = 16384
PAGE_SIZE = 128

N_HEADS = 8
HEAD_DIM = 64
N_KV = 2
Q_PER_KV = N_HEADS // N_KV
CMP_LEN = 32
CMP_STRIDE = 16
CMP_HALVES = CMP_LEN // CMP_STRIDE
CMP_HID = 128
SEL_BLOCK = 64
N_SEL = 16
N_LOCAL_SEL = 2
WINDOW = 512
Q_BLOCK = 128
GMLP_WIDTH = 512
GMLP_GROUPS = 4
GMLP_GROUP_DIM = GMLP_WIDTH // GMLP_GROUPS
CHUNK = 128
D_FF = 2816
CONV_W = 3

EPS = 1e-6
NEG = -1e30
SEL_BONUS = 1e6
ATTN_SCALE = HEAD_DIM ** -0.5
Q_COLS = N_HEADS * HEAD_DIM
KV_COLS = N_KV * HEAD_DIM
IN_COLS = Q_COLS + 6 * KV_COLS + 3 * N_HEADS + 2 * GMLP_WIDTH + 2 * D_MODEL

kernel_name = 'nsa_gmlp_convffn_hybrid_step'


def rmsnorm(x, g):
    xf = x.astype(jnp.float32)
    xf = xf * lax.rsqrt(jnp.mean(xf * xf, axis=-1, keepdims=True) + EPS)
    return xf.astype(x.dtype) * g


def masked_softmax(s, mask):
    s = jnp.where(mask, s, NEG)
    m = jnp.max(s, axis=-1, keepdims=True)
    e = jnp.where(mask, jnp.exp(s - m), 0.0)
    d = jnp.sum(e, axis=-1, keepdims=True)
    return e / jnp.where(d > 0, d, 1.0)


def project(h, w_in):
    B, T = h.shape[:2]
    z = h @ w_in
    q = z[..., :Q_COLS].reshape(B, T, N_HEADS, HEAD_DIM)
    off = Q_COLS
    kv = z[..., off:off + 6 * KV_COLS].reshape(B, T, 3, 2, N_KV, HEAD_DIM)
    off += 6 * KV_COLS
    nsa_g = jax.nn.sigmoid(z[..., off:off + 3 * N_HEADS].reshape(B, T, N_HEADS, 3))
    off += 3 * N_HEADS
    uv = jax.nn.gelu(z[..., off:off + 2 * GMLP_WIDTH])
    u, v = uv[..., :GMLP_WIDTH], uv[..., GMLP_WIDTH:]
    off += 2 * GMLP_WIDTH
    gates = jax.nn.sigmoid(z[..., off:].reshape(B, T, 2, D_MODEL))
    return q, kv, nsa_g, u, v, gates


def dense_gqa(q, k, v, mask):
    B, Tq, H, D = q.shape
    G = k.shape[2]
    qg = q.reshape(B, Tq, G, H // G, D)
    s = jnp.einsum('btgrd,bsgd->bgrts', qg, k).astype(jnp.float32) * ATTN_SCALE
    p = masked_softmax(s, mask)
    o = jnp.einsum('bgrts,bsgd->btgrd', p.astype(v.dtype), v).reshape(B, Tq, H, D)
    return o, p


def compress(x, pe, w1, b1, w2):
    B, T, G, D = x.shape
    n_seg = T // CMP_STRIDE
    n_cmp = n_seg - CMP_HALVES + 1
    seg = x[:, :n_seg * CMP_STRIDE].reshape(B, n_seg, CMP_STRIDE, G, D)
    pe_h = pe.reshape(CMP_HALVES, CMP_STRIDE, D)
    w1_h = w1.reshape(CMP_HALVES, CMP_STRIDE, D, CMP_HID)
    hid = b1
    for r in range(CMP_HALVES):
        part = jnp.einsum('bnsgd,sdh->bngh', seg + pe_h[r][:, None, :], w1_h[r])
        hid = hid + part[:, r:r + n_cmp]
    return jnp.einsum('bngh,hd->bngd', jax.nn.gelu(hid), w2)


def select_blocks(p_cmp, q_pos, n_blocks):
    imp_c = jnp.sum(p_cmp, axis=2)
    n_cmp = imp_c.shape[-1]
    ratio = SEL_BLOCK // CMP_STRIDE
    n_back = CMP_HALVES - 1
    padded = jnp.pad(imp_c, ((0, 0), (0, 0), (0, 0), (n_back, ratio * (n_blocks + 1) - n_cmp)))
    imp = padded[..., 0::ratio][..., :n_blocks]
    for m in range(1, ratio + n_back):
        imp = imp + padded[..., m::ratio][..., :n_blocks]
    blk = jnp.arange(n_blocks, dtype=jnp.int32)[None, :]
    cur = (q_pos // SEL_BLOCK)[:, None]
    valid = blk * SEL_BLOCK <= q_pos[:, None]
    forced = (blk == 0) | ((blk <= cur) & (blk > cur - N_LOCAL_SEL))
    score = jnp.where(valid, imp + jnp.where(forced, SEL_BONUS, 0.0), NEG)
    vals, idx = lax.top_k(score, min(N_SEL, n_blocks))
    return idx, vals > NEG / 2


def to_blocks(x, n_blocks):
    B, T, G, D = x.shape
    xp = jnp.pad(x, ((0, 0), (0, n_blocks * SEL_BLOCK - T), (0, 0), (0, 0)))
    return xp.reshape(B, n_blocks, SEL_BLOCK, G, D).transpose(0, 3, 1, 2, 4)


def nsa_global(q, q_pos, kv_cmp, kv_slc, n_blocks, cmp_pe, cmp_w1, cmp_b1, cmp_w2):
    kc = compress(kv_cmp[:, :, 0], cmp_pe[0], cmp_w1[0], cmp_b1[0], cmp_w2[0])
    vc = compress(kv_cmp[:, :, 1], cmp_pe[1], cmp_w1[1], cmp_b1[1], cmp_w2[1])
    ends = jnp.arange(kc.shape[1], dtype=jnp.int32) * CMP_STRIDE + CMP_LEN - 1
    o_cmp, p_cmp = dense_gqa(q, kc, vc, ends[None, :] <= q_pos[:, None])
    idx, ok = select_blocks(p_cmp, q_pos, n_blocks)
    kb = to_blocks(kv_slc[:, :, 0], n_blocks)
    vb = to_blocks(kv_slc[:, :, 1], n_blocks)
    return o_cmp, idx, ok, kb, vb


def sel_attention(q, kb, vb, idx, ok, q_pos):
    B, Tq, H, D = q.shape
    G = kb.shape[1]
    K = idx.shape[-1]
    take = jax.vmap(jax.vmap(lambda a, i: a[i]))
    kg = take(kb, idx)
    vg = take(vb, idx)
    qg = q.reshape(B, Tq, G, H // G, D)
    s = jnp.einsum('btgrd,bgtksd->bgrtks', qg, kg).astype(jnp.float32) * ATTN_SCALE
    kpos = idx[..., None] * SEL_BLOCK + jnp.arange(SEL_BLOCK, dtype=jnp.int32)
    mask = (kpos <= q_pos[:, None, None]) & ok[..., None]
    mask = mask.reshape(B, G, Tq, K * SEL_BLOCK)[:, :, None]
    p = masked_softmax(s.reshape(B, G, H // G, Tq, K * SEL_BLOCK), mask)
    o = jnp.einsum('bgrtn,bgtnd->btgrd', p.astype(vg.dtype), vg.reshape(B, G, Tq, K * SEL_BLOCK, D))
    return o.reshape(B, Tq, H, D)


def sel_attention_blocked(q, kb, vb, idx, ok, q_pos):
    B, T, H, D = q.shape
    G, K = idx.shape[1], idx.shape[-1]
    nb = T // Q_BLOCK
    qb = q.reshape(B, nb, Q_BLOCK, H, D).transpose(1, 0, 2, 3, 4)
    ib = idx.reshape(B, G, nb, Q_BLOCK, K).transpose(2, 0, 1, 3, 4)
    okb = ok.reshape(B, G, nb, Q_BLOCK, K).transpose(2, 0, 1, 3, 4)
    pb = q_pos.reshape(nb, Q_BLOCK)
    out = lax.map(lambda a: sel_attention(a[0], kb, vb, a[1], a[2], a[3]), (qb, ib, okb, pb))
    return out.transpose(1, 0, 2, 3, 4).reshape(B, T, H, D)


def window_prompt(q, k, v):
    B, T, H, D = q.shape
    G = k.shape[2]
    nb = T // Q_BLOCK
    nband = WINDOW // Q_BLOCK
    pad = nband * Q_BLOCK

    def band(x):
        xp = jnp.pad(x, ((0, 0), (pad, 0), (0, 0), (0, 0))).reshape(B, nb + nband, Q_BLOCK, G, D)
        return jnp.concatenate([xp[:, j:j + nb] for j in range(nband + 1)], axis=2)

    kb, vb = band(k), band(v)
    qb = q.reshape(B, nb, Q_BLOCK, G, H // G, D)
    s = jnp.einsum('bctgrd,bcsgd->bgrcts', qb, kb).astype(jnp.float32) * ATTN_SCALE
    base = jnp.arange(nb, dtype=jnp.int32)[:, None] * Q_BLOCK
    qpos = base + jnp.arange(Q_BLOCK, dtype=jnp.int32)[None, :]
    kpos = base - pad + jnp.arange((nband + 1) * Q_BLOCK, dtype=jnp.int32)[None, :]
    dlt = qpos[:, :, None] - kpos[:, None, :]
    mask = (kpos[:, None, :] >= 0) & (dlt >= 0) & (dlt <= WINDOW)
    p = masked_softmax(s, mask)
    o = jnp.einsum('bgrcts,bcsgd->bctgrd', p.astype(v.dtype), vb)
    return o.reshape(B, T, H, D)


def chunk_mix(u, v, ws, bs):
    B, T, W = v.shape
    nc = -(-T // CHUNK)
    vp = jnp.pad(v, ((0, 0), (0, nc * CHUNK - T), (0, 0))).reshape(B, nc, CHUNK, GMLP_GROUPS, GMLP_GROUP_DIM)
    tri = jnp.tril(jnp.ones((CHUNK, CHUNK), dtype=bool))
    wm = jnp.where(tri, ws, 0.0)
    z = jnp.einsum('gij,bcjgd->bcigd', wm, vp) + bs.T[None, None, :, :, None]
    return u * z.reshape(B, nc * CHUNK, W)[:, :T]


def conv_ffn(h, prev, w_up, conv_w, conv_b, w_down):
    ab = h @ w_up
    a, b = ab[..., :D_FF], ab[..., D_FF:]
    T = a.shape[1]
    ext = jnp.concatenate([prev.astype(a.dtype), a], axis=1)
    c = conv_b + ext[:, 0:T] * conv_w[0]
    for j in range(1, CONV_W):
        c = c + ext[:, j:j + T] * conv_w[j]
    y = (jax.nn.gelu(c) * b) @ w_down
    return y, ext[:, -(CONV_W - 1):]


def front(x, norm_mix, w_in, gmlp_norm):
    h = rmsnorm(x, norm_mix)
    q, kv, nsa_g, u, v, gates = project(h, w_in)
    return q, kv, nsa_g, u, rmsnorm(v, gmlp_norm), gates


def back(x, o_cmp, o_slc, o_win, nsa_g, u, vn, gates, conv_prev,
         gmlp_ws, gmlp_bs, w_proj_a, w_proj_b, w_out, norm_ffn, w_up, conv_w, conv_b, w_down):
    B, T = x.shape[:2]
    o_nsa = nsa_g[..., 0:1] * o_cmp + nsa_g[..., 1:2] * o_slc + nsa_g[..., 2:3] * o_win
    br_a = o_nsa.reshape(B, T, Q_COLS) @ w_proj_a
    br_b = chunk_mix(u, vn, gmlp_ws, gmlp_bs) @ w_proj_b
    x1 = x + (gates[:, :, 0] * br_a + gates[:, :, 1] * br_b) @ w_out
    f, conv_state = conv_ffn(rmsnorm(x1, norm_ffn), conv_prev, w_up, conv_w, conv_b, w_down)
    return x1 + f, conv_state


def setup_inputs(seed: int = 0) -> dict:
    key = jax.random.key(seed)
    ks = jax.random.split(key, 32)
    f32 = jnp.float32

    def nrm(k, shape, scale):
        return jax.random.normal(k, shape, f32) * scale

    n_pages = PAST_LEN // PAGE_SIZE
    n_used = DEC_BATCH * n_pages
    n_pool = n_used + (n_used + 3) // 4
    win_buf = min(WINDOW, PAST_LEN)
    page_table = jax.random.permutation(ks[0], n_pool)[:n_used].reshape(DEC_BATCH, n_pages).astype(jnp.int32)
    return dict(
        x_prompt=nrm(ks[1], (BATCH, SEQ, D_MODEL), 1.0),
        x_sample=nrm(ks[2], (DEC_BATCH, DEC_SEQ, D_MODEL), 1.0),
        cache_cmp=nrm(ks[3], (DEPTH, n_pool, PAGE_SIZE, 2, N_KV, HEAD_DIM), 1.0),
        cache_slc=nrm(ks[4], (DEPTH, n_pool, PAGE_SIZE, 2, N_KV, HEAD_DIM), 1.0),
        cache_win=nrm(ks[5], (DEPTH, DEC_BATCH, win_buf, 2, N_KV, HEAD_DIM), 1.0),
        state_conv=nrm(ks[6], (DEPTH, DEC_BATCH, CONV_W - 1, D_FF), 1.0),
        page_table=page_table,
        norm_mix=1.0 + nrm(ks[7], (DEPTH, D_MODEL), 0.02),
        w_in=nrm(ks[8], (DEPTH, D_MODEL, IN_COLS), D_MODEL ** -0.5),
        cmp_pe=nrm(ks[9], (DEPTH, 2, CMP_LEN, HEAD_DIM), 0.1),
        cmp_w1=nrm(ks[10], (DEPTH, 2, CMP_LEN * HEAD_DIM, CMP_HID), (CMP_LEN * HEAD_DIM) ** -0.5),
        cmp_b1=nrm(ks[11], (DEPTH, 2, CMP_HID), 0.02),
        cmp_w2=nrm(ks[12], (DEPTH, 2, CMP_HID, HEAD_DIM), CMP_HID ** -0.5),
        gmlp_norm=1.0 + nrm(ks[13], (DEPTH, GMLP_WIDTH), 0.02),
        gmlp_ws=nrm(ks[14], (DEPTH, GMLP_GROUPS, CHUNK, CHUNK), CHUNK ** -0.5),
        gmlp_bs=1.0 + nrm(ks[15], (DEPTH, GMLP_GROUPS, CHUNK), 0.02),
        w_proj_a=nrm(ks[16], (DEPTH, Q_COLS, D_MODEL), Q_COLS ** -0.5),
        w_proj_b=nrm(ks[17], (DEPTH, GMLP_WIDTH, D_MODEL), GMLP_WIDTH ** -0.5),
        w_out=nrm(ks[18], (DEPTH, D_MODEL, D_MODEL), D_MODEL ** -0.5),
        norm_ffn=1.0 + nrm(ks[19], (DEPTH, D_MODEL), 0.02),
        w_up=nrm(ks[20], (DEPTH, D_MODEL, 2 * D_FF), D_MODEL ** -0.5),
        conv_w=nrm(ks[21], (DEPTH, CONV_W, D_FF), CONV_W ** -0.5),
        conv_b=nrm(ks[22], (DEPTH, D_FF), 0.02),
        w_down=nrm(ks[23], (DEPTH, D_FF, D_MODEL), D_FF ** -0.5),
        norm_final=1.0 + nrm(ks[24], (D_MODEL,), 0.02),
    )


def reference(x_prompt, x_sample, cache_cmp, cache_slc, cache_win, state_conv, page_table,
              norm_mix, w_in, cmp_pe, cmp_w1, cmp_b1, cmp_w2, gmlp_norm, gmlp_ws, gmlp_bs,
              w_proj_a, w_proj_b, w_out, norm_ffn, w_up, conv_w, conv_b, w_down, norm_final):
    xp, xs = x_prompt, x_sample
    Bp, T = xp.shape[:2]
    Bd, Tn = xs.shape[:2]
    n_pages = page_table.shape[1]
    past_len = n_pages * cache_cmp.shape[2]
    pos_p = jnp.arange(T, dtype=jnp.int32)
    pos_s = past_len + jnp.arange(Tn, dtype=jnp.int32)
    nblk_p = -(-T // SEL_BLOCK)
    nblk_s = -(-(past_len + Tn) // SEL_BLOCK)
    st = {n: [] for n in ('cmp_p', 'slc_p', 'win_p', 'v_p', 'conv_p', 'cmp_s', 'slc_s', 'win_s', 'v_s', 'conv_s')}
    for l in range(DEPTH):
        cw = (cmp_pe[l], cmp_w1[l], cmp_b1[l], cmp_w2[l])
        bw = (gmlp_ws[l], gmlp_bs[l], w_proj_a[l], w_proj_b[l], w_out[l],
              norm_ffn[l], w_up[l], conv_w[l], conv_b[l], w_down[l])
        q, kv, g, u, vn, gates = front(xp, norm_mix[l], w_in[l], gmlp_norm[l])
        kv_cmp, kv_slc, kv_win = kv[:, :, 0], kv[:, :, 1], kv[:, :, 2]
        o_cmp, idx, ok, kb, vb = nsa_global(q, pos_p, kv_cmp, kv_slc, nblk_p, *cw)
        o_slc = sel_attention_blocked(q, kb, vb, idx, ok, pos_p)
        o_win = window_prompt(q, kv_win[:, :, 0], kv_win[:, :, 1])
        conv0 = jnp.zeros((Bp, CONV_W - 1, D_FF), xp.dtype)
        xp, conv_p = back(xp, o_cmp, o_slc, o_win, g, u, vn, gates, conv0, *bw)
        st['cmp_p'].append(kv_cmp)
        st['slc_p'].append(kv_slc)
        st['win_p'].append(kv_win[:, -min(WINDOW, T):])
        st['v_p'].append(vn[:, ((T - 1) // CHUNK) * CHUNK:])
        st['conv_p'].append(conv_p)
        q, kv, g, u, vn, gates = front(xs, norm_mix[l], w_in[l], gmlp_norm[l])
        kv_cmp, kv_slc, kv_win = kv[:, :, 0], kv[:, :, 1], kv[:, :, 2]
        rows = (Bd, past_len) + cache_cmp.shape[3:]
        full_cmp = jnp.concatenate([cache_cmp[l][page_table].reshape(rows), kv_cmp], axis=1)
        full_slc = jnp.concatenate([cache_slc[l][page_table].reshape(rows), kv_slc], axis=1)
        o_cmp, idx, ok, kb, vb = nsa_global(q, pos_s, full_cmp, full_slc, nblk_s, *cw)
        o_slc = sel_attention(q, kb, vb, idx, ok, pos_s)
        win_all = jnp.concatenate([cache_win[l], kv_win], axis=1)
        lb = cache_win.shape[2]
        kpos = past_len - lb + jnp.arange(lb + Tn, dtype=jnp.int32)
        dlt = pos_s[:, None] - kpos[None, :]
        o_win, _ = dense_gqa(q, win_all[:, :, 0], win_all[:, :, 1], (dlt >= 0) & (dlt <= WINDOW))
        xs, conv_s = back(xs, o_cmp, o_slc, o_win, g, u, vn, gates, state_conv[l], *bw)
        st['cmp_s'].append(kv_cmp)
        st['slc_s'].append(kv_slc)
        st['win_s'].append(win_all[:, -min(WINDOW, lb + Tn):])
        st['v_s'].append(vn[:, ((Tn - 1) // CHUNK) * CHUNK:])
        st['conv_s'].append(conv_s)
    y_prompt = rmsnorm(xp, norm_final)
    y_sample = rmsnorm(xs, norm_final)
    return (y_prompt, y_sample,
            jnp.stack(st['cmp_p']), jnp.stack(st['slc_p']), jnp.stack(st['win_p']),
            jnp.stack(st['v_p']), jnp.stack(st['conv_p']),
            jnp.stack(st['cmp_s']), jnp.stack(st['slc_s']), jnp.stack(st['win_s']),
            jnp.stack(st['v_s']), jnp.stack(st['conv_s']))
```

```python
import functools

import jax
import jax.numpy as jnp
from jax import lax
from jax.experimental import pallas as pl
from jax.experimental.pallas import tpu as pltpu

F32 = jnp.float32
BF16 = jnp.bfloat16
I32 = jnp.int32

CMP_STRIDE = 16
SEL_BLOCK = 64
N_SEL = 16
N_LOCAL_SEL = 2
WINDOW = 512
Q_BLOCK = 128
GMLP_GROUPS = 4
EPS = 1e-6
NEG = -1e30
BELOW_NEG = -3e38
SEL_BONUS = 1e6

V7X_VMEM_BYTES = 64 * 1024 * 1024
VMEM_REQUEST_BYTES = 56 * 1024 * 1024
LANES = 128


def _cparams(n_grid):
    return pltpu.CompilerParams(
        dimension_semantics=("arbitrary",) * n_grid, vmem_limit_bytes=VMEM_REQUEST_BYTES)


def _rmsnorm(x, g):
    ms = jnp.mean(x * x, axis=-1, keepdims=True)
    return x * lax.rsqrt(ms + EPS) * g


def _dot(a, b):
    return jnp.dot(a, b, preferred_element_type=F32)


def _dot_nt(a, b):
    return lax.dot_general(a, b, (((1,), (1,)), ((), ())), preferred_element_type=F32)


def _dot_exact(a, b):
    return jnp.dot(a, b, precision=lax.Precision.HIGHEST, preferred_element_type=F32)


def _shr(x, n):
    assert n & (n - 1) == 0
    return x >> (n.bit_length() - 1)


def _const_spec(shape):
    nd = len(shape)
    return pl.BlockSpec(shape, lambda *_: (0,) * nd)


def _front_kernel(x_ref, nw_ref, wkvt_ref, wqt_ref, wgt_ref, cmp_ref, slc_ref, win_ref, qt_ref, gt_ref,
                  *, kv_cols, scale):
    h = _rmsnorm(x_ref[0], nw_ref[...]).astype(BF16)
    kvt = _dot_nt(wkvt_ref[...], h)
    cmp_ref[0] = kvt[0:kv_cols]
    slc_ref[0] = kvt[kv_cols:2 * kv_cols]
    win_ref[0] = kvt[2 * kv_cols:3 * kv_cols]
    qt_ref[0] = (_dot_nt(wqt_ref[...], h) * scale).astype(BF16)
    gt_ref[0] = jax.nn.sigmoid(_dot_nt(wgt_ref[...], h))


def _front(x, nw, wkvt, wqt, wgt, *, kv_cols, scale, tm):
    b, t, d = x.shape
    q_cols, g_rows = wqt.shape[0], wgt.shape[0]
    kv_shape = jax.ShapeDtypeStruct((b, kv_cols, t), F32)
    kv_spec = pl.BlockSpec((1, kv_cols, tm), lambda i, j: (i, 0, j))
    return pl.pallas_call(
        functools.partial(_front_kernel, kv_cols=kv_cols, scale=scale),
        grid=(b, t // tm),
        in_specs=[pl.BlockSpec((1, tm, d), lambda i, j: (i, j, 0)), _const_spec(nw.shape),
                  _const_spec(wkvt.shape), _const_spec(wqt.shape), _const_spec(wgt.shape)],
        out_specs=[kv_spec, kv_spec, kv_spec,
                   pl.BlockSpec((1, q_cols, tm), lambda i, j: (i, 0, j)),
                   pl.BlockSpec((1, g_rows, tm), lambda i, j: (i, 0, j))],
        out_shape=[kv_shape, kv_shape, kv_shape,
                   jax.ShapeDtypeStruct((b, q_cols, t), BF16),
                   jax.ShapeDtypeStruct((b, g_rows, t), F32)],
        compiler_params=_cparams(2), name="front",
    )(x, nw, wkvt, wqt, wgt)


def _front_dec_kernel(x_ref, nw_ref, w_ref, q_ref, kv_ref, g_ref, *, q_cols, kv_cols3, scale):
    h = _rmsnorm(x_ref[...], nw_ref[...]).astype(BF16)
    z = _dot_nt(h, w_ref[...])
    q_ref[...] = z[:, 0:q_cols] * scale
    kv_ref[...] = z[:, q_cols:q_cols + kv_cols3]
    g_ref[...] = jax.nn.sigmoid(z[:, q_cols + kv_cols3:])


def _front_dec(x, nw, w, *, q_cols, kv_cols3, scale):
    n = x.shape[0]
    g_cols = w.shape[0] - q_cols - kv_cols3
    return pl.pallas_call(
        functools.partial(_front_dec_kernel, q_cols=q_cols, kv_cols3=kv_cols3, scale=scale),
        grid=(1,),
        in_specs=[_const_spec(x.shape), _const_spec(nw.shape), _const_spec(w.shape)],
        out_specs=[_const_spec((n, q_cols)), _const_spec((n, kv_cols3)), _const_spec((n, g_cols))],
        out_shape=[jax.ShapeDtypeStruct((n, q_cols), F32), jax.ShapeDtypeStruct((n, kv_cols3), F32),
                   jax.ShapeDtypeStruct((n, g_cols), F32)],
        compiler_params=_cparams(1), name="front_dec",
    )(x, nw, w)


def _compress_kernel(*refs, n_x, n_prefetch, stride, kv_cols, hid2):
    refs = refs[n_prefetch:]
    x_refs = refs[:n_x]
    pe_ref, w1_ref, b1_ref, w2_ref, out_ref, carry_ref, pos_ref = refs[n_x:]
    half = kv_cols // 2
    assert half == LANES

    @pl.when(pl.program_id(1) == 0)
    def _():
        carry_ref[...] = jnp.zeros_like(carry_ref)

    tiles_per_x = x_refs[0].shape[2] // LANES
    for k, r in enumerate(x_refs):
        for c in range(tiles_per_x):
            p0 = (k * tiles_per_x + c) * LANES
            for kv in range(2):
                pos_ref[kv, p0:p0 + LANES, :] = r[0, kv * half:(kv + 1) * half, c * LANES:(c + 1) * LANES].T

    rows = out_ref.shape[1]
    row = lax.broadcasted_iota(I32, (rows, hid2), 0)
    for kv in range(2):
        xkv = jnp.concatenate(
            [pos_ref[kv, pl.ds(s, rows, stride=stride), :] for s in range(stride)],
            axis=1)
        parts = []
        for r in range(2):
            a = (xkv + pe_ref[kv, r]).astype(BF16)
            parts.append(_dot(a, w1_ref[kv, :, r * hid2:(r + 1) * hid2]))
        prev = carry_ref[kv, 0:1, :]
        shifted = jnp.where(row == 0, prev, pltpu.roll(parts[0], 1, 0))
        carry_ref[kv, 0:1, :] = parts[0][rows - 1:rows, :]
        hid = b1_ref[kv] + shifted + parts[1]
        out_ref[0, :, kv * half:(kv + 1) * half] = _dot(jax.nn.gelu(hid).astype(BF16), w2_ref[kv])


def _compress(x_list, pe, w1, b1, w2, *, n_seq, steps, rows, page_table=None, pages_per_step=None):
    stride, kv_cols = CMP_STRIDE, w2.shape[2] * 2
    hid2 = b1.shape[2]
    n_x = len(x_list)
    kern = functools.partial(_compress_kernel, n_x=n_x, n_prefetch=0 if page_table is None else 1,
                             stride=stride, kv_cols=kv_cols, hid2=hid2)
    out_shape = jax.ShapeDtypeStruct((n_seq, steps * rows, kv_cols), F32)
    scratch = [pltpu.VMEM((2, 8, hid2), F32), pltpu.VMEM((2, rows * stride, kv_cols // 2), F32)]
    if page_table is None:
        x_specs = [pl.BlockSpec((1, kv_cols, rows * stride), lambda i, j: (i, 0, j))]
        w_specs = [_const_spec(a.shape) for a in (pe, w1, b1, w2)]
        return pl.pallas_call(
            kern, grid=(n_seq, steps), in_specs=x_specs + w_specs,
            out_specs=pl.BlockSpec((1, rows, kv_cols), lambda i, j: (i, j, 0)),
            out_shape=out_shape, scratch_shapes=scratch, compiler_params=_cparams(2), name="compress",
        )(*x_list, pe, w1, b1, w2)
    page = rows * stride // pages_per_step
    x_specs = [pl.BlockSpec((1, kv_cols, page),
                            functools.partial(lambda i, j, pt, k: (pt[i, j * pages_per_step + k], 0, 0), k=k))
               for k in range(n_x)]
    w_specs = [pl.BlockSpec(a.shape, functools.partial(lambda i, j, pt, nd: (0,) * nd, nd=a.ndim))
               for a in (pe, w1, b1, w2)]
    return pl.pallas_call(
        kern,
        grid_spec=pltpu.PrefetchScalarGridSpec(
            num_scalar_prefetch=1, grid=(n_seq, steps), in_specs=x_specs + w_specs,
            out_specs=pl.BlockSpec((1, rows, kv_cols), lambda i, j, pt: (i, j, 0)),
            scratch_shapes=scratch),
        out_shape=out_shape, compiler_params=_cparams(2), name="compress_paged",
    )(page_table, *x_list, pe, w1, b1, w2)


def _block_scores(imp, blk, q_pos, n_blocks):
    cur = _shr(q_pos, SEL_BLOCK)
    valid = (blk * SEL_BLOCK <= q_pos) & (blk < n_blocks)
    forced = (blk == 0) | ((blk <= cur) & (blk > cur - N_LOCAL_SEL))
    score = jnp.where(valid, imp + jnp.where(forced, SEL_BONUS, 0.0), NEG)
    return jnp.where(blk < n_blocks, score, BELOW_NEG)


def _attn_kernel(qt_ref, gt_ref, kcv_ref, slc_ref, win_ref, o_ref,
                 kaug_ref, vts_ref, kwin_ref, vtw_ref, kc_ref, vct_ref, *, n_rep, hd, n_blocks):
    g = pl.program_id(1)
    i = pl.program_id(2)
    qb = Q_BLOCK
    n_tiles = slc_ref.shape[2] // qb
    n_cmp_rows = kcv_ref.shape[1]
    nq = n_rep * qb

    def build(gg):
        lane = lax.broadcasted_iota(I32, (qb, 2 * hd), 1)
        krow = lax.broadcasted_iota(I32, (qb, 2 * hd), 0)

        def group_lanes(x):
            return x if gg == 0 else pltpu.roll(x, hd, 1)

        for c in range(n_tiles):
            cols = slice(c * qb, (c + 1) * qb)
            onehot = (lane - hd == _shr(c * qb + krow, SEL_BLOCK)).astype(F32)
            kaug_ref[c] = jnp.where(lane < hd, group_lanes(slc_ref[0, 0:2 * hd, cols].T), onehot).astype(BF16)
            vts_ref[c] = slc_ref[0, (2 + gg) * hd:(3 + gg) * hd, cols].astype(BF16)
            kwin_ref[c] = jnp.where(lane < hd, group_lanes(win_ref[0, 0:2 * hd, cols].T), 0.0).astype(BF16)
            vtw_ref[c] = win_ref[0, (2 + gg) * hd:(3 + gg) * hd, cols].astype(BF16)
        for c in range(n_cmp_rows // qb):
            blk = kcv_ref[0, c * qb:(c + 1) * qb, :]
            kc_ref[c * qb:(c + 1) * qb, :] = group_lanes(blk[:, 0:2 * hd])[:, 0:hd].astype(BF16)
            vct_ref[:, c * qb:(c + 1) * qb] = blk[:, 2 * hd:4 * hd].T[gg * hd:(gg + 1) * hd, :].astype(BF16)

    for gg in range(2):
        @pl.when((i == 0) & (g == gg))
        def _(gg=gg):
            build(gg)

    qt = qt_ref[0]
    qcat = jnp.concatenate([qt[h * hd:(h + 1) * hd, :] for h in range(n_rep)], axis=1)
    q_pos = i * qb + (lax.broadcasted_iota(I32, (1, nq), 1) & (qb - 1))

    m_idx = lax.broadcasted_iota(I32, (n_cmp_rows, nq), 0)
    vis = (m_idx >= 1) & ((m_idx - 1) * CMP_STRIDE + 2 * CMP_STRIDE - 1 <= q_pos)
    s = jnp.where(vis, _dot(kc_ref[...], qcat), NEG)
    e = jnp.where(vis, jnp.exp(s - jnp.max(s, axis=0, keepdims=True)), 0.0)
    den = jnp.sum(e, axis=0, keepdims=True)
    p = e * (1.0 / jnp.where(den > 0, den, 1.0))
    o_cmp = _dot(vct_ref[...], p.astype(BF16))

    p_grp = p[:, 0:qb]
    for h in range(1, n_rep):
        p_grp = p_grp + p[:, h * qb:(h + 1) * qb]
    ratio = SEL_BLOCK // CMP_STRIDE
    pj = lax.broadcasted_iota(I32, (n_blocks, n_cmp_rows), 0)
    pm = lax.broadcasted_iota(I32, (n_blocks, n_cmp_rows), 1)
    pool = ((pm >= 1) & (pm >= ratio * pj) & (pm <= ratio * pj + ratio)).astype(F32)
    imp = _dot_exact(pool, p_grp)
    blk_id = lax.broadcasted_iota(I32, (n_blocks, qb), 0)
    score = _block_scores(imp, blk_id, q_pos[:, 0:qb], n_blocks)
    work, sel = score, jnp.zeros((n_blocks, qb), F32)
    blk_f = blk_id.astype(F32)
    for _ in range(min(N_SEL, n_blocks)):
        mx = jnp.max(work, axis=0, keepdims=True)
        first = jnp.min(jnp.where(work == mx, blk_f, float(n_blocks)), axis=0, keepdims=True)
        pick = blk_f == first
        sel = jnp.where(pick, 1.0, sel)
        work = jnp.where(pick, BELOW_NEG, work)
    bias = jnp.where((sel > 0) & (score > NEG / 2), 0.0, NEG).astype(BF16)
    assert n_blocks == hd, "the selection bias rows ride in the key one-hot lanes"
    qaug = jnp.concatenate(
        [jnp.concatenate([qt[h * hd:(h + 1) * hd, :], bias], axis=0) for h in range(n_rep)], axis=1)

    krow = lax.broadcasted_iota(I32, (qb, nq), 0)
    qcol = lax.broadcasted_iota(I32, (qb, nq), 1) & (qb - 1)

    def update(carry, sc, vt):
        m, l, acc = carry
        m_new = jnp.maximum(m, jnp.max(sc, axis=0, keepdims=True))
        alpha = jnp.exp(m - m_new)
        pe = jnp.exp(sc - m_new)
        l = alpha * l + jnp.sum(pe, axis=0, keepdims=True)
        acc = alpha * acc + _dot(vt, pe.astype(BF16))
        return m_new, l, acc

    carry0 = (jnp.full((1, nq), NEG, F32), jnp.zeros((1, nq), F32), jnp.zeros((hd, nq), F32))

    carry = lax.fori_loop(0, i, lambda j, c: update(c, _dot(kaug_ref[j], qaug), vts_ref[j]), carry0)
    sc = jnp.where(krow <= qcol, _dot(kaug_ref[i], qaug), NEG)
    _, l, acc = update(carry, sc, vts_ref[i])
    o_slc = acc * (1.0 / l)

    n_band = WINDOW // qb
    first_t = jnp.maximum(i - n_band, 0)
    sc = jnp.where((krow >= qcol) & (i >= n_band), _dot(kwin_ref[first_t], qaug), NEG)
    carry = update(carry0, sc, vtw_ref[first_t])
    carry = lax.fori_loop(jnp.maximum(i - n_band + 1, 0), i,
                          lambda j, c: update(c, _dot(kwin_ref[j], qaug), vtw_ref[j]), carry)
    sc = jnp.where(krow <= qcol, _dot(kwin_ref[i], qaug), NEG)
    _, l, acc = update(carry, sc, vtw_ref[i])
    o_win = acc * (1.0 / l)

    outs = []
    for h in range(n_rep):
        cols = slice(h * qb, (h + 1) * qb)
        gate = [gt_ref[0, 0, 3 * h + br:3 * h + br + 1, :] for br in range(3)]
        outs.append(gate[0] * o_cmp[:, cols] + gate[1] * o_slc[:, cols] + gate[2] * o_win[:, cols])
    o_ref[0] = jnp.concatenate(outs, axis=0).T.astype(BF16)


def _attention(qt, gt, kcv, slc, win, *, n_rep, hd):
    b, q_cols, t = qt.shape
    n_kv = q_cols // (n_rep * hd)
    qb = Q_BLOCK
    n_tiles = t // qb
    n_blocks = t // SEL_BLOCK
    kv_cols = slc.shape[1]
    gt4 = gt.reshape(b, n_kv, gt.shape[1] // n_kv, t)
    tile_spec = pl.BlockSpec((1, kv_cols, t), lambda bi, g, i: (bi, 0, 0))
    return pl.pallas_call(
        functools.partial(_attn_kernel, n_rep=n_rep, hd=hd, n_blocks=n_blocks),
        grid=(b, n_kv, n_tiles),
        in_specs=[pl.BlockSpec((1, n_rep * hd, qb), lambda bi, g, i: (bi, g, i)),
                  pl.BlockSpec((1, 1, gt4.shape[2], qb), lambda bi, g, i: (bi, g, 0, i)),
                  pl.BlockSpec((1,) + kcv.shape[1:], lambda bi, g, i: (bi, 0, 0)),
                  tile_spec, tile_spec],
        out_specs=pl.BlockSpec((1, qb, n_rep * hd), lambda bi, g, i: (bi, i, g)),
        out_shape=jax.ShapeDtypeStruct((b, t, q_cols), BF16),
        scratch_shapes=[pltpu.VMEM((n_tiles, qb, 2 * hd), BF16), pltpu.VMEM((n_tiles, hd, qb), BF16),
                        pltpu.VMEM((n_tiles, qb, 2 * hd), BF16), pltpu.VMEM((n_tiles, hd, qb), BF16),
                        pltpu.VMEM((kcv.shape[1], hd), BF16), pltpu.VMEM((hd, kcv.shape[1]), BF16)],
        compiler_params=_cparams(3), name="attention",
    )(qt, gt4, kcv, slc, win)


def _spread_q(q, hd, n_rep):
    n_heads = q.shape[0]
    d = lax.broadcasted_iota(I32, (hd, 4 * hd), 0)
    c = lax.broadcasted_iota(I32, (hd, 4 * hd), 1)
    qb16 = q.astype(BF16)
    row = lax.broadcasted_iota(I32, (n_heads, 4 * hd), 0)
    out = jnp.zeros((n_heads, 4 * hd), F32)
    for gg in range(n_heads // n_rep):
        placed = _dot(qb16, (c == d + gg * hd).astype(BF16))
        out = jnp.where(_shr(row, n_rep) == gg, placed, out)
    return out.astype(BF16)


def _masked_softmax_rows(s, mask):
    s = jnp.where(mask, s, NEG)
    e = jnp.where(mask, jnp.exp(s - jnp.max(s, axis=-1, keepdims=True)), 0.0)
    den = jnp.sum(e, axis=-1, keepdims=True)
    return e * (1.0 / jnp.where(den > 0, den, 1.0))


def _group_value_lanes(o_full, hd, n_rep):
    row = lax.broadcasted_iota(I32, (o_full.shape[0], hd), 0)
    out = o_full[:, 2 * hd:3 * hd]
    for gg in range(1, o_full.shape[0] // n_rep):
        out = jnp.where(_shr(row, n_rep) == gg, o_full[:, (2 + gg) * hd:(3 + gg) * hd], out)
    return out


def _cmp_select_dec_kernel(q_ref, kcv_ref, o_ref, idx_ref, *, hd, n_rep, q_pos, n_blocks, blk_lanes):
    n_heads = q_ref.shape[1]
    n_rows = kcv_ref.shape[1]
    q2 = _spread_q(q_ref[0], hd, n_rep)
    kcv = kcv_ref[0].astype(BF16)
    m_idx = lax.broadcasted_iota(I32, (n_heads, n_rows), 1)
    vis = (m_idx >= 1) & ((m_idx - 1) * CMP_STRIDE + 2 * CMP_STRIDE - 1 <= q_pos)
    p = _masked_softmax_rows(_dot_nt(q2, kcv), vis)
    o_ref[0] = _group_value_lanes(_dot(p.astype(BF16), kcv), hd, n_rep)

    row = lax.broadcasted_iota(I32, (n_heads, n_rows), 0)
    grp = jnp.zeros((n_heads, n_rows), F32)
    for gg in range(n_heads // n_rep):
        tot = jnp.sum(jnp.where(_shr(row, n_rep) == gg, p, 0.0), axis=0, keepdims=True)
        grp = jnp.where(row == gg, tot, grp)
    ratio = SEL_BLOCK // CMP_STRIDE
    pm = lax.broadcasted_iota(I32, (n_rows, blk_lanes), 0)
    pj = lax.broadcasted_iota(I32, (n_rows, blk_lanes), 1)
    pool = ((pm >= 1) & (pm >= ratio * pj) & (pm <= ratio * pj + ratio)).astype(F32)
    imp = _dot_exact(grp, pool)
    blk_id = lax.broadcasted_iota(I32, (n_heads, blk_lanes), 1)
    work = _block_scores(imp, blk_id, q_pos, n_blocks)
    out_lane = lax.broadcasted_iota(I32, (n_heads, LANES), 1)
    out = jnp.full((n_heads, LANES), -1, I32)
    blk_f = blk_id.astype(F32)
    for it in range(min(N_SEL, n_blocks)):
        mx = jnp.max(work, axis=-1, keepdims=True)
        first = jnp.min(jnp.where(work == mx, blk_f, float(blk_lanes)), axis=-1, keepdims=True)
        out = jnp.where(out_lane == it, jnp.where(mx > NEG / 2, first.astype(I32), -1), out)
        work = jnp.where(blk_f == first, BELOW_NEG, work)
    idx_ref[0] = out


def _cmp_select_dec(q3, kcv, *, hd, n_rep, q_pos, n_blocks):
    n, n_heads, _ = q3.shape
    blk_lanes = -(-n_blocks // LANES) * LANES
    return pl.pallas_call(
        functools.partial(_cmp_select_dec_kernel, hd=hd, n_rep=n_rep, q_pos=q_pos, n_blocks=n_blocks,
                          blk_lanes=blk_lanes),
        grid=(n,),
        in_specs=[pl.BlockSpec((1, n_heads, hd), lambda i: (i, 0, 0)),
                  pl.BlockSpec((1,) + kcv.shape[1:], lambda i: (i, 0, 0))],
        out_specs=[pl.BlockSpec((1, n_heads, hd), lambda i: (i, 0, 0)),
                   pl.BlockSpec((1, n_heads, LANES), lambda i: (i, 0, 0))],
        out_shape=[jax.ShapeDtypeStruct((n, n_heads, hd), F32), jax.ShapeDtypeStruct((n, n_heads, LANES), I32)],
        compiler_params=_cparams(1), name="cmp_select_dec",
    )(q3, kcv)


def _attend_dec_kernel(pt_ref, sel_ref, *refs, hd, n_rep, n_sel, n_past_blocks):
    n_kv = (len(refs) - 7) // n_sel
    blk_refs = refs[:n_kv * n_sel]
    q_ref, new_slc_ref, new_win_ref, cwin_ref, g_ref, ocmp_ref, o_ref = refs[n_kv * n_sel:]
    b = pl.program_id(0)
    n_heads = q_ref.shape[1]
    q2 = _spread_q(q_ref[0], hd, n_rep)
    q2f = q2.astype(F32)
    head_grp = _shr(lax.broadcasted_iota(I32, (n_heads, 1), 0), n_rep)

    def with_new_key(s, mask, keys_t, new_row, new_ok):
        nr = new_row.astype(BF16).astype(F32)
        s_new = jnp.sum(q2f * nr, axis=-1, keepdims=True)
        s = jnp.where(mask, s, NEG)
        s_new = jnp.where(new_ok, s_new, NEG)
        m = jnp.maximum(jnp.max(s, axis=-1, keepdims=True), s_new)
        e = jnp.where(mask, jnp.exp(s - m), 0.0)
        e_new = jnp.where(new_ok, jnp.exp(s_new - m), 0.0)
        den = jnp.sum(e, axis=-1, keepdims=True) + e_new
        inv = 1.0 / jnp.where(den > 0, den, 1.0)
        return _dot_nt((e * inv).astype(BF16), keys_t) + (e_new * inv) * nr

    page = blk_refs[0].shape[2]
    per = page // SEL_BLOCK
    keys_t = jnp.concatenate([r[0] for r in blk_refs], axis=1).astype(BF16)
    n_keys = keys_t.shape[1]
    col = lax.broadcasted_iota(I32, (1, n_keys), 1)
    col_slot = _shr(col, page)
    col_blk = _shr(col & (page - 1), SEL_BLOCK)
    slot_ok = jnp.zeros((1, n_keys), I32)
    new_ok = jnp.zeros((n_heads, 1), I32)
    for gg in range(n_kv):
        for k in range(n_sel):
            idx = sel_ref[b, gg * n_sel + k]
            cached = ((idx >= 0) & (idx < n_past_blocks)).astype(I32)
            in_blk = (col_blk == (idx & (per - 1))).astype(I32) * cached
            slot_ok = jnp.where(col_slot == gg * n_sel + k, in_blk, slot_ok)
            new_ok = jnp.where(head_grp == gg, new_ok | (idx == n_past_blocks).astype(I32), new_ok)
    mask = (slot_ok > 0) & (_shr(col_slot, n_sel) == head_grp)
    o_slc = _group_value_lanes(with_new_key(_dot(q2, keys_t), mask, keys_t, new_slc_ref[0], new_ok > 0), hd, n_rep)

    keys_t = cwin_ref[0].astype(BF16)
    all_ok = jnp.full((n_heads, keys_t.shape[1]), True)
    o_win = _group_value_lanes(
        with_new_key(_dot(q2, keys_t), all_ok, keys_t, new_win_ref[0], jnp.full((n_heads, 1), True)), hd, n_rep)

    gates = g_ref[0]
    o_ref[0] = gates[:, 0:1] * ocmp_ref[0] + gates[:, 1:2] * o_slc + gates[:, 2:3] * o_win


def _attend_dec(page_table, sel, slc_pages, q3, new_slc, new_win, cache_win, gates3, o_cmp, *, hd, n_rep, n_sel,
                n_past_blocks):
    n, n_heads, _ = q3.shape
    n_kv = n_heads // n_rep
    per = n_past_blocks // page_table.shape[1]

    def blk_map(i, pt, sl, slot):
        idx = jnp.clip(sl[i, slot], 0, n_past_blocks - 1)
        return (pt[i, _shr(idx, per)], 0, 0)

    blk_specs = [pl.BlockSpec((1,) + slc_pages.shape[1:], functools.partial(blk_map, slot=s))
                 for s in range(n_kv * n_sel)]

    def row_spec(shape):
        nd = len(shape)
        return pl.BlockSpec((1,) + tuple(shape[1:]), lambda i, pt, sl: (i,) + (0,) * (nd - 1))

    others = (q3, new_slc, new_win, cache_win, gates3, o_cmp)
    return pl.pallas_call(
        functools.partial(_attend_dec_kernel, hd=hd, n_rep=n_rep, n_sel=n_sel, n_past_blocks=n_past_blocks),
        grid_spec=pltpu.PrefetchScalarGridSpec(
            num_scalar_prefetch=2, grid=(n,),
            in_specs=blk_specs + [row_spec(a.shape) for a in others],
            out_specs=row_spec(o_cmp.shape)),
        out_shape=jax.ShapeDtypeStruct(o_cmp.shape, F32),
        compiler_params=_cparams(1), name="attend_dec",
    )(page_table, sel, *([slc_pages] * (n_kv * n_sel)), *others)


def _mix_kernel(x_ref, o_ref, nw_ref, wuv_ref, wgate_ref, gn_ref, ws_ref, bs_ref, wpa_ref, wpb_ref, wout_ref,
                x1_ref, v_ref, *, width, chunk, single_pos):
    x = x_ref[0]
    d = x.shape[-1]
    h = _rmsnorm(x, nw_ref[...]).astype(BF16)
    uv = jax.nn.gelu(_dot_nt(h, wuv_ref[...]))
    u, vn = uv[:, 0:width], _rmsnorm(uv[:, width:2 * width], gn_ref[...])
    gates = jax.nn.sigmoid(_dot_nt(h, wgate_ref[...]))
    gw = width // GMLP_GROUPS
    if single_pos:
        v_ref[0] = vn
        mixed = u * (ws_ref[...] * vn + bs_ref[...])
    else:
        rows = x.shape[0]
        v_ref[0] = vn[rows - chunk:rows, :]
        tri = lax.broadcasted_iota(I32, (chunk, chunk), 0) >= lax.broadcasted_iota(I32, (chunk, chunk), 1)
        vb = vn.astype(BF16)
        pieces = []
        for c in range(rows // chunk):
            zs = []
            for gi in range(GMLP_GROUPS):
                wm = jnp.where(tri, ws_ref[gi], 0.0).astype(BF16)
                zs.append(_dot(wm, vb[c * chunk:(c + 1) * chunk, gi * gw:(gi + 1) * gw]) + bs_ref[:, gi:gi + 1])
            pieces.append(jnp.concatenate(zs, axis=1))
        mixed = u * jnp.concatenate(pieces, axis=0)
    br_a = _dot(o_ref[0].astype(BF16), wpa_ref[...])
    br_b = _dot(mixed.astype(BF16), wpb_ref[...])
    merged = gates[:, 0:d] * br_a + gates[:, d:2 * d] * br_b
    x1_ref[0] = x + _dot(merged.astype(BF16), wout_ref[...])


def _mix(x, o_nsa, nw, wuv, wgate, gn, ws, bs, wpa, wpb, wout, *, tm, chunk, single_pos):
    b, t, d = x.shape
    width = wuv.shape[0] // 2
    v_rows = tm if single_pos else chunk
    weights = (nw, wuv, wgate, gn, ws, bs, wpa, wpb, wout)
    return pl.pallas_call(
        functools.partial(_mix_kernel, width=width, chunk=chunk, single_pos=single_pos),
        grid=(b, t // tm),
        in_specs=[pl.BlockSpec((1, tm, d), lambda i, j: (i, j, 0)),
                  pl.BlockSpec((1, tm, o_nsa.shape[-1]), lambda i, j: (i, j, 0))]
                 + [_const_spec(a.shape) for a in weights],
        out_specs=[pl.BlockSpec((1, tm, d), lambda i, j: (i, j, 0)),
                   pl.BlockSpec((1, v_rows, width), lambda i, j: (i, 0, 0))],
        out_shape=[jax.ShapeDtypeStruct((b, t, d), F32), jax.ShapeDtypeStruct((b, v_rows, width), F32)],
        compiler_params=_cparams(2), name="mix_dec" if single_pos else "mix",
    )(x, o_nsa, *weights)


def _ffn_kernel(x1_ref, prev_ref, nf_ref, wup_ref, cw_ref, cb_ref, wdown_ref, nfin_ref, y_ref, a_ref,
                *, d_ff, f_tile, halo, single_pos):
    x1 = x1_ref[0]
    rows = x1.shape[0]
    if single_pos:
        h = _rmsnorm(x1, nf_ref[...]).astype(BF16)
    else:
        h = _rmsnorm(jnp.concatenate([prev_ref[0], x1], axis=0), nf_ref[...]).astype(BF16)
        ext_row = lax.broadcasted_iota(I32, (rows + halo, f_tile), 0)
        first = pl.program_id(1) == 0
    y = jnp.zeros_like(x1)
    for f0 in range(0, d_ff, f_tile):
        cols = slice(f0, f0 + f_tile)
        a = _dot(h, wup_ref[:, cols])
        bgate = _dot(h, wup_ref[:, d_ff + f0:d_ff + f0 + f_tile])
        if single_pos:
            a_ref[0, :, cols] = a
            c = cb_ref[:, cols] + prev_ref[0, :, cols] * cw_ref[0:1, cols] + prev_ref[1, :, cols] * cw_ref[1:2, cols] \
                + a * cw_ref[2:3, cols]
        else:
            a = jnp.where((ext_row < halo) & first, 0.0, a)
            back2 = pltpu.roll(a, 2, 0)
            a_ref[0, :, cols] = back2[0:2, :]
            c = cb_ref[:, cols] + back2[halo:, :] * cw_ref[0:1, cols] \
                + pltpu.roll(a, 1, 0)[halo:, :] * cw_ref[1:2, cols] + a[halo:, :] * cw_ref[2:3, cols]
            bgate = bgate[halo:, :]
        y = y + _dot((jax.nn.gelu(c) * bgate).astype(BF16), wdown_ref[cols, :])
    y_ref[0] = _rmsnorm(x1 + y, nfin_ref[...])


def _ffn(x1, prev, nf, wup, cw, cb, wdown, nfin, *, tm, f_tile, single_pos):
    b, t, d = x1.shape
    d_ff = wdown.shape[0]
    halo = 8
    weights = (nf, wup, cw, cb, wdown, nfin)
    if single_pos:
        prev_spec = _const_spec(prev.shape)
        a_rows = tm
    else:
        per = tm // halo
        prev_spec = pl.BlockSpec((1, halo, d), lambda i, j: (i, jnp.maximum(j * per - 1, 0), 0))
        a_rows = 2
    return pl.pallas_call(
        functools.partial(_ffn_kernel, d_ff=d_ff, f_tile=f_tile, halo=halo, single_pos=single_pos),
        grid=(b, t // tm),
        in_specs=[pl.BlockSpec((1, tm, d), lambda i, j: (i, j, 0)), prev_spec] + [_const_spec(a.shape) for a in weights],
        out_specs=[pl.BlockSpec((1, tm, d), lambda i, j: (i, j, 0)),
                   pl.BlockSpec((1, a_rows, d_ff), lambda i, j: (i, 0, 0))],
        out_shape=[jax.ShapeDtypeStruct((b, t, d), F32), jax.ShapeDtypeStruct((b, a_rows, d_ff), F32)],
        compiler_params=_cparams(2), name="ffn_dec" if single_pos else "ffn",
    )(x1, prev, *weights)


def _compress_params(pe, w1, b1, w2, n_kv):
    cmp_len, hd = pe.shape[1], pe.shape[2]
    hid = w1.shape[2]
    halves = cmp_len // CMP_STRIDE
    eye = jnp.eye(n_kv, dtype=w1.dtype)
    pe_t = jnp.broadcast_to(pe.reshape(2, halves, CMP_STRIDE, 1, hd), (2, halves, CMP_STRIDE, n_kv, hd))
    pe_t = pe_t.reshape(2, halves, 1, CMP_STRIDE * n_kv * hd)
    w1h = w1.reshape(2, halves, CMP_STRIDE, hd, hid)
    w1b = jnp.einsum('krsdh,gf->ksgdrfh', w1h, eye).reshape(2, CMP_STRIDE * n_kv * hd, halves * n_kv * hid)
    b1t = jnp.tile(b1, (1, n_kv)).reshape(2, 1, n_kv * hid)
    w2b = jnp.einsum('khd,gf->kghfd', w2, eye).reshape(2, n_kv * hid, n_kv * hd)
    return pe_t, w1b.astype(BF16), b1t, w2b.astype(BF16)


def kernel(x_prompt, x_sample, cache_cmp, cache_slc, cache_win, state_conv, page_table, norm_mix, w_in, cmp_pe,
           cmp_w1, cmp_b1, cmp_w2, gmlp_norm, gmlp_ws, gmlp_bs, w_proj_a, w_proj_b, w_out, norm_ffn, w_up, conv_w,
           conv_b, w_down, norm_final):
    depth = w_in.shape[0]
    assert depth == 1, "single-layer step"
    bp, t, d = x_prompt.shape
    bd, tn, _ = x_sample.shape
    assert tn == 1
    n_kv, hd = cache_cmp.shape[4], cache_cmp.shape[5]
    page = cache_cmp.shape[2]
    q_cols = w_proj_a.shape[1]
    n_heads = q_cols // hd
    n_rep = n_heads // n_kv
    kv_cols = 2 * n_kv * hd
    width = w_proj_b.shape[1]
    chunk = gmlp_ws.shape[-1]
    d_ff = w_down.shape[1]
    n_pages = page_table.shape[1]
    past_len = n_pages * page
    scale = hd ** -0.5
    assert conv_w.shape[1] == 3 and cache_win.shape[2] <= WINDOW and past_len % SEL_BLOCK == 0

    wit = w_in[0].T.astype(BF16)
    off_kv, off_g = q_cols, q_cols + 3 * kv_cols
    off_uv = off_g + 3 * n_heads
    off_gate = off_uv + 2 * width
    w_qt, w_kvt, w_g = wit[0:off_kv], wit[off_kv:off_g], wit[off_g:off_uv]
    w_uv, w_gate = wit[off_uv:off_gate], wit[off_gate:]
    g_rows = 16
    w_gt = jnp.pad(w_g.reshape(n_kv, 3 * n_rep, d), ((0, 0), (0, g_rows - 3 * n_rep), (0, 0))).reshape(n_kv * g_rows, d)
    w_dec = jnp.concatenate([w_qt, w_kvt, jnp.pad(w_g, ((0, LANES - 3 * n_heads), (0, 0)))], axis=0)
    nm, nf, nfin, gn = norm_mix[0][None], norm_ffn[0][None], norm_final[None], gmlp_norm[0][None]
    pe_t, w1b, b1t, w2b = _compress_params(cmp_pe[0], cmp_w1[0], cmp_b1[0], cmp_w2[0], n_kv)
    wpa, wpb, wout = w_proj_a[0].astype(BF16), w_proj_b[0].astype(BF16), w_out[0].astype(BF16)
    wup, wdown = w_up[0].astype(BF16), w_down[0].astype(BF16)
    cw, cb = conv_w[0], conv_b[0][None]
    ws, bs = gmlp_ws[0], gmlp_bs[0]

    cmp_t, slc_t, win_t, qt, gt = _front(x_prompt, nm, w_kvt, w_qt, w_gt, kv_cols=kv_cols, scale=scale, tm=512)
    kcv = _compress([cmp_t], pe_t, w1b, b1t, w2b, n_seq=bp, steps=1, rows=t // CMP_STRIDE)
    o_nsa = _attention(qt, gt, kcv, slc_t, win_t, n_rep=n_rep, hd=hd)
    x1, v_p = _mix(x_prompt, o_nsa, nm, w_uv, w_gate, gn, ws, bs.T, wpa, wpb, wout,
                   tm=256, chunk=chunk, single_pos=False)
    y_p, conv_p = _ffn(x1, x1, nf, wup, cw, cb, wdown, nfin, tm=256, f_tile=d_ff // 2, single_pos=False)

    xs = x_sample.reshape(bd, d)
    q_s, kv_s, g_s = _front_dec(xs, nm, w_dec, q_cols=q_cols, kv_cols3=3 * kv_cols, scale=scale)
    kv_cmp_s, kv_slc_s, kv_win_s = kv_s[:, 0:kv_cols], kv_s[:, kv_cols:2 * kv_cols], kv_s[:, 2 * kv_cols:]
    def positions_last(c):
        return jnp.transpose(c, (0, 2, 3, 4, 1)).reshape(c.shape[0], kv_cols, c.shape[1])

    cmp_pages, slc_pages, win_rows = positions_last(cache_cmp[0]), positions_last(cache_slc[0]), positions_last(cache_win[0])
    pages_per_step = 32
    kcv_s = _compress([cmp_pages] * pages_per_step, pe_t, w1b, b1t, w2b, n_seq=bd,
                      steps=n_pages // pages_per_step, rows=pages_per_step * page // CMP_STRIDE,
                      page_table=page_table, pages_per_step=pages_per_step)
    q3 = q_s.reshape(bd, n_heads, hd)
    n_blocks_s = -(-(past_len + tn) // SEL_BLOCK)
    o_cmp_s, idx_s = _cmp_select_dec(q3, kcv_s, hd=hd, n_rep=n_rep, q_pos=past_len, n_blocks=n_blocks_s)
    sel = idx_s[:, 0:n_kv, 0:N_SEL].reshape(bd, n_kv * N_SEL)
    o_nsa_s = _attend_dec(page_table, sel, slc_pages, q3, kv_slc_s[:, None, :], kv_win_s[:, None, :], win_rows,
                          g_s[:, 0:3 * n_heads].reshape(bd, n_heads, 3), o_cmp_s,
                          hd=hd, n_rep=n_rep, n_sel=N_SEL, n_past_blocks=past_len // SEL_BLOCK)
    gw = width // GMLP_GROUPS
    ws0 = jnp.repeat(ws[:, 0, 0], gw)[None]
    bs0 = jnp.repeat(bs[:, 0], gw)[None]
    x1_s, v_s = _mix(xs[None], o_nsa_s.reshape(1, bd, q_cols), nm, w_uv, w_gate, gn, ws0, bs0, wpa, wpb, wout,
                     tm=bd, chunk=chunk, single_pos=True)
    prev_s = jnp.swapaxes(state_conv[0], 0, 1)
    y_s, a_s = _ffn(x1_s, prev_s, nf, wup, cw, cb, wdown, nfin, tm=bd, f_tile=d_ff // 2, single_pos=True)

    def rows6(a_t):
        n, _, npos = a_t.shape
        return jnp.transpose(a_t.reshape(n, 2, n_kv, hd, npos), (0, 4, 1, 2, 3))[None]

    win_keep = min(WINDOW, t)
    win_keep_s = min(WINDOW, cache_win.shape[2] + tn)
    win_s = jnp.concatenate([win_rows, kv_win_s[:, :, None]], axis=2)[:, :, cache_win.shape[2] + tn - win_keep_s:]
    conv_s = jnp.concatenate([state_conv[0][:, 1:], a_s[0][:, None, :]], axis=1)
    return (y_p, y_s.reshape(bd, tn, d),
            rows6(cmp_t), rows6(slc_t), rows6(win_t[:, :, t - win_keep:]),
            v_p[None], conv_p[None],
            rows6(kv_cmp_s[:, :, None]), rows6(kv_slc_s[:, :, None]), rows6(win_s),
            v_s.reshape(1, bd, tn, width), conv_s[None])
```

```python
import functools

import jax
import jax.numpy as jnp
from jax import lax
from jax.experimental import pallas as pl
from jax.experimental.pallas import tpu as pltpu

F32 = jnp.float32
BF16 = jnp.bfloat16
I32 = jnp.int32

CMP_STRIDE = 16
SEL_BLOCK = 64
N_SEL = 16
N_LOCAL_SEL = 2
WINDOW = 512
Q_BLOCK = 128
SLC_TILES = 4
SLC_PART = 1
V_PAD = 16
LOG2_E = 1.4426950408889634
GMLP_GROUPS = 4
EPS = 1e-6
NEG = -1e30
BELOW_NEG = -3e38
SEL_BONUS = 1e6

V7X_VMEM_BYTES = 64 * 1024 * 1024
VMEM_REQUEST_BYTES = 56 * 1024 * 1024
LANES = 128


def _cparams(n_grid):
    return pltpu.CompilerParams(
        dimension_semantics=("arbitrary",) * n_grid, vmem_limit_bytes=VMEM_REQUEST_BYTES)


def _rmsnorm(x, g):
    ms = jnp.mean(x * x, axis=-1, keepdims=True)
    return x * lax.rsqrt(ms + EPS) * g


def _dot(a, b):
    return jnp.dot(a, b, preferred_element_type=F32)


def _dot_nt(a, b):
    return lax.dot_general(a, b, (((1,), (1,)), ((), ())), preferred_element_type=F32)


def _dot_exact(a, b):
    return jnp.dot(a, b, precision=lax.Precision.HIGHEST, preferred_element_type=F32)


def _shr(x, n):
    assert n & (n - 1) == 0
    return x >> (n.bit_length() - 1)


def _const_spec(shape):
    nd = len(shape)
    return pl.BlockSpec(shape, lambda *_: (0,) * nd)


def _front_kernel(x_ref, nw_ref, wkvt_ref, wqt_ref, wgt_ref, cmp_ref, slc_ref, win_ref, qt_ref, gt_ref,
                  *, kv_cols, scale):
    h = _rmsnorm(x_ref[0], nw_ref[...]).astype(BF16)
    kvt = _dot_nt(wkvt_ref[...], h)
    cmp_ref[0] = kvt[0:kv_cols]
    slc_ref[0] = kvt[kv_cols:2 * kv_cols]
    win_ref[0] = kvt[2 * kv_cols:3 * kv_cols]
    qt_ref[0] = (_dot_nt(wqt_ref[...], h) * scale).astype(BF16)
    gt_ref[0] = jax.nn.sigmoid(_dot_nt(wgt_ref[...], h))


def _front(x, nw, wkvt, wqt, wgt, *, kv_cols, scale, tm):
    b, t, d = x.shape
    q_cols, g_rows = wqt.shape[0], wgt.shape[0]
    kv_shape = jax.ShapeDtypeStruct((b, kv_cols, t), F32)
    kv_spec = pl.BlockSpec((1, kv_cols, tm), lambda i, j: (i, 0, j))
    return pl.pallas_call(
        functools.partial(_front_kernel, kv_cols=kv_cols, scale=scale),
        grid=(b, t // tm),
        in_specs=[pl.BlockSpec((1, tm, d), lambda i, j: (i, j, 0)), _const_spec(nw.shape),
                  _const_spec(wkvt.shape), _const_spec(wqt.shape), _const_spec(wgt.shape)],
        out_specs=[kv_spec, kv_spec, kv_spec,
                   pl.BlockSpec((1, q_cols, tm), lambda i, j: (i, 0, j)),
                   pl.BlockSpec((1, g_rows, tm), lambda i, j: (i, 0, j))],
        out_shape=[kv_shape, kv_shape, kv_shape,
                   jax.ShapeDtypeStruct((b, q_cols, t), BF16),
                   jax.ShapeDtypeStruct((b, g_rows, t), F32)],
        compiler_params=_cparams(2), name="front",
    )(x, nw, wkvt, wqt, wgt)


def _front_dec_kernel(x_ref, nw_ref, w_ref, q_ref, kv_ref, g_ref, *, q_cols, kv_cols3, scale):
    h = _rmsnorm(x_ref[...], nw_ref[...]).astype(BF16)
    z = _dot_nt(h, w_ref[...])
    q_ref[...] = z[:, 0:q_cols] * scale
    kv_ref[...] = z[:, q_cols:q_cols + kv_cols3]
    g_ref[...] = jax.nn.sigmoid(z[:, q_cols + kv_cols3:])


def _front_dec(x, nw, w, *, q_cols, kv_cols3, scale):
    n = x.shape[0]
    g_cols = w.shape[0] - q_cols - kv_cols3
    return pl.pallas_call(
        functools.partial(_front_dec_kernel, q_cols=q_cols, kv_cols3=kv_cols3, scale=scale),
        grid=(1,),
        in_specs=[_const_spec(x.shape), _const_spec(nw.shape), _const_spec(w.shape)],
        out_specs=[_const_spec((n, q_cols)), _const_spec((n, kv_cols3)), _const_spec((n, g_cols))],
        out_shape=[jax.ShapeDtypeStruct((n, q_cols), F32), jax.ShapeDtypeStruct((n, kv_cols3), F32),
                   jax.ShapeDtypeStruct((n, g_cols), F32)],
        compiler_params=_cparams(1), name="front_dec",
    )(x, nw, w)


def _compress_kernel(*refs, n_x, n_prefetch, stride, kv_cols, hid2):
    refs = refs[n_prefetch:]
    x_refs = refs[:n_x]
    pe_ref, w1_ref, b1_ref, w2_ref, out_ref, carry_ref, pos_ref = refs[n_x:]
    half = kv_cols // 2
    assert half == LANES

    @pl.when(pl.program_id(1) == 0)
    def _():
        carry_ref[...] = jnp.zeros_like(carry_ref)

    tiles_per_x = x_refs[0].shape[2] // LANES
    for k, r in enumerate(x_refs):
        for c in range(tiles_per_x):
            p0 = (k * tiles_per_x + c) * LANES
            for kv in range(2):
                pos_ref[kv, p0:p0 + LANES, :] = r[0, kv * half:(kv + 1) * half, c * LANES:(c + 1) * LANES].T

    rows = out_ref.shape[1]
    row = lax.broadcasted_iota(I32, (rows, hid2), 0)
    for kv in range(2):
        xkv = jnp.concatenate(
            [pos_ref[kv, pl.ds(s, rows, stride=stride), :] for s in range(stride)],
            axis=1)
        parts = []
        for r in range(2):
            a = (xkv + pe_ref[kv, r]).astype(BF16)
            parts.append(_dot(a, w1_ref[kv, :, r * hid2:(r + 1) * hid2]))
        prev = carry_ref[kv, 0:1, :]
        shifted = jnp.where(row == 0, prev, pltpu.roll(parts[0], 1, 0))
        carry_ref[kv, 0:1, :] = parts[0][rows - 1:rows, :]
        hid = b1_ref[kv] + shifted + parts[1]
        out_ref[0, :, kv * half:(kv + 1) * half] = _dot(jax.nn.gelu(hid).astype(BF16), w2_ref[kv])


def _compress(x_list, pe, w1, b1, w2, *, n_seq, steps, rows, page_table=None, pages_per_step=None):
    stride, kv_cols = CMP_STRIDE, w2.shape[2] * 2
    hid2 = b1.shape[2]
    n_x = len(x_list)
    kern = functools.partial(_compress_kernel, n_x=n_x, n_prefetch=0 if page_table is None else 1,
                             stride=stride, kv_cols=kv_cols, hid2=hid2)
    out_shape = jax.ShapeDtypeStruct((n_seq, steps * rows, kv_cols), F32)
    scratch = [pltpu.VMEM((2, 8, hid2), F32), pltpu.VMEM((2, rows * stride, kv_cols // 2), F32)]
    if page_table is None:
        x_specs = [pl.BlockSpec((1, kv_cols, rows * stride), lambda i, j: (i, 0, j))]
        w_specs = [_const_spec(a.shape) for a in (pe, w1, b1, w2)]
        return pl.pallas_call(
            kern, grid=(n_seq, steps), in_specs=x_specs + w_specs,
            out_specs=pl.BlockSpec((1, rows, kv_cols), lambda i, j: (i, j, 0)),
            out_shape=out_shape, scratch_shapes=scratch, compiler_params=_cparams(2), name="compress",
        )(*x_list, pe, w1, b1, w2)
    page = rows * stride // pages_per_step
    x_specs = [pl.BlockSpec((1, kv_cols, page),
                            functools.partial(lambda i, j, pt, k: (pt[i, j * pages_per_step + k], 0, 0), k=k))
               for k in range(n_x)]
    w_specs = [pl.BlockSpec(a.shape, functools.partial(lambda i, j, pt, nd: (0,) * nd, nd=a.ndim))
               for a in (pe, w1, b1, w2)]
    return pl.pallas_call(
        kern,
        grid_spec=pltpu.PrefetchScalarGridSpec(
            num_scalar_prefetch=1, grid=(n_seq, steps), in_specs=x_specs + w_specs,
            out_specs=pl.BlockSpec((1, rows, kv_cols), lambda i, j, pt: (i, j, 0)),
            scratch_shapes=scratch),
        out_shape=out_shape, compiler_params=_cparams(2), name="compress_paged",
    )(page_table, *x_list, pe, w1, b1, w2)


def _block_scores(imp, blk, q_pos, n_blocks):
    cur = _shr(q_pos, SEL_BLOCK)
    valid = (blk * SEL_BLOCK <= q_pos) & (blk < n_blocks)
    forced = (blk == 0) | ((blk <= cur) & (blk > cur - N_LOCAL_SEL))
    score = jnp.where(valid, imp + jnp.where(forced, SEL_BONUS, 0.0), NEG)
    return jnp.where(blk < n_blocks, score, BELOW_NEG)


def _attn_kernel(qt_ref, gt_ref, kcv_ref, slc_ref, win_ref, o_ref,
                 kaug_ref, vts_ref, kwin_ref, vtw_ref, kc_ref, vct_ref, *, n_kv, n_rep, hd, n_blocks):
    i = pl.program_id(1)
    qb = Q_BLOCK
    n_tiles = slc_ref.shape[2] // qb
    n_cmp_rows = kcv_ref.shape[1]
    nq = n_rep * qb
    kd = n_kv * hd
    g_rows = gt_ref.shape[1] // n_kv
    assert kd == LANES and n_blocks == hd, "the selection bias rows ride in the key one-hot lanes"

    @pl.when(i == 0)
    def _():
        lane = lax.broadcasted_iota(I32, (qb, kd), 1)
        krow = lax.broadcasted_iota(I32, (qb, kd), 0)

        def group_lanes(x, gg):
            return x if gg == 0 else pltpu.roll(x, kd - gg * hd, 1)

        ones_row = (lax.broadcasted_iota(I32, (V_PAD, qb), 0) == 0).astype(BF16)
        for c in range(n_tiles):
            cols = slice(c * qb, (c + 1) * qb)
            onehot = (lane - hd == _shr(c * qb + krow, SEL_BLOCK)).astype(F32)
            kt_s, kt_w = slc_ref[0, 0:kd, cols].T, win_ref[0, 0:kd, cols].T
            for gg in range(n_kv):
                v_rows = slice(kd + gg * hd, kd + (gg + 1) * hd)
                kaug_ref[gg, cols, :] = jnp.where(lane < hd, group_lanes(kt_s, gg), onehot).astype(BF16)
                vts_ref[gg, c, 0:hd, :] = slc_ref[0, v_rows, cols].astype(BF16)
                vts_ref[gg, c, hd:, :] = ones_row
                kwin_ref[gg, cols, :] = group_lanes(kt_w, gg)[:, 0:hd].astype(BF16)
                vtw_ref[gg, c, 0:hd, :] = win_ref[0, v_rows, cols].astype(BF16)
                vtw_ref[gg, c, hd:, :] = ones_row
        for c in range(n_cmp_rows // qb):
            rows = slice(c * qb, (c + 1) * qb)
            blk = kcv_ref[0, rows, :]
            vt = blk[:, kd:2 * kd].T
            for gg in range(n_kv):
                kc_ref[gg, rows, :] = group_lanes(blk[:, 0:kd], gg)[:, 0:hd].astype(BF16)
                vct_ref[gg, :, rows] = vt[gg * hd:(gg + 1) * hd, :].astype(BF16)

    q_pos = i * qb + (lax.broadcasted_iota(I32, (1, nq), 1) & (qb - 1))

    groups = range(n_kv)
    heads = range(n_rep)

    def q_minus_k(n_keys):
        return (lax.broadcasted_iota(I32, (n_keys, nq), 1) & (qb - 1)) - lax.broadcasted_iota(I32, (n_keys, nq), 0)

    def softmax_step(m, sc):
        m_new = jnp.maximum(m, jnp.max(sc, axis=0, keepdims=True))
        return m_new, jnp.exp2(m - m_new), jnp.exp2(sc - m_new).astype(BF16)

    def weighted_values(pb, vt_ref, gg, tile0):
        out = _dot(vt_ref[gg, tile0], pb[0:qb, :])
        for u in range(1, pb.shape[0] // qb):
            out = out + _dot(vt_ref[gg, tile0 + u], pb[u * qb:(u + 1) * qb, :])
        return out

    def normalised(acc):
        return acc[0:hd, :] * (1.0 / acc[hd:hd + 1, :])

    qts = [qt_ref[0, gg * n_rep * hd:(gg + 1) * n_rep * hd, :] for gg in groups]
    qcats = [jnp.concatenate([qts[gg][h * hd:(h + 1) * hd, :] for h in heads], axis=1) for gg in groups]

    def attend(k_ref, v_ref, qs, tile0, n_t, state, mask_of):
        parts = [(u, min(SLC_PART, n_t - u)) for u in range(0, n_t, SLC_PART)]
        scores = [[_dot(k_ref[gg, pl.ds(pl.multiple_of((tile0 + u) * qb, qb), n * qb), :], qs[gg]) for gg in groups]
                  for u, n in parts]
        maxes, accs = list(state[0]), list(state[1])
        for (u, n), sc_part in zip(parts, scores):
            mask = mask_of(u, n * qb)
            for gg in groups:
                sc = sc_part[gg] if mask is None else jnp.where(mask, sc_part[gg], NEG)
                maxes[gg], alpha, pb = softmax_step(maxes[gg], sc)
                accs[gg] = alpha * accs[gg] + weighted_values(pb, v_ref, gg, tile0 + u)
        return tuple(maxes), tuple(accs)

    state0 = ((jnp.full((1, nq), NEG, F32),) * n_kv, (jnp.zeros((hd + V_PAD, nq), F32),) * n_kv)

    s_cmp = [_dot(kc_ref[gg], qcats[gg]) for gg in groups]
    n_band = WINDOW // qb
    first_t = jnp.maximum(i - n_band, 0)

    def in_window(u, n_keys):
        dlt = q_minus_k(n_keys) + (i - first_t - u) * qb
        return (dlt >= 0) & (dlt <= WINDOW)

    _, accs = attend(kwin_ref, vtw_ref, qcats, first_t, n_band + 1, state0, in_window)
    o_win = [normalised(accs[gg]) for gg in groups]

    m_idx = lax.broadcasted_iota(I32, (n_cmp_rows, nq), 0)
    vis = (m_idx >= 1) & ((m_idx - 1) * CMP_STRIDE + 2 * CMP_STRIDE - 1 <= q_pos)
    ratio = SEL_BLOCK // CMP_STRIDE
    pj = lax.broadcasted_iota(I32, (n_blocks, n_cmp_rows), 0)
    pm = lax.broadcasted_iota(I32, (n_blocks, n_cmp_rows), 1)
    pool = ((pm >= 1) & (pm >= ratio * pj) & (pm <= ratio * pj + ratio)).astype(F32)
    o_cmp, imp = [], []
    for gg in groups:
        s = jnp.where(vis, s_cmp[gg], NEG)
        e = jnp.where(vis, jnp.exp2(s - jnp.max(s, axis=0, keepdims=True)), 0.0)
        den = jnp.sum(e, axis=0, keepdims=True)
        p = e * (1.0 / jnp.where(den > 0, den, 1.0))
        o_cmp.append(_dot(vct_ref[gg], p.astype(BF16)))
        p_grp = p[:, 0:qb]
        for h in range(1, n_rep):
            p_grp = p_grp + p[:, h * qb:(h + 1) * qb]
        imp.append(_dot_exact(pool, p_grp))

    blk_id = lax.broadcasted_iota(I32, (n_blocks, qb), 0)
    blk_f = blk_id.astype(F32)
    qaug = []
    for gg in groups:
        score = _block_scores(imp[gg], blk_id, q_pos[:, 0:qb], n_blocks)
        work, sel = score, jnp.zeros((n_blocks, qb), F32)
        for _ in range(min(N_SEL, n_blocks)):
            mx = jnp.max(work, axis=0, keepdims=True)
            first = jnp.min(jnp.where(work == mx, blk_f, float(n_blocks)), axis=0, keepdims=True)
            pick = blk_f == first
            sel = jnp.where(pick, 1.0, sel)
            work = jnp.where(pick, BELOW_NEG, work)
        bias = jnp.where((sel > 0) & (score > NEG / 2), 0.0, NEG).astype(BF16)
        qaug.append(jnp.concatenate(
            [jnp.concatenate([qts[gg][h * hd:(h + 1) * hd, :], bias], axis=0) for h in heads], axis=1))

    assert n_tiles % SLC_TILES == 0
    n_full = _shr(i, SLC_TILES)
    state = lax.fori_loop(
        0, n_full, lambda j, st: attend(kaug_ref, vts_ref, qaug, j * SLC_TILES, SLC_TILES, st, lambda u, n: None), state0)
    _, accs = attend(kaug_ref, vts_ref, qaug, n_full * SLC_TILES, SLC_TILES, state,
                     lambda u, n_keys: q_minus_k(n_keys) + (i - n_full * SLC_TILES - u) * qb >= 0)

    outs = []
    for gg in groups:
        o_slc = normalised(accs[gg])
        for h in heads:
            cols = slice(h * qb, (h + 1) * qb)
            gate = [gt_ref[0, gg * g_rows + 3 * h + br:gg * g_rows + 3 * h + br + 1, :] for br in range(3)]
            outs.append(gate[0] * o_cmp[gg][:, cols] + gate[1] * o_slc[:, cols] + gate[2] * o_win[gg][:, cols])
    o_ref[0] = jnp.concatenate(outs, axis=0).T.astype(BF16)


def _attention(qt, gt, kcv, slc, win, *, n_rep, hd):
    b, q_cols, t = qt.shape
    n_kv = q_cols // (n_rep * hd)
    qb = Q_BLOCK
    n_tiles = t // qb
    n_blocks = t // SEL_BLOCK
    kv_cols = slc.shape[1]
    cmp_rows = kcv.shape[1]
    tile_spec = pl.BlockSpec((1, kv_cols, t), lambda bi, i: (bi, 0, 0))
    return pl.pallas_call(
        functools.partial(_attn_kernel, n_kv=n_kv, n_rep=n_rep, hd=hd, n_blocks=n_blocks),
        grid=(b, n_tiles),
        in_specs=[pl.BlockSpec((1, q_cols, qb), lambda bi, i: (bi, 0, i)),
                  pl.BlockSpec((1, gt.shape[1], qb), lambda bi, i: (bi, 0, i)),
                  pl.BlockSpec((1,) + kcv.shape[1:], lambda bi, i: (bi, 0, 0)),
                  tile_spec, tile_spec],
        out_specs=pl.BlockSpec((1, qb, q_cols), lambda bi, i: (bi, i, 0)),
        out_shape=jax.ShapeDtypeStruct((b, t, q_cols), BF16),
        scratch_shapes=[pltpu.VMEM((n_kv, t, 2 * hd), BF16), pltpu.VMEM((n_kv, n_tiles, hd + V_PAD, qb), BF16),
                        pltpu.VMEM((n_kv, t, hd), BF16), pltpu.VMEM((n_kv, n_tiles, hd + V_PAD, qb), BF16),
                        pltpu.VMEM((n_kv, cmp_rows, hd), BF16), pltpu.VMEM((n_kv, hd, cmp_rows), BF16)],
        compiler_params=_cparams(2), name="attention",
    )(qt, gt, kcv, slc, win)


def _spread_q(q, hd, n_rep):
    n_heads = q.shape[0]
    d = lax.broadcasted_iota(I32, (hd, 4 * hd), 0)
    c = lax.broadcasted_iota(I32, (hd, 4 * hd), 1)
    qb16 = q.astype(BF16)
    row = lax.broadcasted_iota(I32, (n_heads, 4 * hd), 0)
    out = jnp.zeros((n_heads, 4 * hd), F32)
    for gg in range(n_heads // n_rep):
        placed = _dot(qb16, (c == d + gg * hd).astype(BF16))
        out = jnp.where(_shr(row, n_rep) == gg, placed, out)
    return out.astype(BF16)


def _masked_softmax_rows(s, mask):
    s = jnp.where(mask, s, NEG)
    e = jnp.where(mask, jnp.exp(s - jnp.max(s, axis=-1, keepdims=True)), 0.0)
    den = jnp.sum(e, axis=-1, keepdims=True)
    return e * (1.0 / jnp.where(den > 0, den, 1.0))


def _group_value_lanes(o_full, hd, n_rep):
    row = lax.broadcasted_iota(I32, (o_full.shape[0], hd), 0)
    out = o_full[:, 2 * hd:3 * hd]
    for gg in range(1, o_full.shape[0] // n_rep):
        out = jnp.where(_shr(row, n_rep) == gg, o_full[:, (2 + gg) * hd:(3 + gg) * hd], out)
    return out


def _cmp_select_dec_kernel(q_ref, kcv_ref, o_ref, idx_ref, *, hd, n_rep, q_pos, n_blocks, blk_lanes):
    n_heads = q_ref.shape[1]
    n_rows = kcv_ref.shape[1]
    q2 = _spread_q(q_ref[0], hd, n_rep)
    kcv = kcv_ref[0].astype(BF16)
    m_idx = lax.broadcasted_iota(I32, (n_heads, n_rows), 1)
    vis = (m_idx >= 1) & ((m_idx - 1) * CMP_STRIDE + 2 * CMP_STRIDE - 1 <= q_pos)
    p = _masked_softmax_rows(_dot_nt(q2, kcv), vis)
    o_ref[0] = _group_value_lanes(_dot(p.astype(BF16), kcv), hd, n_rep)

    row = lax.broadcasted_iota(I32, (n_heads, n_rows), 0)
    grp = jnp.zeros((n_heads, n_rows), F32)
    for gg in range(n_heads // n_rep):
        tot = jnp.sum(jnp.where(_shr(row, n_rep) == gg, p, 0.0), axis=0, keepdims=True)
        grp = jnp.where(row == gg, tot, grp)
    ratio = SEL_BLOCK // CMP_STRIDE
    pm = lax.broadcasted_iota(I32, (n_rows, blk_lanes), 0)
    pj = lax.broadcasted_iota(I32, (n_rows, blk_lanes), 1)
    pool = ((pm >= 1) & (pm >= ratio * pj) & (pm <= ratio * pj + ratio)).astype(F32)
    imp = _dot_exact(grp, pool)
    blk_id = lax.broadcasted_iota(I32, (n_heads, blk_lanes), 1)
    work = _block_scores(imp, blk_id, q_pos, n_blocks)
    out_lane = lax.broadcasted_iota(I32, (n_heads, LANES), 1)
    out = jnp.full((n_heads, LANES), -1, I32)
    blk_f = blk_id.astype(F32)
    for it in range(min(N_SEL, n_blocks)):
        mx = jnp.max(work, axis=-1, keepdims=True)
        first = jnp.min(jnp.where(work == mx, blk_f, float(blk_lanes)), axis=-1, keepdims=True)
        out = jnp.where(out_lane == it, jnp.where(mx > NEG / 2, first.astype(I32), -1), out)
        work = jnp.where(blk_f == first, BELOW_NEG, work)
    idx_ref[0] = out


def _cmp_select_dec(q3, kcv, *, hd, n_rep, q_pos, n_blocks):
    n, n_heads, _ = q3.shape
    blk_lanes = -(-n_blocks // LANES) * LANES
    return pl.pallas_call(
        functools.partial(_cmp_select_dec_kernel, hd=hd, n_rep=n_rep, q_pos=q_pos, n_blocks=n_blocks,
                          blk_lanes=blk_lanes),
        grid=(n,),
        in_specs=[pl.BlockSpec((1, n_heads, hd), lambda i: (i, 0, 0)),
                  pl.BlockSpec((1,) + kcv.shape[1:], lambda i: (i, 0, 0))],
        out_specs=[pl.BlockSpec((1, n_heads, hd), lambda i: (i, 0, 0)),
                   pl.BlockSpec((1, n_heads, LANES), lambda i: (i, 0, 0))],
        out_shape=[jax.ShapeDtypeStruct((n, n_heads, hd), F32), jax.ShapeDtypeStruct((n, n_heads, LANES), I32)],
        compiler_params=_cparams(1), name="cmp_select_dec",
    )(q3, kcv)


def _attend_dec_kernel(pt_ref, sel_ref, *refs, hd, n_rep, n_sel, n_past_blocks):
    n_kv = (len(refs) - 7) // n_sel
    blk_refs = refs[:n_kv * n_sel]
    q_ref, new_slc_ref, new_win_ref, cwin_ref, g_ref, ocmp_ref, o_ref = refs[n_kv * n_sel:]
    b = pl.program_id(0)
    n_heads = q_ref.shape[1]
    q2 = _spread_q(q_ref[0], hd, n_rep)
    q2f = q2.astype(F32)
    head_grp = _shr(lax.broadcasted_iota(I32, (n_heads, 1), 0), n_rep)

    def with_new_key(s, mask, keys_t, new_row, new_ok):
        nr = new_row.astype(BF16).astype(F32)
        s_new = jnp.sum(q2f * nr, axis=-1, keepdims=True)
        s = jnp.where(mask, s, NEG)
        s_new = jnp.where(new_ok, s_new, NEG)
        m = jnp.maximum(jnp.max(s, axis=-1, keepdims=True), s_new)
        e = jnp.where(mask, jnp.exp(s - m), 0.0)
        e_new = jnp.where(new_ok, jnp.exp(s_new - m), 0.0)
        den = jnp.sum(e, axis=-1, keepdims=True) + e_new
        inv = 1.0 / jnp.where(den > 0, den, 1.0)
        return _dot_nt((e * inv).astype(BF16), keys_t) + (e_new * inv) * nr

    page = blk_refs[0].shape[2]
    per = page // SEL_BLOCK
    keys_t = jnp.concatenate([r[0] for r in blk_refs], axis=1).astype(BF16)
    n_keys = keys_t.shape[1]
    col = lax.broadcasted_iota(I32, (1, n_keys), 1)
    col_slot = _shr(col, page)
    col_blk = _shr(col & (page - 1), SEL_BLOCK)
    slot_ok = jnp.zeros((1, n_keys), I32)
    new_ok = jnp.zeros((n_heads, 1), I32)
    for gg in range(n_kv):
        for k in range(n_sel):
            idx = sel_ref[b, gg * n_sel + k]
            cached = ((idx >= 0) & (idx < n_past_blocks)).astype(I32)
            in_blk = (col_blk == (idx & (per - 1))).astype(I32) * cached
            slot_ok = jnp.where(col_slot == gg * n_sel + k, in_blk, slot_ok)
            new_ok = jnp.where(head_grp == gg, new_ok | (idx == n_past_blocks).astype(I32), new_ok)
    mask = (slot_ok > 0) & (_shr(col_slot, n_sel) == head_grp)
    o_slc = _group_value_lanes(with_new_key(_dot(q2, keys_t), mask, keys_t, new_slc_ref[0], new_ok > 0), hd, n_rep)

    keys_t = cwin_ref[0].astype(BF16)
    all_ok = jnp.full((n_heads, keys_t.shape[1]), True)
    o_win = _group_value_lanes(
        with_new_key(_dot(q2, keys_t), all_ok, keys_t, new_win_ref[0], jnp.full((n_heads, 1), True)), hd, n_rep)

    gates = g_ref[0]
    o_ref[0] = gates[:, 0:1] * ocmp_ref[0] + gates[:, 1:2] * o_slc + gates[:, 2:3] * o_win


def _attend_dec(page_table, sel, slc_pages, q3, new_slc, new_win, cache_win, gates3, o_cmp, *, hd, n_rep, n_sel,
                n_past_blocks):
    n, n_heads, _ = q3.shape
    n_kv = n_heads // n_rep
    per = n_past_blocks // page_table.shape[1]

    def blk_map(i, pt, sl, slot):
        idx = jnp.clip(sl[i, slot], 0, n_past_blocks - 1)
        return (pt[i, _shr(idx, per)], 0, 0)

    blk_specs = [pl.BlockSpec((1,) + slc_pages.shape[1:], functools.partial(blk_map, slot=s))
                 for s in range(n_kv * n_sel)]

    def row_spec(shape):
        nd = len(shape)
        return pl.BlockSpec((1,) + tuple(shape[1:]), lambda i, pt, sl: (i,) + (0,) * (nd - 1))

    others = (q3, new_slc, new_win, cache_win, gates3, o_cmp)
    return pl.pallas_call(
        functools.partial(_attend_dec_kernel, hd=hd, n_rep=n_rep, n_sel=n_sel, n_past_blocks=n_past_blocks),
        grid_spec=pltpu.PrefetchScalarGridSpec(
            num_scalar_prefetch=2, grid=(n,),
            in_specs=blk_specs + [row_spec(a.shape) for a in others],
            out_specs=row_spec(o_cmp.shape)),
        out_shape=jax.ShapeDtypeStruct(o_cmp.shape, F32),
        compiler_params=_cparams(1), name="attend_dec",
    )(page_table, sel, *([slc_pages] * (n_kv * n_sel)), *others)


def _mix_kernel(x_ref, o_ref, nw_ref, wuv_ref, wgate_ref, gn_ref, ws_ref, bs_ref, wpa_ref, wpb_ref, wout_ref,
                x1_ref, v_ref, *, width, chunk, single_pos):
    x = x_ref[0]
    d = x.shape[-1]
    h = _rmsnorm(x, nw_ref[...]).astype(BF16)
    uv = jax.nn.gelu(_dot_nt(h, wuv_ref[...]))
    u, vn = uv[:, 0:width], _rmsnorm(uv[:, width:2 * width], gn_ref[...])
    gates = jax.nn.sigmoid(_dot_nt(h, wgate_ref[...]))
    gw = width // GMLP_GROUPS
    if single_pos:
        v_ref[0] = vn
        mixed = u * (ws_ref[...] * vn + bs_ref[...])
    else:
        rows = x.shape[0]
        v_ref[0] = vn[rows - chunk:rows, :]
        tri = lax.broadcasted_iota(I32, (chunk, chunk), 0) >= lax.broadcasted_iota(I32, (chunk, chunk), 1)
        vb = vn.astype(BF16)
        pieces = []
        for c in range(rows // chunk):
            zs = []
            for gi in range(GMLP_GROUPS):
                wm = jnp.where(tri, ws_ref[gi], 0.0).astype(BF16)
                zs.append(_dot(wm, vb[c * chunk:(c + 1) * chunk, gi * gw:(gi + 1) * gw]) + bs_ref[:, gi:gi + 1])
            pieces.append(jnp.concatenate(zs, axis=1))
        mixed = u * jnp.concatenate(pieces, axis=0)
    br_a = _dot(o_ref[0].astype(BF16), wpa_ref[...])
    br_b = _dot(mixed.astype(BF16), wpb_ref[...])
    merged = gates[:, 0:d] * br_a + gates[:, d:2 * d] * br_b
    x1_ref[0] = x + _dot(merged.astype(BF16), wout_ref[...])


def _mix(x, o_nsa, nw, wuv, wgate, gn, ws, bs, wpa, wpb, wout, *, tm, chunk, single_pos):
    b, t, d = x.shape
    width = wuv.shape[0] // 2
    v_rows = tm if single_pos else chunk
    weights = (nw, wuv, wgate, gn, ws, bs, wpa, wpb, wout)
    return pl.pallas_call(
        functools.partial(_mix_kernel, width=width, chunk=chunk, single_pos=single_pos),
        grid=(b, t // tm),
        in_specs=[pl.BlockSpec((1, tm, d), lambda i, j: (i, j, 0)),
                  pl.BlockSpec((1, tm, o_nsa.shape[-1]), lambda i, j: (i, j, 0))]
                 + [_const_spec(a.shape) for a in weights],
        out_specs=[pl.BlockSpec((1, tm, d), lambda i, j: (i, j, 0)),
                   pl.BlockSpec((1, v_rows, width), lambda i, j: (i, 0, 0))],
        out_shape=[jax.ShapeDtypeStruct((b, t, d), F32), jax.ShapeDtypeStruct((b, v_rows, width), F32)],
        compiler_params=_cparams(2), name="mix_dec" if single_pos else "mix",
    )(x, o_nsa, *weights)


def _ffn_kernel(x1_ref, prev_ref, nf_ref, wup_ref, cw_ref, cb_ref, wdown_ref, nfin_ref, y_ref, a_ref,
                *, d_ff, f_tile, halo, single_pos):
    x1 = x1_ref[0]
    rows = x1.shape[0]
    if single_pos:
        h = _rmsnorm(x1, nf_ref[...]).astype(BF16)
    else:
        h = _rmsnorm(jnp.concatenate([prev_ref[0], x1], axis=0), nf_ref[...]).astype(BF16)
        ext_row = lax.broadcasted_iota(I32, (rows + halo, f_tile), 0)
        first = pl.program_id(1) == 0
    y = jnp.zeros_like(x1)
    for f0 in range(0, d_ff, f_tile):
        cols = slice(f0, f0 + f_tile)
        a = _dot(h, wup_ref[:, cols])
        bgate = _dot(h, wup_ref[:, d_ff + f0:d_ff + f0 + f_tile])
        if single_pos:
            a_ref[0, :, cols] = a
            c = cb_ref[:, cols] + prev_ref[0, :, cols] * cw_ref[0:1, cols] + prev_ref[1, :, cols] * cw_ref[1:2, cols] \
                + a * cw_ref[2:3, cols]
        else:
            a = jnp.where((ext_row < halo) & first, 0.0, a)
            back2 = pltpu.roll(a, 2, 0)
            a_ref[0, :, cols] = back2[0:2, :]
            c = cb_ref[:, cols] + back2[halo:, :] * cw_ref[0:1, cols] \
                + pltpu.roll(a, 1, 0)[halo:, :] * cw_ref[1:2, cols] + a[halo:, :] * cw_ref[2:3, cols]
            bgate = bgate[halo:, :]
        y = y + _dot((jax.nn.gelu(c) * bgate).astype(BF16), wdown_ref[cols, :])
    y_ref[0] = _rmsnorm(x1 + y, nfin_ref[...])


def _ffn(x1, prev, nf, wup, cw, cb, wdown, nfin, *, tm, f_tile, single_pos):
    b, t, d = x1.shape
    d_ff = wdown.shape[0]
    halo = 8
    weights = (nf, wup, cw, cb, wdown, nfin)
    if single_pos:
        prev_spec = _const_spec(prev.shape)
        a_rows = tm
    else:
        per = tm // halo
        prev_spec = pl.BlockSpec((1, halo, d), lambda i, j: (i, jnp.maximum(j * per - 1, 0), 0))
        a_rows = 2
    return pl.pallas_call(
        functools.partial(_ffn_kernel, d_ff=d_ff, f_tile=f_tile, halo=halo, single_pos=single_pos),
        grid=(b, t // tm),
        in_specs=[pl.BlockSpec((1, tm, d), lambda i, j: (i, j, 0)), prev_spec] + [_const_spec(a.shape) for a in weights],
        out_specs=[pl.BlockSpec((1, tm, d), lambda i, j: (i, j, 0)),
                   pl.BlockSpec((1, a_rows, d_ff), lambda i, j: (i, 0, 0))],
        out_shape=[jax.ShapeDtypeStruct((b, t, d), F32), jax.ShapeDtypeStruct((b, a_rows, d_ff), F32)],
        compiler_params=_cparams(2), name="ffn_dec" if single_pos else "ffn",
    )(x1, prev, *weights)


def _compress_params(pe, w1, b1, w2, n_kv):
    cmp_len, hd = pe.shape[1], pe.shape[2]
    hid = w1.shape[2]
    halves = cmp_len // CMP_STRIDE
    eye = jnp.eye(n_kv, dtype=w1.dtype)
    pe_t = jnp.broadcast_to(pe.reshape(2, halves, CMP_STRIDE, 1, hd), (2, halves, CMP_STRIDE, n_kv, hd))
    pe_t = pe_t.reshape(2, halves, 1, CMP_STRIDE * n_kv * hd)
    w1h = w1.reshape(2, halves, CMP_STRIDE, hd, hid)
    w1b = jnp.einsum('krsdh,gf->ksgdrfh', w1h, eye).reshape(2, CMP_STRIDE * n_kv * hd, halves * n_kv * hid)
    b1t = jnp.tile(b1, (1, n_kv)).reshape(2, 1, n_kv * hid)
    w2b = jnp.einsum('khd,gf->kghfd', w2, eye).reshape(2, n_kv * hid, n_kv * hd)
    return pe_t, w1b.astype(BF16), b1t, w2b.astype(BF16)


def kernel(x_prompt, x_sample, cache_cmp, cache_slc, cache_win, state_conv, page_table, norm_mix, w_in, cmp_pe,
           cmp_w1, cmp_b1, cmp_w2, gmlp_norm, gmlp_ws, gmlp_bs, w_proj_a, w_proj_b, w_out, norm_ffn, w_up, conv_w,
           conv_b, w_down, norm_final):
    depth = w_in.shape[0]
    assert depth == 1, "single-layer step"
    bp, t, d = x_prompt.shape
    bd, tn, _ = x_sample.shape
    assert tn == 1
    n_kv, hd = cache_cmp.shape[4], cache_cmp.shape[5]
    page = cache_cmp.shape[2]
    q_cols = w_proj_a.shape[1]
    n_heads = q_cols // hd
    n_rep = n_heads // n_kv
    kv_cols = 2 * n_kv * hd
    width = w_proj_b.shape[1]
    chunk = gmlp_ws.shape[-1]
    d_ff = w_down.shape[1]
    n_pages = page_table.shape[1]
    past_len = n_pages * page
    scale = hd ** -0.5
    assert conv_w.shape[1] == 3 and cache_win.shape[2] <= WINDOW and past_len % SEL_BLOCK == 0

    wit = w_in[0].T.astype(BF16)
    off_kv, off_g = q_cols, q_cols + 3 * kv_cols
    off_uv = off_g + 3 * n_heads
    off_gate = off_uv + 2 * width
    w_qt, w_kvt, w_g = wit[0:off_kv], wit[off_kv:off_g], wit[off_g:off_uv]
    w_uv, w_gate = wit[off_uv:off_gate], wit[off_gate:]
    g_rows = 16
    w_gt = jnp.pad(w_g.reshape(n_kv, 3 * n_rep, d), ((0, 0), (0, g_rows - 3 * n_rep), (0, 0))).reshape(n_kv * g_rows, d)
    w_dec = jnp.concatenate([w_qt, w_kvt, jnp.pad(w_g, ((0, LANES - 3 * n_heads), (0, 0)))], axis=0)
    nm, nf, nfin, gn = norm_mix[0][None], norm_ffn[0][None], norm_final[None], gmlp_norm[0][None]
    pe_t, w1b, b1t, w2b = _compress_params(cmp_pe[0], cmp_w1[0], cmp_b1[0], cmp_w2[0], n_kv)
    wpa, wpb, wout = w_proj_a[0].astype(BF16), w_proj_b[0].astype(BF16), w_out[0].astype(BF16)
    wup, wdown = w_up[0].astype(BF16), w_down[0].astype(BF16)
    cw, cb = conv_w[0], conv_b[0][None]
    ws, bs = gmlp_ws[0], gmlp_bs[0]

    cmp_t, slc_t, win_t, qt, gt = _front(x_prompt, nm, w_kvt, w_qt, w_gt, kv_cols=kv_cols, scale=scale * LOG2_E,
                                         tm=512)
    kcv = _compress([cmp_t], pe_t, w1b, b1t, w2b, n_seq=bp, steps=1, rows=t // CMP_STRIDE)
    o_nsa = _attention(qt, gt, kcv, slc_t, win_t, n_rep=n_rep, hd=hd)
    x1, v_p = _mix(x_prompt, o_nsa, nm, w_uv, w_gate, gn, ws, bs.T, wpa, wpb, wout,
                   tm=256, chunk=chunk, single_pos=False)
    y_p, conv_p = _ffn(x1, x1, nf, wup, cw, cb, wdown, nfin, tm=256, f_tile=d_ff // 2, single_pos=False)

    xs = x_sample.reshape(bd, d)
    q_s, kv_s, g_s = _front_dec(xs, nm, w_dec, q_cols=q_cols, kv_cols3=3 * kv_cols, scale=scale)
    kv_cmp_s, kv_slc_s, kv_win_s = kv_s[:, 0:kv_cols], kv_s[:, kv_cols:2 * kv_cols], kv_s[:, 2 * kv_cols:]
    def positions_last(c):
        return jnp.transpose(c, (0, 2, 3, 4, 1)).reshape(c.shape[0], kv_cols, c.shape[1])

    cmp_pages, slc_pages, win_rows = positions_last(cache_cmp[0]), positions_last(cache_slc[0]), positions_last(cache_win[0])
    pages_per_step = 32
    kcv_s = _compress([cmp_pages] * pages_per_step, pe_t, w1b, b1t, w2b, n_seq=bd,
                      steps=n_pages // pages_per_step, rows=pages_per_step * page // CMP_STRIDE,
                      page_table=page_table, pages_per_step=pages_per_step)
    q3 = q_s.reshape(bd, n_heads, hd)
    n_blocks_s = -(-(past_len + tn) // SEL_BLOCK)
    o_cmp_s, idx_s = _cmp_select_dec(q3, kcv_s, hd=hd, n_rep=n_rep, q_pos=past_len, n_blocks=n_blocks_s)
    sel = idx_s[:, 0:n_kv, 0:N_SEL].reshape(bd, n_kv * N_SEL)
    o_nsa_s = _attend_dec(page_table, sel, slc_pages, q3, kv_slc_s[:, None, :], kv_win_s[:, None, :], win_rows,
                          g_s[:, 0:3 * n_heads].reshape(bd, n_heads, 3), o_cmp_s,
                          hd=hd, n_rep=n_rep, n_sel=N_SEL, n_past_blocks=past_len // SEL_BLOCK)
    gw = width // GMLP_GROUPS
    ws0 = jnp.repeat(ws[:, 0, 0], gw)[None]
    bs0 = jnp.repeat(bs[:, 0], gw)[None]
    x1_s, v_s = _mix(xs[None], o_nsa_s.reshape(1, bd, q_cols), nm, w_uv, w_gate, gn, ws0, bs0, wpa, wpb, wout,
                     tm=bd, chunk=chunk, single_pos=True)
    prev_s = jnp.swapaxes(state_conv[0], 0, 1)
    y_s, a_s = _ffn(x1_s, prev_s, nf, wup, cw, cb, wdown, nfin, tm=bd, f_tile=d_ff // 2, single_pos=True)

    def rows6(a_t):
        n, _, npos = a_t.shape
        return jnp.transpose(a_t.reshape(n, 2, n_kv, hd, npos), (0, 4, 1, 2, 3))[None]

    win_keep = min(WINDOW, t)
    win_keep_s = min(WINDOW, cache_win.shape[2] + tn)
    win_s = jnp.concatenate([win_rows, kv_win_s[:, :, None]], axis=2)[:, :, cache_win.shape[2] + tn - win_keep_s:]
    conv_s = jnp.concatenate([state_conv[0][:, 1:], a_s[0][:, None, :]], axis=1)
    return (y_p, y_s.reshape(bd, tn, d),
            rows6(cmp_t), rows6(slc_t), rows6(win_t[:, :, t - win_keep:]),
            v_p[None], conv_p[None],
            rows6(kv_cmp_s[:, :, None]), rows6(kv_slc_s[:, :, None]), rows6(win_s),
            v_s.reshape(1, bd, tn, width), conv_s[None])
```

```python
import functools

import jax
import jax.numpy as jnp
from jax import lax
from jax.experimental import pallas as pl
from jax.experimental.pallas import tpu as pltpu

F32 = jnp.float32
BF16 = jnp.bfloat16
I32 = jnp.int32

CMP_STRIDE = 16
SEG_PITCH = 24
SEL_BLOCK = 64
N_SEL = 16
N_LOCAL_SEL = 2
WINDOW = 512
Q_BLOCK = 128
SLC_TILES = 4
SLC_PART = 1
V_PAD = 16
LOG2_E = 1.4426950408889634
GMLP_GROUPS = 4
EPS = 1e-6
NEG = -1e30
BELOW_NEG = -3e38
SEL_BONUS = 1e6

V7X_VMEM_BYTES = 64 * 1024 * 1024
VMEM_REQUEST_BYTES = 56 * 1024 * 1024
LANES = 128


def _cparams(n_grid):
    return pltpu.CompilerParams(
        dimension_semantics=("arbitrary",) * n_grid, vmem_limit_bytes=VMEM_REQUEST_BYTES)


def _rmsnorm(x, g):
    ms = jnp.mean(x * x, axis=-1, keepdims=True)
    return x * lax.rsqrt(ms + EPS) * g


def _dot(a, b):
    return jnp.dot(a, b, preferred_element_type=F32)


def _dot_nt(a, b):
    return lax.dot_general(a, b, (((1,), (1,)), ((), ())), preferred_element_type=F32)


def _dot_exact(a, b):
    return jnp.dot(a, b, precision=lax.Precision.HIGHEST, preferred_element_type=F32)


def _shr(x, n):
    assert n & (n - 1) == 0
    return x >> (n.bit_length() - 1)


def _const_spec(shape):
    nd = len(shape)
    return pl.BlockSpec(shape, lambda *_: (0,) * nd)


def _front_kernel(x_ref, nw_ref, wkvt_ref, wqt_ref, wgt_ref, cmp_ref, slc_ref, win_ref, qt_ref, gt_ref,
                  *, kv_cols, scale):
    h = _rmsnorm(x_ref[0], nw_ref[...]).astype(BF16)
    kvt = _dot_nt(wkvt_ref[...], h)
    cmp_ref[0] = kvt[0:kv_cols]
    slc_ref[0] = kvt[kv_cols:2 * kv_cols]
    win_ref[0] = kvt[2 * kv_cols:3 * kv_cols]
    qt_ref[0] = (_dot_nt(wqt_ref[...], h) * scale).astype(BF16)
    gt_ref[0] = jax.nn.sigmoid(_dot_nt(wgt_ref[...], h))


def _front(x, nw, wkvt, wqt, wgt, *, kv_cols, scale, tm):
    b, t, d = x.shape
    q_cols, g_rows = wqt.shape[0], wgt.shape[0]
    kv_shape = jax.ShapeDtypeStruct((b, kv_cols, t), F32)
    kv_spec = pl.BlockSpec((1, kv_cols, tm), lambda i, j: (i, 0, j))
    return pl.pallas_call(
        functools.partial(_front_kernel, kv_cols=kv_cols, scale=scale),
        grid=(b, t // tm),
        in_specs=[pl.BlockSpec((1, tm, d), lambda i, j: (i, j, 0)), _const_spec(nw.shape),
                  _const_spec(wkvt.shape), _const_spec(wqt.shape), _const_spec(wgt.shape)],
        out_specs=[kv_spec, kv_spec, kv_spec,
                   pl.BlockSpec((1, q_cols, tm), lambda i, j: (i, 0, j)),
                   pl.BlockSpec((1, g_rows, tm), lambda i, j: (i, 0, j))],
        out_shape=[kv_shape, kv_shape, kv_shape,
                   jax.ShapeDtypeStruct((b, q_cols, t), BF16),
                   jax.ShapeDtypeStruct((b, g_rows, t), F32)],
        compiler_params=_cparams(2), name="front",
    )(x, nw, wkvt, wqt, wgt)


def _front_dec_kernel(x_ref, nw_ref, w_ref, q_ref, kv_ref, g_ref, *, q_cols, kv_cols3, scale):
    h = _rmsnorm(x_ref[...], nw_ref[...]).astype(BF16)
    z = _dot_nt(h, w_ref[...])
    q_ref[...] = z[:, 0:q_cols] * scale
    kv_ref[...] = z[:, q_cols:q_cols + kv_cols3]
    g_ref[...] = jax.nn.sigmoid(z[:, q_cols + kv_cols3:])


def _front_dec(x, nw, w, *, q_cols, kv_cols3, scale):
    n = x.shape[0]
    g_cols = w.shape[0] - q_cols - kv_cols3
    return pl.pallas_call(
        functools.partial(_front_dec_kernel, q_cols=q_cols, kv_cols3=kv_cols3, scale=scale),
        grid=(1,),
        in_specs=[_const_spec(x.shape), _const_spec(nw.shape), _const_spec(w.shape)],
        out_specs=[_const_spec((n, q_cols)), _const_spec((n, kv_cols3)), _const_spec((n, g_cols))],
        out_shape=[jax.ShapeDtypeStruct((n, q_cols), F32), jax.ShapeDtypeStruct((n, kv_cols3), F32),
                   jax.ShapeDtypeStruct((n, g_cols), F32)],
        compiler_params=_cparams(1), name="front_dec",
    )(x, nw, w)


def _compress_kernel(*refs, n_x, n_prefetch, stride, kv_cols, hid2):
    refs = refs[n_prefetch:]
    x_refs = refs[:n_x]
    pe_ref, w1_ref, b1_ref, w2_ref, out_ref, carry_ref, pos_ref = refs[n_x:]
    half = kv_cols // 2
    assert half == LANES

    @pl.when(pl.program_id(1) == 0)
    def _():
        carry_ref[...] = jnp.zeros_like(carry_ref)

    tiles_per_x = x_refs[0].shape[2] // LANES
    seg_per_tile = LANES // stride
    rows = out_ref.shape[1]
    row = lax.broadcasted_iota(I32, (rows, hid2), 0)
    for kv in range(2):
        for k, r in enumerate(x_refs):
            for c in range(tiles_per_x):
                t = r[0, kv * half:(kv + 1) * half, c * LANES:(c + 1) * LANES].T
                for n in range(seg_per_tile):
                    p0 = ((k * tiles_per_x + c) * seg_per_tile + n) * SEG_PITCH
                    pos_ref[kv, p0:p0 + stride, :] = t[n * stride:(n + 1) * stride, :]
        xkv = jnp.concatenate(
            [pos_ref[kv, pl.ds(s, rows, stride=SEG_PITCH), :] for s in range(stride)],
            axis=1)
        parts = []
        for r in range(2):
            a = (xkv + pe_ref[kv, r]).astype(BF16)
            parts.append(_dot(a, w1_ref[kv, :, r * hid2:(r + 1) * hid2]))
        prev = carry_ref[kv, 0:1, :]
        shifted = jnp.where(row == 0, prev, pltpu.roll(parts[0], 1, 0))
        carry_ref[kv, 0:1, :] = parts[0][rows - 1:rows, :]
        hid = b1_ref[kv] + shifted + parts[1]
        out_ref[0, :, kv * half:(kv + 1) * half] = _dot(jax.nn.gelu(hid).astype(BF16), w2_ref[kv])


def _compress(x_list, pe, w1, b1, w2, *, n_seq, steps, rows, page_table=None, pages_per_step=None):
    stride, kv_cols = CMP_STRIDE, w2.shape[2] * 2
    hid2 = b1.shape[2]
    n_x = len(x_list)
    kern = functools.partial(_compress_kernel, n_x=n_x, n_prefetch=0 if page_table is None else 1,
                             stride=stride, kv_cols=kv_cols, hid2=hid2)
    out_shape = jax.ShapeDtypeStruct((n_seq, steps * rows, kv_cols), F32)
    scratch = [pltpu.VMEM((2, 8, hid2), F32), pltpu.VMEM((2, rows * SEG_PITCH, kv_cols // 2), F32)]
    if page_table is None:
        x_specs = [pl.BlockSpec((1, kv_cols, rows * stride), lambda i, j: (i, 0, j))]
        w_specs = [_const_spec(a.shape) for a in (pe, w1, b1, w2)]
        return pl.pallas_call(
            kern, grid=(n_seq, steps), in_specs=x_specs + w_specs,
            out_specs=pl.BlockSpec((1, rows, kv_cols), lambda i, j: (i, j, 0)),
            out_shape=out_shape, scratch_shapes=scratch, compiler_params=_cparams(2), name="compress",
        )(*x_list, pe, w1, b1, w2)
    page = rows * stride // pages_per_step
    x_specs = [pl.BlockSpec((1, kv_cols, page),
                            functools.partial(lambda i, j, pt, k: (pt[i, j * pages_per_step + k], 0, 0), k=k))
               for k in range(n_x)]
    w_specs = [pl.BlockSpec(a.shape, functools.partial(lambda i, j, pt, nd: (0,) * nd, nd=a.ndim))
               for a in (pe, w1, b1, w2)]
    return pl.pallas_call(
        kern,
        grid_spec=pltpu.PrefetchScalarGridSpec(
            num_scalar_prefetch=1, grid=(n_seq, steps), in_specs=x_specs + w_specs,
            out_specs=pl.BlockSpec((1, rows, kv_cols), lambda i, j, pt: (i, j, 0)),
            scratch_shapes=scratch),
        out_shape=out_shape, compiler_params=_cparams(2), name="compress_paged",
    )(page_table, *x_list, pe, w1, b1, w2)


def _block_scores(imp, blk, q_pos, n_blocks):
    cur = _shr(q_pos, SEL_BLOCK)
    valid = (blk * SEL_BLOCK <= q_pos) & (blk < n_blocks)
    forced = (blk == 0) | ((blk <= cur) & (blk > cur - N_LOCAL_SEL))
    score = jnp.where(valid, imp + jnp.where(forced, SEL_BONUS, 0.0), NEG)
    return jnp.where(blk < n_blocks, score, BELOW_NEG)


def _attn_kernel(qt_ref, gt_ref, kcv_ref, slc_ref, win_ref, o_ref,
                 kaug_ref, vts_ref, kwin_ref, vtw_ref, kc_ref, vct_ref, *, n_kv, n_rep, hd, n_blocks):
    i = pl.program_id(1)
    qb = Q_BLOCK
    n_tiles = slc_ref.shape[2] // qb
    n_cmp_rows = kcv_ref.shape[1]
    nq = n_rep * qb
    kd = n_kv * hd
    g_rows = gt_ref.shape[1] // n_kv
    assert kd == LANES and n_blocks == hd, "the selection bias rows ride in the key one-hot lanes"

    @pl.when(i == 0)
    def _():
        lane = lax.broadcasted_iota(I32, (qb, kd), 1)
        krow = lax.broadcasted_iota(I32, (qb, kd), 0)

        def group_lanes(x, gg):
            return x if gg == 0 else pltpu.roll(x, kd - gg * hd, 1)

        ones_row = (lax.broadcasted_iota(I32, (V_PAD, qb), 0) == 0).astype(BF16)
        for c in range(n_tiles):
            cols = slice(c * qb, (c + 1) * qb)
            onehot = (lane - hd == _shr(c * qb + krow, SEL_BLOCK)).astype(F32)
            kt_s, kt_w = slc_ref[0, 0:kd, cols].T, win_ref[0, 0:kd, cols].T
            for gg in range(n_kv):
                v_rows = slice(kd + gg * hd, kd + (gg + 1) * hd)
                kaug_ref[gg, cols, :] = jnp.where(lane < hd, group_lanes(kt_s, gg), onehot).astype(BF16)
                vts_ref[gg, c, 0:hd, :] = slc_ref[0, v_rows, cols].astype(BF16)
                vts_ref[gg, c, hd:, :] = ones_row
                kwin_ref[gg, cols, :] = group_lanes(kt_w, gg)[:, 0:hd].astype(BF16)
                vtw_ref[gg, c, 0:hd, :] = win_ref[0, v_rows, cols].astype(BF16)
                vtw_ref[gg, c, hd:, :] = ones_row
        for c in range(n_cmp_rows // qb):
            rows = slice(c * qb, (c + 1) * qb)
            blk = kcv_ref[0, rows, :]
            vt = blk[:, kd:2 * kd].T
            for gg in range(n_kv):
                kc_ref[gg, rows, :] = group_lanes(blk[:, 0:kd], gg)[:, 0:hd].astype(BF16)
                vct_ref[gg, :, rows] = vt[gg * hd:(gg + 1) * hd, :].astype(BF16)

    q_pos = i * qb + (lax.broadcasted_iota(I32, (1, nq), 1) & (qb - 1))

    groups = range(n_kv)
    heads = range(n_rep)

    def q_minus_k(n_keys):
        return (lax.broadcasted_iota(I32, (n_keys, nq), 1) & (qb - 1)) - lax.broadcasted_iota(I32, (n_keys, nq), 0)

    def softmax_step(m, sc):
        m_new = jnp.maximum(m, jnp.max(sc, axis=0, keepdims=True))
        return m_new, jnp.exp2(m - m_new), jnp.exp2(sc - m_new).astype(BF16)

    def weighted_values(pb, vt_ref, gg, tile0):
        out = _dot(vt_ref[gg, tile0], pb[0:qb, :])
        for u in range(1, pb.shape[0] // qb):
            out = out + _dot(vt_ref[gg, tile0 + u], pb[u * qb:(u + 1) * qb, :])
        return out

    def normalised(acc):
        return acc[0:hd, :] * (1.0 / acc[hd:hd + 1, :])

    qts = [qt_ref[0, gg * n_rep * hd:(gg + 1) * n_rep * hd, :] for gg in groups]
    qcats = [jnp.concatenate([qts[gg][h * hd:(h + 1) * hd, :] for h in heads], axis=1) for gg in groups]

    def attend(k_ref, v_ref, qs, tile0, n_t, state, mask_of):
        parts = [(u, min(SLC_PART, n_t - u)) for u in range(0, n_t, SLC_PART)]
        scores = [[_dot(k_ref[gg, pl.ds(pl.multiple_of((tile0 + u) * qb, qb), n * qb), :], qs[gg]) for gg in groups]
                  for u, n in parts]
        maxes, accs = list(state[0]), list(state[1])
        for (u, n), sc_part in zip(parts, scores):
            mask = mask_of(u, n * qb)
            for gg in groups:
                sc = sc_part[gg] if mask is None else jnp.where(mask, sc_part[gg], NEG)
                maxes[gg], alpha, pb = softmax_step(maxes[gg], sc)
                accs[gg] = alpha * accs[gg] + weighted_values(pb, v_ref, gg, tile0 + u)
        return tuple(maxes), tuple(accs)

    state0 = ((jnp.full((1, nq), NEG, F32),) * n_kv, (jnp.zeros((hd + V_PAD, nq), F32),) * n_kv)

    s_cmp = [_dot(kc_ref[gg], qcats[gg]) for gg in groups]
    m_idx = lax.broadcasted_iota(I32, (n_cmp_rows, nq), 0)
    vis = (m_idx >= 1) & ((m_idx - 1) * CMP_STRIDE + 2 * CMP_STRIDE - 1 <= q_pos)
    ratio = SEL_BLOCK // CMP_STRIDE
    pj = lax.broadcasted_iota(I32, (n_blocks, n_cmp_rows), 0)
    pm = lax.broadcasted_iota(I32, (n_blocks, n_cmp_rows), 1)
    pool = ((pm >= 1) & (pm >= ratio * pj) & (pm <= ratio * pj + ratio)).astype(F32)
    o_cmp, imp = [], []
    for gg in groups:
        s = jnp.where(vis, s_cmp[gg], NEG)
        e = jnp.where(vis, jnp.exp2(s - jnp.max(s, axis=0, keepdims=True)), 0.0)
        den = jnp.sum(e, axis=0, keepdims=True)
        p = e * (1.0 / jnp.where(den > 0, den, 1.0))
        o_cmp.append(_dot(vct_ref[gg], p.astype(BF16)))
        p_grp = p[:, 0:qb]
        for h in range(1, n_rep):
            p_grp = p_grp + p[:, h * qb:(h + 1) * qb]
        imp.append(_dot_exact(pool, p_grp))

    n_band = WINDOW // qb
    first_t = jnp.maximum(i - n_band, 0)

    def in_window(u, n_keys):
        dlt = q_minus_k(n_keys) + (i - first_t - u) * qb
        return (dlt >= 0) & (dlt <= WINDOW)

    _, accs = attend(kwin_ref, vtw_ref, qcats, first_t, n_band + 1, state0, in_window)
    o_win = [normalised(accs[gg]) for gg in groups]

    blk_id = lax.broadcasted_iota(I32, (n_blocks, qb), 0)
    blk_f = blk_id.astype(F32)
    qaug = []
    for gg in groups:
        score = _block_scores(imp[gg], blk_id, q_pos[:, 0:qb], n_blocks)
        work, sel = score, jnp.zeros((n_blocks, qb), F32)
        for _ in range(min(N_SEL, n_blocks)):
            mx = jnp.max(work, axis=0, keepdims=True)
            first = jnp.min(jnp.where(work == mx, blk_f, float(n_blocks)), axis=0, keepdims=True)
            pick = blk_f == first
            sel = jnp.where(pick, 1.0, sel)
            work = jnp.where(pick, BELOW_NEG, work)
        bias = jnp.where((sel > 0) & (score > NEG / 2), 0.0, NEG).astype(BF16)
        qaug.append(jnp.concatenate(
            [jnp.concatenate([qts[gg][h * hd:(h + 1) * hd, :], bias], axis=0) for h in heads], axis=1))

    assert n_tiles % SLC_TILES == 0
    n_full = _shr(i, SLC_TILES)
    state = lax.fori_loop(
        0, n_full, lambda j, st: attend(kaug_ref, vts_ref, qaug, j * SLC_TILES, SLC_TILES, st, lambda u, n: None), state0)
    _, accs = attend(kaug_ref, vts_ref, qaug, n_full * SLC_TILES, SLC_TILES, state,
                     lambda u, n_keys: q_minus_k(n_keys) + (i - n_full * SLC_TILES - u) * qb >= 0)

    outs = []
    for gg in groups:
        o_slc = normalised(accs[gg])
        for h in heads:
            cols = slice(h * qb, (h + 1) * qb)
            gate = [gt_ref[0, gg * g_rows + 3 * h + br:gg * g_rows + 3 * h + br + 1, :] for br in range(3)]
            outs.append(gate[0] * o_cmp[gg][:, cols] + gate[1] * o_slc[:, cols] + gate[2] * o_win[gg][:, cols])
    o_ref[0] = jnp.concatenate(outs, axis=0).T.astype(BF16)


def _attention(qt, gt, kcv, slc, win, *, n_rep, hd):
    b, q_cols, t = qt.shape
    n_kv = q_cols // (n_rep * hd)
    qb = Q_BLOCK
    n_tiles = t // qb
    n_blocks = t // SEL_BLOCK
    kv_cols = slc.shape[1]
    cmp_rows = kcv.shape[1]
    tile_spec = pl.BlockSpec((1, kv_cols, t), lambda bi, i: (bi, 0, 0))
    return pl.pallas_call(
        functools.partial(_attn_kernel, n_kv=n_kv, n_rep=n_rep, hd=hd, n_blocks=n_blocks),
        grid=(b, n_tiles),
        in_specs=[pl.BlockSpec((1, q_cols, qb), lambda bi, i: (bi, 0, i)),
                  pl.BlockSpec((1, gt.shape[1], qb), lambda bi, i: (bi, 0, i)),
                  pl.BlockSpec((1,) + kcv.shape[1:], lambda bi, i: (bi, 0, 0)),
                  tile_spec, tile_spec],
        out_specs=pl.BlockSpec((1, qb, q_cols), lambda bi, i: (bi, i, 0)),
        out_shape=jax.ShapeDtypeStruct((b, t, q_cols), BF16),
        scratch_shapes=[pltpu.VMEM((n_kv, t, 2 * hd), BF16), pltpu.VMEM((n_kv, n_tiles, hd + V_PAD, qb), BF16),
                        pltpu.VMEM((n_kv, t, hd), BF16), pltpu.VMEM((n_kv, n_tiles, hd + V_PAD, qb), BF16),
                        pltpu.VMEM((n_kv, cmp_rows, hd), BF16), pltpu.VMEM((n_kv, hd, cmp_rows), BF16)],
        compiler_params=_cparams(2), name="attention",
    )(qt, gt, kcv, slc, win)


def _spread_q(q, hd, n_rep):
    n_heads = q.shape[0]
    d = lax.broadcasted_iota(I32, (hd, 4 * hd), 0)
    c = lax.broadcasted_iota(I32, (hd, 4 * hd), 1)
    qb16 = q.astype(BF16)
    row = lax.broadcasted_iota(I32, (n_heads, 4 * hd), 0)
    out = jnp.zeros((n_heads, 4 * hd), F32)
    for gg in range(n_heads // n_rep):
        placed = _dot(qb16, (c == d + gg * hd).astype(BF16))
        out = jnp.where(_shr(row, n_rep) == gg, placed, out)
    return out.astype(BF16)


def _masked_softmax_rows(s, mask):
    s = jnp.where(mask, s, NEG)
    e = jnp.where(mask, jnp.exp(s - jnp.max(s, axis=-1, keepdims=True)), 0.0)
    den = jnp.sum(e, axis=-1, keepdims=True)
    return e * (1.0 / jnp.where(den > 0, den, 1.0))


def _group_value_lanes(o_full, hd, n_rep):
    row = lax.broadcasted_iota(I32, (o_full.shape[0], hd), 0)
    out = o_full[:, 2 * hd:3 * hd]
    for gg in range(1, o_full.shape[0] // n_rep):
        out = jnp.where(_shr(row, n_rep) == gg, o_full[:, (2 + gg) * hd:(3 + gg) * hd], out)
    return out


def _cmp_select_dec_kernel(q_ref, kcv_ref, o_ref, idx_ref, pool_ref, score_ref, *, hd, n_rep, q_pos, n_blocks,
                           blk_lanes):
    n_heads = q_ref.shape[1]
    n_rows = kcv_ref.shape[1]
    b = pl.program_id(0)
    n_seq = idx_ref.shape[0]

    @pl.when(b == 0)
    def _():
        ratio = SEL_BLOCK // CMP_STRIDE
        pm = lax.broadcasted_iota(I32, (n_rows, blk_lanes), 0)
        pj = lax.broadcasted_iota(I32, (n_rows, blk_lanes), 1)
        pool_ref[...] = ((pm >= 1) & (pm >= ratio * pj) & (pm <= ratio * pj + ratio)).astype(BF16)

    q2 = _spread_q(q_ref[0], hd, n_rep)
    kcv = kcv_ref[0].astype(BF16)
    m_idx = lax.broadcasted_iota(I32, (n_heads, n_rows), 1)
    vis = (m_idx >= 1) & ((m_idx - 1) * CMP_STRIDE + 2 * CMP_STRIDE - 1 <= q_pos)
    p = _masked_softmax_rows(_dot_nt(q2, kcv), vis)
    o_ref[0] = _group_value_lanes(_dot(p.astype(BF16), kcv), hd, n_rep)

    row = lax.broadcasted_iota(I32, (n_heads, n_rows), 0)
    grp = jnp.zeros((n_heads, n_rows), F32)
    for gg in range(n_heads // n_rep):
        tot = jnp.sum(jnp.where(_shr(row, n_rep) == gg, p, 0.0), axis=0, keepdims=True)
        grp = jnp.where(row == gg, tot, grp)
    hi = grp.astype(BF16).astype(F32)
    mid = (grp - hi).astype(BF16).astype(F32)
    lo = grp - hi - mid
    pieces = _dot(jnp.concatenate([hi, mid, lo], axis=0).astype(BF16), pool_ref[...])
    imp = pieces[0:n_heads] + pieces[n_heads:2 * n_heads] + pieces[2 * n_heads:]
    score_ref[b] = _block_scores(imp, lax.broadcasted_iota(I32, (n_heads, blk_lanes), 1), q_pos, n_blocks)

    @pl.when(b == n_seq - 1)
    def _():
        work = score_ref[...].reshape(n_seq * n_heads, blk_lanes)
        blk_f = lax.broadcasted_iota(I32, work.shape, 1).astype(F32)
        out_lane = lax.broadcasted_iota(I32, (n_seq * n_heads, LANES), 1)
        out = jnp.full((n_seq * n_heads, LANES), -1, I32)
        for it in range(min(N_SEL, n_blocks)):
            mx = jnp.max(work, axis=-1, keepdims=True)
            first = jnp.min(jnp.where(work == mx, blk_f, float(blk_lanes)), axis=-1, keepdims=True)
            out = jnp.where(out_lane == it, jnp.where(mx > NEG / 2, first.astype(I32), -1), out)
            work = jnp.where(blk_f == first, BELOW_NEG, work)
        idx_ref[...] = out.reshape(n_seq, n_heads, LANES)


def _cmp_select_dec(q3, kcv, *, hd, n_rep, q_pos, n_blocks):
    n, n_heads, _ = q3.shape
    blk_lanes = -(-n_blocks // LANES) * LANES
    return pl.pallas_call(
        functools.partial(_cmp_select_dec_kernel, hd=hd, n_rep=n_rep, q_pos=q_pos, n_blocks=n_blocks,
                          blk_lanes=blk_lanes),
        grid=(n,),
        in_specs=[pl.BlockSpec((1, n_heads, hd), lambda i: (i, 0, 0)),
                  pl.BlockSpec((1,) + kcv.shape[1:], lambda i: (i, 0, 0))],
        out_specs=[pl.BlockSpec((1, n_heads, hd), lambda i: (i, 0, 0)),
                   _const_spec((n, n_heads, LANES))],
        out_shape=[jax.ShapeDtypeStruct((n, n_heads, hd), F32), jax.ShapeDtypeStruct((n, n_heads, LANES), I32)],
        scratch_shapes=[pltpu.VMEM((kcv.shape[1], blk_lanes), BF16), pltpu.VMEM((n, n_heads, blk_lanes), F32)],
        compiler_params=_cparams(1), name="cmp_select_dec",
    )(q3, kcv)


def _attend_dec_kernel(pt_ref, sel_ref, *refs, hd, n_rep, n_sel, n_past_blocks):
    n_kv = (len(refs) - 7) // n_sel
    blk_refs = refs[:n_kv * n_sel]
    q_ref, new_slc_ref, new_win_ref, cwin_ref, g_ref, ocmp_ref, o_ref = refs[n_kv * n_sel:]
    b = pl.program_id(0)
    n_heads = q_ref.shape[1]
    q2 = _spread_q(q_ref[0], hd, n_rep)
    q2f = q2.astype(F32)
    head_grp = _shr(lax.broadcasted_iota(I32, (n_heads, 1), 0), n_rep)

    def with_new_key(s, mask, keys_t, new_row, new_ok):
        nr = new_row.astype(BF16).astype(F32)
        s_new = jnp.sum(q2f * nr, axis=-1, keepdims=True)
        s = jnp.where(mask, s, NEG)
        s_new = jnp.where(new_ok, s_new, NEG)
        m = jnp.maximum(jnp.max(s, axis=-1, keepdims=True), s_new)
        e = jnp.where(mask, jnp.exp(s - m), 0.0)
        e_new = jnp.where(new_ok, jnp.exp(s_new - m), 0.0)
        den = jnp.sum(e, axis=-1, keepdims=True) + e_new
        inv = 1.0 / jnp.where(den > 0, den, 1.0)
        return _dot_nt((e * inv).astype(BF16), keys_t) + (e_new * inv) * nr

    page = blk_refs[0].shape[2]
    per = page // SEL_BLOCK
    keys_t = jnp.concatenate([r[0] for r in blk_refs], axis=1).astype(BF16)
    n_keys = keys_t.shape[1]
    col_slot = _shr(lax.broadcasted_iota(I32, (1, n_keys), 1), page)
    page_blk = _shr(lax.broadcasted_iota(I32, (1, page), 1), SEL_BLOCK)
    slot_ok = []
    new_ok = jnp.zeros((n_heads, 1), I32)
    for gg in range(n_kv):
        for k in range(n_sel):
            idx = sel_ref[b, gg * n_sel + k]
            cached = ((idx >= 0) & (idx < n_past_blocks)).astype(I32)
            slot_ok.append(jnp.where(page_blk == (idx & (per - 1)), cached, 0))
            new_ok = jnp.where(head_grp == gg, new_ok | (idx == n_past_blocks).astype(I32), new_ok)
    mask = (jnp.concatenate(slot_ok, axis=1) > 0) & (_shr(col_slot, n_sel) == head_grp)
    o_slc = _group_value_lanes(with_new_key(_dot(q2, keys_t), mask, keys_t, new_slc_ref[0], new_ok > 0), hd, n_rep)

    keys_t = cwin_ref[0].astype(BF16)
    all_ok = jnp.full((n_heads, keys_t.shape[1]), True)
    o_win = _group_value_lanes(
        with_new_key(_dot(q2, keys_t), all_ok, keys_t, new_win_ref[0], jnp.full((n_heads, 1), True)), hd, n_rep)

    gates = g_ref[0]
    o_ref[0] = gates[:, 0:1] * ocmp_ref[0] + gates[:, 1:2] * o_slc + gates[:, 2:3] * o_win


def _attend_dec(page_table, sel, slc_pages, q3, new_slc, new_win, cache_win, gates3, o_cmp, *, hd, n_rep, n_sel,
                n_past_blocks):
    n, n_heads, _ = q3.shape
    n_kv = n_heads // n_rep
    per = n_past_blocks // page_table.shape[1]

    def blk_map(i, pt, sl, slot):
        idx = jnp.clip(sl[i, slot], 0, n_past_blocks - 1)
        return (pt[i, _shr(idx, per)], 0, 0)

    blk_specs = [pl.BlockSpec((1,) + slc_pages.shape[1:], functools.partial(blk_map, slot=s))
                 for s in range(n_kv * n_sel)]

    def row_spec(shape):
        nd = len(shape)
        return pl.BlockSpec((1,) + tuple(shape[1:]), lambda i, pt, sl: (i,) + (0,) * (nd - 1))

    others = (q3, new_slc, new_win, cache_win, gates3, o_cmp)
    return pl.pallas_call(
        functools.partial(_attend_dec_kernel, hd=hd, n_rep=n_rep, n_sel=n_sel, n_past_blocks=n_past_blocks),
        grid_spec=pltpu.PrefetchScalarGridSpec(
            num_scalar_prefetch=2, grid=(n,),
            in_specs=blk_specs + [row_spec(a.shape) for a in others],
            out_specs=row_spec(o_cmp.shape)),
        out_shape=jax.ShapeDtypeStruct(o_cmp.shape, F32),
        compiler_params=_cparams(1), name="attend_dec",
    )(page_table, sel, *([slc_pages] * (n_kv * n_sel)), *others)


def _mix_kernel(x_ref, o_ref, nw_ref, wuv_ref, wgate_ref, gn_ref, ws_ref, bs_ref, wpa_ref, wpb_ref, wout_ref,
                x1_ref, v_ref, *, width, chunk, single_pos):
    x = x_ref[0]
    d = x.shape[-1]
    h = _rmsnorm(x, nw_ref[...]).astype(BF16)
    uv = jax.nn.gelu(_dot_nt(h, wuv_ref[...]))
    u, vn = uv[:, 0:width], _rmsnorm(uv[:, width:2 * width], gn_ref[...])
    gates = jax.nn.sigmoid(_dot_nt(h, wgate_ref[...]))
    gw = width // GMLP_GROUPS
    if single_pos:
        v_ref[0] = vn
        mixed = u * (ws_ref[...] * vn + bs_ref[...])
    else:
        rows = x.shape[0]
        v_ref[0] = vn[rows - chunk:rows, :]
        tri = lax.broadcasted_iota(I32, (chunk, chunk), 0) >= lax.broadcasted_iota(I32, (chunk, chunk), 1)
        vb = vn.astype(BF16)
        pieces = []
        for c in range(rows // chunk):
            zs = []
            for gi in range(GMLP_GROUPS):
                wm = jnp.where(tri, ws_ref[gi], 0.0).astype(BF16)
                zs.append(_dot(wm, vb[c * chunk:(c + 1) * chunk, gi * gw:(gi + 1) * gw]) + bs_ref[:, gi:gi + 1])
            pieces.append(jnp.concatenate(zs, axis=1))
        mixed = u * jnp.concatenate(pieces, axis=0)
    br_a = _dot(o_ref[0].astype(BF16), wpa_ref[...])
    br_b = _dot(mixed.astype(BF16), wpb_ref[...])
    merged = gates[:, 0:d] * br_a + gates[:, d:2 * d] * br_b
    x1_ref[0] = x + _dot(merged.astype(BF16), wout_ref[...])


def _mix(x, o_nsa, nw, wuv, wgate, gn, ws, bs, wpa, wpb, wout, *, tm, chunk, single_pos):
    b, t, d = x.shape
    width = wuv.shape[0] // 2
    v_rows = tm if single_pos else chunk
    weights = (nw, wuv, wgate, gn, ws, bs, wpa, wpb, wout)
    return pl.pallas_call(
        functools.partial(_mix_kernel, width=width, chunk=chunk, single_pos=single_pos),
        grid=(b, t // tm),
        in_specs=[pl.BlockSpec((1, tm, d), lambda i, j: (i, j, 0)),
                  pl.BlockSpec((1, tm, o_nsa.shape[-1]), lambda i, j: (i, j, 0))]
                 + [_const_spec(a.shape) for a in weights],
        out_specs=[pl.BlockSpec((1, tm, d), lambda i, j: (i, j, 0)),
                   pl.BlockSpec((1, v_rows, width), lambda i, j: (i, 0, 0))],
        out_shape=[jax.ShapeDtypeStruct((b, t, d), F32), jax.ShapeDtypeStruct((b, v_rows, width), F32)],
        compiler_params=_cparams(2), name="mix_dec" if single_pos else "mix",
    )(x, o_nsa, *weights)


def _ffn_kernel(x1_ref, prev_ref, nf_ref, wup_ref, cw_ref, cb_ref, wdown_ref, nfin_ref, y_ref, a_ref,
                *, d_ff, f_tile, halo, single_pos):
    x1 = x1_ref[0]
    rows = x1.shape[0]
    if single_pos:
        h = _rmsnorm(x1, nf_ref[...]).astype(BF16)
    else:
        h = _rmsnorm(jnp.concatenate([prev_ref[0], x1], axis=0), nf_ref[...]).astype(BF16)
        ext_row = lax.broadcasted_iota(I32, (rows + halo, f_tile), 0)
        first = pl.program_id(1) == 0
    y = jnp.zeros_like(x1)
    for f0 in range(0, d_ff, f_tile):
        cols = slice(f0, f0 + f_tile)
        a = _dot(h, wup_ref[:, cols])
        bgate = _dot(h, wup_ref[:, d_ff + f0:d_ff + f0 + f_tile])
        if single_pos:
            a_ref[0, :, cols] = a
            c = cb_ref[:, cols] + prev_ref[0, :, cols] * cw_ref[0:1, cols] + prev_ref[1, :, cols] * cw_ref[1:2, cols] \
                + a * cw_ref[2:3, cols]
        else:
            a = jnp.where((ext_row < halo) & first, 0.0, a)
            back2 = pltpu.roll(a, 2, 0)
            a_ref[0, :, cols] = back2[0:2, :]
            c = cb_ref[:, cols] + back2[halo:, :] * cw_ref[0:1, cols] \
                + pltpu.roll(a, 1, 0)[halo:, :] * cw_ref[1:2, cols] + a[halo:, :] * cw_ref[2:3, cols]
            bgate = bgate[halo:, :]
        y = y + _dot((jax.nn.gelu(c) * bgate).astype(BF16), wdown_ref[cols, :])
    y_ref[0] = _rmsnorm(x1 + y, nfin_ref[...])


def _ffn(x1, prev, nf, wup, cw, cb, wdown, nfin, *, tm, f_tile, single_pos):
    b, t, d = x1.shape
    d_ff = wdown.shape[0]
    halo = 8
    weights = (nf, wup, cw, cb, wdown, nfin)
    if single_pos:
        prev_spec = _const_spec(prev.shape)
        a_rows = tm
    else:
        per = tm // halo
        prev_spec = pl.BlockSpec((1, halo, d), lambda i, j: (i, jnp.maximum(j * per - 1, 0), 0))
        a_rows = 2
    return pl.pallas_call(
        functools.partial(_ffn_kernel, d_ff=d_ff, f_tile=f_tile, halo=halo, single_pos=single_pos),
        grid=(b, t // tm),
        in_specs=[pl.BlockSpec((1, tm, d), lambda i, j: (i, j, 0)), prev_spec] + [_const_spec(a.shape) for a in weights],
        out_specs=[pl.BlockSpec((1, tm, d), lambda i, j: (i, j, 0)),
                   pl.BlockSpec((1, a_rows, d_ff), lambda i, j: (i, 0, 0))],
        out_shape=[jax.ShapeDtypeStruct((b, t, d), F32), jax.ShapeDtypeStruct((b, a_rows, d_ff), F32)],
        compiler_params=_cparams(2), name="ffn_dec" if single_pos else "ffn",
    )(x1, prev, *weights)


def _compress_params(pe, w1, b1, w2, n_kv):
    cmp_len, hd = pe.shape[1], pe.shape[2]
    hid = w1.shape[2]
    halves = cmp_len // CMP_STRIDE
    eye = jnp.eye(n_kv, dtype=w1.dtype)
    pe_t = jnp.broadcast_to(pe.reshape(2, halves, CMP_STRIDE, 1, hd), (2, halves, CMP_STRIDE, n_kv, hd))
    pe_t = pe_t.reshape(2, halves, 1, CMP_STRIDE * n_kv * hd)
    w1h = w1.reshape(2, halves, CMP_STRIDE, hd, hid)
    w1b = jnp.einsum('krsdh,gf->ksgdrfh', w1h, eye).reshape(2, CMP_STRIDE * n_kv * hd, halves * n_kv * hid)
    b1t = jnp.tile(b1, (1, n_kv)).reshape(2, 1, n_kv * hid)
    w2b = jnp.einsum('khd,gf->kghfd', w2, eye).reshape(2, n_kv * hid, n_kv * hd)
    return pe_t, w1b.astype(BF16), b1t, w2b.astype(BF16)


def kernel(x_prompt, x_sample, cache_cmp, cache_slc, cache_win, state_conv, page_table, norm_mix, w_in, cmp_pe,
           cmp_w1, cmp_b1, cmp_w2, gmlp_norm, gmlp_ws, gmlp_bs, w_proj_a, w_proj_b, w_out, norm_ffn, w_up, conv_w,
           conv_b, w_down, norm_final):
    depth = w_in.shape[0]
    assert depth == 1, "single-layer step"
    bp, t, d = x_prompt.shape
    bd, tn, _ = x_sample.shape
    assert tn == 1
    n_kv, hd = cache_cmp.shape[4], cache_cmp.shape[5]
    page = cache_cmp.shape[2]
    q_cols = w_proj_a.shape[1]
    n_heads = q_cols // hd
    n_rep = n_heads // n_kv
    kv_cols = 2 * n_kv * hd
    width = w_proj_b.shape[1]
    chunk = gmlp_ws.shape[-1]
    d_ff = w_down.shape[1]
    n_pages = page_table.shape[1]
    past_len = n_pages * page
    scale = hd ** -0.5
    assert conv_w.shape[1] == 3 and cache_win.shape[2] <= WINDOW and past_len % SEL_BLOCK == 0

    wit = w_in[0].T.astype(BF16)
    off_kv, off_g = q_cols, q_cols + 3 * kv_cols
    off_uv = off_g + 3 * n_heads
    off_gate = off_uv + 2 * width
    w_qt, w_kvt, w_g = wit[0:off_kv], wit[off_kv:off_g], wit[off_g:off_uv]
    w_uv, w_gate = wit[off_uv:off_gate], wit[off_gate:]
    g_rows = 16
    w_gt = jnp.pad(w_g.reshape(n_kv, 3 * n_rep, d), ((0, 0), (0, g_rows - 3 * n_rep), (0, 0))).reshape(n_kv * g_rows, d)
    w_dec = jnp.concatenate([w_qt, w_kvt, jnp.pad(w_g, ((0, LANES - 3 * n_heads), (0, 0)))], axis=0)
    nm, nf, nfin, gn = norm_mix[0][None], norm_ffn[0][None], norm_final[None], gmlp_norm[0][None]
    pe_t, w1b, b1t, w2b = _compress_params(cmp_pe[0], cmp_w1[0], cmp_b1[0], cmp_w2[0], n_kv)
    wpa, wpb, wout = w_proj_a[0].astype(BF16), w_proj_b[0].astype(BF16), w_out[0].astype(BF16)
    wup, wdown = w_up[0].astype(BF16), w_down[0].astype(BF16)
    cw, cb = conv_w[0], conv_b[0][None]
    ws, bs = gmlp_ws[0], gmlp_bs[0]

    cmp_t, slc_t, win_t, qt, gt = _front(x_prompt, nm, w_kvt, w_qt, w_gt, kv_cols=kv_cols, scale=scale * LOG2_E,
                                         tm=512)
    kcv = _compress([cmp_t], pe_t, w1b, b1t, w2b, n_seq=bp, steps=1, rows=t // CMP_STRIDE)
    o_nsa = _attention(qt, gt, kcv, slc_t, win_t, n_rep=n_rep, hd=hd)
    x1, v_p = _mix(x_prompt, o_nsa, nm, w_uv, w_gate, gn, ws, bs.T, wpa, wpb, wout,
                   tm=256, chunk=chunk, single_pos=False)
    y_p, conv_p = _ffn(x1, x1, nf, wup, cw, cb, wdown, nfin, tm=256, f_tile=d_ff // 2, single_pos=False)

    xs = x_sample.reshape(bd, d)
    q_s, kv_s, g_s = _front_dec(xs, nm, w_dec, q_cols=q_cols, kv_cols3=3 * kv_cols, scale=scale)
    kv_cmp_s, kv_slc_s, kv_win_s = kv_s[:, 0:kv_cols], kv_s[:, kv_cols:2 * kv_cols], kv_s[:, 2 * kv_cols:]
    def positions_last(c):
        return jnp.transpose(c, (0, 2, 3, 4, 1)).reshape(c.shape[0], kv_cols, c.shape[1])

    cmp_pages, slc_pages, win_rows = positions_last(cache_cmp[0]), positions_last(cache_slc[0]), positions_last(cache_win[0])
    pages_per_step = 32
    kcv_s = _compress([cmp_pages] * pages_per_step, pe_t, w1b, b1t, w2b, n_seq=bd,
                      steps=n_pages // pages_per_step, rows=pages_per_step * page // CMP_STRIDE,
                      page_table=page_table, pages_per_step=pages_per_step)
    q3 = q_s.reshape(bd, n_heads, hd)
    n_blocks_s = -(-(past_len + tn) // SEL_BLOCK)
    o_cmp_s, idx_s = _cmp_select_dec(q3, kcv_s, hd=hd, n_rep=n_rep, q_pos=past_len, n_blocks=n_blocks_s)
    sel = idx_s[:, 0:n_kv, 0:N_SEL].reshape(bd, n_kv * N_SEL)
    o_nsa_s = _attend_dec(page_table, sel, slc_pages, q3, kv_slc_s[:, None, :], kv_win_s[:, None, :], win_rows,
                          g_s[:, 0:3 * n_heads].reshape(bd, n_heads, 3), o_cmp_s,
                          hd=hd, n_rep=n_rep, n_sel=N_SEL, n_past_blocks=past_len // SEL_BLOCK)
    gw = width // GMLP_GROUPS
    ws0 = jnp.repeat(ws[:, 0, 0], gw)[None]
    bs0 = jnp.repeat(bs[:, 0], gw)[None]
    x1_s, v_s = _mix(xs[None], o_nsa_s.reshape(1, bd, q_cols), nm, w_uv, w_gate, gn, ws0, bs0, wpa, wpb, wout,
                     tm=bd, chunk=chunk, single_pos=True)
    prev_s = jnp.swapaxes(state_conv[0], 0, 1)
    y_s, a_s = _ffn(x1_s, prev_s, nf, wup, cw, cb, wdown, nfin, tm=bd, f_tile=d_ff // 2, single_pos=True)

    def rows6(a_t):
        n, _, npos = a_t.shape
        return jnp.transpose(a_t.reshape(n, 2, n_kv, hd, npos), (0, 4, 1, 2, 3))[None]

    win_keep = min(WINDOW, t)
    win_keep_s = min(WINDOW, cache_win.shape[2] + tn)
    win_s = jnp.concatenate([win_rows, kv_win_s[:, :, None]], axis=2)[:, :, cache_win.shape[2] + tn - win_keep_s:]
    conv_s = jnp.concatenate([state_conv[0][:, 1:], a_s[0][:, None, :]], axis=1)
    return (y_p, y_s.reshape(bd, tn, d),
            rows6(cmp_t), rows6(slc_t), rows6(win_t[:, :, t - win_keep:]),
            v_p[None], conv_p[None],
            rows6(kv_cmp_s[:, :, None]), rows6(kv_slc_s[:, :, None]), rows6(win_s),
            v_s.reshape(1, bd, tn, width), conv_s[None])
```

```python
import functools

import jax
import jax.numpy as jnp
from jax import lax
from jax.experimental import pallas as pl
from jax.experimental.pallas import tpu as pltpu

F32 = jnp.float32
BF16 = jnp.bfloat16
I32 = jnp.int32

CMP_STRIDE = 16
SEG_PITCH = 24
CMP_CHUNKS = 1
SEL_BLOCK = 64
N_SEL = 16
N_LOCAL_SEL = 2
WINDOW = 512
Q_BLOCK = 256
KEY_TILE = 128
SLC_TILES = 4
V_PAD = 16
LOG2_E = 1.4426950408889634
GMLP_GROUPS = 4
EPS = 1e-6
NEG = -1e30
BELOW_NEG = -3e38
SEL_BONUS = 1e6

V7X_VMEM_BYTES = 64 * 1024 * 1024
VMEM_REQUEST_BYTES = 56 * 1024 * 1024
LANES = 128


def _cparams(n_grid):
    return pltpu.CompilerParams(
        dimension_semantics=("arbitrary",) * n_grid, vmem_limit_bytes=VMEM_REQUEST_BYTES)


def _rmsnorm(x, g):
    ms = jnp.mean(x * x, axis=-1, keepdims=True)
    return x * lax.rsqrt(ms + EPS) * g


def _dot(a, b):
    return jnp.dot(a, b, preferred_element_type=F32)


def _dot_nt(a, b):
    return lax.dot_general(a, b, (((1,), (1,)), ((), ())), preferred_element_type=F32)


def _dot_exact(a, b):
    return jnp.dot(a, b, precision=lax.Precision.HIGHEST, preferred_element_type=F32)


def _shr(x, n):
    assert n & (n - 1) == 0
    return x >> (n.bit_length() - 1)


def _const_spec(shape):
    nd = len(shape)
    return pl.BlockSpec(shape, lambda *_: (0,) * nd)


def _front_kernel(x_ref, nw_ref, wkvt_ref, wqt_ref, wgt_ref, cmp_ref, slc_ref, win_ref, qt_ref, gt_ref,
                  *, kv_cols, scale):
    h = _rmsnorm(x_ref[0], nw_ref[...]).astype(BF16)
    kvt = _dot_nt(wkvt_ref[...], h)
    cmp_ref[0] = kvt[0:kv_cols]
    slc_ref[0] = kvt[kv_cols:2 * kv_cols]
    win_ref[0] = kvt[2 * kv_cols:3 * kv_cols]
    qt_ref[0] = (_dot_nt(wqt_ref[...], h) * scale).astype(BF16)
    gt_ref[0] = jax.nn.sigmoid(_dot_nt(wgt_ref[...], h))


def _front(x, nw, wkvt, wqt, wgt, *, kv_cols, scale, tm):
    b, t, d = x.shape
    q_cols, g_rows = wqt.shape[0], wgt.shape[0]
    kv_shape = jax.ShapeDtypeStruct((b, kv_cols, t), F32)
    kv_spec = pl.BlockSpec((1, kv_cols, tm), lambda i, j: (i, 0, j))
    return pl.pallas_call(
        functools.partial(_front_kernel, kv_cols=kv_cols, scale=scale),
        grid=(b, t // tm),
        in_specs=[pl.BlockSpec((1, tm, d), lambda i, j: (i, j, 0)), _const_spec(nw.shape),
                  _const_spec(wkvt.shape), _const_spec(wqt.shape), _const_spec(wgt.shape)],
        out_specs=[kv_spec, kv_spec, kv_spec,
                   pl.BlockSpec((1, q_cols, tm), lambda i, j: (i, 0, j)),
                   pl.BlockSpec((1, g_rows, tm), lambda i, j: (i, 0, j))],
        out_shape=[kv_shape, kv_shape, kv_shape,
                   jax.ShapeDtypeStruct((b, q_cols, t), BF16),
                   jax.ShapeDtypeStruct((b, g_rows, t), F32)],
        compiler_params=_cparams(2), name="front",
    )(x, nw, wkvt, wqt, wgt)


def _front_dec_kernel(x_ref, nw_ref, w_ref, q_ref, kv_ref, g_ref, *, q_cols, kv_cols3, scale):
    h = _rmsnorm(x_ref[...], nw_ref[...]).astype(BF16)
    z = _dot_nt(h, w_ref[...])
    q_ref[...] = z[:, 0:q_cols] * scale
    kv_ref[...] = z[:, q_cols:q_cols + kv_cols3]
    g_ref[...] = jax.nn.sigmoid(z[:, q_cols + kv_cols3:])


def _front_dec(x, nw, w, *, q_cols, kv_cols3, scale):
    n = x.shape[0]
    g_cols = w.shape[0] - q_cols - kv_cols3
    return pl.pallas_call(
        functools.partial(_front_dec_kernel, q_cols=q_cols, kv_cols3=kv_cols3, scale=scale),
        grid=(1,),
        in_specs=[_const_spec(x.shape), _const_spec(nw.shape), _const_spec(w.shape)],
        out_specs=[_const_spec((n, q_cols)), _const_spec((n, kv_cols3)), _const_spec((n, g_cols))],
        out_shape=[jax.ShapeDtypeStruct((n, q_cols), F32), jax.ShapeDtypeStruct((n, kv_cols3), F32),
                   jax.ShapeDtypeStruct((n, g_cols), F32)],
        compiler_params=_cparams(1), name="front_dec",
    )(x, nw, w)


def _compress_kernel(*refs, n_x, n_prefetch, stride, kv_cols, hid2):
    refs = refs[n_prefetch:]
    x_refs = refs[:n_x]
    pe_ref, w1_ref, b1_ref, w2_ref, out_ref, carry_ref, pos_ref = refs[n_x:]
    half = kv_cols // 2
    assert half == LANES

    @pl.when(pl.program_id(1) == 0)
    def _():
        carry_ref[...] = jnp.zeros_like(carry_ref)

    seg_per_tile = LANES // stride
    tiles = [(r, c) for r in x_refs for c in range(r.shape[2] // LANES)]
    per = len(tiles) // CMP_CHUNKS
    rows = per * seg_per_tile
    row = lax.broadcasted_iota(I32, (rows, hid2), 0)
    prev = [carry_ref[kv, 0:1, :] for kv in range(2)]
    for ch in range(CMP_CHUNKS):
        for kv in range(2):
            for ti, (r, c) in enumerate(tiles[ch * per:(ch + 1) * per]):
                t = r[0, kv * half:(kv + 1) * half, c * LANES:(c + 1) * LANES].T
                for n in range(seg_per_tile):
                    p0 = (ti * seg_per_tile + n) * SEG_PITCH
                    pos_ref[ch, kv, p0:p0 + stride, :] = t[n * stride:(n + 1) * stride, :]
            xkv = jnp.concatenate(
                [pos_ref[ch, kv, pl.ds(s, rows, stride=SEG_PITCH), :] for s in range(stride)],
                axis=1)
            parts = []
            for r in range(2):
                a = (xkv + pe_ref[kv, r]).astype(BF16)
                parts.append(_dot(a, w1_ref[kv, :, r * hid2:(r + 1) * hid2]))
            shifted = jnp.where(row == 0, prev[kv], pltpu.roll(parts[0], 1, 0))
            prev[kv] = parts[0][rows - 1:rows, :]
            hid = b1_ref[kv] + shifted + parts[1]
            out_ref[0, ch * rows:(ch + 1) * rows, kv * half:(kv + 1) * half] = \
                _dot(jax.nn.gelu(hid).astype(BF16), w2_ref[kv])
    for kv in range(2):
        carry_ref[kv, 0:1, :] = prev[kv]


def _compress(x_list, pe, w1, b1, w2, *, n_seq, steps, rows, page_table=None, pages_per_step=None):
    stride, kv_cols = CMP_STRIDE, w2.shape[2] * 2
    hid2 = b1.shape[2]
    n_x = len(x_list)
    kern = functools.partial(_compress_kernel, n_x=n_x, n_prefetch=0 if page_table is None else 1,
                             stride=stride, kv_cols=kv_cols, hid2=hid2)
    out_shape = jax.ShapeDtypeStruct((n_seq, steps * rows, kv_cols), F32)
    scratch = [pltpu.VMEM((2, 8, hid2), F32),
               pltpu.VMEM((CMP_CHUNKS, 2, rows // CMP_CHUNKS * SEG_PITCH, kv_cols // 2), F32)]
    if page_table is None:
        x_specs = [pl.BlockSpec((1, kv_cols, rows * stride), lambda i, j: (i, 0, j))]
        w_specs = [_const_spec(a.shape) for a in (pe, w1, b1, w2)]
        return pl.pallas_call(
            kern, grid=(n_seq, steps), in_specs=x_specs + w_specs,
            out_specs=pl.BlockSpec((1, rows, kv_cols), lambda i, j: (i, j, 0)),
            out_shape=out_shape, scratch_shapes=scratch, compiler_params=_cparams(2), name="compress",
        )(*x_list, pe, w1, b1, w2)
    page = rows * stride // pages_per_step
    x_specs = [pl.BlockSpec((1, kv_cols, page),
                            functools.partial(lambda i, j, pt, k: (pt[i, j * pages_per_step + k], 0, 0), k=k))
               for k in range(n_x)]
    w_specs = [pl.BlockSpec(a.shape, functools.partial(lambda i, j, pt, nd: (0,) * nd, nd=a.ndim))
               for a in (pe, w1, b1, w2)]
    return pl.pallas_call(
        kern,
        grid_spec=pltpu.PrefetchScalarGridSpec(
            num_scalar_prefetch=1, grid=(n_seq, steps), in_specs=x_specs + w_specs,
            out_specs=pl.BlockSpec((1, rows, kv_cols), lambda i, j, pt: (i, j, 0)),
            scratch_shapes=scratch),
        out_shape=out_shape, compiler_params=_cparams(2), name="compress_paged",
    )(page_table, *x_list, pe, w1, b1, w2)


def _block_scores(imp, blk, q_pos, n_blocks):
    cur = _shr(q_pos, SEL_BLOCK)
    valid = (blk * SEL_BLOCK <= q_pos) & (blk < n_blocks)
    forced = (blk == 0) | ((blk <= cur) & (blk > cur - N_LOCAL_SEL))
    score = jnp.where(valid, imp + jnp.where(forced, SEL_BONUS, 0.0), NEG)
    return jnp.where(blk < n_blocks, score, BELOW_NEG)


def _attn_kernel(qt_ref, gt_ref, kcv_ref, slc_ref, win_ref, o_ref,
                 kaug_ref, vts_ref, kwin_ref, vtw_ref, kc_ref, vct_ref, *, n_kv, n_rep, hd, n_blocks):
    i = pl.program_id(1)
    qb, kt = Q_BLOCK, KEY_TILE
    q_tiles = qb // kt
    n_tiles = slc_ref.shape[2] // kt
    n_cmp_rows = kcv_ref.shape[1]
    nq = n_rep * qb
    kd = n_kv * hd
    g_rows = gt_ref.shape[1] // n_kv
    assert kd == LANES and n_blocks == hd, "the selection bias rows ride in the key one-hot lanes"

    @pl.when(i == 0)
    def _():
        lane = lax.broadcasted_iota(I32, (kt, kd), 1)
        krow = lax.broadcasted_iota(I32, (kt, kd), 0)

        def group_lanes(x, gg):
            return x if gg == 0 else pltpu.roll(x, kd - gg * hd, 1)

        ones_row = (lax.broadcasted_iota(I32, (V_PAD, kt), 0) == 0).astype(BF16)
        for c in range(n_tiles):
            cols = slice(c * kt, (c + 1) * kt)
            onehot = (lane - hd == _shr(c * kt + krow, SEL_BLOCK)).astype(F32)
            kt_s, kt_w = slc_ref[0, 0:kd, cols].T, win_ref[0, 0:kd, cols].T
            for gg in range(n_kv):
                v_rows = slice(kd + gg * hd, kd + (gg + 1) * hd)
                kaug_ref[gg, cols, :] = jnp.where(lane < hd, group_lanes(kt_s, gg), onehot).astype(BF16)
                vts_ref[gg, c, 0:hd, :] = slc_ref[0, v_rows, cols].astype(BF16)
                vts_ref[gg, c, hd:, :] = ones_row
                kwin_ref[gg, cols, :] = group_lanes(kt_w, gg)[:, 0:hd].astype(BF16)
                vtw_ref[gg, c, 0:hd, :] = win_ref[0, v_rows, cols].astype(BF16)
                vtw_ref[gg, c, hd:, :] = ones_row
        for c in range(n_cmp_rows // kt):
            rows = slice(c * kt, (c + 1) * kt)
            blk = kcv_ref[0, rows, :]
            vt = blk[:, kd:2 * kd].T
            for gg in range(n_kv):
                kc_ref[gg, rows, :] = group_lanes(blk[:, 0:kd], gg)[:, 0:hd].astype(BF16)
                vct_ref[gg, :, rows] = vt[gg * hd:(gg + 1) * hd, :].astype(BF16)

    q_pos = i * qb + (lax.broadcasted_iota(I32, (1, nq), 1) & (qb - 1))

    groups = range(n_kv)
    heads = range(n_rep)

    q_minus_k = (lax.broadcasted_iota(I32, (kt, nq), 1) & (qb - 1)) - lax.broadcasted_iota(I32, (kt, nq), 0)

    def softmax_step(m, sc):
        m_new = jnp.maximum(m, jnp.max(sc, axis=0, keepdims=True))
        return m_new, jnp.exp2(m - m_new), jnp.exp2(sc - m_new).astype(BF16)

    def normalised(acc):
        return acc[0:hd, :] * (1.0 / acc[hd:hd + 1, :])

    qts = [qt_ref[0, gg * n_rep * hd:(gg + 1) * n_rep * hd, :] for gg in groups]
    qcats = [jnp.concatenate([qts[gg][h * hd:(h + 1) * hd, :] for h in heads], axis=1) for gg in groups]

    def attend(k_ref, v_ref, qs, tile_ids, state, masked):
        scores = [[_dot(k_ref[gg, pl.ds(pl.multiple_of(t * kt, kt), kt), :], qs[gg]) for gg in groups]
                  for t in tile_ids]
        maxes, accs = list(state[0]), list(state[1])
        for u, t in enumerate(tile_ids):
            for gg in groups:
                maxes[gg], alpha, pb = softmax_step(maxes[gg], masked(u, scores[u][gg]))
                accs[gg] = alpha * accs[gg] + _dot(v_ref[gg, t], pb)
        return tuple(maxes), tuple(accs)

    state0 = ((jnp.full((1, nq), NEG, F32),) * n_kv, (jnp.zeros((hd + V_PAD, nq), F32),) * n_kv)

    s_cmp = [_dot(kc_ref[gg], qcats[gg]) for gg in groups]
    m_idx = lax.broadcasted_iota(I32, (n_cmp_rows, nq), 0)
    vis = (m_idx >= 1) & ((m_idx - 1) * CMP_STRIDE + 2 * CMP_STRIDE - 1 <= q_pos)
    ratio = SEL_BLOCK // CMP_STRIDE
    pj = lax.broadcasted_iota(I32, (n_blocks, n_cmp_rows), 0)
    pm = lax.broadcasted_iota(I32, (n_blocks, n_cmp_rows), 1)
    pool = ((pm >= 1) & (pm >= ratio * pj) & (pm <= ratio * pj + ratio)).astype(F32)
    o_cmp, imp = [], []
    for gg in groups:
        s = jnp.where(vis, s_cmp[gg], NEG)
        e = jnp.where(vis, jnp.exp2(s - jnp.max(s, axis=0, keepdims=True)), 0.0)
        den = jnp.sum(e, axis=0, keepdims=True)
        p = e * (1.0 / jnp.where(den > 0, den, 1.0))
        o_cmp.append(_dot(vct_ref[gg], p.astype(BF16)))
        p_grp = p[:, 0:qb]
        for h in range(1, n_rep):
            p_grp = p_grp + p[:, h * qb:(h + 1) * qb]
        imp.append(_dot_exact(pool, p_grp))

    n_band = WINDOW // kt
    first_t = i * q_tiles - n_band
    win_tiles = [jnp.maximum(first_t + u, 0) for u in range(n_band + q_tiles)]

    def in_window(u, sc):
        dlt = q_minus_k + (n_band - u) * kt
        if u < q_tiles:
            sc = jnp.where(dlt <= WINDOW, sc, NEG)
        if u >= n_band:
            sc = jnp.where(dlt >= 0, sc, NEG)
        if u < n_band:
            sc = sc + jnp.where(first_t + u < 0, NEG, 0.0)
        if u == 0:
            sc = sc + after_importance
        return sc

    after_importance = sum(imp[gg][0:1, 0:1] for gg in groups) * 0.0

    _, accs = attend(kwin_ref, vtw_ref, qcats, win_tiles, state0, in_window)
    o_win = [normalised(accs[gg]) for gg in groups]

    blk_id = lax.broadcasted_iota(I32, (n_blocks, qb), 0)
    blk_f = blk_id.astype(F32)
    qaug = []
    for gg in groups:
        score = _block_scores(imp[gg], blk_id, q_pos[:, 0:qb], n_blocks)
        work, sel = score, jnp.zeros((n_blocks, qb), F32)
        for _ in range(min(N_SEL, n_blocks)):
            mx = jnp.max(work, axis=0, keepdims=True)
            first = jnp.min(jnp.where(work == mx, blk_f, float(n_blocks)), axis=0, keepdims=True)
            pick = blk_f == first
            sel = jnp.where(pick, 1.0, sel)
            work = jnp.where(pick, BELOW_NEG, work)
        bias = jnp.where((sel > 0) & (score > NEG / 2), 0.0, NEG).astype(BF16)
        qaug.append(jnp.concatenate(
            [jnp.concatenate([qts[gg][h * hd:(h + 1) * hd, :], bias], axis=0) for h in heads], axis=1))

    assert n_tiles % SLC_TILES == 0 and SLC_TILES % q_tiles == 0
    n_full = _shr(i * q_tiles, SLC_TILES)
    state = lax.fori_loop(
        0, n_full,
        lambda j, st: attend(kaug_ref, vts_ref, qaug, [j * SLC_TILES + u for u in range(SLC_TILES)], st,
                             lambda u, sc: sc), state0)
    last = n_full * SLC_TILES
    _, accs = attend(kaug_ref, vts_ref, qaug, [last + u for u in range(SLC_TILES)], state,
                     lambda u, sc: jnp.where(q_minus_k + (i * q_tiles - last - u) * kt >= 0, sc, NEG))

    outs = []
    for gg in groups:
        o_slc = normalised(accs[gg])
        for h in heads:
            cols = slice(h * qb, (h + 1) * qb)
            gate = [gt_ref[0, gg * g_rows + 3 * h + br:gg * g_rows + 3 * h + br + 1, :] for br in range(3)]
            outs.append(gate[0] * o_cmp[gg][:, cols] + gate[1] * o_slc[:, cols] + gate[2] * o_win[gg][:, cols])
    o_ref[0] = jnp.concatenate(outs, axis=0).T.astype(BF16)


def _attention(qt, gt, kcv, slc, win, *, n_rep, hd):
    b, q_cols, t = qt.shape
    n_kv = q_cols // (n_rep * hd)
    qb, kt = Q_BLOCK, KEY_TILE
    n_tiles = t // kt
    n_blocks = t // SEL_BLOCK
    kv_cols = slc.shape[1]
    cmp_rows = kcv.shape[1]
    tile_spec = pl.BlockSpec((1, kv_cols, t), lambda bi, i: (bi, 0, 0))
    return pl.pallas_call(
        functools.partial(_attn_kernel, n_kv=n_kv, n_rep=n_rep, hd=hd, n_blocks=n_blocks),
        grid=(b, t // qb),
        in_specs=[pl.BlockSpec((1, q_cols, qb), lambda bi, i: (bi, 0, i)),
                  pl.BlockSpec((1, gt.shape[1], qb), lambda bi, i: (bi, 0, i)),
                  pl.BlockSpec((1,) + kcv.shape[1:], lambda bi, i: (bi, 0, 0)),
                  tile_spec, tile_spec],
        out_specs=pl.BlockSpec((1, qb, q_cols), lambda bi, i: (bi, i, 0)),
        out_shape=jax.ShapeDtypeStruct((b, t, q_cols), BF16),
        scratch_shapes=[pltpu.VMEM((n_kv, t, 2 * hd), BF16), pltpu.VMEM((n_kv, n_tiles, hd + V_PAD, kt), BF16),
                        pltpu.VMEM((n_kv, t, hd), BF16), pltpu.VMEM((n_kv, n_tiles, hd + V_PAD, kt), BF16),
                        pltpu.VMEM((n_kv, cmp_rows, hd), BF16), pltpu.VMEM((n_kv, hd, cmp_rows), BF16)],
        compiler_params=_cparams(2), name="attention",
    )(qt, gt, kcv, slc, win)


def _spread_q(q, hd, n_rep):
    n_heads = q.shape[0]
    d = lax.broadcasted_iota(I32, (hd, 4 * hd), 0)
    c = lax.broadcasted_iota(I32, (hd, 4 * hd), 1)
    qb16 = q.astype(BF16)
    row = lax.broadcasted_iota(I32, (n_heads, 4 * hd), 0)
    out = jnp.zeros((n_heads, 4 * hd), F32)
    for gg in range(n_heads // n_rep):
        placed = _dot(qb16, (c == d + gg * hd).astype(BF16))
        out = jnp.where(_shr(row, n_rep) == gg, placed, out)
    return out.astype(BF16)


def _masked_softmax_rows(s, mask):
    s = jnp.where(mask, s, NEG)
    e = jnp.where(mask, jnp.exp(s - jnp.max(s, axis=-1, keepdims=True)), 0.0)
    den = jnp.sum(e, axis=-1, keepdims=True)
    return e * (1.0 / jnp.where(den > 0, den, 1.0))


def _group_value_lanes(o_full, hd, n_rep):
    row = lax.broadcasted_iota(I32, (o_full.shape[0], hd), 0)
    out = o_full[:, 2 * hd:3 * hd]
    for gg in range(1, o_full.shape[0] // n_rep):
        out = jnp.where(_shr(row, n_rep) == gg, o_full[:, (2 + gg) * hd:(3 + gg) * hd], out)
    return out


def _cmp_select_dec_kernel(q_ref, kcv_ref, o_ref, idx_ref, pool_ref, score_ref, *, hd, n_rep, q_pos, n_blocks,
                           blk_lanes):
    n_heads = q_ref.shape[1]
    n_rows = kcv_ref.shape[1]
    b = pl.program_id(0)
    n_seq = idx_ref.shape[0]

    @pl.when(b == 0)
    def _():
        ratio = SEL_BLOCK // CMP_STRIDE
        pm = lax.broadcasted_iota(I32, (n_rows, blk_lanes), 0)
        pj = lax.broadcasted_iota(I32, (n_rows, blk_lanes), 1)
        pool_ref[...] = ((pm >= 1) & (pm >= ratio * pj) & (pm <= ratio * pj + ratio)).astype(BF16)

    q2 = _spread_q(q_ref[0], hd, n_rep)
    kcv = kcv_ref[0].astype(BF16)
    m_idx = lax.broadcasted_iota(I32, (n_heads, n_rows), 1)
    vis = (m_idx >= 1) & ((m_idx - 1) * CMP_STRIDE + 2 * CMP_STRIDE - 1 <= q_pos)
    p = _masked_softmax_rows(_dot_nt(q2, kcv), vis)
    o_ref[0] = _group_value_lanes(_dot(p.astype(BF16), kcv), hd, n_rep)

    row = lax.broadcasted_iota(I32, (n_heads, n_rows), 0)
    grp = jnp.zeros((n_heads, n_rows), F32)
    for gg in range(n_heads // n_rep):
        tot = jnp.sum(jnp.where(_shr(row, n_rep) == gg, p, 0.0), axis=0, keepdims=True)
        grp = jnp.where(row == gg, tot, grp)
    hi = grp.astype(BF16).astype(F32)
    mid = (grp - hi).astype(BF16).astype(F32)
    lo = grp - hi - mid
    pieces = _dot(jnp.concatenate([hi, mid, lo], axis=0).astype(BF16), pool_ref[...])
    imp = pieces[0:n_heads] + pieces[n_heads:2 * n_heads] + pieces[2 * n_heads:]
    score_ref[b] = _block_scores(imp, lax.broadcasted_iota(I32, (n_heads, blk_lanes), 1), q_pos, n_blocks)

    @pl.when(b == n_seq - 1)
    def _():
        work = score_ref[...].reshape(n_seq * n_heads, blk_lanes)
        blk_f = lax.broadcasted_iota(I32, work.shape, 1).astype(F32)
        out_lane = lax.broadcasted_iota(I32, (n_seq * n_heads, LANES), 1)
        out = jnp.full((n_seq * n_heads, LANES), -1, I32)
        for it in range(min(N_SEL, n_blocks)):
            mx = jnp.max(work, axis=-1, keepdims=True)
            first = jnp.min(jnp.where(work == mx, blk_f, float(blk_lanes)), axis=-1, keepdims=True)
            out = jnp.where(out_lane == it, jnp.where(mx > NEG / 2, first.astype(I32), -1), out)
            work = jnp.where(blk_f == first, BELOW_NEG, work)
        idx_ref[...] = out.reshape(n_seq, n_heads, LANES)


def _cmp_select_dec(q3, kcv, *, hd, n_rep, q_pos, n_blocks):
    n, n_heads, _ = q3.shape
    blk_lanes = -(-n_blocks // LANES) * LANES
    return pl.pallas_call(
        functools.partial(_cmp_select_dec_kernel, hd=hd, n_rep=n_rep, q_pos=q_pos, n_blocks=n_blocks,
                          blk_lanes=blk_lanes),
        grid=(n,),
        in_specs=[pl.BlockSpec((1, n_heads, hd), lambda i: (i, 0, 0)),
                  pl.BlockSpec((1,) + kcv.shape[1:], lambda i: (i, 0, 0))],
        out_specs=[pl.BlockSpec((1, n_heads, hd), lambda i: (i, 0, 0)),
                   _const_spec((n, n_heads, LANES))],
        out_shape=[jax.ShapeDtypeStruct((n, n_heads, hd), F32), jax.ShapeDtypeStruct((n, n_heads, LANES), I32)],
        scratch_shapes=[pltpu.VMEM((kcv.shape[1], blk_lanes), BF16), pltpu.VMEM((n, n_heads, blk_lanes), F32)],
        compiler_params=_cparams(1), name="cmp_select_dec",
    )(q3, kcv)


def _attend_dec_kernel(pt_ref, sel_ref, *refs, hd, n_rep, n_sel, n_past_blocks):
    n_kv = (len(refs) - 7) // n_sel
    blk_refs = refs[:n_kv * n_sel]
    q_ref, new_slc_ref, new_win_ref, cwin_ref, g_ref, ocmp_ref, o_ref = refs[n_kv * n_sel:]
    b = pl.program_id(0)
    n_heads = q_ref.shape[1]
    q2 = _spread_q(q_ref[0], hd, n_rep)
    q2f = q2.astype(F32)
    head_grp = _shr(lax.broadcasted_iota(I32, (n_heads, 1), 0), n_rep)

    def with_new_key(s, mask, keys_t, new_row, new_ok):
        nr = new_row.astype(BF16).astype(F32)
        s_new = jnp.sum(q2f * nr, axis=-1, keepdims=True)
        s = jnp.where(mask, s, NEG)
        s_new = jnp.where(new_ok, s_new, NEG)
        m = jnp.maximum(jnp.max(s, axis=-1, keepdims=True), s_new)
        e = jnp.where(mask, jnp.exp(s - m), 0.0)
        e_new = jnp.where(new_ok, jnp.exp(s_new - m), 0.0)
        den = jnp.sum(e, axis=-1, keepdims=True) + e_new
        inv = 1.0 / jnp.where(den > 0, den, 1.0)
        return _dot_nt((e * inv).astype(BF16), keys_t) + (e_new * inv) * nr

    page = blk_refs[0].shape[2]
    per = page // SEL_BLOCK
    keys_t = jnp.concatenate([r[0] for r in blk_refs], axis=1).astype(BF16)
    n_keys = keys_t.shape[1]
    col_slot = _shr(lax.broadcasted_iota(I32, (1, n_keys), 1), page)
    page_blk = _shr(lax.broadcasted_iota(I32, (1, page), 1), SEL_BLOCK)
    slot_ok = []
    new_ok = jnp.zeros((n_heads, 1), I32)
    for gg in range(n_kv):
        for k in range(n_sel):
            idx = sel_ref[b, gg * n_sel + k]
            cached = ((idx >= 0) & (idx < n_past_blocks)).astype(I32)
            slot_ok.append(jnp.where(page_blk == (idx & (per - 1)), cached, 0))
            new_ok = jnp.where(head_grp == gg, new_ok | (idx == n_past_blocks).astype(I32), new_ok)
    mask = (jnp.concatenate(slot_ok, axis=1) > 0) & (_shr(col_slot, n_sel) == head_grp)
    o_slc = _group_value_lanes(with_new_key(_dot(q2, keys_t), mask, keys_t, new_slc_ref[0], new_ok > 0), hd, n_rep)

    keys_t = cwin_ref[0].astype(BF16)
    all_ok = jnp.full((n_heads, keys_t.shape[1]), True)
    o_win = _group_value_lanes(
        with_new_key(_dot(q2, keys_t), all_ok, keys_t, new_win_ref[0], jnp.full((n_heads, 1), True)), hd, n_rep)

    gates = g_ref[0]
    o_ref[0] = gates[:, 0:1] * ocmp_ref[0] + gates[:, 1:2] * o_slc + gates[:, 2:3] * o_win


def _attend_dec(page_table, sel, slc_pages, q3, new_slc, new_win, cache_win, gates3, o_cmp, *, hd, n_rep, n_sel,
                n_past_blocks):
    n, n_heads, _ = q3.shape
    n_kv = n_heads // n_rep
    per = n_past_blocks // page_table.shape[1]

    def blk_map(i, pt, sl, slot):
        idx = jnp.clip(sl[i, slot], 0, n_past_blocks - 1)
        return (pt[i, _shr(idx, per)], 0, 0)

    blk_specs = [pl.BlockSpec((1,) + slc_pages.shape[1:], functools.partial(blk_map, slot=s))
                 for s in range(n_kv * n_sel)]

    def row_spec(shape):
        nd = len(shape)
        return pl.BlockSpec((1,) + tuple(shape[1:]), lambda i, pt, sl: (i,) + (0,) * (nd - 1))

    others = (q3, new_slc, new_win, cache_win, gates3, o_cmp)
    return pl.pallas_call(
        functools.partial(_attend_dec_kernel, hd=hd, n_rep=n_rep, n_sel=n_sel, n_past_blocks=n_past_blocks),
        grid_spec=pltpu.PrefetchScalarGridSpec(
            num_scalar_prefetch=2, grid=(n,),
            in_specs=blk_specs + [row_spec(a.shape) for a in others],
            out_specs=row_spec(o_cmp.shape)),
        out_shape=jax.ShapeDtypeStruct(o_cmp.shape, F32),
        compiler_params=_cparams(1), name="attend_dec",
    )(page_table, sel, *([slc_pages] * (n_kv * n_sel)), *others)


def _mix_kernel(x_ref, o_ref, nw_ref, wuv_ref, wgate_ref, gn_ref, ws_ref, bs_ref, wpa_ref, wpb_ref, wout_ref,
                x1_ref, v_ref, *, width, chunk, single_pos):
    x = x_ref[0]
    d = x.shape[-1]
    h = _rmsnorm(x, nw_ref[...]).astype(BF16)
    uv = jax.nn.gelu(_dot_nt(h, wuv_ref[...]))
    u, vn = uv[:, 0:width], _rmsnorm(uv[:, width:2 * width], gn_ref[...])
    gates = jax.nn.sigmoid(_dot_nt(h, wgate_ref[...]))
    gw = width // GMLP_GROUPS
    if single_pos:
        v_ref[0] = vn
        mixed = u * (ws_ref[...] * vn + bs_ref[...])
    else:
        rows = x.shape[0]
        v_ref[0] = vn[rows - chunk:rows, :]
        tri = lax.broadcasted_iota(I32, (chunk, chunk), 0) >= lax.broadcasted_iota(I32, (chunk, chunk), 1)
        vb = vn.astype(BF16)
        pieces = []
        for c in range(rows // chunk):
            zs = []
            for gi in range(GMLP_GROUPS):
                wm = jnp.where(tri, ws_ref[gi], 0.0).astype(BF16)
                zs.append(_dot(wm, vb[c * chunk:(c + 1) * chunk, gi * gw:(gi + 1) * gw]) + bs_ref[:, gi:gi + 1])
            pieces.append(jnp.concatenate(zs, axis=1))
        mixed = u * jnp.concatenate(pieces, axis=0)
    br_a = _dot(o_ref[0].astype(BF16), wpa_ref[...])
    br_b = _dot(mixed.astype(BF16), wpb_ref[...])
    merged = gates[:, 0:d] * br_a + gates[:, d:2 * d] * br_b
    x1_ref[0] = x + _dot(merged.astype(BF16), wout_ref[...])


def _mix(x, o_nsa, nw, wuv, wgate, gn, ws, bs, wpa, wpb, wout, *, tm, chunk, single_pos):
    b, t, d = x.shape
    width = wuv.shape[0] // 2
    v_rows = tm if single_pos else chunk
    weights = (nw, wuv, wgate, gn, ws, bs, wpa, wpb, wout)
    return pl.pallas_call(
        functools.partial(_mix_kernel, width=width, chunk=chunk, single_pos=single_pos),
        grid=(b, t // tm),
        in_specs=[pl.BlockSpec((1, tm, d), lambda i, j: (i, j, 0)),
                  pl.BlockSpec((1, tm, o_nsa.shape[-1]), lambda i, j: (i, j, 0))]
                 + [_const_spec(a.shape) for a in weights],
        out_specs=[pl.BlockSpec((1, tm, d), lambda i, j: (i, j, 0)),
                   pl.BlockSpec((1, v_rows, width), lambda i, j: (i, 0, 0))],
        out_shape=[jax.ShapeDtypeStruct((b, t, d), F32), jax.ShapeDtypeStruct((b, v_rows, width), F32)],
        compiler_params=_cparams(2), name="mix_dec" if single_pos else "mix",
    )(x, o_nsa, *weights)


def _ffn_kernel(x1_ref, prev_ref, nf_ref, wup_ref, cw_ref, cb_ref, wdown_ref, nfin_ref, y_ref, a_ref,
                *, d_ff, f_tile, halo, single_pos):
    x1 = x1_ref[0]
    rows = x1.shape[0]
    if single_pos:
        h = _rmsnorm(x1, nf_ref[...]).astype(BF16)
    else:
        h = _rmsnorm(jnp.concatenate([prev_ref[0], x1], axis=0), nf_ref[...]).astype(BF16)
        ext_row = lax.broadcasted_iota(I32, (rows + halo, f_tile), 0)
        first = pl.program_id(1) == 0
    y = jnp.zeros_like(x1)
    for f0 in range(0, d_ff, f_tile):
        cols = slice(f0, f0 + f_tile)
        a = _dot(h, wup_ref[:, cols])
        bgate = _dot(h, wup_ref[:, d_ff + f0:d_ff + f0 + f_tile])
        if single_pos:
            a_ref[0, :, cols] = a
            c = cb_ref[:, cols] + prev_ref[0, :, cols] * cw_ref[0:1, cols] + prev_ref[1, :, cols] * cw_ref[1:2, cols] \
                + a * cw_ref[2:3, cols]
        else:
            a = jnp.where((ext_row < halo) & first, 0.0, a)
            back2 = pltpu.roll(a, 2, 0)
            a_ref[0, :, cols] = back2[0:2, :]
            c = cb_ref[:, cols] + back2[halo:, :] * cw_ref[0:1, cols] \
                + pltpu.roll(a, 1, 0)[halo:, :] * cw_ref[1:2, cols] + a[halo:, :] * cw_ref[2:3, cols]
            bgate = bgate[halo:, :]
        y = y + _dot((jax.nn.gelu(c) * bgate).astype(BF16), wdown_ref[cols, :])
    y_ref[0] = _rmsnorm(x1 + y, nfin_ref[...])


def _ffn(x1, prev, nf, wup, cw, cb, wdown, nfin, *, tm, f_tile, single_pos):
    b, t, d = x1.shape
    d_ff = wdown.shape[0]
    halo = 8
    weights = (nf, wup, cw, cb, wdown, nfin)
    if single_pos:
        prev_spec = _const_spec(prev.shape)
        a_rows = tm
    else:
        per = tm // halo
        prev_spec = pl.BlockSpec((1, halo, d), lambda i, j: (i, jnp.maximum(j * per - 1, 0), 0))
        a_rows = 2
    return pl.pallas_call(
        functools.partial(_ffn_kernel, d_ff=d_ff, f_tile=f_tile, halo=halo, single_pos=single_pos),
        grid=(b, t // tm),
        in_specs=[pl.BlockSpec((1, tm, d), lambda i, j: (i, j, 0)), prev_spec] + [_const_spec(a.shape) for a in weights],
        out_specs=[pl.BlockSpec((1, tm, d), lambda i, j: (i, j, 0)),
                   pl.BlockSpec((1, a_rows, d_ff), lambda i, j: (i, 0, 0))],
        out_shape=[jax.ShapeDtypeStruct((b, t, d), F32), jax.ShapeDtypeStruct((b, a_rows, d_ff), F32)],
        compiler_params=_cparams(2), name="ffn_dec" if single_pos else "ffn",
    )(x1, prev, *weights)


def _compress_params(pe, w1, b1, w2, n_kv):
    cmp_len, hd = pe.shape[1], pe.shape[2]
    hid = w1.shape[2]
    halves = cmp_len // CMP_STRIDE
    eye = jnp.eye(n_kv, dtype=w1.dtype)
    pe_t = jnp.broadcast_to(pe.reshape(2, halves, CMP_STRIDE, 1, hd), (2, halves, CMP_STRIDE, n_kv, hd))
    pe_t = pe_t.reshape(2, halves, 1, CMP_STRIDE * n_kv * hd)
    w1h = w1.reshape(2, halves, CMP_STRIDE, hd, hid)
    w1b = jnp.einsum('krsdh,gf->ksgdrfh', w1h, eye).reshape(2, CMP_STRIDE * n_kv * hd, halves * n_kv * hid)
    b1t = jnp.tile(b1, (1, n_kv)).reshape(2, 1, n_kv * hid)
    w2b = jnp.einsum('khd,gf->kghfd', w2, eye).reshape(2, n_kv * hid, n_kv * hd)
    return pe_t, w1b.astype(BF16), b1t, w2b.astype(BF16)


def kernel(x_prompt, x_sample, cache_cmp, cache_slc, cache_win, state_conv, page_table, norm_mix, w_in, cmp_pe,
           cmp_w1, cmp_b1, cmp_w2, gmlp_norm, gmlp_ws, gmlp_bs, w_proj_a, w_proj_b, w_out, norm_ffn, w_up, conv_w,
           conv_b, w_down, norm_final):
    depth = w_in.shape[0]
    assert depth == 1, "single-layer step"
    bp, t, d = x_prompt.shape
    bd, tn, _ = x_sample.shape
    assert tn == 1
    n_kv, hd = cache_cmp.shape[4], cache_cmp.shape[5]
    page = cache_cmp.shape[2]
    q_cols = w_proj_a.shape[1]
    n_heads = q_cols // hd
    n_rep = n_heads // n_kv
    kv_cols = 2 * n_kv * hd
    width = w_proj_b.shape[1]
    chunk = gmlp_ws.shape[-1]
    d_ff = w_down.shape[1]
    n_pages = page_table.shape[1]
    past_len = n_pages * page
    scale = hd ** -0.5
    assert conv_w.shape[1] == 3 and cache_win.shape[2] <= WINDOW and past_len % SEL_BLOCK == 0

    wit = w_in[0].T.astype(BF16)
    off_kv, off_g = q_cols, q_cols + 3 * kv_cols
    off_uv = off_g + 3 * n_heads
    off_gate = off_uv + 2 * width
    w_qt, w_kvt, w_g = wit[0:off_kv], wit[off_kv:off_g], wit[off_g:off_uv]
    w_uv, w_gate = wit[off_uv:off_gate], wit[off_gate:]
    g_rows = 16
    w_gt = jnp.pad(w_g.reshape(n_kv, 3 * n_rep, d), ((0, 0), (0, g_rows - 3 * n_rep), (0, 0))).reshape(n_kv * g_rows, d)
    w_dec = jnp.concatenate([w_qt, w_kvt, jnp.pad(w_g, ((0, LANES - 3 * n_heads), (0, 0)))], axis=0)
    nm, nf, nfin, gn = norm_mix[0][None], norm_ffn[0][None], norm_final[None], gmlp_norm[0][None]
    pe_t, w1b, b1t, w2b = _compress_params(cmp_pe[0], cmp_w1[0], cmp_b1[0], cmp_w2[0], n_kv)
    wpa, wpb, wout = w_proj_a[0].astype(BF16), w_proj_b[0].astype(BF16), w_out[0].astype(BF16)
    wup, wdown = w_up[0].astype(BF16), w_down[0].astype(BF16)
    cw, cb = conv_w[0], conv_b[0][None]
    ws, bs = gmlp_ws[0], gmlp_bs[0]

    cmp_t, slc_t, win_t, qt, gt = _front(x_prompt, nm, w_kvt, w_qt, w_gt, kv_cols=kv_cols, scale=scale * LOG2_E,
                                         tm=512)
    kcv = _compress([cmp_t], pe_t, w1b, b1t, w2b, n_seq=bp, steps=1, rows=t // CMP_STRIDE)
    o_nsa = _attention(qt, gt, kcv, slc_t, win_t, n_rep=n_rep, hd=hd)
    x1, v_p = _mix(x_prompt, o_nsa, nm, w_uv, w_gate, gn, ws, bs.T, wpa, wpb, wout,
                   tm=256, chunk=chunk, single_pos=False)
    y_p, conv_p = _ffn(x1, x1, nf, wup, cw, cb, wdown, nfin, tm=256, f_tile=d_ff // 2, single_pos=False)

    xs = x_sample.reshape(bd, d)
    q_s, kv_s, g_s = _front_dec(xs, nm, w_dec, q_cols=q_cols, kv_cols3=3 * kv_cols, scale=scale)
    kv_cmp_s, kv_slc_s, kv_win_s = kv_s[:, 0:kv_cols], kv_s[:, kv_cols:2 * kv_cols], kv_s[:, 2 * kv_cols:]
    def positions_last(c):
        return jnp.transpose(c, (0, 2, 3, 4, 1)).reshape(c.shape[0], kv_cols, c.shape[1])

    cmp_pages, slc_pages, win_rows = positions_last(cache_cmp[0]), positions_last(cache_slc[0]), positions_last(cache_win[0])
    pages_per_step = 32
    kcv_s = _compress([cmp_pages] * pages_per_step, pe_t, w1b, b1t, w2b, n_seq=bd,
                      steps=n_pages // pages_per_step, rows=pages_per_step * page // CMP_STRIDE,
                      page_table=page_table, pages_per_step=pages_per_step)
    q3 = q_s.reshape(bd, n_heads, hd)
    n_blocks_s = -(-(past_len + tn) // SEL_BLOCK)
    o_cmp_s, idx_s = _cmp_select_dec(q3, kcv_s, hd=hd, n_rep=n_rep, q_pos=past_len, n_blocks=n_blocks_s)
    sel = idx_s[:, 0:n_kv, 0:N_SEL].reshape(bd, n_kv * N_SEL)
    o_nsa_s = _attend_dec(page_table, sel, slc_pages, q3, kv_slc_s[:, None, :], kv_win_s[:, None, :], win_rows,
                          g_s[:, 0:3 * n_heads].reshape(bd, n_heads, 3), o_cmp_s,
                          hd=hd, n_rep=n_rep, n_sel=N_SEL, n_past_blocks=past_len // SEL_BLOCK)
    gw = width // GMLP_GROUPS
    ws0 = jnp.repeat(ws[:, 0, 0], gw)[None]
    bs0 = jnp.repeat(bs[:, 0], gw)[None]
    x1_s, v_s = _mix(xs[None], o_nsa_s.reshape(1, bd, q_cols), nm, w_uv, w_gate, gn, ws0, bs0, wpa, wpb, wout,
                     tm=bd, chunk=chunk, single_pos=True)
    prev_s = jnp.swapaxes(state_conv[0], 0, 1)
    y_s, a_s = _ffn(x1_s, prev_s, nf, wup, cw, cb, wdown, nfin, tm=bd, f_tile=d_ff // 2, single_pos=True)

    def rows6(a_t):
        n, _, npos = a_t.shape
        return jnp.transpose(a_t.reshape(n, 2, n_kv, hd, npos), (0, 4, 1, 2, 3))[None]

    win_keep = min(WINDOW, t)
    win_keep_s = min(WINDOW, cache_win.shape[2] + tn)
    win_s = jnp.concatenate([win_rows, kv_win_s[:, :, None]], axis=2)[:, :, cache_win.shape[2] + tn - win_keep_s:]
    conv_s = jnp.concatenate([state_conv[0][:, 1:], a_s[0][:, None, :]], axis=1)
    return (y_p, y_s.reshape(bd, tn, d),
            rows6(cmp_t), rows6(slc_t), rows6(win_t[:, :, t - win_keep:]),
            v_p[None], conv_p[None],
            rows6(kv_cmp_s[:, :, None]), rows6(kv_slc_s[:, :, None]), rows6(win_s),
            v_s.reshape(1, bd, tn, width), conv_s[None])
```

```python
import functools

import jax
import jax.numpy as jnp
from jax import lax
from jax.experimental import pallas as pl
from jax.experimental.pallas import tpu as pltpu

F32 = jnp.float32
BF16 = jnp.bfloat16
I32 = jnp.int32

CMP_STRIDE = 16
SEG_PITCH = 24
PAGE_LOOKAHEAD = 2
PAGE_SLOTS = PAGE_LOOKAHEAD + 1
SEL_BLOCK = 64
N_SEL = 16
N_LOCAL_SEL = 2
WINDOW = 512
Q_BLOCK = 256
KEY_TILE = 128
SLC_TILES = 4
V_PAD = 16
LOG2_E = 1.4426950408889634
GMLP_GROUPS = 4
EPS = 1e-6
NEG = -1e30
BELOW_NEG = -3e38
SEL_BONUS = 1e6

V7X_VMEM_BYTES = 64 * 1024 * 1024
VMEM_REQUEST_BYTES = 56 * 1024 * 1024
LANES = 128


def _cparams(n_grid):
    return pltpu.CompilerParams(
        dimension_semantics=("arbitrary",) * n_grid, vmem_limit_bytes=VMEM_REQUEST_BYTES)


def _rmsnorm(x, g):
    ms = jnp.mean(x * x, axis=-1, keepdims=True)
    return x * lax.rsqrt(ms + EPS) * g


def _dot(a, b):
    return jnp.dot(a, b, preferred_element_type=F32)


def _dot_nt(a, b):
    return lax.dot_general(a, b, (((1,), (1,)), ((), ())), preferred_element_type=F32)


def _dot_exact(a, b):
    return jnp.dot(a, b, precision=lax.Precision.HIGHEST, preferred_element_type=F32)


def _shr(x, n):
    assert n & (n - 1) == 0
    return x >> (n.bit_length() - 1)


def _const_spec(shape):
    nd = len(shape)
    return pl.BlockSpec(shape, lambda *_: (0,) * nd)


def _front_kernel(x_ref, nw_ref, wkvt_ref, wqt_ref, wgt_ref, cmp_ref, slc_ref, win_ref, qt_ref, gt_ref,
                  *, kv_cols, scale):
    h = _rmsnorm(x_ref[0], nw_ref[...]).astype(BF16)
    kvt = _dot_nt(wkvt_ref[...], h)
    cmp_ref[0] = kvt[0:kv_cols]
    slc_ref[0] = kvt[kv_cols:2 * kv_cols]
    win_ref[0] = kvt[2 * kv_cols:3 * kv_cols]
    qt_ref[0] = (_dot_nt(wqt_ref[...], h) * scale).astype(BF16)
    gt_ref[0] = jax.nn.sigmoid(_dot_nt(wgt_ref[...], h))


def _front(x, nw, wkvt, wqt, wgt, *, kv_cols, scale, tm):
    b, t, d = x.shape
    q_cols, g_rows = wqt.shape[0], wgt.shape[0]
    kv_shape = jax.ShapeDtypeStruct((b, kv_cols, t), F32)
    kv_spec = pl.BlockSpec((1, kv_cols, tm), lambda i, j: (i, 0, j))
    return pl.pallas_call(
        functools.partial(_front_kernel, kv_cols=kv_cols, scale=scale),
        grid=(b, t // tm),
        in_specs=[pl.BlockSpec((1, tm, d), lambda i, j: (i, j, 0)), _const_spec(nw.shape),
                  _const_spec(wkvt.shape), _const_spec(wqt.shape), _const_spec(wgt.shape)],
        out_specs=[kv_spec, kv_spec, kv_spec,
                   pl.BlockSpec((1, q_cols, tm), lambda i, j: (i, 0, j)),
                   pl.BlockSpec((1, g_rows, tm), lambda i, j: (i, 0, j))],
        out_shape=[kv_shape, kv_shape, kv_shape,
                   jax.ShapeDtypeStruct((b, q_cols, t), BF16),
                   jax.ShapeDtypeStruct((b, g_rows, t), F32)],
        compiler_params=_cparams(2), name="front",
    )(x, nw, wkvt, wqt, wgt)


def _front_dec_kernel(x_ref, nw_ref, w_ref, q_ref, kv_ref, g_ref, *, q_cols, kv_cols3, scale):
    h = _rmsnorm(x_ref[...], nw_ref[...]).astype(BF16)
    z = _dot_nt(h, w_ref[...])
    q_ref[...] = z[:, 0:q_cols] * scale
    kv_ref[...] = z[:, q_cols:q_cols + kv_cols3]
    g_ref[...] = jax.nn.sigmoid(z[:, q_cols + kv_cols3:])


def _front_dec(x, nw, w, *, q_cols, kv_cols3, scale):
    n = x.shape[0]
    g_cols = w.shape[0] - q_cols - kv_cols3
    return pl.pallas_call(
        functools.partial(_front_dec_kernel, q_cols=q_cols, kv_cols3=kv_cols3, scale=scale),
        grid=(1,),
        in_specs=[_const_spec(x.shape), _const_spec(nw.shape), _const_spec(w.shape)],
        out_specs=[_const_spec((n, q_cols)), _const_spec((n, kv_cols3)), _const_spec((n, g_cols))],
        out_shape=[jax.ShapeDtypeStruct((n, q_cols), F32), jax.ShapeDtypeStruct((n, kv_cols3), F32),
                   jax.ShapeDtypeStruct((n, g_cols), F32)],
        compiler_params=_cparams(1), name="front_dec",
    )(x, nw, w)


def _compress_tiles(tile_of, n_tiles, pe_ref, w1_ref, b1_ref, w2_ref, out_ref, carry_ref, pos_ref, *, stride, half,
                    hid2):
    seg_per_tile = LANES // stride
    rows = n_tiles * seg_per_tile
    row = lax.broadcasted_iota(I32, (rows, hid2), 0)
    for kv in range(2):
        for ti in range(n_tiles):
            t = tile_of(ti, kv).T
            for n in range(seg_per_tile):
                p0 = (ti * seg_per_tile + n) * SEG_PITCH
                pos_ref[kv, p0:p0 + stride, :] = t[n * stride:(n + 1) * stride, :]
        xkv = jnp.concatenate(
            [pos_ref[kv, pl.ds(s, rows, stride=SEG_PITCH), :] for s in range(stride)],
            axis=1)
        parts = []
        for r in range(2):
            a = (xkv + pe_ref[kv, r]).astype(BF16)
            parts.append(_dot(a, w1_ref[kv, :, r * hid2:(r + 1) * hid2]))
        prev = carry_ref[kv, 0:1, :]
        shifted = jnp.where(row == 0, prev, pltpu.roll(parts[0], 1, 0))
        carry_ref[kv, 0:1, :] = parts[0][rows - 1:rows, :]
        hid = b1_ref[kv] + shifted + parts[1]
        out_ref[0, :, kv * half:(kv + 1) * half] = _dot(jax.nn.gelu(hid).astype(BF16), w2_ref[kv])


def _compress_kernel(x_ref, pe_ref, w1_ref, b1_ref, w2_ref, out_ref, carry_ref, pos_ref, *, stride, kv_cols, hid2):
    half = kv_cols // 2
    assert half == LANES

    @pl.when(pl.program_id(1) == 0)
    def _():
        carry_ref[...] = jnp.zeros_like(carry_ref)

    _compress_tiles(lambda ti, kv: x_ref[0, kv * half:(kv + 1) * half, ti * LANES:(ti + 1) * LANES],
                    x_ref.shape[2] // LANES, pe_ref, w1_ref, b1_ref, w2_ref, out_ref, carry_ref, pos_ref,
                    stride=stride, half=half, hid2=hid2)


def _compress_paged_kernel(pt_ref, pages_ref, pe_ref, w1_ref, b1_ref, w2_ref, out_ref, carry_ref, pos_ref, buf_ref,
                           sem, *, stride, kv_cols, hid2, n_pages):
    half = kv_cols // 2
    assert half == LANES and buf_ref.shape[3] == LANES
    steps = pl.num_programs(1)
    total = pl.num_programs(0) * steps
    t = pl.program_id(0) * steps + pl.program_id(1)

    def page_copy(src_step, ring_step, k):
        page = pt_ref[src_step // steps, (src_step % steps) * n_pages + k]
        slot = ring_step % PAGE_SLOTS
        return pltpu.make_async_copy(pages_ref.at[page], buf_ref.at[slot, k], sem.at[slot])

    @pl.when(t == 0)
    def _():
        for ahead in range(PAGE_LOOKAHEAD):
            for k in range(n_pages):
                page_copy(jnp.minimum(ahead, total - 1), ahead, k).start()

    @pl.when(pl.program_id(1) == 0)
    def _():
        carry_ref[...] = jnp.zeros_like(carry_ref)

    for k in range(n_pages):
        page_copy(t, t, k).wait()
    slot = t % PAGE_SLOTS
    _compress_tiles(lambda ti, kv: buf_ref[slot, ti, kv * half:(kv + 1) * half, :],
                    n_pages, pe_ref, w1_ref, b1_ref, w2_ref, out_ref, carry_ref, pos_ref,
                    stride=stride, half=half, hid2=hid2)
    for k in range(n_pages):
        page_copy(jnp.minimum(t + PAGE_LOOKAHEAD, total - 1), t + PAGE_LOOKAHEAD, k).start()

    @pl.when(t == total - 1)
    def _():
        for ahead in range(1, PAGE_LOOKAHEAD + 1):
            for k in range(n_pages):
                page_copy(t, t + ahead, k).wait()


def _compress(x, pe, w1, b1, w2, *, n_seq, steps, rows, page_table=None):
    stride, kv_cols = CMP_STRIDE, w2.shape[2] * 2
    hid2 = b1.shape[2]
    static = dict(stride=stride, kv_cols=kv_cols, hid2=hid2)
    out_shape = jax.ShapeDtypeStruct((n_seq, steps * rows, kv_cols), F32)
    scratch = [pltpu.VMEM((2, 8, hid2), F32), pltpu.VMEM((2, rows * SEG_PITCH, kv_cols // 2), F32)]
    weights = (pe, w1, b1, w2)
    if page_table is None:
        return pl.pallas_call(
            functools.partial(_compress_kernel, **static), grid=(n_seq, steps),
            in_specs=[pl.BlockSpec((1, kv_cols, rows * stride), lambda i, j: (i, 0, j))]
                     + [_const_spec(a.shape) for a in weights],
            out_specs=pl.BlockSpec((1, rows, kv_cols), lambda i, j: (i, j, 0)),
            out_shape=out_shape, scratch_shapes=scratch, compiler_params=_cparams(2), name="compress",
        )(x, *weights)
    page = x.shape[2]
    n_pages = rows * stride // page
    assert n_seq * steps > PAGE_LOOKAHEAD
    w_specs = [pl.BlockSpec(a.shape, functools.partial(lambda i, j, pt, nd: (0,) * nd, nd=a.ndim)) for a in weights]
    return pl.pallas_call(
        functools.partial(_compress_paged_kernel, n_pages=n_pages, **static),
        grid_spec=pltpu.PrefetchScalarGridSpec(
            num_scalar_prefetch=1, grid=(n_seq, steps),
            in_specs=[pl.BlockSpec(memory_space=pl.ANY)] + w_specs,
            out_specs=pl.BlockSpec((1, rows, kv_cols), lambda i, j, pt: (i, j, 0)),
            scratch_shapes=scratch + [pltpu.VMEM((PAGE_SLOTS, n_pages, kv_cols, page), F32),
                                      pltpu.SemaphoreType.DMA((PAGE_SLOTS,))]),
        out_shape=out_shape, compiler_params=_cparams(2), name="compress_paged",
    )(page_table, x, *weights)


def _block_scores(imp, blk, q_pos, n_blocks):
    cur = _shr(q_pos, SEL_BLOCK)
    valid = (blk * SEL_BLOCK <= q_pos) & (blk < n_blocks)
    forced = (blk == 0) | ((blk <= cur) & (blk > cur - N_LOCAL_SEL))
    score = jnp.where(valid, imp + jnp.where(forced, SEL_BONUS, 0.0), NEG)
    return jnp.where(blk < n_blocks, score, BELOW_NEG)


def _attn_kernel(qt_ref, gt_ref, kcv_ref, slc_ref, win_ref, o_ref,
                 kaug_ref, vts_ref, kwin_ref, vtw_ref, kc_ref, vct_ref, *, n_kv, n_rep, hd, n_blocks):
    i = pl.program_id(1)
    qb, kt = Q_BLOCK, KEY_TILE
    q_tiles = qb // kt
    n_tiles = slc_ref.shape[2] // kt
    n_cmp_rows = kcv_ref.shape[1]
    nq = n_rep * qb
    kd = n_kv * hd
    g_rows = gt_ref.shape[1] // n_kv
    assert kd == LANES and n_blocks == hd, "the selection bias rows ride in the key one-hot lanes"

    @pl.when(i == 0)
    def _():
        lane = lax.broadcasted_iota(I32, (kt, kd), 1)
        krow = lax.broadcasted_iota(I32, (kt, kd), 0)

        def group_lanes(x, gg):
            return x if gg == 0 else pltpu.roll(x, kd - gg * hd, 1)

        ones_row = (lax.broadcasted_iota(I32, (V_PAD, kt), 0) == 0).astype(BF16)
        for c in range(n_tiles):
            cols = slice(c * kt, (c + 1) * kt)
            onehot = (lane - hd == _shr(c * kt + krow, SEL_BLOCK)).astype(F32)
            kt_s, kt_w = slc_ref[0, 0:kd, cols].T, win_ref[0, 0:kd, cols].T
            for gg in range(n_kv):
                v_rows = slice(kd + gg * hd, kd + (gg + 1) * hd)
                kaug_ref[gg, cols, :] = jnp.where(lane < hd, group_lanes(kt_s, gg), onehot).astype(BF16)
                vts_ref[gg, c, 0:hd, :] = slc_ref[0, v_rows, cols].astype(BF16)
                vts_ref[gg, c, hd:, :] = ones_row
                kwin_ref[gg, cols, :] = group_lanes(kt_w, gg)[:, 0:hd].astype(BF16)
                vtw_ref[gg, c, 0:hd, :] = win_ref[0, v_rows, cols].astype(BF16)
                vtw_ref[gg, c, hd:, :] = ones_row
        for c in range(n_cmp_rows // kt):
            rows = slice(c * kt, (c + 1) * kt)
            blk = kcv_ref[0, rows, :]
            vt = blk[:, kd:2 * kd].T
            for gg in range(n_kv):
                kc_ref[gg, rows, :] = group_lanes(blk[:, 0:kd], gg)[:, 0:hd].astype(BF16)
                vct_ref[gg, :, rows] = vt[gg * hd:(gg + 1) * hd, :].astype(BF16)

    q_pos = i * qb + (lax.broadcasted_iota(I32, (1, nq), 1) & (qb - 1))

    groups = range(n_kv)
    heads = range(n_rep)

    q_minus_k = (lax.broadcasted_iota(I32, (kt, nq), 1) & (qb - 1)) - lax.broadcasted_iota(I32, (kt, nq), 0)

    def softmax_step(m, sc):
        m_new = jnp.maximum(m, jnp.max(sc, axis=0, keepdims=True))
        return m_new, jnp.exp2(m - m_new), jnp.exp2(sc - m_new).astype(BF16)

    def normalised(acc):
        return acc[0:hd, :] * (1.0 / acc[hd:hd + 1, :])

    qts = [qt_ref[0, gg * n_rep * hd:(gg + 1) * n_rep * hd, :] for gg in groups]
    qcats = [jnp.concatenate([qts[gg][h * hd:(h + 1) * hd, :] for h in heads], axis=1) for gg in groups]

    def attend(k_ref, v_ref, qs, tile_ids, state, masked):
        scores = [[_dot(k_ref[gg, pl.ds(pl.multiple_of(t * kt, kt), kt), :], qs[gg]) for gg in groups]
                  for t in tile_ids]
        maxes, accs = list(state[0]), list(state[1])
        for u, t in enumerate(tile_ids):
            for gg in groups:
                maxes[gg], alpha, pb = softmax_step(maxes[gg], masked(u, scores[u][gg]))
                accs[gg] = alpha * accs[gg] + _dot(v_ref[gg, t], pb)
        return tuple(maxes), tuple(accs)

    state0 = ((jnp.full((1, nq), NEG, F32),) * n_kv, (jnp.zeros((hd + V_PAD, nq), F32),) * n_kv)

    s_cmp = [_dot(kc_ref[gg], qcats[gg]) for gg in groups]
    m_idx = lax.broadcasted_iota(I32, (n_cmp_rows, nq), 0)
    vis = (m_idx >= 1) & ((m_idx - 1) * CMP_STRIDE + 2 * CMP_STRIDE - 1 <= q_pos)
    ratio = SEL_BLOCK // CMP_STRIDE
    pj = lax.broadcasted_iota(I32, (n_blocks, n_cmp_rows), 0)
    pm = lax.broadcasted_iota(I32, (n_blocks, n_cmp_rows), 1)
    pool = ((pm >= 1) & (pm >= ratio * pj) & (pm <= ratio * pj + ratio)).astype(F32)
    o_cmp, imp = [], []
    for gg in groups:
        s = jnp.where(vis, s_cmp[gg], NEG)
        e = jnp.where(vis, jnp.exp2(s - jnp.max(s, axis=0, keepdims=True)), 0.0)
        den = jnp.sum(e, axis=0, keepdims=True)
        p = e * (1.0 / jnp.where(den > 0, den, 1.0))
        o_cmp.append(_dot(vct_ref[gg], p.astype(BF16)))
        p_grp = p[:, 0:qb]
        for h in range(1, n_rep):
            p_grp = p_grp + p[:, h * qb:(h + 1) * qb]
        imp.append(_dot_exact(pool, p_grp))

    n_band = WINDOW // kt
    first_t = i * q_tiles - n_band
    win_tiles = [jnp.maximum(first_t + u, 0) for u in range(n_band + q_tiles)]

    def in_window(u, sc):
        dlt = q_minus_k + (n_band - u) * kt
        if u < q_tiles:
            sc = jnp.where(dlt <= WINDOW, sc, NEG)
        if u >= n_band:
            sc = jnp.where(dlt >= 0, sc, NEG)
        if u < n_band:
            sc = sc + jnp.where(first_t + u < 0, NEG, 0.0)
        if u == 0:
            sc = sc + after_importance
        return sc

    after_importance = sum(imp[gg][0:1, 0:1] for gg in groups) * 0.0

    _, accs = attend(kwin_ref, vtw_ref, qcats, win_tiles, state0, in_window)
    o_win = [normalised(accs[gg]) for gg in groups]

    blk_id = lax.broadcasted_iota(I32, (n_blocks, qb), 0)
    blk_f = blk_id.astype(F32)
    qaug = []
    for gg in groups:
        score = _block_scores(imp[gg], blk_id, q_pos[:, 0:qb], n_blocks)
        work, sel = score, jnp.zeros((n_blocks, qb), F32)
        for _ in range(min(N_SEL, n_blocks)):
            mx = jnp.max(work, axis=0, keepdims=True)
            first = jnp.min(jnp.where(work == mx, blk_f, float(n_blocks)), axis=0, keepdims=True)
            pick = blk_f == first
            sel = jnp.where(pick, 1.0, sel)
            work = jnp.where(pick, BELOW_NEG, work)
        bias = jnp.where((sel > 0) & (score > NEG / 2), 0.0, NEG).astype(BF16)
        qaug.append(jnp.concatenate(
            [jnp.concatenate([qts[gg][h * hd:(h + 1) * hd, :], bias], axis=0) for h in heads], axis=1))

    assert n_tiles % SLC_TILES == 0 and SLC_TILES % q_tiles == 0
    n_full = _shr(i * q_tiles, SLC_TILES)
    state = lax.fori_loop(
        0, n_full,
        lambda j, st: attend(kaug_ref, vts_ref, qaug, [j * SLC_TILES + u for u in range(SLC_TILES)], st,
                             lambda u, sc: sc), state0)
    last = n_full * SLC_TILES
    _, accs = attend(kaug_ref, vts_ref, qaug, [last + u for u in range(SLC_TILES)], state,
                     lambda u, sc: jnp.where(q_minus_k + (i * q_tiles - last - u) * kt >= 0, sc, NEG))

    outs = []
    for gg in groups:
        o_slc = normalised(accs[gg])
        for h in heads:
            cols = slice(h * qb, (h + 1) * qb)
            gate = [gt_ref[0, gg * g_rows + 3 * h + br:gg * g_rows + 3 * h + br + 1, :] for br in range(3)]
            outs.append(gate[0] * o_cmp[gg][:, cols] + gate[1] * o_slc[:, cols] + gate[2] * o_win[gg][:, cols])
    o_ref[0] = jnp.concatenate(outs, axis=0).T.astype(BF16)


def _attention(qt, gt, kcv, slc, win, *, n_rep, hd):
    b, q_cols, t = qt.shape
    n_kv = q_cols // (n_rep * hd)
    qb, kt = Q_BLOCK, KEY_TILE
    n_tiles = t // kt
    n_blocks = t // SEL_BLOCK
    kv_cols = slc.shape[1]
    cmp_rows = kcv.shape[1]
    tile_spec = pl.BlockSpec((1, kv_cols, t), lambda bi, i: (bi, 0, 0))
    return pl.pallas_call(
        functools.partial(_attn_kernel, n_kv=n_kv, n_rep=n_rep, hd=hd, n_blocks=n_blocks),
        grid=(b, t // qb),
        in_specs=[pl.BlockSpec((1, q_cols, qb), lambda bi, i: (bi, 0, i)),
                  pl.BlockSpec((1, gt.shape[1], qb), lambda bi, i: (bi, 0, i)),
                  pl.BlockSpec((1,) + kcv.shape[1:], lambda bi, i: (bi, 0, 0)),
                  tile_spec, tile_spec],
        out_specs=pl.BlockSpec((1, qb, q_cols), lambda bi, i: (bi, i, 0)),
        out_shape=jax.ShapeDtypeStruct((b, t, q_cols), BF16),
        scratch_shapes=[pltpu.VMEM((n_kv, t, 2 * hd), BF16), pltpu.VMEM((n_kv, n_tiles, hd + V_PAD, kt), BF16),
                        pltpu.VMEM((n_kv, t, hd), BF16), pltpu.VMEM((n_kv, n_tiles, hd + V_PAD, kt), BF16),
                        pltpu.VMEM((n_kv, cmp_rows, hd), BF16), pltpu.VMEM((n_kv, hd, cmp_rows), BF16)],
        compiler_params=_cparams(2), name="attention",
    )(qt, gt, kcv, slc, win)


def _spread_q(q, hd, n_rep):
    n_heads = q.shape[0]
    d = lax.broadcasted_iota(I32, (hd, 4 * hd), 0)
    c = lax.broadcasted_iota(I32, (hd, 4 * hd), 1)
    qb16 = q.astype(BF16)
    row = lax.broadcasted_iota(I32, (n_heads, 4 * hd), 0)
    out = jnp.zeros((n_heads, 4 * hd), F32)
    for gg in range(n_heads // n_rep):
        placed = _dot(qb16, (c == d + gg * hd).astype(BF16))
        out = jnp.where(_shr(row, n_rep) == gg, placed, out)
    return out.astype(BF16)


def _masked_softmax_rows(s, mask):
    s = jnp.where(mask, s, NEG)
    e = jnp.where(mask, jnp.exp(s - jnp.max(s, axis=-1, keepdims=True)), 0.0)
    den = jnp.sum(e, axis=-1, keepdims=True)
    return e * (1.0 / jnp.where(den > 0, den, 1.0))


def _group_value_lanes(o_full, hd, n_rep):
    row = lax.broadcasted_iota(I32, (o_full.shape[0], hd), 0)
    out = o_full[:, 2 * hd:3 * hd]
    for gg in range(1, o_full.shape[0] // n_rep):
        out = jnp.where(_shr(row, n_rep) == gg, o_full[:, (2 + gg) * hd:(3 + gg) * hd], out)
    return out


def _cmp_select_dec_kernel(q_ref, kcv_ref, o_ref, idx_ref, pool_ref, score_ref, *, hd, n_rep, q_pos, n_blocks,
                           blk_lanes):
    n_heads = q_ref.shape[1]
    n_rows = kcv_ref.shape[1]
    b = pl.program_id(0)
    n_seq = idx_ref.shape[0]

    @pl.when(b == 0)
    def _():
        ratio = SEL_BLOCK // CMP_STRIDE
        pm = lax.broadcasted_iota(I32, (n_rows, blk_lanes), 0)
        pj = lax.broadcasted_iota(I32, (n_rows, blk_lanes), 1)
        pool_ref[...] = ((pm >= 1) & (pm >= ratio * pj) & (pm <= ratio * pj + ratio)).astype(BF16)

    q2 = _spread_q(q_ref[0], hd, n_rep)
    kcv = kcv_ref[0].astype(BF16)
    m_idx = lax.broadcasted_iota(I32, (n_heads, n_rows), 1)
    vis = (m_idx >= 1) & ((m_idx - 1) * CMP_STRIDE + 2 * CMP_STRIDE - 1 <= q_pos)
    p = _masked_softmax_rows(_dot_nt(q2, kcv), vis)
    o_ref[0] = _group_value_lanes(_dot(p.astype(BF16), kcv), hd, n_rep)

    row = lax.broadcasted_iota(I32, (n_heads, n_rows), 0)
    grp = jnp.zeros((n_heads, n_rows), F32)
    for gg in range(n_heads // n_rep):
        tot = jnp.sum(jnp.where(_shr(row, n_rep) == gg, p, 0.0), axis=0, keepdims=True)
        grp = jnp.where(row == gg, tot, grp)
    hi = grp.astype(BF16).astype(F32)
    mid = (grp - hi).astype(BF16).astype(F32)
    lo = grp - hi - mid
    pieces = _dot(jnp.concatenate([hi, mid, lo], axis=0).astype(BF16), pool_ref[...])
    imp = pieces[0:n_heads] + pieces[n_heads:2 * n_heads] + pieces[2 * n_heads:]
    score_ref[b] = _block_scores(imp, lax.broadcasted_iota(I32, (n_heads, blk_lanes), 1), q_pos, n_blocks)

    @pl.when(b == n_seq - 1)
    def _():
        work = score_ref[...].reshape(n_seq * n_heads, blk_lanes)
        blk_f = lax.broadcasted_iota(I32, work.shape, 1).astype(F32)
        out_lane = lax.broadcasted_iota(I32, (n_seq * n_heads, LANES), 1)
        out = jnp.full((n_seq * n_heads, LANES), -1, I32)
        for it in range(min(N_SEL, n_blocks)):
            mx = jnp.max(work, axis=-1, keepdims=True)
            first = jnp.min(jnp.where(work == mx, blk_f, float(blk_lanes)), axis=-1, keepdims=True)
            out = jnp.where(out_lane == it, jnp.where(mx > NEG / 2, first.astype(I32), -1), out)
            work = jnp.where(blk_f == first, BELOW_NEG, work)
        idx_ref[...] = out.reshape(n_seq, n_heads, LANES)


def _cmp_select_dec(q3, kcv, *, hd, n_rep, q_pos, n_blocks):
    n, n_heads, _ = q3.shape
    blk_lanes = -(-n_blocks // LANES) * LANES
    return pl.pallas_call(
        functools.partial(_cmp_select_dec_kernel, hd=hd, n_rep=n_rep, q_pos=q_pos, n_blocks=n_blocks,
                          blk_lanes=blk_lanes),
        grid=(n,),
        in_specs=[pl.BlockSpec((1, n_heads, hd), lambda i: (i, 0, 0)),
                  pl.BlockSpec((1,) + kcv.shape[1:], lambda i: (i, 0, 0))],
        out_specs=[pl.BlockSpec((1, n_heads, hd), lambda i: (i, 0, 0)),
                   _const_spec((n, n_heads, LANES))],
        out_shape=[jax.ShapeDtypeStruct((n, n_heads, hd), F32), jax.ShapeDtypeStruct((n, n_heads, LANES), I32)],
        scratch_shapes=[pltpu.VMEM((kcv.shape[1], blk_lanes), BF16), pltpu.VMEM((n, n_heads, blk_lanes), F32)],
        compiler_params=_cparams(1), name="cmp_select_dec",
    )(q3, kcv)


def _attend_dec_kernel(pt_ref, sel_ref, pages_ref, q_ref, new_slc_ref, new_win_ref, cwin_ref, g_ref, ocmp_ref, o_ref,
                       buf_ref, sem, *, hd, n_rep, n_sel, n_past_blocks):
    b = pl.program_id(0)
    total = pl.num_programs(0)
    n_heads = q_ref.shape[1]
    n_kv = n_heads // n_rep
    n_pages = n_kv * n_sel
    page = buf_ref.shape[3]
    per = page // SEL_BLOCK

    def page_copy(src_seq, ring_step, k):
        idx = jnp.clip(sel_ref[src_seq, k], 0, n_past_blocks - 1)
        slot = ring_step % PAGE_SLOTS
        return pltpu.make_async_copy(pages_ref.at[pt_ref[src_seq, _shr(idx, per)]], buf_ref.at[slot, k], sem.at[slot])

    @pl.when(b == 0)
    def _():
        for ahead in range(PAGE_LOOKAHEAD):
            for k in range(n_pages):
                page_copy(jnp.minimum(ahead, total - 1), ahead, k).start()

    for k in range(n_pages):
        page_copy(b, b, k).wait()
    ring_slot = b % PAGE_SLOTS

    q2 = _spread_q(q_ref[0], hd, n_rep)
    q2f = q2.astype(F32)
    head_grp = _shr(lax.broadcasted_iota(I32, (n_heads, 1), 0), n_rep)

    def with_new_key(s, mask, keys_t, new_row, new_ok):
        nr = new_row.astype(BF16).astype(F32)
        s_new = jnp.sum(q2f * nr, axis=-1, keepdims=True)
        s = jnp.where(mask, s, NEG)
        s_new = jnp.where(new_ok, s_new, NEG)
        m = jnp.maximum(jnp.max(s, axis=-1, keepdims=True), s_new)
        e = jnp.where(mask, jnp.exp(s - m), 0.0)
        e_new = jnp.where(new_ok, jnp.exp(s_new - m), 0.0)
        den = jnp.sum(e, axis=-1, keepdims=True) + e_new
        inv = 1.0 / jnp.where(den > 0, den, 1.0)
        return _dot_nt((e * inv).astype(BF16), keys_t) + (e_new * inv) * nr

    keys_t = jnp.concatenate([buf_ref[ring_slot, k] for k in range(n_pages)], axis=1).astype(BF16)
    n_keys = keys_t.shape[1]
    col_slot = _shr(lax.broadcasted_iota(I32, (1, n_keys), 1), page)
    page_blk = _shr(lax.broadcasted_iota(I32, (1, page), 1), SEL_BLOCK)
    slot_ok = []
    new_ok = jnp.zeros((n_heads, 1), I32)
    for gg in range(n_kv):
        for k in range(n_sel):
            idx = sel_ref[b, gg * n_sel + k]
            cached = ((idx >= 0) & (idx < n_past_blocks)).astype(I32)
            slot_ok.append(jnp.where(page_blk == (idx & (per - 1)), cached, 0))
            new_ok = jnp.where(head_grp == gg, new_ok | (idx == n_past_blocks).astype(I32), new_ok)
    mask = (jnp.concatenate(slot_ok, axis=1) > 0) & (_shr(col_slot, n_sel) == head_grp)
    o_slc = _group_value_lanes(with_new_key(_dot(q2, keys_t), mask, keys_t, new_slc_ref[0], new_ok > 0), hd, n_rep)

    keys_t = cwin_ref[0].astype(BF16)
    all_ok = jnp.full((n_heads, keys_t.shape[1]), True)
    o_win = _group_value_lanes(
        with_new_key(_dot(q2, keys_t), all_ok, keys_t, new_win_ref[0], jnp.full((n_heads, 1), True)), hd, n_rep)

    gates = g_ref[0]
    o_ref[0] = gates[:, 0:1] * ocmp_ref[0] + gates[:, 1:2] * o_slc + gates[:, 2:3] * o_win

    for k in range(n_pages):
        page_copy(jnp.minimum(b + PAGE_LOOKAHEAD, total - 1), b + PAGE_LOOKAHEAD, k).start()

    @pl.when(b == total - 1)
    def _():
        for ahead in range(1, PAGE_LOOKAHEAD + 1):
            for k in range(n_pages):
                page_copy(b, b + ahead, k).wait()


def _attend_dec(page_table, sel, slc_pages, q3, new_slc, new_win, cache_win, gates3, o_cmp, *, hd, n_rep, n_sel,
                n_past_blocks):
    n, n_heads, _ = q3.shape
    n_kv = n_heads // n_rep
    assert n > PAGE_LOOKAHEAD and n_past_blocks * SEL_BLOCK == page_table.shape[1] * slc_pages.shape[2]

    def row_spec(shape):
        nd = len(shape)
        return pl.BlockSpec((1,) + tuple(shape[1:]), lambda i, pt, sl: (i,) + (0,) * (nd - 1))

    others = (q3, new_slc, new_win, cache_win, gates3, o_cmp)
    return pl.pallas_call(
        functools.partial(_attend_dec_kernel, hd=hd, n_rep=n_rep, n_sel=n_sel, n_past_blocks=n_past_blocks),
        grid_spec=pltpu.PrefetchScalarGridSpec(
            num_scalar_prefetch=2, grid=(n,),
            in_specs=[pl.BlockSpec(memory_space=pl.ANY)] + [row_spec(a.shape) for a in others],
            out_specs=row_spec(o_cmp.shape),
            scratch_shapes=[pltpu.VMEM((PAGE_SLOTS, n_kv * n_sel) + slc_pages.shape[1:], F32),
                            pltpu.SemaphoreType.DMA((PAGE_SLOTS,))]),
        out_shape=jax.ShapeDtypeStruct(o_cmp.shape, F32),
        compiler_params=_cparams(1), name="attend_dec",
    )(page_table, sel, slc_pages, *others)


def _mix_kernel(x_ref, o_ref, nw_ref, wuv_ref, wgate_ref, gn_ref, ws_ref, bs_ref, wpa_ref, wpb_ref, wout_ref,
                x1_ref, v_ref, *, width, chunk, single_pos):
    x = x_ref[0]
    d = x.shape[-1]
    h = _rmsnorm(x, nw_ref[...]).astype(BF16)
    uv = jax.nn.gelu(_dot_nt(h, wuv_ref[...]))
    u, vn = uv[:, 0:width], _rmsnorm(uv[:, width:2 * width], gn_ref[...])
    gates = jax.nn.sigmoid(_dot_nt(h, wgate_ref[...]))
    gw = width // GMLP_GROUPS
    if single_pos:
        v_ref[0] = vn
        mixed = u * (ws_ref[...] * vn + bs_ref[...])
    else:
        rows = x.shape[0]
        v_ref[0] = vn[rows - chunk:rows, :]
        tri = lax.broadcasted_iota(I32, (chunk, chunk), 0) >= lax.broadcasted_iota(I32, (chunk, chunk), 1)
        vb = vn.astype(BF16)
        pieces = []
        for c in range(rows // chunk):
            zs = []
            for gi in range(GMLP_GROUPS):
                wm = jnp.where(tri, ws_ref[gi], 0.0).astype(BF16)
                zs.append(_dot(wm, vb[c * chunk:(c + 1) * chunk, gi * gw:(gi + 1) * gw]) + bs_ref[:, gi:gi + 1])
            pieces.append(jnp.concatenate(zs, axis=1))
        mixed = u * jnp.concatenate(pieces, axis=0)
    br_a = _dot(o_ref[0].astype(BF16), wpa_ref[...])
    br_b = _dot(mixed.astype(BF16), wpb_ref[...])
    merged = gates[:, 0:d] * br_a + gates[:, d:2 * d] * br_b
    x1_ref[0] = x + _dot(merged.astype(BF16), wout_ref[...])


def _mix(x, o_nsa, nw, wuv, wgate, gn, ws, bs, wpa, wpb, wout, *, tm, chunk, single_pos):
    b, t, d = x.shape
    width = wuv.shape[0] // 2
    v_rows = tm if single_pos else chunk
    weights = (nw, wuv, wgate, gn, ws, bs, wpa, wpb, wout)
    return pl.pallas_call(
        functools.partial(_mix_kernel, width=width, chunk=chunk, single_pos=single_pos),
        grid=(b, t // tm),
        in_specs=[pl.BlockSpec((1, tm, d), lambda i, j: (i, j, 0)),
                  pl.BlockSpec((1, tm, o_nsa.shape[-1]), lambda i, j: (i, j, 0))]
                 + [_const_spec(a.shape) for a in weights],
        out_specs=[pl.BlockSpec((1, tm, d), lambda i, j: (i, j, 0)),
                   pl.BlockSpec((1, v_rows, width), lambda i, j: (i, 0, 0))],
        out_shape=[jax.ShapeDtypeStruct((b, t, d), F32), jax.ShapeDtypeStruct((b, v_rows, width), F32)],
        compiler_params=_cparams(2), name="mix_dec" if single_pos else "mix",
    )(x, o_nsa, *weights)


def _ffn_kernel(x1_ref, prev_ref, nf_ref, wup_ref, cw_ref, cb_ref, wdown_ref, nfin_ref, y_ref, a_ref,
                *, d_ff, f_tile, halo, single_pos):
    x1 = x1_ref[0]
    rows = x1.shape[0]
    if single_pos:
        h = _rmsnorm(x1, nf_ref[...]).astype(BF16)
    else:
        h = _rmsnorm(jnp.concatenate([prev_ref[0], x1], axis=0), nf_ref[...]).astype(BF16)
        ext_row = lax.broadcasted_iota(I32, (rows + halo, f_tile), 0)
        first = pl.program_id(1) == 0
    y = jnp.zeros_like(x1)
    for f0 in range(0, d_ff, f_tile):
        cols = slice(f0, f0 + f_tile)
        a = _dot(h, wup_ref[:, cols])
        bgate = _dot(h, wup_ref[:, d_ff + f0:d_ff + f0 + f_tile])
        if single_pos:
            a_ref[0, :, cols] = a
            c = cb_ref[:, cols] + prev_ref[0, :, cols] * cw_ref[0:1, cols] + prev_ref[1, :, cols] * cw_ref[1:2, cols] \
                + a * cw_ref[2:3, cols]
        else:
            a = jnp.where((ext_row < halo) & first, 0.0, a)
            back2 = pltpu.roll(a, 2, 0)
            a_ref[0, :, cols] = back2[0:2, :]
            c = cb_ref[:, cols] + back2[halo:, :] * cw_ref[0:1, cols] \
                + pltpu.roll(a, 1, 0)[halo:, :] * cw_ref[1:2, cols] + a[halo:, :] * cw_ref[2:3, cols]
            bgate = bgate[halo:, :]
        y = y + _dot((jax.nn.gelu(c) * bgate).astype(BF16), wdown_ref[cols, :])
    y_ref[0] = _rmsnorm(x1 + y, nfin_ref[...])


def _ffn(x1, prev, nf, wup, cw, cb, wdown, nfin, *, tm, f_tile, single_pos):
    b, t, d = x1.shape
    d_ff = wdown.shape[0]
    halo = 8
    weights = (nf, wup, cw, cb, wdown, nfin)
    if single_pos:
        prev_spec = _const_spec(prev.shape)
        a_rows = tm
    else:
        per = tm // halo
        prev_spec = pl.BlockSpec((1, halo, d), lambda i, j: (i, jnp.maximum(j * per - 1, 0), 0))
        a_rows = 2
    return pl.pallas_call(
        functools.partial(_ffn_kernel, d_ff=d_ff, f_tile=f_tile, halo=halo, single_pos=single_pos),
        grid=(b, t // tm),
        in_specs=[pl.BlockSpec((1, tm, d), lambda i, j: (i, j, 0)), prev_spec] + [_const_spec(a.shape) for a in weights],
        out_specs=[pl.BlockSpec((1, tm, d), lambda i, j: (i, j, 0)),
                   pl.BlockSpec((1, a_rows, d_ff), lambda i, j: (i, 0, 0))],
        out_shape=[jax.ShapeDtypeStruct((b, t, d), F32), jax.ShapeDtypeStruct((b, a_rows, d_ff), F32)],
        compiler_params=_cparams(2), name="ffn_dec" if single_pos else "ffn",
    )(x1, prev, *weights)


def _compress_params(pe, w1, b1, w2, n_kv):
    cmp_len, hd = pe.shape[1], pe.shape[2]
    hid = w1.shape[2]
    halves = cmp_len // CMP_STRIDE
    eye = jnp.eye(n_kv, dtype=w1.dtype)
    pe_t = jnp.broadcast_to(pe.reshape(2, halves, CMP_STRIDE, 1, hd), (2, halves, CMP_STRIDE, n_kv, hd))
    pe_t = pe_t.reshape(2, halves, 1, CMP_STRIDE * n_kv * hd)
    w1h = w1.reshape(2, halves, CMP_STRIDE, hd, hid)
    w1b = jnp.einsum('krsdh,gf->ksgdrfh', w1h, eye).reshape(2, CMP_STRIDE * n_kv * hd, halves * n_kv * hid)
    b1t = jnp.tile(b1, (1, n_kv)).reshape(2, 1, n_kv * hid)
    w2b = jnp.einsum('khd,gf->kghfd', w2, eye).reshape(2, n_kv * hid, n_kv * hd)
    return pe_t, w1b.astype(BF16), b1t, w2b.astype(BF16)


def kernel(x_prompt, x_sample, cache_cmp, cache_slc, cache_win, state_conv, page_table, norm_mix, w_in, cmp_pe,
           cmp_w1, cmp_b1, cmp_w2, gmlp_norm, gmlp_ws, gmlp_bs, w_proj_a, w_proj_b, w_out, norm_ffn, w_up, conv_w,
           conv_b, w_down, norm_final):
    depth = w_in.shape[0]
    assert depth == 1, "single-layer step"
    bp, t, d = x_prompt.shape
    bd, tn, _ = x_sample.shape
    assert tn == 1
    n_kv, hd = cache_cmp.shape[4], cache_cmp.shape[5]
    page = cache_cmp.shape[2]
    q_cols = w_proj_a.shape[1]
    n_heads = q_cols // hd
    n_rep = n_heads // n_kv
    kv_cols = 2 * n_kv * hd
    width = w_proj_b.shape[1]
    chunk = gmlp_ws.shape[-1]
    d_ff = w_down.shape[1]
    n_pages = page_table.shape[1]
    past_len = n_pages * page
    scale = hd ** -0.5
    assert conv_w.shape[1] == 3 and cache_win.shape[2] <= WINDOW and past_len % SEL_BLOCK == 0

    wit = w_in[0].T.astype(BF16)
    off_kv, off_g = q_cols, q_cols + 3 * kv_cols
    off_uv = off_g + 3 * n_heads
    off_gate = off_uv + 2 * width
    w_qt, w_kvt, w_g = wit[0:off_kv], wit[off_kv:off_g], wit[off_g:off_uv]
    w_uv, w_gate = wit[off_uv:off_gate], wit[off_gate:]
    g_rows = 16
    w_gt = jnp.pad(w_g.reshape(n_kv, 3 * n_rep, d), ((0, 0), (0, g_rows - 3 * n_rep), (0, 0))).reshape(n_kv * g_rows, d)
    w_dec = jnp.concatenate([w_qt, w_kvt, jnp.pad(w_g, ((0, LANES - 3 * n_heads), (0, 0)))], axis=0)
    nm, nf, nfin, gn = norm_mix[0][None], norm_ffn[0][None], norm_final[None], gmlp_norm[0][None]
    pe_t, w1b, b1t, w2b = _compress_params(cmp_pe[0], cmp_w1[0], cmp_b1[0], cmp_w2[0], n_kv)
    wpa, wpb, wout = w_proj_a[0].astype(BF16), w_proj_b[0].astype(BF16), w_out[0].astype(BF16)
    wup, wdown = w_up[0].astype(BF16), w_down[0].astype(BF16)
    cw, cb = conv_w[0], conv_b[0][None]
    ws, bs = gmlp_ws[0], gmlp_bs[0]

    cmp_t, slc_t, win_t, qt, gt = _front(x_prompt, nm, w_kvt, w_qt, w_gt, kv_cols=kv_cols, scale=scale * LOG2_E,
                                         tm=512)
    kcv = _compress(cmp_t, pe_t, w1b, b1t, w2b, n_seq=bp, steps=1, rows=t // CMP_STRIDE)
    o_nsa = _attention(qt, gt, kcv, slc_t, win_t, n_rep=n_rep, hd=hd)
    x1, v_p = _mix(x_prompt, o_nsa, nm, w_uv, w_gate, gn, ws, bs.T, wpa, wpb, wout,
                   tm=256, chunk=chunk, single_pos=False)
    y_p, conv_p = _ffn(x1, x1, nf, wup, cw, cb, wdown, nfin, tm=256, f_tile=d_ff // 2, single_pos=False)

    xs = x_sample.reshape(bd, d)
    q_s, kv_s, g_s = _front_dec(xs, nm, w_dec, q_cols=q_cols, kv_cols3=3 * kv_cols, scale=scale)
    kv_cmp_s, kv_slc_s, kv_win_s = kv_s[:, 0:kv_cols], kv_s[:, kv_cols:2 * kv_cols], kv_s[:, 2 * kv_cols:]
    def positions_last(c):
        return jnp.transpose(c, (0, 2, 3, 4, 1)).reshape(c.shape[0], kv_cols, c.shape[1])

    cmp_pages, slc_pages, win_rows = positions_last(cache_cmp[0]), positions_last(cache_slc[0]), positions_last(cache_win[0])
    pages_per_step = 32
    kcv_s = _compress(cmp_pages, pe_t, w1b, b1t, w2b, n_seq=bd, steps=n_pages // pages_per_step,
                      rows=pages_per_step * page // CMP_STRIDE, page_table=page_table)
    q3 = q_s.reshape(bd, n_heads, hd)
    n_blocks_s = -(-(past_len + tn) // SEL_BLOCK)
    o_cmp_s, idx_s = _cmp_select_dec(q3, kcv_s, hd=hd, n_rep=n_rep, q_pos=past_len, n_blocks=n_blocks_s)
    sel = idx_s[:, 0:n_kv, 0:N_SEL].reshape(bd, n_kv * N_SEL)
    o_nsa_s = _attend_dec(page_table, sel, slc_pages, q3, kv_slc_s[:, None, :], kv_win_s[:, None, :], win_rows,
                          g_s[:, 0:3 * n_heads].reshape(bd, n_heads, 3), o_cmp_s,
                          hd=hd, n_rep=n_rep, n_sel=N_SEL, n_past_blocks=past_len // SEL_BLOCK)
    gw = width // GMLP_GROUPS
    ws0 = jnp.repeat(ws[:, 0, 0], gw)[None]
    bs0 = jnp.repeat(bs[:, 0], gw)[None]
    x1_s, v_s = _mix(xs[None], o_nsa_s.reshape(1, bd, q_cols), nm, w_uv, w_gate, gn, ws0, bs0, wpa, wpb, wout,
                     tm=bd, chunk=chunk, single_pos=True)
    prev_s = jnp.swapaxes(state_conv[0], 0, 1)
    y_s, a_s = _ffn(x1_s, prev_s, nf, wup, cw, cb, wdown, nfin, tm=bd, f_tile=d_ff // 2, single_pos=True)

    def rows6(a_t):
        n, _, npos = a_t.shape
        return jnp.transpose(a_t.reshape(n, 2, n_kv, hd, npos), (0, 4, 1, 2, 3))[None]

    win_keep = min(WINDOW, t)
    win_keep_s = min(WINDOW, cache_win.shape[2] + tn)
    win_s = jnp.concatenate([win_rows, kv_win_s[:, :, None]], axis=2)[:, :, cache_win.shape[2] + tn - win_keep_s:]
    conv_s = jnp.concatenate([state_conv[0][:, 1:], a_s[0][:, None, :]], axis=1)
    return (y_p, y_s.reshape(bd, tn, d),
            rows6(cmp_t), rows6(slc_t), rows6(win_t[:, :, t - win_keep:]),
            v_p[None], conv_p[None],
            rows6(kv_cmp_s[:, :, None]), rows6(kv_slc_s[:, :, None]), rows6(win_s),
            v_s.reshape(1, bd, tn, width), conv_s[None])
```

```python
import functools

import jax
import jax.numpy as jnp
from jax import lax
from jax.experimental import pallas as pl
from jax.experimental.pallas import tpu as pltpu

F32 = jnp.float32
BF16 = jnp.bfloat16
I32 = jnp.int32

CMP_STRIDE = 16
SEG_PITCH = 24
PAGE_LOOKAHEAD = 2
PAGE_SLOTS = PAGE_LOOKAHEAD + 1
CMP_PAGE_LOOKAHEAD = 3
CMP_PAGE_SLOTS = CMP_PAGE_LOOKAHEAD + 1
SEL_BLOCK = 64
N_SEL = 16
N_LOCAL_SEL = 2
WINDOW = 512
Q_BLOCK = 256
KEY_TILE = 128
SLC_TILES = 4
SLC_PAIR = 1
V_PAD = 16
LOG2_E = 1.4426950408889634
GMLP_GROUPS = 4
EPS = 1e-6
NEG = -1e30
BELOW_NEG = -3e38
SEL_BONUS = 1e6

V7X_VMEM_BYTES = 64 * 1024 * 1024
VMEM_REQUEST_BYTES = 56 * 1024 * 1024
LANES = 128


def _cparams(n_grid):
    return pltpu.CompilerParams(
        dimension_semantics=("arbitrary",) * n_grid, vmem_limit_bytes=VMEM_REQUEST_BYTES)


def _rmsnorm(x, g):
    ms = jnp.mean(x * x, axis=-1, keepdims=True)
    return x * lax.rsqrt(ms + EPS) * g


def _dot(a, b):
    return jnp.dot(a, b, preferred_element_type=F32)


def _dot_nt(a, b):
    return lax.dot_general(a, b, (((1,), (1,)), ((), ())), preferred_element_type=F32)


def _dot_exact(a, b):
    return jnp.dot(a, b, precision=lax.Precision.HIGHEST, preferred_element_type=F32)


def _shr(x, n):
    assert n & (n - 1) == 0
    return x >> (n.bit_length() - 1)


def _const_spec(shape):
    nd = len(shape)
    return pl.BlockSpec(shape, lambda *_: (0,) * nd)


def _front_kernel(x_ref, nw_ref, wkvt_ref, wqt_ref, wgt_ref, cmp_ref, slc_ref, win_ref, qt_ref, gt_ref,
                  *, kv_cols, scale):
    h = _rmsnorm(x_ref[0], nw_ref[...]).astype(BF16)
    kvt = _dot_nt(wkvt_ref[...], h)
    cmp_ref[0] = kvt[0:kv_cols]
    slc_ref[0] = kvt[kv_cols:2 * kv_cols]
    win_ref[0] = kvt[2 * kv_cols:3 * kv_cols]
    qt_ref[0] = (_dot_nt(wqt_ref[...], h) * scale).astype(BF16)
    gt_ref[0] = jax.nn.sigmoid(_dot_nt(wgt_ref[...], h))


def _front(x, nw, wkvt, wqt, wgt, *, kv_cols, scale, tm):
    b, t, d = x.shape
    q_cols, g_rows = wqt.shape[0], wgt.shape[0]
    kv_shape = jax.ShapeDtypeStruct((b, kv_cols, t), F32)
    kv_spec = pl.BlockSpec((1, kv_cols, tm), lambda i, j: (i, 0, j))
    return pl.pallas_call(
        functools.partial(_front_kernel, kv_cols=kv_cols, scale=scale),
        grid=(b, t // tm),
        in_specs=[pl.BlockSpec((1, tm, d), lambda i, j: (i, j, 0)), _const_spec(nw.shape),
                  _const_spec(wkvt.shape), _const_spec(wqt.shape), _const_spec(wgt.shape)],
        out_specs=[kv_spec, kv_spec, kv_spec,
                   pl.BlockSpec((1, q_cols, tm), lambda i, j: (i, 0, j)),
                   pl.BlockSpec((1, g_rows, tm), lambda i, j: (i, 0, j))],
        out_shape=[kv_shape, kv_shape, kv_shape,
                   jax.ShapeDtypeStruct((b, q_cols, t), BF16),
                   jax.ShapeDtypeStruct((b, g_rows, t), F32)],
        compiler_params=_cparams(2), name="front",
    )(x, nw, wkvt, wqt, wgt)


def _front_dec_kernel(x_ref, nw_ref, w_ref, q_ref, kv_ref, g_ref, *, q_cols, kv_cols3, scale):
    h = _rmsnorm(x_ref[...], nw_ref[...]).astype(BF16)
    z = _dot_nt(h, w_ref[...])
    q_ref[...] = z[:, 0:q_cols] * scale
    kv_ref[...] = z[:, q_cols:q_cols + kv_cols3]
    g_ref[...] = jax.nn.sigmoid(z[:, q_cols + kv_cols3:])


def _front_dec(x, nw, w, *, q_cols, kv_cols3, scale):
    n = x.shape[0]
    g_cols = w.shape[0] - q_cols - kv_cols3
    return pl.pallas_call(
        functools.partial(_front_dec_kernel, q_cols=q_cols, kv_cols3=kv_cols3, scale=scale),
        grid=(1,),
        in_specs=[_const_spec(x.shape), _const_spec(nw.shape), _const_spec(w.shape)],
        out_specs=[_const_spec((n, q_cols)), _const_spec((n, kv_cols3)), _const_spec((n, g_cols))],
        out_shape=[jax.ShapeDtypeStruct((n, q_cols), F32), jax.ShapeDtypeStruct((n, kv_cols3), F32),
                   jax.ShapeDtypeStruct((n, g_cols), F32)],
        compiler_params=_cparams(1), name="front_dec",
    )(x, nw, w)


def _segments_onto_rows(tile_of, n_tiles, pos_ref, kv, stride):
    seg_per_tile = LANES // stride
    for ti in range(n_tiles):
        t = tile_of(ti).T
        for n in range(seg_per_tile):
            p0 = (ti * seg_per_tile + n) * SEG_PITCH
            pos_ref[kv, p0:p0 + stride, :] = t[n * stride:(n + 1) * stride, :]


def _compress_rows(kv, pe_ref, w1_ref, b1_ref, w2_ref, out_ref, carry_ref, pos_ref, *, stride, half, hid2):
    rows = out_ref.shape[1]
    xkv = jnp.concatenate(
        [pos_ref[kv, pl.ds(s, rows, stride=SEG_PITCH), :] for s in range(stride)],
        axis=1)
    parts = []
    for r in range(2):
        a = (xkv + pe_ref[kv, r]).astype(BF16)
        parts.append(_dot(a, w1_ref[kv, :, r * hid2:(r + 1) * hid2]))
    prev = carry_ref[kv, 0:1, :]
    row = lax.broadcasted_iota(I32, (rows, hid2), 0)
    shifted = jnp.where(row == 0, prev, pltpu.roll(parts[0], 1, 0))
    carry_ref[kv, 0:1, :] = parts[0][rows - 1:rows, :]
    hid = b1_ref[kv] + shifted + parts[1]
    out_ref[0, :, kv * half:(kv + 1) * half] = _dot(jax.nn.gelu(hid).astype(BF16), w2_ref[kv])


def _compress_kernel(x_ref, pe_ref, w1_ref, b1_ref, w2_ref, out_ref, carry_ref, pos_ref, *, stride, kv_cols, hid2):
    half = kv_cols // 2
    assert half == LANES

    @pl.when(pl.program_id(1) == 0)
    def _():
        carry_ref[...] = jnp.zeros_like(carry_ref)

    for kv in range(2):
        _segments_onto_rows(lambda ti: x_ref[0, kv * half:(kv + 1) * half, ti * LANES:(ti + 1) * LANES],
                            x_ref.shape[2] // LANES, pos_ref, kv, stride)
        _compress_rows(kv, pe_ref, w1_ref, b1_ref, w2_ref, out_ref, carry_ref, pos_ref,
                       stride=stride, half=half, hid2=hid2)


def _compress_paged_kernel(pt_ref, pages_ref, pe_ref, w1_ref, b1_ref, w2_ref, out_ref, carry_ref, pos_ref, buf_ref,
                           sem, *, stride, kv_cols, hid2, n_pages):
    half = kv_cols // 2
    assert half == LANES and buf_ref.shape[3] == LANES
    steps = pl.num_programs(1)
    total = pl.num_programs(0) * steps
    t = pl.program_id(0) * steps + pl.program_id(1)
    look, slots = CMP_PAGE_LOOKAHEAD, CMP_PAGE_SLOTS
    static = dict(stride=stride, half=half, hid2=hid2)

    def page_copy(src_step, ring_step, k):
        page = pt_ref[src_step // steps, (src_step % steps) * n_pages + k]
        slot = ring_step % slots
        return pltpu.make_async_copy(pages_ref.at[page], buf_ref.at[slot, k], sem.at[slot])

    def rows_of(ring_step, kv):
        slot = ring_step % slots
        _segments_onto_rows(lambda ti: buf_ref[slot, ti, kv * half:(kv + 1) * half, :], n_pages, pos_ref, kv, stride)

    @pl.when(t == 0)
    def _():
        for ahead in range(look):
            for k in range(n_pages):
                page_copy(jnp.minimum(ahead, total - 1), ahead, k).start()
        for k in range(n_pages):
            page_copy(0, 0, k).wait()
        rows_of(0, 0)

    @pl.when(pl.program_id(1) == 0)
    def _():
        carry_ref[...] = jnp.zeros_like(carry_ref)

    for k in range(n_pages):
        page_copy(t, t + 1, k).wait()
    _compress_rows(0, pe_ref, w1_ref, b1_ref, w2_ref, out_ref, carry_ref, pos_ref, **static)
    rows_of(t, 1)
    _compress_rows(1, pe_ref, w1_ref, b1_ref, w2_ref, out_ref, carry_ref, pos_ref, **static)
    rows_of(t + 1, 0)
    for k in range(n_pages):
        page_copy(jnp.minimum(t + look, total - 1), t + look, k).start()

    @pl.when(t == total - 1)
    def _():
        for ahead in range(2, look + 1):
            for k in range(n_pages):
                page_copy(t, t + ahead, k).wait()


def _compress(x, pe, w1, b1, w2, *, n_seq, steps, rows, page_table=None):
    stride, kv_cols = CMP_STRIDE, w2.shape[2] * 2
    hid2 = b1.shape[2]
    static = dict(stride=stride, kv_cols=kv_cols, hid2=hid2)
    out_shape = jax.ShapeDtypeStruct((n_seq, steps * rows, kv_cols), F32)
    scratch = [pltpu.VMEM((2, 8, hid2), F32), pltpu.VMEM((2, rows * SEG_PITCH, kv_cols // 2), F32)]
    weights = (pe, w1, b1, w2)
    if page_table is None:
        return pl.pallas_call(
            functools.partial(_compress_kernel, **static), grid=(n_seq, steps),
            in_specs=[pl.BlockSpec((1, kv_cols, rows * stride), lambda i, j: (i, 0, j))]
                     + [_const_spec(a.shape) for a in weights],
            out_specs=pl.BlockSpec((1, rows, kv_cols), lambda i, j: (i, j, 0)),
            out_shape=out_shape, scratch_shapes=scratch, compiler_params=_cparams(2), name="compress",
        )(x, *weights)
    page = x.shape[2]
    n_pages = rows * stride // page
    assert n_seq * steps > CMP_PAGE_LOOKAHEAD
    w_specs = [pl.BlockSpec(a.shape, functools.partial(lambda i, j, pt, nd: (0,) * nd, nd=a.ndim)) for a in weights]
    return pl.pallas_call(
        functools.partial(_compress_paged_kernel, n_pages=n_pages, **static),
        grid_spec=pltpu.PrefetchScalarGridSpec(
            num_scalar_prefetch=1, grid=(n_seq, steps),
            in_specs=[pl.BlockSpec(memory_space=pl.ANY)] + w_specs,
            out_specs=pl.BlockSpec((1, rows, kv_cols), lambda i, j, pt: (i, j, 0)),
            scratch_shapes=scratch + [pltpu.VMEM((CMP_PAGE_SLOTS, n_pages, kv_cols, page), F32),
                                      pltpu.SemaphoreType.DMA((CMP_PAGE_SLOTS,))]),
        out_shape=out_shape, compiler_params=_cparams(2), name="compress_paged",
    )(page_table, x, *weights)


def _block_scores(imp, blk, q_pos, n_blocks):
    cur = _shr(q_pos, SEL_BLOCK)
    valid = (blk * SEL_BLOCK <= q_pos) & (blk < n_blocks)
    forced = (blk == 0) | ((blk <= cur) & (blk > cur - N_LOCAL_SEL))
    score = jnp.where(valid, imp + jnp.where(forced, SEL_BONUS, 0.0), NEG)
    return jnp.where(blk < n_blocks, score, BELOW_NEG)


def _attn_kernel(qt_ref, gt_ref, kcv_ref, slc_ref, win_ref, o_ref,
                 kaug_ref, vts_ref, kwin_ref, vtw_ref, kc_ref, vct_ref, *, n_kv, n_rep, hd, n_blocks):
    i = pl.program_id(1)
    qb, kt = Q_BLOCK, KEY_TILE
    q_tiles = qb // kt
    n_tiles = slc_ref.shape[2] // kt
    n_cmp_rows = kcv_ref.shape[1]
    nq = n_rep * qb
    kd = n_kv * hd
    g_rows = gt_ref.shape[1] // n_kv
    assert kd == LANES and n_blocks == hd, "the selection bias rows ride in the key one-hot lanes"

    @pl.when(i == 0)
    def _():
        lane = lax.broadcasted_iota(I32, (kt, kd), 1)
        krow = lax.broadcasted_iota(I32, (kt, kd), 0)

        def group_lanes(x, gg):
            return x if gg == 0 else pltpu.roll(x, kd - gg * hd, 1)

        ones_row = (lax.broadcasted_iota(I32, (V_PAD, kt), 0) == 0).astype(BF16)
        for c in range(n_tiles):
            cols = slice(c * kt, (c + 1) * kt)
            onehot = (lane - hd == _shr(c * kt + krow, SEL_BLOCK)).astype(F32)
            kt_s, kt_w = slc_ref[0, 0:kd, cols].T, win_ref[0, 0:kd, cols].T
            for gg in range(n_kv):
                v_rows = slice(kd + gg * hd, kd + (gg + 1) * hd)
                kaug_ref[gg, cols, :] = jnp.where(lane < hd, group_lanes(kt_s, gg), onehot).astype(BF16)
                side = slice((c % SLC_PAIR) * kt, (c % SLC_PAIR + 1) * kt)
                vts_ref[gg, c // SLC_PAIR, 0:hd, side] = slc_ref[0, v_rows, cols].astype(BF16)
                vts_ref[gg, c // SLC_PAIR, hd:, side] = ones_row
                kwin_ref[gg, cols, :] = group_lanes(kt_w, gg)[:, 0:hd].astype(BF16)
                vtw_ref[gg, c, 0:hd, :] = win_ref[0, v_rows, cols].astype(BF16)
                vtw_ref[gg, c, hd:, :] = ones_row
        for c in range(n_cmp_rows // kt):
            rows = slice(c * kt, (c + 1) * kt)
            blk = kcv_ref[0, rows, :]
            vt = blk[:, kd:2 * kd].T
            for gg in range(n_kv):
                kc_ref[gg, rows, :] = group_lanes(blk[:, 0:kd], gg)[:, 0:hd].astype(BF16)
                vct_ref[gg, :, rows] = vt[gg * hd:(gg + 1) * hd, :].astype(BF16)

    q_pos = i * qb + (lax.broadcasted_iota(I32, (1, nq), 1) & (qb - 1))

    groups = range(n_kv)
    heads = range(n_rep)

    q_minus_k = (lax.broadcasted_iota(I32, (kt, nq), 1) & (qb - 1)) - lax.broadcasted_iota(I32, (kt, nq), 0)

    def softmax_step(m, sc):
        m_new = jnp.maximum(m, jnp.max(sc, axis=0, keepdims=True))
        return m_new, jnp.exp2(m - m_new), jnp.exp2(sc - m_new).astype(BF16)

    def normalised(acc):
        return acc[0:hd, :] * (1.0 / acc[hd:hd + 1, :])

    qts = [qt_ref[0, gg * n_rep * hd:(gg + 1) * n_rep * hd, :] for gg in groups]
    qcats = [jnp.concatenate([qts[gg][h * hd:(h + 1) * hd, :] for h in heads], axis=1) for gg in groups]

    def attend(k_ref, v_ref, qs, tile_ids, state, masked, per_update=1):
        scores = [[masked(u, _dot(k_ref[gg, pl.ds(pl.multiple_of(t * kt, kt), kt), :], qs[gg])) for gg in groups]
                  for u, t in enumerate(tile_ids)]
        maxes, accs = list(state[0]), list(state[1])
        for u in range(0, len(tile_ids), per_update):
            for gg in groups:
                sc = jnp.concatenate([scores[u + d][gg] for d in range(per_update)], axis=0)
                maxes[gg], alpha, pb = softmax_step(maxes[gg], sc)
                accs[gg] = alpha * accs[gg] + _dot(v_ref[gg, _shr(tile_ids[u], per_update)], pb)
        return tuple(maxes), tuple(accs)

    state0 = ((jnp.full((1, nq), NEG, F32),) * n_kv, (jnp.zeros((hd + V_PAD, nq), F32),) * n_kv)

    s_cmp = [_dot(kc_ref[gg], qcats[gg]) for gg in groups]
    m_idx = lax.broadcasted_iota(I32, (n_cmp_rows, nq), 0)
    vis = (m_idx >= 1) & ((m_idx - 1) * CMP_STRIDE + 2 * CMP_STRIDE - 1 <= q_pos)
    ratio = SEL_BLOCK // CMP_STRIDE
    pj = lax.broadcasted_iota(I32, (n_blocks, n_cmp_rows), 0)
    pm = lax.broadcasted_iota(I32, (n_blocks, n_cmp_rows), 1)
    pool = ((pm >= 1) & (pm >= ratio * pj) & (pm <= ratio * pj + ratio)).astype(F32)
    o_cmp, imp = [], []
    for gg in groups:
        s = jnp.where(vis, s_cmp[gg], NEG)
        e = jnp.where(vis, jnp.exp2(s - jnp.max(s, axis=0, keepdims=True)), 0.0)
        den = jnp.sum(e, axis=0, keepdims=True)
        p = e * (1.0 / jnp.where(den > 0, den, 1.0))
        o_cmp.append(_dot(vct_ref[gg], p.astype(BF16)))
        p_grp = p[:, 0:qb]
        for h in range(1, n_rep):
            p_grp = p_grp + p[:, h * qb:(h + 1) * qb]
        imp.append(_dot_exact(pool, p_grp))

    n_band = WINDOW // kt
    first_t = i * q_tiles - n_band
    win_tiles = [jnp.maximum(first_t + u, 0) for u in range(n_band + q_tiles)]

    def in_window(u, sc):
        dlt = q_minus_k + (n_band - u) * kt
        if u < q_tiles:
            sc = jnp.where(dlt <= WINDOW, sc, NEG)
        if u >= n_band:
            sc = jnp.where(dlt >= 0, sc, NEG)
        if u < n_band:
            sc = sc + jnp.where(first_t + u < 0, NEG, 0.0)
        if u == 0:
            sc = sc + after_importance
        return sc

    after_importance = sum(imp[gg][0:1, 0:1] for gg in groups) * 0.0

    _, accs = attend(kwin_ref, vtw_ref, qcats, win_tiles, state0, in_window)
    o_win = [normalised(accs[gg]) for gg in groups]

    blk_id = lax.broadcasted_iota(I32, (n_blocks, qb), 0)
    blk_f = blk_id.astype(F32)
    qaug = []
    for gg in groups:
        score = _block_scores(imp[gg], blk_id, q_pos[:, 0:qb], n_blocks)
        work, sel = score, jnp.zeros((n_blocks, qb), F32)
        for _ in range(min(N_SEL, n_blocks)):
            mx = jnp.max(work, axis=0, keepdims=True)
            first = jnp.min(jnp.where(work == mx, blk_f, float(n_blocks)), axis=0, keepdims=True)
            pick = blk_f == first
            sel = jnp.where(pick, 1.0, sel)
            work = jnp.where(pick, BELOW_NEG, work)
        bias = jnp.where((sel > 0) & (score > NEG / 2), 0.0, NEG).astype(BF16)
        qaug.append(jnp.concatenate(
            [jnp.concatenate([qts[gg][h * hd:(h + 1) * hd, :], bias], axis=0) for h in heads], axis=1))

    assert n_tiles % SLC_TILES == 0 and SLC_TILES % q_tiles == 0
    n_full = _shr(i * q_tiles, SLC_TILES)
    state = lax.fori_loop(
        0, n_full,
        lambda j, st: attend(kaug_ref, vts_ref, qaug, [j * SLC_TILES + u for u in range(SLC_TILES)], st,
                             lambda u, sc: sc, SLC_PAIR), state0)
    last = n_full * SLC_TILES
    _, accs = attend(kaug_ref, vts_ref, qaug, [last + u for u in range(SLC_TILES)], state,
                     lambda u, sc: jnp.where(q_minus_k + (i * q_tiles - last - u) * kt >= 0, sc, NEG), SLC_PAIR)

    outs = []
    for gg in groups:
        o_slc = normalised(accs[gg])
        for h in heads:
            cols = slice(h * qb, (h + 1) * qb)
            gate = [gt_ref[0, gg * g_rows + 3 * h + br:gg * g_rows + 3 * h + br + 1, :] for br in range(3)]
            outs.append(gate[0] * o_cmp[gg][:, cols] + gate[1] * o_slc[:, cols] + gate[2] * o_win[gg][:, cols])
    o_ref[0] = jnp.concatenate(outs, axis=0).T.astype(BF16)


def _attention(qt, gt, kcv, slc, win, *, n_rep, hd):
    b, q_cols, t = qt.shape
    n_kv = q_cols // (n_rep * hd)
    qb, kt = Q_BLOCK, KEY_TILE
    n_tiles = t // kt
    n_blocks = t // SEL_BLOCK
    kv_cols = slc.shape[1]
    cmp_rows = kcv.shape[1]
    tile_spec = pl.BlockSpec((1, kv_cols, t), lambda bi, i: (bi, 0, 0))
    return pl.pallas_call(
        functools.partial(_attn_kernel, n_kv=n_kv, n_rep=n_rep, hd=hd, n_blocks=n_blocks),
        grid=(b, t // qb),
        in_specs=[pl.BlockSpec((1, q_cols, qb), lambda bi, i: (bi, 0, i)),
                  pl.BlockSpec((1, gt.shape[1], qb), lambda bi, i: (bi, 0, i)),
                  pl.BlockSpec((1,) + kcv.shape[1:], lambda bi, i: (bi, 0, 0)),
                  tile_spec, tile_spec],
        out_specs=pl.BlockSpec((1, qb, q_cols), lambda bi, i: (bi, i, 0)),
        out_shape=jax.ShapeDtypeStruct((b, t, q_cols), BF16),
        scratch_shapes=[pltpu.VMEM((n_kv, t, 2 * hd), BF16),
                        pltpu.VMEM((n_kv, n_tiles // SLC_PAIR, hd + V_PAD, SLC_PAIR * kt), BF16),
                        pltpu.VMEM((n_kv, t, hd), BF16), pltpu.VMEM((n_kv, n_tiles, hd + V_PAD, kt), BF16),
                        pltpu.VMEM((n_kv, cmp_rows, hd), BF16), pltpu.VMEM((n_kv, hd, cmp_rows), BF16)],
        compiler_params=_cparams(2), name="attention",
    )(qt, gt, kcv, slc, win)


def _spread_q(q, hd, n_rep):
    n_heads = q.shape[0]
    d = lax.broadcasted_iota(I32, (hd, 4 * hd), 0)
    c = lax.broadcasted_iota(I32, (hd, 4 * hd), 1)
    qb16 = q.astype(BF16)
    row = lax.broadcasted_iota(I32, (n_heads, 4 * hd), 0)
    out = jnp.zeros((n_heads, 4 * hd), F32)
    for gg in range(n_heads // n_rep):
        placed = _dot(qb16, (c == d + gg * hd).astype(BF16))
        out = jnp.where(_shr(row, n_rep) == gg, placed, out)
    return out.astype(BF16)


def _masked_softmax_rows(s, mask):
    s = jnp.where(mask, s, NEG)
    e = jnp.where(mask, jnp.exp(s - jnp.max(s, axis=-1, keepdims=True)), 0.0)
    den = jnp.sum(e, axis=-1, keepdims=True)
    return e * (1.0 / jnp.where(den > 0, den, 1.0))


def _group_value_lanes(o_full, hd, n_rep):
    row = lax.broadcasted_iota(I32, (o_full.shape[0], hd), 0)
    out = o_full[:, 2 * hd:3 * hd]
    for gg in range(1, o_full.shape[0] // n_rep):
        out = jnp.where(_shr(row, n_rep) == gg, o_full[:, (2 + gg) * hd:(3 + gg) * hd], out)
    return out


def _cmp_select_dec_kernel(q_ref, kcv_ref, o_ref, idx_ref, pool_ref, score_ref, *, hd, n_rep, q_pos, n_blocks,
                           blk_lanes):
    n_heads = q_ref.shape[1]
    n_rows = kcv_ref.shape[1]
    b = pl.program_id(0)
    n_seq = idx_ref.shape[0]

    @pl.when(b == 0)
    def _():
        ratio = SEL_BLOCK // CMP_STRIDE
        pm = lax.broadcasted_iota(I32, (n_rows, blk_lanes), 0)
        pj = lax.broadcasted_iota(I32, (n_rows, blk_lanes), 1)
        pool_ref[...] = ((pm >= 1) & (pm >= ratio * pj) & (pm <= ratio * pj + ratio)).astype(BF16)

    q2 = _spread_q(q_ref[0], hd, n_rep)
    kcv = kcv_ref[0].astype(BF16)
    m_idx = lax.broadcasted_iota(I32, (n_heads, n_rows), 1)
    vis = (m_idx >= 1) & ((m_idx - 1) * CMP_STRIDE + 2 * CMP_STRIDE - 1 <= q_pos)
    p = _masked_softmax_rows(_dot_nt(q2, kcv), vis)
    o_ref[0] = _group_value_lanes(_dot(p.astype(BF16), kcv), hd, n_rep)

    row = lax.broadcasted_iota(I32, (n_heads, n_rows), 0)
    grp = jnp.zeros((n_heads, n_rows), F32)
    for gg in range(n_heads // n_rep):
        tot = jnp.sum(jnp.where(_shr(row, n_rep) == gg, p, 0.0), axis=0, keepdims=True)
        grp = jnp.where(row == gg, tot, grp)
    hi = grp.astype(BF16).astype(F32)
    mid = (grp - hi).astype(BF16).astype(F32)
    lo = grp - hi - mid
    pieces = _dot(jnp.concatenate([hi, mid, lo], axis=0).astype(BF16), pool_ref[...])
    imp = pieces[0:n_heads] + pieces[n_heads:2 * n_heads] + pieces[2 * n_heads:]
    score_ref[b] = _block_scores(imp, lax.broadcasted_iota(I32, (n_heads, blk_lanes), 1), q_pos, n_blocks)

    @pl.when(b == n_seq - 1)
    def _():
        work = score_ref[...].reshape(n_seq * n_heads, blk_lanes)
        blk_f = lax.broadcasted_iota(I32, work.shape, 1).astype(F32)
        out_lane = lax.broadcasted_iota(I32, (n_seq * n_heads, LANES), 1)
        out = jnp.full((n_seq * n_heads, LANES), -1, I32)
        for it in range(min(N_SEL, n_blocks)):
            mx = jnp.max(work, axis=-1, keepdims=True)
            first = jnp.min(jnp.where(work == mx, blk_f, float(blk_lanes)), axis=-1, keepdims=True)
            out = jnp.where(out_lane == it, jnp.where(mx > NEG / 2, first.astype(I32), -1), out)
            work = jnp.where(blk_f == first, BELOW_NEG, work)
        idx_ref[...] = out.reshape(n_seq, n_heads, LANES)


def _cmp_select_dec(q3, kcv, *, hd, n_rep, q_pos, n_blocks):
    n, n_heads, _ = q3.shape
    blk_lanes = -(-n_blocks // LANES) * LANES
    return pl.pallas_call(
        functools.partial(_cmp_select_dec_kernel, hd=hd, n_rep=n_rep, q_pos=q_pos, n_blocks=n_blocks,
                          blk_lanes=blk_lanes),
        grid=(n,),
        in_specs=[pl.BlockSpec((1, n_heads, hd), lambda i: (i, 0, 0)),
                  pl.BlockSpec((1,) + kcv.shape[1:], lambda i: (i, 0, 0))],
        out_specs=[pl.BlockSpec((1, n_heads, hd), lambda i: (i, 0, 0)),
                   _const_spec((n, n_heads, LANES))],
        out_shape=[jax.ShapeDtypeStruct((n, n_heads, hd), F32), jax.ShapeDtypeStruct((n, n_heads, LANES), I32)],
        scratch_shapes=[pltpu.VMEM((kcv.shape[1], blk_lanes), BF16), pltpu.VMEM((n, n_heads, blk_lanes), F32)],
        compiler_params=_cparams(1), name="cmp_select_dec",
    )(q3, kcv)


def _attend_dec_kernel(pt_ref, sel_ref, pages_ref, q_ref, new_slc_ref, new_win_ref, cwin_ref, g_ref, ocmp_ref, o_ref,
                       buf_ref, sem, *, hd, n_rep, n_sel, n_past_blocks):
    b = pl.program_id(0)
    total = pl.num_programs(0)
    n_heads = q_ref.shape[1]
    n_kv = n_heads // n_rep
    n_pages = n_kv * n_sel
    page = buf_ref.shape[3]
    per = page // SEL_BLOCK

    def page_copy(src_seq, ring_step, k):
        idx = jnp.clip(sel_ref[src_seq, k], 0, n_past_blocks - 1)
        slot = ring_step % PAGE_SLOTS
        return pltpu.make_async_copy(pages_ref.at[pt_ref[src_seq, _shr(idx, per)]], buf_ref.at[slot, k], sem.at[slot])

    @pl.when(b == 0)
    def _():
        for ahead in range(PAGE_LOOKAHEAD):
            for k in range(n_pages):
                page_copy(jnp.minimum(ahead, total - 1), ahead, k).start()

    for k in range(n_pages):
        page_copy(b, b, k).wait()
    ring_slot = b % PAGE_SLOTS

    q2 = _spread_q(q_ref[0], hd, n_rep)
    q2f = q2.astype(F32)
    head_grp = _shr(lax.broadcasted_iota(I32, (n_heads, 1), 0), n_rep)

    def with_new_key(s, mask, keys_t, new_row, new_ok):
        nr = new_row.astype(BF16).astype(F32)
        s_new = jnp.sum(q2f * nr, axis=-1, keepdims=True)
        s = jnp.where(mask, s, NEG)
        s_new = jnp.where(new_ok, s_new, NEG)
        m = jnp.maximum(jnp.max(s, axis=-1, keepdims=True), s_new)
        e = jnp.where(mask, jnp.exp(s - m), 0.0)
        e_new = jnp.where(new_ok, jnp.exp(s_new - m), 0.0)
        den = jnp.sum(e, axis=-1, keepdims=True) + e_new
        inv = 1.0 / jnp.where(den > 0, den, 1.0)
        return _dot_nt((e * inv).astype(BF16), keys_t) + (e_new * inv) * nr

    keys_t = jnp.concatenate([buf_ref[ring_slot, k] for k in range(n_pages)], axis=1).astype(BF16)
    n_keys = keys_t.shape[1]
    col_slot = _shr(lax.broadcasted_iota(I32, (1, n_keys), 1), page)
    page_blk = _shr(lax.broadcasted_iota(I32, (1, page), 1), SEL_BLOCK)
    slot_ok = []
    new_ok = jnp.zeros((n_heads, 1), I32)
    for gg in range(n_kv):
        for k in range(n_sel):
            idx = sel_ref[b, gg * n_sel + k]
            cached = ((idx >= 0) & (idx < n_past_blocks)).astype(I32)
            slot_ok.append(jnp.where(page_blk == (idx & (per - 1)), cached, 0))
            new_ok = jnp.where(head_grp == gg, new_ok | (idx == n_past_blocks).astype(I32), new_ok)
    mask = (jnp.concatenate(slot_ok, axis=1) > 0) & (_shr(col_slot, n_sel) == head_grp)
    o_slc = _group_value_lanes(with_new_key(_dot(q2, keys_t), mask, keys_t, new_slc_ref[0], new_ok > 0), hd, n_rep)

    keys_t = cwin_ref[0].astype(BF16)
    all_ok = jnp.full((n_heads, keys_t.shape[1]), True)
    o_win = _group_value_lanes(
        with_new_key(_dot(q2, keys_t), all_ok, keys_t, new_win_ref[0], jnp.full((n_heads, 1), True)), hd, n_rep)

    gates = g_ref[0]
    o_ref[0] = gates[:, 0:1] * ocmp_ref[0] + gates[:, 1:2] * o_slc + gates[:, 2:3] * o_win

    for k in range(n_pages):
        page_copy(jnp.minimum(b + PAGE_LOOKAHEAD, total - 1), b + PAGE_LOOKAHEAD, k).start()

    @pl.when(b == total - 1)
    def _():
        for ahead in range(1, PAGE_LOOKAHEAD + 1):
            for k in range(n_pages):
                page_copy(b, b + ahead, k).wait()


def _attend_dec(page_table, sel, slc_pages, q3, new_slc, new_win, cache_win, gates3, o_cmp, *, hd, n_rep, n_sel,
                n_past_blocks):
    n, n_heads, _ = q3.shape
    n_kv = n_heads // n_rep
    assert n > PAGE_LOOKAHEAD and n_past_blocks * SEL_BLOCK == page_table.shape[1] * slc_pages.shape[2]

    def row_spec(shape):
        nd = len(shape)
        return pl.BlockSpec((1,) + tuple(shape[1:]), lambda i, pt, sl: (i,) + (0,) * (nd - 1))

    others = (q3, new_slc, new_win, cache_win, gates3, o_cmp)
    return pl.pallas_call(
        functools.partial(_attend_dec_kernel, hd=hd, n_rep=n_rep, n_sel=n_sel, n_past_blocks=n_past_blocks),
        grid_spec=pltpu.PrefetchScalarGridSpec(
            num_scalar_prefetch=2, grid=(n,),
            in_specs=[pl.BlockSpec(memory_space=pl.ANY)] + [row_spec(a.shape) for a in others],
            out_specs=row_spec(o_cmp.shape),
            scratch_shapes=[pltpu.VMEM((PAGE_SLOTS, n_kv * n_sel) + slc_pages.shape[1:], F32),
                            pltpu.SemaphoreType.DMA((PAGE_SLOTS,))]),
        out_shape=jax.ShapeDtypeStruct(o_cmp.shape, F32),
        compiler_params=_cparams(1), name="attend_dec",
    )(page_table, sel, slc_pages, *others)


def _mix_kernel(x_ref, o_ref, nw_ref, wuv_ref, wgate_ref, gn_ref, ws_ref, bs_ref, wpa_ref, wpb_ref, wout_ref,
                x1_ref, v_ref, *, width, chunk, single_pos):
    x = x_ref[0]
    d = x.shape[-1]
    h = _rmsnorm(x, nw_ref[...]).astype(BF16)
    uv = jax.nn.gelu(_dot_nt(h, wuv_ref[...]))
    u, vn = uv[:, 0:width], _rmsnorm(uv[:, width:2 * width], gn_ref[...])
    gates = jax.nn.sigmoid(_dot_nt(h, wgate_ref[...]))
    gw = width // GMLP_GROUPS
    if single_pos:
        v_ref[0] = vn
        mixed = u * (ws_ref[...] * vn + bs_ref[...])
    else:
        rows = x.shape[0]
        v_ref[0] = vn[rows - chunk:rows, :]
        tri = lax.broadcasted_iota(I32, (chunk, chunk), 0) >= lax.broadcasted_iota(I32, (chunk, chunk), 1)
        vb = vn.astype(BF16)
        pieces = []
        for c in range(rows // chunk):
            zs = []
            for gi in range(GMLP_GROUPS):
                wm = jnp.where(tri, ws_ref[gi], 0.0).astype(BF16)
                zs.append(_dot(wm, vb[c * chunk:(c + 1) * chunk, gi * gw:(gi + 1) * gw]) + bs_ref[:, gi:gi + 1])
            pieces.append(jnp.concatenate(zs, axis=1))
        mixed = u * jnp.concatenate(pieces, axis=0)
    br_a = _dot(o_ref[0].astype(BF16), wpa_ref[...])
    br_b = _dot(mixed.astype(BF16), wpb_ref[...])
    merged = gates[:, 0:d] * br_a + gates[:, d:2 * d] * br_b
    x1_ref[0] = x + _dot(merged.astype(BF16), wout_ref[...])


def _mix(x, o_nsa, nw, wuv, wgate, gn, ws, bs, wpa, wpb, wout, *, tm, chunk, single_pos):
    b, t, d = x.shape
    width = wuv.shape[0] // 2
    v_rows = tm if single_pos else chunk
    weights = (nw, wuv, wgate, gn, ws, bs, wpa, wpb, wout)
    return pl.pallas_call(
        functools.partial(_mix_kernel, width=width, chunk=chunk, single_pos=single_pos),
        grid=(b, t // tm),
        in_specs=[pl.BlockSpec((1, tm, d), lambda i, j: (i, j, 0)),
                  pl.BlockSpec((1, tm, o_nsa.shape[-1]), lambda i, j: (i, j, 0))]
                 + [_const_spec(a.shape) for a in weights],
        out_specs=[pl.BlockSpec((1, tm, d), lambda i, j: (i, j, 0)),
                   pl.BlockSpec((1, v_rows, width), lambda i, j: (i, 0, 0))],
        out_shape=[jax.ShapeDtypeStruct((b, t, d), F32), jax.ShapeDtypeStruct((b, v_rows, width), F32)],
        compiler_params=_cparams(2), name="mix_dec" if single_pos else "mix",
    )(x, o_nsa, *weights)


def _ffn_kernel(x1_ref, prev_ref, nf_ref, wup_ref, cw_ref, cb_ref, wdown_ref, nfin_ref, y_ref, a_ref,
                *, d_ff, f_tile, halo, single_pos):
    x1 = x1_ref[0]
    rows = x1.shape[0]
    if single_pos:
        h = _rmsnorm(x1, nf_ref[...]).astype(BF16)
    else:
        h = _rmsnorm(jnp.concatenate([prev_ref[0], x1], axis=0), nf_ref[...]).astype(BF16)
        ext_row = lax.broadcasted_iota(I32, (rows + halo, f_tile), 0)
        first = pl.program_id(1) == 0
    y = jnp.zeros_like(x1)
    for f0 in range(0, d_ff, f_tile):
        cols = slice(f0, f0 + f_tile)
        a = _dot(h, wup_ref[:, cols])
        bgate = _dot(h, wup_ref[:, d_ff + f0:d_ff + f0 + f_tile])
        if single_pos:
            a_ref[0, :, cols] = a
            c = cb_ref[:, cols] + prev_ref[0, :, cols] * cw_ref[0:1, cols] + prev_ref[1, :, cols] * cw_ref[1:2, cols] \
                + a * cw_ref[2:3, cols]
        else:
            a = jnp.where((ext_row < halo) & first, 0.0, a)
            back2 = pltpu.roll(a, 2, 0)
            a_ref[0, :, cols] = back2[0:2, :]
            c = cb_ref[:, cols] + back2[halo:, :] * cw_ref[0:1, cols] \
                + pltpu.roll(a, 1, 0)[halo:, :] * cw_ref[1:2, cols] + a[halo:, :] * cw_ref[2:3, cols]
            bgate = bgate[halo:, :]
        y = y + _dot((jax.nn.gelu(c) * bgate).astype(BF16), wdown_ref[cols, :])
    y_ref[0] = _rmsnorm(x1 + y, nfin_ref[...])


def _ffn(x1, prev, nf, wup, cw, cb, wdown, nfin, *, tm, f_tile, single_pos):
    b, t, d = x1.shape
    d_ff = wdown.shape[0]
    halo = 8
    weights = (nf, wup, cw, cb, wdown, nfin)
    if single_pos:
        prev_spec = _const_spec(prev.shape)
        a_rows = tm
    else:
        per = tm // halo
        prev_spec = pl.BlockSpec((1, halo, d), lambda i, j: (i, jnp.maximum(j * per - 1, 0), 0))
        a_rows = 2
    return pl.pallas_call(
        functools.partial(_ffn_kernel, d_ff=d_ff, f_tile=f_tile, halo=halo, single_pos=single_pos),
        grid=(b, t // tm),
        in_specs=[pl.BlockSpec((1, tm, d), lambda i, j: (i, j, 0)), prev_spec] + [_const_spec(a.shape) for a in weights],
        out_specs=[pl.BlockSpec((1, tm, d), lambda i, j: (i, j, 0)),
                   pl.BlockSpec((1, a_rows, d_ff), lambda i, j: (i, 0, 0))],
        out_shape=[jax.ShapeDtypeStruct((b, t, d), F32), jax.ShapeDtypeStruct((b, a_rows, d_ff), F32)],
        compiler_params=_cparams(2), name="ffn_dec" if single_pos else "ffn",
    )(x1, prev, *weights)


def _compress_params(pe, w1, b1, w2, n_kv):
    cmp_len, hd = pe.shape[1], pe.shape[2]
    hid = w1.shape[2]
    halves = cmp_len // CMP_STRIDE
    eye = jnp.eye(n_kv, dtype=w1.dtype)
    pe_t = jnp.broadcast_to(pe.reshape(2, halves, CMP_STRIDE, 1, hd), (2, halves, CMP_STRIDE, n_kv, hd))
    pe_t = pe_t.reshape(2, halves, 1, CMP_STRIDE * n_kv * hd)
    w1h = w1.reshape(2, halves, CMP_STRIDE, hd, hid)
    w1b = jnp.einsum('krsdh,gf->ksgdrfh', w1h, eye).reshape(2, CMP_STRIDE * n_kv * hd, halves * n_kv * hid)
    b1t = jnp.tile(b1, (1, n_kv)).reshape(2, 1, n_kv * hid)
    w2b = jnp.einsum('khd,gf->kghfd', w2, eye).reshape(2, n_kv * hid, n_kv * hd)
    return pe_t, w1b.astype(BF16), b1t, w2b.astype(BF16)


def kernel(x_prompt, x_sample, cache_cmp, cache_slc, cache_win, state_conv, page_table, norm_mix, w_in, cmp_pe,
           cmp_w1, cmp_b1, cmp_w2, gmlp_norm, gmlp_ws, gmlp_bs, w_proj_a, w_proj_b, w_out, norm_ffn, w_up, conv_w,
           conv_b, w_down, norm_final):
    depth = w_in.shape[0]
    assert depth == 1, "single-layer step"
    bp, t, d = x_prompt.shape
    bd, tn, _ = x_sample.shape
    assert tn == 1
    n_kv, hd = cache_cmp.shape[4], cache_cmp.shape[5]
    page = cache_cmp.shape[2]
    q_cols = w_proj_a.shape[1]
    n_heads = q_cols // hd
    n_rep = n_heads // n_kv
    kv_cols = 2 * n_kv * hd
    width = w_proj_b.shape[1]
    chunk = gmlp_ws.shape[-1]
    d_ff = w_down.shape[1]
    n_pages = page_table.shape[1]
    past_len = n_pages * page
    scale = hd ** -0.5
    assert conv_w.shape[1] == 3 and cache_win.shape[2] <= WINDOW and past_len % SEL_BLOCK == 0

    wit = w_in[0].T.astype(BF16)
    off_kv, off_g = q_cols, q_cols + 3 * kv_cols
    off_uv = off_g + 3 * n_heads
    off_gate = off_uv + 2 * width
    w_qt, w_kvt, w_g = wit[0:off_kv], wit[off_kv:off_g], wit[off_g:off_uv]
    w_uv, w_gate = wit[off_uv:off_gate], wit[off_gate:]
    g_rows = 16
    w_gt = jnp.pad(w_g.reshape(n_kv, 3 * n_rep, d), ((0, 0), (0, g_rows - 3 * n_rep), (0, 0))).reshape(n_kv * g_rows, d)
    w_dec = jnp.concatenate([w_qt, w_kvt, jnp.pad(w_g, ((0, LANES - 3 * n_heads), (0, 0)))], axis=0)
    nm, nf, nfin, gn = norm_mix[0][None], norm_ffn[0][None], norm_final[None], gmlp_norm[0][None]
    pe_t, w1b, b1t, w2b = _compress_params(cmp_pe[0], cmp_w1[0], cmp_b1[0], cmp_w2[0], n_kv)
    wpa, wpb, wout = w_proj_a[0].astype(BF16), w_proj_b[0].astype(BF16), w_out[0].astype(BF16)
    wup, wdown = w_up[0].astype(BF16), w_down[0].astype(BF16)
    cw, cb = conv_w[0], conv_b[0][None]
    ws, bs = gmlp_ws[0], gmlp_bs[0]

    cmp_t, slc_t, win_t, qt, gt = _front(x_prompt, nm, w_kvt, w_qt, w_gt, kv_cols=kv_cols, scale=scale * LOG2_E,
                                         tm=512)
    kcv = _compress(cmp_t, pe_t, w1b, b1t, w2b, n_seq=bp, steps=1, rows=t // CMP_STRIDE)
    o_nsa = _attention(qt, gt, kcv, slc_t, win_t, n_rep=n_rep, hd=hd)
    x1, v_p = _mix(x_prompt, o_nsa, nm, w_uv, w_gate, gn, ws, bs.T, wpa, wpb, wout,
                   tm=256, chunk=chunk, single_pos=False)
    y_p, conv_p = _ffn(x1, x1, nf, wup, cw, cb, wdown, nfin, tm=256, f_tile=d_ff // 2, single_pos=False)

    xs = x_sample.reshape(bd, d)
    q_s, kv_s, g_s = _front_dec(xs, nm, w_dec, q_cols=q_cols, kv_cols3=3 * kv_cols, scale=scale)
    kv_cmp_s, kv_slc_s, kv_win_s = kv_s[:, 0:kv_cols], kv_s[:, kv_cols:2 * kv_cols], kv_s[:, 2 * kv_cols:]
    def positions_last(c):
        return jnp.transpose(c, (0, 2, 3, 4, 1)).reshape(c.shape[0], kv_cols, c.shape[1])

    cmp_pages, slc_pages, win_rows = positions_last(cache_cmp[0]), positions_last(cache_slc[0]), positions_last(cache_win[0])
    pages_per_step = 32
    kcv_s = _compress(cmp_pages, pe_t, w1b, b1t, w2b, n_seq=bd, steps=n_pages // pages_per_step,
                      rows=pages_per_step * page // CMP_STRIDE, page_table=page_table)
    q3 = q_s.reshape(bd, n_heads, hd)
    n_blocks_s = -(-(past_len + tn) // SEL_BLOCK)
    o_cmp_s, idx_s = _cmp_select_dec(q3, kcv_s, hd=hd, n_rep=n_rep, q_pos=past_len, n_blocks=n_blocks_s)
    sel = idx_s[:, 0:n_kv, 0:N_SEL].reshape(bd, n_kv * N_SEL)
    o_nsa_s = _attend_dec(page_table, sel, slc_pages, q3, kv_slc_s[:, None, :], kv_win_s[:, None, :], win_rows,
                          g_s[:, 0:3 * n_heads].reshape(bd, n_heads, 3), o_cmp_s,
                          hd=hd, n_rep=n_rep, n_sel=N_SEL, n_past_blocks=past_len // SEL_BLOCK)
    gw = width // GMLP_GROUPS
    ws0 = jnp.repeat(ws[:, 0, 0], gw)[None]
    bs0 = jnp.repeat(bs[:, 0], gw)[None]
    x1_s, v_s = _mix(xs[None], o_nsa_s.reshape(1, bd, q_cols), nm, w_uv, w_gate, gn, ws0, bs0, wpa, wpb, wout,
                     tm=bd, chunk=chunk, single_pos=True)
    prev_s = jnp.swapaxes(state_conv[0], 0, 1)
    y_s, a_s = _ffn(x1_s, prev_s, nf, wup, cw, cb, wdown, nfin, tm=bd, f_tile=d_ff // 2, single_pos=True)

    def rows6(a_t):
        n, _, npos = a_t.shape
        return jnp.transpose(a_t.reshape(n, 2, n_kv, hd, npos), (0, 4, 1, 2, 3))[None]

    win_keep = min(WINDOW, t)
    win_keep_s = min(WINDOW, cache_win.shape[2] + tn)
    win_s = jnp.concatenate([win_rows, kv_win_s[:, :, None]], axis=2)[:, :, cache_win.shape[2] + tn - win_keep_s:]
    conv_s = jnp.concatenate([state_conv[0][:, 1:], a_s[0][:, None, :]], axis=1)
    return (y_p, y_s.reshape(bd, tn, d),
            rows6(cmp_t), rows6(slc_t), rows6(win_t[:, :, t - win_keep:]),
            v_p[None], conv_p[None],
            rows6(kv_cmp_s[:, :, None]), rows6(kv_slc_s[:, :, None]), rows6(win_s),
            v_s.reshape(1, bd, tn, width), conv_s[None])
```

```python
import functools

import jax
import jax.numpy as jnp
from jax import lax
from jax.experimental import pallas as pl
from jax.experimental.pallas import tpu as pltpu

F32 = jnp.float32
BF16 = jnp.bfloat16
I32 = jnp.int32

CMP_STRIDE = 16
SEG_PITCH = 24
PAGE_LOOKAHEAD = 2
PAGE_SLOTS = PAGE_LOOKAHEAD + 1
CMP_PAGE_LOOKAHEAD = 3
CMP_PAGE_SLOTS = CMP_PAGE_LOOKAHEAD + 1
SEL_BLOCK = 64
N_SEL = 16
N_LOCAL_SEL = 2
WINDOW = 512
Q_BLOCK = 256
KEY_TILE = 128
SLC_TILES = 4
SLC_PAIR = 1
V_PAD = 16
LOG2_E = 1.4426950408889634
GMLP_GROUPS = 4
EPS = 1e-6
NEG = -1e30
BELOW_NEG = -3e38
SEL_BONUS = 1e6

V7X_VMEM_BYTES = 64 * 1024 * 1024
VMEM_REQUEST_BYTES = 56 * 1024 * 1024
LANES = 128


def _cparams(n_grid):
    return pltpu.CompilerParams(
        dimension_semantics=("arbitrary",) * n_grid, vmem_limit_bytes=VMEM_REQUEST_BYTES)


def _rmsnorm(x, g):
    ms = jnp.mean(x * x, axis=-1, keepdims=True)
    return x * lax.rsqrt(ms + EPS) * g


def _dot(a, b):
    return jnp.dot(a, b, preferred_element_type=F32)


def _dot_nt(a, b):
    return lax.dot_general(a, b, (((1,), (1,)), ((), ())), preferred_element_type=F32)


def _dot_exact(a, b):
    return jnp.dot(a, b, precision=lax.Precision.HIGHEST, preferred_element_type=F32)


def _shr(x, n):
    assert n & (n - 1) == 0
    return x >> (n.bit_length() - 1)


def _const_spec(shape):
    nd = len(shape)
    return pl.BlockSpec(shape, lambda *_: (0,) * nd)


def _resident_spec(shape):
    nd = len(shape)
    return pl.BlockSpec(shape, lambda *_: (0,) * nd, pipeline_mode=pl.Buffered(1))


def _front_kernel(x_ref, nw_ref, wkvt_ref, wqt_ref, wgt_ref, cmp_ref, slc_ref, win_ref, qt_ref, gt_ref,
                  *, kv_cols, scale):
    h = _rmsnorm(x_ref[0], nw_ref[...]).astype(BF16)
    kvt = _dot_nt(wkvt_ref[...], h)
    cmp_ref[0] = kvt[0:kv_cols]
    slc_ref[0] = kvt[kv_cols:2 * kv_cols]
    win_ref[0] = kvt[2 * kv_cols:3 * kv_cols]
    qt_ref[0] = (_dot_nt(wqt_ref[...], h) * scale).astype(BF16)
    gt_ref[0] = jax.nn.sigmoid(_dot_nt(wgt_ref[...], h))


def _front(x, nw, wkvt, wqt, wgt, *, kv_cols, scale, tm):
    b, t, d = x.shape
    q_cols, g_rows = wqt.shape[0], wgt.shape[0]
    kv_shape = jax.ShapeDtypeStruct((b, kv_cols, t), F32)
    kv_spec = pl.BlockSpec((1, kv_cols, tm), lambda i, j: (i, 0, j))
    return pl.pallas_call(
        functools.partial(_front_kernel, kv_cols=kv_cols, scale=scale),
        grid=(b, t // tm),
        in_specs=[pl.BlockSpec((1, tm, d), lambda i, j: (i, j, 0)), _const_spec(nw.shape),
                  _const_spec(wkvt.shape), _const_spec(wqt.shape), _const_spec(wgt.shape)],
        out_specs=[kv_spec, kv_spec, kv_spec,
                   pl.BlockSpec((1, q_cols, tm), lambda i, j: (i, 0, j)),
                   pl.BlockSpec((1, g_rows, tm), lambda i, j: (i, 0, j))],
        out_shape=[kv_shape, kv_shape, kv_shape,
                   jax.ShapeDtypeStruct((b, q_cols, t), BF16),
                   jax.ShapeDtypeStruct((b, g_rows, t), F32)],
        compiler_params=_cparams(2), name="front",
    )(x, nw, wkvt, wqt, wgt)


def _front_dec_kernel(x_ref, nw_ref, w_ref, q_ref, kv_ref, g_ref, *, q_cols, kv_cols3, scale):
    h = _rmsnorm(x_ref[...], nw_ref[...]).astype(BF16)
    z = _dot_nt(h, w_ref[...])
    q_ref[...] = z[:, 0:q_cols] * scale
    kv_ref[...] = z[:, q_cols:q_cols + kv_cols3]
    g_ref[...] = jax.nn.sigmoid(z[:, q_cols + kv_cols3:])


def _front_dec(x, nw, w, *, q_cols, kv_cols3, scale):
    n = x.shape[0]
    g_cols = w.shape[0] - q_cols - kv_cols3
    return pl.pallas_call(
        functools.partial(_front_dec_kernel, q_cols=q_cols, kv_cols3=kv_cols3, scale=scale),
        grid=(1,),
        in_specs=[_const_spec(x.shape), _const_spec(nw.shape), _const_spec(w.shape)],
        out_specs=[_const_spec((n, q_cols)), _const_spec((n, kv_cols3)), _const_spec((n, g_cols))],
        out_shape=[jax.ShapeDtypeStruct((n, q_cols), F32), jax.ShapeDtypeStruct((n, kv_cols3), F32),
                   jax.ShapeDtypeStruct((n, g_cols), F32)],
        compiler_params=_cparams(1), name="front_dec",
    )(x, nw, w)


def _segments_onto_rows(tile_of, n_tiles, pos_ref, kv, stride):
    seg_per_tile = LANES // stride
    for ti in range(n_tiles):
        t = tile_of(ti).T
        for n in range(seg_per_tile):
            p0 = (ti * seg_per_tile + n) * SEG_PITCH
            pos_ref[kv, p0:p0 + stride, :] = t[n * stride:(n + 1) * stride, :]


def _compress_rows(kv, pe_ref, w1_ref, b1_ref, w2_ref, out_ref, carry_ref, pos_ref, *, stride, half, hid2):
    rows = out_ref.shape[1]
    xkv = jnp.concatenate(
        [pos_ref[kv, pl.ds(s, rows, stride=SEG_PITCH), :] for s in range(stride)],
        axis=1)
    parts = []
    for r in range(2):
        a = (xkv + pe_ref[kv, r]).astype(BF16)
        parts.append(_dot(a, w1_ref[kv, :, r * hid2:(r + 1) * hid2]))
    prev = carry_ref[kv, 0:1, :]
    row = lax.broadcasted_iota(I32, (rows, hid2), 0)
    shifted = jnp.where(row == 0, prev, pltpu.roll(parts[0], 1, 0))
    carry_ref[kv, 0:1, :] = parts[0][rows - 1:rows, :]
    hid = b1_ref[kv] + shifted + parts[1]
    out_ref[0, :, kv * half:(kv + 1) * half] = _dot(jax.nn.gelu(hid).astype(BF16), w2_ref[kv])


def _compress_kernel(x_ref, pe_ref, w1_ref, b1_ref, w2_ref, out_ref, carry_ref, pos_ref, *, stride, kv_cols, hid2):
    half = kv_cols // 2
    assert half == LANES

    @pl.when(pl.program_id(1) == 0)
    def _():
        carry_ref[...] = jnp.zeros_like(carry_ref)

    for kv in range(2):
        _segments_onto_rows(lambda ti: x_ref[0, kv * half:(kv + 1) * half, ti * LANES:(ti + 1) * LANES],
                            x_ref.shape[2] // LANES, pos_ref, kv, stride)
        _compress_rows(kv, pe_ref, w1_ref, b1_ref, w2_ref, out_ref, carry_ref, pos_ref,
                       stride=stride, half=half, hid2=hid2)


def _compress_paged_kernel(pt_ref, pages_ref, pe_ref, w1_ref, b1_ref, w2_ref, out_ref, carry_ref, pos_ref, buf_ref,
                           sem, *, stride, kv_cols, hid2, n_pages):
    half = kv_cols // 2
    assert half == LANES and buf_ref.shape[3] == LANES
    steps = pl.num_programs(1)
    total = pl.num_programs(0) * steps
    t = pl.program_id(0) * steps + pl.program_id(1)
    look, slots = CMP_PAGE_LOOKAHEAD, CMP_PAGE_SLOTS
    static = dict(stride=stride, half=half, hid2=hid2)

    def page_copy(src_step, ring_step, k):
        page = pt_ref[src_step // steps, (src_step % steps) * n_pages + k]
        slot = ring_step % slots
        return pltpu.make_async_copy(pages_ref.at[page], buf_ref.at[slot, k], sem.at[slot])

    def rows_of(ring_step, kv):
        slot = ring_step % slots
        _segments_onto_rows(lambda ti: buf_ref[slot, ti, kv * half:(kv + 1) * half, :], n_pages, pos_ref, kv, stride)

    @pl.when(t == 0)
    def _():
        for ahead in range(look):
            for k in range(n_pages):
                page_copy(jnp.minimum(ahead, total - 1), ahead, k).start()
        for k in range(n_pages):
            page_copy(0, 0, k).wait()
        rows_of(0, 0)

    @pl.when(pl.program_id(1) == 0)
    def _():
        carry_ref[...] = jnp.zeros_like(carry_ref)

    for k in range(n_pages):
        page_copy(t, t + 1, k).wait()
    _compress_rows(0, pe_ref, w1_ref, b1_ref, w2_ref, out_ref, carry_ref, pos_ref, **static)
    rows_of(t, 1)
    _compress_rows(1, pe_ref, w1_ref, b1_ref, w2_ref, out_ref, carry_ref, pos_ref, **static)
    rows_of(t + 1, 0)
    for k in range(n_pages):
        page_copy(jnp.minimum(t + look, total - 1), t + look, k).start()

    @pl.when(t == total - 1)
    def _():
        for ahead in range(2, look + 1):
            for k in range(n_pages):
                page_copy(t, t + ahead, k).wait()


def _compress(x, pe, w1, b1, w2, *, n_seq, steps, rows, page_table=None):
    stride, kv_cols = CMP_STRIDE, w2.shape[2] * 2
    hid2 = b1.shape[2]
    static = dict(stride=stride, kv_cols=kv_cols, hid2=hid2)
    out_shape = jax.ShapeDtypeStruct((n_seq, steps * rows, kv_cols), F32)
    scratch = [pltpu.VMEM((2, 8, hid2), F32), pltpu.VMEM((2, rows * SEG_PITCH, kv_cols // 2), F32)]
    weights = (pe, w1, b1, w2)
    if page_table is None:
        return pl.pallas_call(
            functools.partial(_compress_kernel, **static), grid=(n_seq, steps),
            in_specs=[pl.BlockSpec((1, kv_cols, rows * stride), lambda i, j: (i, 0, j))]
                     + [_const_spec(a.shape) for a in weights],
            out_specs=pl.BlockSpec((1, rows, kv_cols), lambda i, j: (i, j, 0)),
            out_shape=out_shape, scratch_shapes=scratch, compiler_params=_cparams(2), name="compress",
        )(x, *weights)
    page = x.shape[2]
    n_pages = rows * stride // page
    assert n_seq * steps > CMP_PAGE_LOOKAHEAD
    w_specs = [pl.BlockSpec(a.shape, functools.partial(lambda i, j, pt, nd: (0,) * nd, nd=a.ndim)) for a in weights]
    return pl.pallas_call(
        functools.partial(_compress_paged_kernel, n_pages=n_pages, **static),
        grid_spec=pltpu.PrefetchScalarGridSpec(
            num_scalar_prefetch=1, grid=(n_seq, steps),
            in_specs=[pl.BlockSpec(memory_space=pl.ANY)] + w_specs,
            out_specs=pl.BlockSpec((1, rows, kv_cols), lambda i, j, pt: (i, j, 0)),
            scratch_shapes=scratch + [pltpu.VMEM((CMP_PAGE_SLOTS, n_pages, kv_cols, page), F32),
                                      pltpu.SemaphoreType.DMA((CMP_PAGE_SLOTS,))]),
        out_shape=out_shape, compiler_params=_cparams(2), name="compress_paged",
    )(page_table, x, *weights)


def _block_scores(imp, blk, q_pos, n_blocks):
    cur = _shr(q_pos, SEL_BLOCK)
    valid = (blk * SEL_BLOCK <= q_pos) & (blk < n_blocks)
    forced = (blk == 0) | ((blk <= cur) & (blk > cur - N_LOCAL_SEL))
    score = jnp.where(valid, imp + jnp.where(forced, SEL_BONUS, 0.0), NEG)
    return jnp.where(blk < n_blocks, score, BELOW_NEG)


def _attn_kernel(qt_ref, gt_ref, kcv_ref, slc_ref, win_ref, o_ref,
                 kaug_ref, vts_ref, kwin_ref, vtw_ref, kc_ref, vct_ref, *, n_kv, n_rep, hd, n_blocks):
    i = pl.program_id(1)
    qb, kt = Q_BLOCK, KEY_TILE
    q_tiles = qb // kt
    n_tiles = slc_ref.shape[2] // kt
    n_cmp_rows = kcv_ref.shape[1]
    nq = n_rep * qb
    kd = n_kv * hd
    g_rows = gt_ref.shape[1] // n_kv
    assert kd == LANES and n_blocks == hd, "the selection bias rows ride in the key one-hot lanes"

    @pl.when(i == 0)
    def _():
        lane = lax.broadcasted_iota(I32, (kt, kd), 1)
        krow = lax.broadcasted_iota(I32, (kt, kd), 0)

        def group_lanes(x, gg):
            return x if gg == 0 else pltpu.roll(x, kd - gg * hd, 1)

        ones_row = (lax.broadcasted_iota(I32, (V_PAD, kt), 0) == 0).astype(BF16)
        for c in range(n_tiles):
            cols = slice(c * kt, (c + 1) * kt)
            onehot = (lane - hd == _shr(c * kt + krow, SEL_BLOCK)).astype(F32)
            kt_s, kt_w = slc_ref[0, 0:kd, cols].T, win_ref[0, 0:kd, cols].T
            for gg in range(n_kv):
                v_rows = slice(kd + gg * hd, kd + (gg + 1) * hd)
                kaug_ref[gg, cols, :] = jnp.where(lane < hd, group_lanes(kt_s, gg), onehot).astype(BF16)
                side = slice((c % SLC_PAIR) * kt, (c % SLC_PAIR + 1) * kt)
                vts_ref[gg, c // SLC_PAIR, 0:hd, side] = slc_ref[0, v_rows, cols].astype(BF16)
                vts_ref[gg, c // SLC_PAIR, hd:, side] = ones_row
                kwin_ref[gg, cols, :] = group_lanes(kt_w, gg)[:, 0:hd].astype(BF16)
                vtw_ref[gg, c, 0:hd, :] = win_ref[0, v_rows, cols].astype(BF16)
                vtw_ref[gg, c, hd:, :] = ones_row
        for c in range(n_cmp_rows // kt):
            rows = slice(c * kt, (c + 1) * kt)
            blk = kcv_ref[0, rows, :]
            vt = blk[:, kd:2 * kd].T
            for gg in range(n_kv):
                kc_ref[gg, rows, :] = group_lanes(blk[:, 0:kd], gg)[:, 0:hd].astype(BF16)
                vct_ref[gg, :, rows] = vt[gg * hd:(gg + 1) * hd, :].astype(BF16)

    q_pos = i * qb + (lax.broadcasted_iota(I32, (1, nq), 1) & (qb - 1))

    groups = range(n_kv)
    heads = range(n_rep)

    q_minus_k = (lax.broadcasted_iota(I32, (kt, nq), 1) & (qb - 1)) - lax.broadcasted_iota(I32, (kt, nq), 0)

    def softmax_step(m, sc):
        m_new = jnp.maximum(m, jnp.max(sc, axis=0, keepdims=True))
        return m_new, jnp.exp2(m - m_new), jnp.exp2(sc - m_new).astype(BF16)

    def normalised(acc):
        return acc[0:hd, :] * (1.0 / acc[hd:hd + 1, :])

    qts = [qt_ref[0, gg * n_rep * hd:(gg + 1) * n_rep * hd, :] for gg in groups]
    qcats = [jnp.concatenate([qts[gg][h * hd:(h + 1) * hd, :] for h in heads], axis=1) for gg in groups]

    def attend(k_ref, v_ref, qs, tile_ids, state, masked, per_update=1):
        scores = [[masked(u, _dot(k_ref[gg, pl.ds(pl.multiple_of(t * kt, kt), kt), :], qs[gg])) for gg in groups]
                  for u, t in enumerate(tile_ids)]
        maxes, accs = list(state[0]), list(state[1])
        for u in range(0, len(tile_ids), per_update):
            for gg in groups:
                sc = jnp.concatenate([scores[u + d][gg] for d in range(per_update)], axis=0)
                maxes[gg], alpha, pb = softmax_step(maxes[gg], sc)
                accs[gg] = alpha * accs[gg] + _dot(v_ref[gg, _shr(tile_ids[u], per_update)], pb)
        return tuple(maxes), tuple(accs)

    state0 = ((jnp.full((1, nq), NEG, F32),) * n_kv, (jnp.zeros((hd + V_PAD, nq), F32),) * n_kv)

    s_cmp = [_dot(kc_ref[gg], qcats[gg]) for gg in groups]
    m_idx = lax.broadcasted_iota(I32, (n_cmp_rows, nq), 0)
    vis = (m_idx >= 1) & ((m_idx - 1) * CMP_STRIDE + 2 * CMP_STRIDE - 1 <= q_pos)
    ratio = SEL_BLOCK // CMP_STRIDE
    pj = lax.broadcasted_iota(I32, (n_blocks, n_cmp_rows), 0)
    pm = lax.broadcasted_iota(I32, (n_blocks, n_cmp_rows), 1)
    pool = ((pm >= 1) & (pm >= ratio * pj) & (pm <= ratio * pj + ratio)).astype(F32)
    o_cmp, imp = [], []
    for gg in groups:
        s = jnp.where(vis, s_cmp[gg], NEG)
        e = jnp.where(vis, jnp.exp2(s - jnp.max(s, axis=0, keepdims=True)), 0.0)
        den = jnp.sum(e, axis=0, keepdims=True)
        p = e * (1.0 / jnp.where(den > 0, den, 1.0))
        o_cmp.append(_dot(vct_ref[gg], p.astype(BF16)))
        p_grp = p[:, 0:qb]
        for h in range(1, n_rep):
            p_grp = p_grp + p[:, h * qb:(h + 1) * qb]
        imp.append(_dot_exact(pool, p_grp))

    n_band = WINDOW // kt
    first_t = i * q_tiles - n_band
    win_tiles = [jnp.maximum(first_t + u, 0) for u in range(n_band + q_tiles)]

    def in_window(u, sc):
        dlt = q_minus_k + (n_band - u) * kt
        if u < q_tiles:
            sc = jnp.where(dlt <= WINDOW, sc, NEG)
        if u >= n_band:
            sc = jnp.where(dlt >= 0, sc, NEG)
        if u < n_band:
            sc = sc + jnp.where(first_t + u < 0, NEG, 0.0)
        if u == 0:
            sc = sc + after_importance
        return sc

    after_importance = sum(imp[gg][0:1, 0:1] for gg in groups) * 0.0

    _, accs = attend(kwin_ref, vtw_ref, qcats, win_tiles, state0, in_window)
    o_win = [normalised(accs[gg]) for gg in groups]

    blk_id = lax.broadcasted_iota(I32, (n_blocks, qb), 0)
    blk_f = blk_id.astype(F32)
    qaug = []
    for gg in groups:
        score = _block_scores(imp[gg], blk_id, q_pos[:, 0:qb], n_blocks)
        work, sel = score, jnp.zeros((n_blocks, qb), F32)
        for _ in range(min(N_SEL, n_blocks)):
            mx = jnp.max(work, axis=0, keepdims=True)
            first = jnp.min(jnp.where(work == mx, blk_f, float(n_blocks)), axis=0, keepdims=True)
            pick = blk_f == first
            sel = jnp.where(pick, 1.0, sel)
            work = jnp.where(pick, BELOW_NEG, work)
        bias = jnp.where((sel > 0) & (score > NEG / 2), 0.0, NEG).astype(BF16)
        qaug.append(jnp.concatenate(
            [jnp.concatenate([qts[gg][h * hd:(h + 1) * hd, :], bias], axis=0) for h in heads], axis=1))

    assert n_tiles % SLC_TILES == 0 and SLC_TILES % q_tiles == 0
    n_full = _shr(i * q_tiles, SLC_TILES)

    def unmasked(first_tile, n_t):
        return lambda j, st: attend(kaug_ref, vts_ref, qaug, [first_tile + j * n_t + u for u in range(n_t)], st,
                                    lambda u, sc: sc, SLC_PAIR)

    state = lax.fori_loop(0, n_full >> 1, unmasked(0, 2 * SLC_TILES), state0)
    state = lax.fori_loop(0, n_full & 1, unmasked((n_full >> 1) * 2 * SLC_TILES, SLC_TILES), state)
    last = n_full * SLC_TILES
    _, accs = attend(kaug_ref, vts_ref, qaug, [last + u for u in range(SLC_TILES)], state,
                     lambda u, sc: jnp.where(q_minus_k + (i * q_tiles - last - u) * kt >= 0, sc, NEG), SLC_PAIR)

    outs = []
    for gg in groups:
        o_slc = normalised(accs[gg])
        for h in heads:
            cols = slice(h * qb, (h + 1) * qb)
            gate = [gt_ref[0, gg * g_rows + 3 * h + br:gg * g_rows + 3 * h + br + 1, :] for br in range(3)]
            outs.append(gate[0] * o_cmp[gg][:, cols] + gate[1] * o_slc[:, cols] + gate[2] * o_win[gg][:, cols])
    o_ref[0] = jnp.concatenate(outs, axis=0).T.astype(BF16)


def _attention(qt, gt, kcv, slc, win, *, n_rep, hd):
    b, q_cols, t = qt.shape
    n_kv = q_cols // (n_rep * hd)
    qb, kt = Q_BLOCK, KEY_TILE
    n_tiles = t // kt
    n_blocks = t // SEL_BLOCK
    kv_cols = slc.shape[1]
    cmp_rows = kcv.shape[1]
    tile_spec = pl.BlockSpec((1, kv_cols, t), lambda bi, i: (bi, 0, 0))
    return pl.pallas_call(
        functools.partial(_attn_kernel, n_kv=n_kv, n_rep=n_rep, hd=hd, n_blocks=n_blocks),
        grid=(b, t // qb),
        in_specs=[pl.BlockSpec((1, q_cols, qb), lambda bi, i: (bi, 0, i)),
                  pl.BlockSpec((1, gt.shape[1], qb), lambda bi, i: (bi, 0, i)),
                  pl.BlockSpec((1,) + kcv.shape[1:], lambda bi, i: (bi, 0, 0)),
                  tile_spec, tile_spec],
        out_specs=pl.BlockSpec((1, qb, q_cols), lambda bi, i: (bi, i, 0)),
        out_shape=jax.ShapeDtypeStruct((b, t, q_cols), BF16),
        scratch_shapes=[pltpu.VMEM((n_kv, t, 2 * hd), BF16),
                        pltpu.VMEM((n_kv, n_tiles // SLC_PAIR, hd + V_PAD, SLC_PAIR * kt), BF16),
                        pltpu.VMEM((n_kv, t, hd), BF16), pltpu.VMEM((n_kv, n_tiles, hd + V_PAD, kt), BF16),
                        pltpu.VMEM((n_kv, cmp_rows, hd), BF16), pltpu.VMEM((n_kv, hd, cmp_rows), BF16)],
        compiler_params=_cparams(2), name="attention",
    )(qt, gt, kcv, slc, win)


def _spread_q(q, hd, n_rep):
    n_heads = q.shape[0]
    d = lax.broadcasted_iota(I32, (hd, 4 * hd), 0)
    c = lax.broadcasted_iota(I32, (hd, 4 * hd), 1)
    qb16 = q.astype(BF16)
    row = lax.broadcasted_iota(I32, (n_heads, 4 * hd), 0)
    out = jnp.zeros((n_heads, 4 * hd), F32)
    for gg in range(n_heads // n_rep):
        placed = _dot(qb16, (c == d + gg * hd).astype(BF16))
        out = jnp.where(_shr(row, n_rep) == gg, placed, out)
    return out.astype(BF16)


def _masked_softmax_rows(s, mask):
    s = jnp.where(mask, s, NEG)
    e = jnp.where(mask, jnp.exp(s - jnp.max(s, axis=-1, keepdims=True)), 0.0)
    den = jnp.sum(e, axis=-1, keepdims=True)
    return e * (1.0 / jnp.where(den > 0, den, 1.0))


def _group_value_lanes(o_full, hd, n_rep):
    row = lax.broadcasted_iota(I32, (o_full.shape[0], hd), 0)
    out = o_full[:, 2 * hd:3 * hd]
    for gg in range(1, o_full.shape[0] // n_rep):
        out = jnp.where(_shr(row, n_rep) == gg, o_full[:, (2 + gg) * hd:(3 + gg) * hd], out)
    return out


def _cmp_select_dec_kernel(q_ref, kcv_ref, o_ref, idx_ref, pool_ref, score_ref, *, hd, n_rep, q_pos, n_blocks,
                           blk_lanes):
    n_heads = q_ref.shape[1]
    n_rows = kcv_ref.shape[1]
    b = pl.program_id(0)
    n_seq = idx_ref.shape[0]

    @pl.when(b == 0)
    def _():
        ratio = SEL_BLOCK // CMP_STRIDE
        pm = lax.broadcasted_iota(I32, (n_rows, blk_lanes), 0)
        pj = lax.broadcasted_iota(I32, (n_rows, blk_lanes), 1)
        pool_ref[...] = ((pm >= 1) & (pm >= ratio * pj) & (pm <= ratio * pj + ratio)).astype(BF16)

    q2 = _spread_q(q_ref[0], hd, n_rep)
    kcv = kcv_ref[0].astype(BF16)
    m_idx = lax.broadcasted_iota(I32, (n_heads, n_rows), 1)
    vis = (m_idx >= 1) & ((m_idx - 1) * CMP_STRIDE + 2 * CMP_STRIDE - 1 <= q_pos)
    p = _masked_softmax_rows(_dot_nt(q2, kcv), vis)
    o_ref[0] = _group_value_lanes(_dot(p.astype(BF16), kcv), hd, n_rep)

    row = lax.broadcasted_iota(I32, (n_heads, n_rows), 0)
    grp = jnp.zeros((n_heads, n_rows), F32)
    for gg in range(n_heads // n_rep):
        tot = jnp.sum(jnp.where(_shr(row, n_rep) == gg, p, 0.0), axis=0, keepdims=True)
        grp = jnp.where(row == gg, tot, grp)
    hi = grp.astype(BF16).astype(F32)
    mid = (grp - hi).astype(BF16).astype(F32)
    lo = grp - hi - mid
    pieces = _dot(jnp.concatenate([hi, mid, lo], axis=0).astype(BF16), pool_ref[...])
    imp = pieces[0:n_heads] + pieces[n_heads:2 * n_heads] + pieces[2 * n_heads:]
    score_ref[b] = _block_scores(imp, lax.broadcasted_iota(I32, (n_heads, blk_lanes), 1), q_pos, n_blocks)

    @pl.when(b == n_seq - 1)
    def _():
        work = score_ref[...].reshape(n_seq * n_heads, blk_lanes)
        blk_f = lax.broadcasted_iota(I32, work.shape, 1).astype(F32)
        out_lane = lax.broadcasted_iota(I32, (n_seq * n_heads, LANES), 1)
        out = jnp.full((n_seq * n_heads, LANES), -1, I32)
        for it in range(min(N_SEL, n_blocks)):
            mx = jnp.max(work, axis=-1, keepdims=True)
            first = jnp.min(jnp.where(work == mx, blk_f, float(blk_lanes)), axis=-1, keepdims=True)
            out = jnp.where(out_lane == it, jnp.where(mx > NEG / 2, first.astype(I32), -1), out)
            work = jnp.where(blk_f == first, BELOW_NEG, work)
        idx_ref[...] = out.reshape(n_seq, n_heads, LANES)


def _cmp_select_dec(q3, kcv, *, hd, n_rep, q_pos, n_blocks):
    n, n_heads, _ = q3.shape
    blk_lanes = -(-n_blocks // LANES) * LANES
    return pl.pallas_call(
        functools.partial(_cmp_select_dec_kernel, hd=hd, n_rep=n_rep, q_pos=q_pos, n_blocks=n_blocks,
                          blk_lanes=blk_lanes),
        grid=(n,),
        in_specs=[pl.BlockSpec((1, n_heads, hd), lambda i: (i, 0, 0)),
                  pl.BlockSpec((1,) + kcv.shape[1:], lambda i: (i, 0, 0))],
        out_specs=[pl.BlockSpec((1, n_heads, hd), lambda i: (i, 0, 0)),
                   _const_spec((n, n_heads, LANES))],
        out_shape=[jax.ShapeDtypeStruct((n, n_heads, hd), F32), jax.ShapeDtypeStruct((n, n_heads, LANES), I32)],
        scratch_shapes=[pltpu.VMEM((kcv.shape[1], blk_lanes), BF16), pltpu.VMEM((n, n_heads, blk_lanes), F32)],
        compiler_params=_cparams(1), name="cmp_select_dec",
    )(q3, kcv)


def _attend_dec_kernel(pt_ref, sel_ref, pages_ref, q_ref, new_slc_ref, new_win_ref, cwin_ref, g_ref, ocmp_ref, o_ref,
                       buf_ref, sem, *, hd, n_rep, n_sel, n_past_blocks):
    b = pl.program_id(0)
    total = pl.num_programs(0)
    n_heads = q_ref.shape[1]
    n_kv = n_heads // n_rep
    n_pages = n_kv * n_sel
    page = buf_ref.shape[3]
    per = page // SEL_BLOCK

    def page_copy(src_seq, ring_step, k):
        idx = jnp.clip(sel_ref[src_seq, k], 0, n_past_blocks - 1)
        slot = ring_step % PAGE_SLOTS
        return pltpu.make_async_copy(pages_ref.at[pt_ref[src_seq, _shr(idx, per)]], buf_ref.at[slot, k], sem.at[slot])

    @pl.when(b == 0)
    def _():
        for ahead in range(PAGE_LOOKAHEAD):
            for k in range(n_pages):
                page_copy(jnp.minimum(ahead, total - 1), ahead, k).start()

    for k in range(n_pages):
        page_copy(b, b, k).wait()
    ring_slot = b % PAGE_SLOTS

    q2 = _spread_q(q_ref[0], hd, n_rep)
    q2f = q2.astype(F32)
    head_grp = _shr(lax.broadcasted_iota(I32, (n_heads, 1), 0), n_rep)

    def with_new_key(s, mask, keys_t, new_row, new_ok):
        nr = new_row.astype(BF16).astype(F32)
        s_new = jnp.sum(q2f * nr, axis=-1, keepdims=True)
        s = jnp.where(mask, s, NEG)
        s_new = jnp.where(new_ok, s_new, NEG)
        m = jnp.maximum(jnp.max(s, axis=-1, keepdims=True), s_new)
        e = jnp.where(mask, jnp.exp(s - m), 0.0)
        e_new = jnp.where(new_ok, jnp.exp(s_new - m), 0.0)
        den = jnp.sum(e, axis=-1, keepdims=True) + e_new
        inv = 1.0 / jnp.where(den > 0, den, 1.0)
        return _dot_nt((e * inv).astype(BF16), keys_t) + (e_new * inv) * nr

    keys_t = jnp.concatenate([buf_ref[ring_slot, k] for k in range(n_pages)], axis=1).astype(BF16)
    n_keys = keys_t.shape[1]
    col_slot = _shr(lax.broadcasted_iota(I32, (1, n_keys), 1), page)
    page_blk = _shr(lax.broadcasted_iota(I32, (1, page), 1), SEL_BLOCK)
    slot_ok = []
    new_ok = jnp.zeros((n_heads, 1), I32)
    for gg in range(n_kv):
        for k in range(n_sel):
            idx = sel_ref[b, gg * n_sel + k]
            cached = ((idx >= 0) & (idx < n_past_blocks)).astype(I32)
            slot_ok.append(jnp.where(page_blk == (idx & (per - 1)), cached, 0))
            new_ok = jnp.where(head_grp == gg, new_ok | (idx == n_past_blocks).astype(I32), new_ok)
    mask = (jnp.concatenate(slot_ok, axis=1) > 0) & (_shr(col_slot, n_sel) == head_grp)
    o_slc = _group_value_lanes(with_new_key(_dot(q2, keys_t), mask, keys_t, new_slc_ref[0], new_ok > 0), hd, n_rep)

    keys_t = cwin_ref[0].astype(BF16)
    all_ok = jnp.full((n_heads, keys_t.shape[1]), True)
    o_win = _group_value_lanes(
        with_new_key(_dot(q2, keys_t), all_ok, keys_t, new_win_ref[0], jnp.full((n_heads, 1), True)), hd, n_rep)

    gates = g_ref[0]
    o_ref[0] = gates[:, 0:1] * ocmp_ref[0] + gates[:, 1:2] * o_slc + gates[:, 2:3] * o_win

    for k in range(n_pages):
        page_copy(jnp.minimum(b + PAGE_LOOKAHEAD, total - 1), b + PAGE_LOOKAHEAD, k).start()

    @pl.when(b == total - 1)
    def _():
        for ahead in range(1, PAGE_LOOKAHEAD + 1):
            for k in range(n_pages):
                page_copy(b, b + ahead, k).wait()


def _attend_dec(page_table, sel, slc_pages, q3, new_slc, new_win, cache_win, gates3, o_cmp, *, hd, n_rep, n_sel,
                n_past_blocks):
    n, n_heads, _ = q3.shape
    n_kv = n_heads // n_rep
    assert n > PAGE_LOOKAHEAD and n_past_blocks * SEL_BLOCK == page_table.shape[1] * slc_pages.shape[2]

    def row_spec(shape):
        nd = len(shape)
        return pl.BlockSpec((1,) + tuple(shape[1:]), lambda i, pt, sl: (i,) + (0,) * (nd - 1))

    others = (q3, new_slc, new_win, cache_win, gates3, o_cmp)
    return pl.pallas_call(
        functools.partial(_attend_dec_kernel, hd=hd, n_rep=n_rep, n_sel=n_sel, n_past_blocks=n_past_blocks),
        grid_spec=pltpu.PrefetchScalarGridSpec(
            num_scalar_prefetch=2, grid=(n,),
            in_specs=[pl.BlockSpec(memory_space=pl.ANY)] + [row_spec(a.shape) for a in others],
            out_specs=row_spec(o_cmp.shape),
            scratch_shapes=[pltpu.VMEM((PAGE_SLOTS, n_kv * n_sel) + slc_pages.shape[1:], F32),
                            pltpu.SemaphoreType.DMA((PAGE_SLOTS,))]),
        out_shape=jax.ShapeDtypeStruct(o_cmp.shape, F32),
        compiler_params=_cparams(1), name="attend_dec",
    )(page_table, sel, slc_pages, *others)


def _mix_kernel(x_ref, o_ref, nw_ref, wuv_ref, wgate_ref, gn_ref, ws_ref, bs_ref, wpa_ref, wpb_ref, wout_ref,
                x1_ref, v_ref, *, width, chunk, single_pos):
    x = x_ref[0]
    d = x.shape[-1]
    h = _rmsnorm(x, nw_ref[...]).astype(BF16)
    uv = jax.nn.gelu(_dot_nt(h, wuv_ref[...]))
    u, vn = uv[:, 0:width], _rmsnorm(uv[:, width:2 * width], gn_ref[...])
    gates = jax.nn.sigmoid(_dot_nt(h, wgate_ref[...]))
    gw = width // GMLP_GROUPS
    if single_pos:
        v_ref[0] = vn
        mixed = u * (ws_ref[...] * vn + bs_ref[...])
    else:
        rows = x.shape[0]
        v_ref[0] = vn[rows - chunk:rows, :]
        tri = lax.broadcasted_iota(I32, (chunk, chunk), 0) >= lax.broadcasted_iota(I32, (chunk, chunk), 1)
        vb = vn.astype(BF16)
        pieces = []
        for c in range(rows // chunk):
            zs = []
            for gi in range(GMLP_GROUPS):
                wm = jnp.where(tri, ws_ref[gi], 0.0).astype(BF16)
                zs.append(_dot(wm, vb[c * chunk:(c + 1) * chunk, gi * gw:(gi + 1) * gw]) + bs_ref[:, gi:gi + 1])
            pieces.append(jnp.concatenate(zs, axis=1))
        mixed = u * jnp.concatenate(pieces, axis=0)
    br_a = _dot(o_ref[0].astype(BF16), wpa_ref[...])
    br_b = _dot(mixed.astype(BF16), wpb_ref[...])
    merged = gates[:, 0:d] * br_a + gates[:, d:2 * d] * br_b
    x1_ref[0] = x + _dot(merged.astype(BF16), wout_ref[...])


def _mix(x, o_nsa, nw, wuv, wgate, gn, ws, bs, wpa, wpb, wout, *, tm, chunk, single_pos):
    b, t, d = x.shape
    width = wuv.shape[0] // 2
    v_rows = tm if single_pos else chunk
    weights = (nw, wuv, wgate, gn, ws, bs, wpa, wpb, wout)
    return pl.pallas_call(
        functools.partial(_mix_kernel, width=width, chunk=chunk, single_pos=single_pos),
        grid=(b, t // tm),
        in_specs=[pl.BlockSpec((1, tm, d), lambda i, j: (i, j, 0)),
                  pl.BlockSpec((1, tm, o_nsa.shape[-1]), lambda i, j: (i, j, 0))]
                 + [_resident_spec(a.shape) for a in weights],
        out_specs=[pl.BlockSpec((1, tm, d), lambda i, j: (i, j, 0)),
                   pl.BlockSpec((1, v_rows, width), lambda i, j: (i, 0, 0))],
        out_shape=[jax.ShapeDtypeStruct((b, t, d), F32), jax.ShapeDtypeStruct((b, v_rows, width), F32)],
        compiler_params=_cparams(2), name="mix_dec" if single_pos else "mix",
    )(x, o_nsa, *weights)


def _ffn_kernel(x1_ref, prev_ref, nf_ref, wup_ref, cw_ref, cb_ref, wdown_ref, nfin_ref, y_ref, a_ref,
                *, d_ff, f_tile, halo, single_pos):
    x1 = x1_ref[0]
    rows = x1.shape[0]
    if single_pos:
        h = _rmsnorm(x1, nf_ref[...]).astype(BF16)
    else:
        h = _rmsnorm(jnp.concatenate([prev_ref[0], x1], axis=0), nf_ref[...]).astype(BF16)
        ext_row = lax.broadcasted_iota(I32, (rows + halo, f_tile), 0)
        first = pl.program_id(1) == 0
    y = jnp.zeros_like(x1)
    for f0 in range(0, d_ff, f_tile):
        cols = slice(f0, f0 + f_tile)
        a = _dot(h, wup_ref[:, cols])
        bgate = _dot(h, wup_ref[:, d_ff + f0:d_ff + f0 + f_tile])
        if single_pos:
            a_ref[0, :, cols] = a
            c = cb_ref[:, cols] + prev_ref[0, :, cols] * cw_ref[0:1, cols] + prev_ref[1, :, cols] * cw_ref[1:2, cols] \
                + a * cw_ref[2:3, cols]
        else:
            a = jnp.where((ext_row < halo) & first, 0.0, a)
            back2 = pltpu.roll(a, 2, 0)
            a_ref[0, :, cols] = back2[0:2, :]
            c = cb_ref[:, cols] + back2[halo:, :] * cw_ref[0:1, cols] \
                + pltpu.roll(a, 1, 0)[halo:, :] * cw_ref[1:2, cols] + a[halo:, :] * cw_ref[2:3, cols]
            bgate = bgate[halo:, :]
        y = y + _dot((jax.nn.gelu(c) * bgate).astype(BF16), wdown_ref[cols, :])
    y_ref[0] = _rmsnorm(x1 + y, nfin_ref[...])


def _ffn(x1, prev, nf, wup, cw, cb, wdown, nfin, *, tm, f_tile, single_pos):
    b, t, d = x1.shape
    d_ff = wdown.shape[0]
    halo = 8
    weights = (nf, wup, cw, cb, wdown, nfin)
    if single_pos:
        prev_spec = _const_spec(prev.shape)
        a_rows = tm
    else:
        per = tm // halo
        prev_spec = pl.BlockSpec((1, halo, d), lambda i, j: (i, jnp.maximum(j * per - 1, 0), 0))
        a_rows = 2
    return pl.pallas_call(
        functools.partial(_ffn_kernel, d_ff=d_ff, f_tile=f_tile, halo=halo, single_pos=single_pos),
        grid=(b, t // tm),
        in_specs=[pl.BlockSpec((1, tm, d), lambda i, j: (i, j, 0)), prev_spec]
                 + [_resident_spec(a.shape) for a in weights],
        out_specs=[pl.BlockSpec((1, tm, d), lambda i, j: (i, j, 0)),
                   pl.BlockSpec((1, a_rows, d_ff), lambda i, j: (i, 0, 0))],
        out_shape=[jax.ShapeDtypeStruct((b, t, d), F32), jax.ShapeDtypeStruct((b, a_rows, d_ff), F32)],
        compiler_params=_cparams(2), name="ffn_dec" if single_pos else "ffn",
    )(x1, prev, *weights)


def _compress_params(pe, w1, b1, w2, n_kv):
    cmp_len, hd = pe.shape[1], pe.shape[2]
    hid = w1.shape[2]
    halves = cmp_len // CMP_STRIDE
    eye = jnp.eye(n_kv, dtype=w1.dtype)
    pe_t = jnp.broadcast_to(pe.reshape(2, halves, CMP_STRIDE, 1, hd), (2, halves, CMP_STRIDE, n_kv, hd))
    pe_t = pe_t.reshape(2, halves, 1, CMP_STRIDE * n_kv * hd)
    w1h = w1.reshape(2, halves, CMP_STRIDE, hd, hid)
    w1b = jnp.einsum('krsdh,gf->ksgdrfh', w1h, eye).reshape(2, CMP_STRIDE * n_kv * hd, halves * n_kv * hid)
    b1t = jnp.tile(b1, (1, n_kv)).reshape(2, 1, n_kv * hid)
    w2b = jnp.einsum('khd,gf->kghfd', w2, eye).reshape(2, n_kv * hid, n_kv * hd)
    return pe_t, w1b.astype(BF16), b1t, w2b.astype(BF16)


def kernel(x_prompt, x_sample, cache_cmp, cache_slc, cache_win, state_conv, page_table, norm_mix, w_in, cmp_pe,
           cmp_w1, cmp_b1, cmp_w2, gmlp_norm, gmlp_ws, gmlp_bs, w_proj_a, w_proj_b, w_out, norm_ffn, w_up, conv_w,
           conv_b, w_down, norm_final):
    depth = w_in.shape[0]
    assert depth == 1, "single-layer step"
    bp, t, d = x_prompt.shape
    bd, tn, _ = x_sample.shape
    assert tn == 1
    n_kv, hd = cache_cmp.shape[4], cache_cmp.shape[5]
    page = cache_cmp.shape[2]
    q_cols = w_proj_a.shape[1]
    n_heads = q_cols // hd
    n_rep = n_heads // n_kv
    kv_cols = 2 * n_kv * hd
    width = w_proj_b.shape[1]
    chunk = gmlp_ws.shape[-1]
    d_ff = w_down.shape[1]
    n_pages = page_table.shape[1]
    past_len = n_pages * page
    scale = hd ** -0.5
    assert conv_w.shape[1] == 3 and cache_win.shape[2] <= WINDOW and past_len % SEL_BLOCK == 0

    wit = w_in[0].T.astype(BF16)
    off_kv, off_g = q_cols, q_cols + 3 * kv_cols
    off_uv = off_g + 3 * n_heads
    off_gate = off_uv + 2 * width
    w_qt, w_kvt, w_g = wit[0:off_kv], wit[off_kv:off_g], wit[off_g:off_uv]
    w_uv, w_gate = wit[off_uv:off_gate], wit[off_gate:]
    g_rows = 16
    w_gt = jnp.pad(w_g.reshape(n_kv, 3 * n_rep, d), ((0, 0), (0, g_rows - 3 * n_rep), (0, 0))).reshape(n_kv * g_rows, d)
    w_dec = jnp.concatenate([w_qt, w_kvt, jnp.pad(w_g, ((0, LANES - 3 * n_heads), (0, 0)))], axis=0)
    nm, nf, nfin, gn = norm_mix[0][None], norm_ffn[0][None], norm_final[None], gmlp_norm[0][None]
    pe_t, w1b, b1t, w2b = _compress_params(cmp_pe[0], cmp_w1[0], cmp_b1[0], cmp_w2[0], n_kv)
    wpa, wpb, wout = w_proj_a[0].astype(BF16), w_proj_b[0].astype(BF16), w_out[0].astype(BF16)
    wup, wdown = w_up[0].astype(BF16), w_down[0].astype(BF16)
    cw, cb = conv_w[0], conv_b[0][None]
    ws, bs = gmlp_ws[0], gmlp_bs[0]

    cmp_t, slc_t, win_t, qt, gt = _front(x_prompt, nm, w_kvt, w_qt, w_gt, kv_cols=kv_cols, scale=scale * LOG2_E,
                                         tm=1024)
    kcv = _compress(cmp_t, pe_t, w1b, b1t, w2b, n_seq=bp, steps=1, rows=t // CMP_STRIDE)
    o_nsa = _attention(qt, gt, kcv, slc_t, win_t, n_rep=n_rep, hd=hd)
    x1, v_p = _mix(x_prompt, o_nsa, nm, w_uv, w_gate, gn, ws, bs.T, wpa, wpb, wout,
                   tm=512, chunk=chunk, single_pos=False)
    y_p, conv_p = _ffn(x1, x1, nf, wup, cw, cb, wdown, nfin, tm=512, f_tile=d_ff // 2, single_pos=False)

    xs = x_sample.reshape(bd, d)
    q_s, kv_s, g_s = _front_dec(xs, nm, w_dec, q_cols=q_cols, kv_cols3=3 * kv_cols, scale=scale)
    kv_cmp_s, kv_slc_s, kv_win_s = kv_s[:, 0:kv_cols], kv_s[:, kv_cols:2 * kv_cols], kv_s[:, 2 * kv_cols:]
    def positions_last(c):
        return jnp.transpose(c, (0, 2, 3, 4, 1)).reshape(c.shape[0], kv_cols, c.shape[1])

    cmp_pages, slc_pages, win_rows = positions_last(cache_cmp[0]), positions_last(cache_slc[0]), positions_last(cache_win[0])
    pages_per_step = 32
    kcv_s = _compress(cmp_pages, pe_t, w1b, b1t, w2b, n_seq=bd, steps=n_pages // pages_per_step,
                      rows=pages_per_step * page // CMP_STRIDE, page_table=page_table)
    q3 = q_s.reshape(bd, n_heads, hd)
    n_blocks_s = -(-(past_len + tn) // SEL_BLOCK)
    o_cmp_s, idx_s = _cmp_select_dec(q3, kcv_s, hd=hd, n_rep=n_rep, q_pos=past_len, n_blocks=n_blocks_s)
    sel = idx_s[:, 0:n_kv, 0:N_SEL].reshape(bd, n_kv * N_SEL)
    o_nsa_s = _attend_dec(page_table, sel, slc_pages, q3, kv_slc_s[:, None, :], kv_win_s[:, None, :], win_rows,
                          g_s[:, 0:3 * n_heads].reshape(bd, n_heads, 3), o_cmp_s,
                          hd=hd, n_rep=n_rep, n_sel=N_SEL, n_past_blocks=past_len // SEL_BLOCK)
    gw = width // GMLP_GROUPS
    ws0 = jnp.repeat(ws[:, 0, 0], gw)[None]
    bs0 = jnp.repeat(bs[:, 0], gw)[None]
    x1_s, v_s = _mix(xs[None], o_nsa_s.reshape(1, bd, q_cols), nm, w_uv, w_gate, gn, ws0, bs0, wpa, wpb, wout,
                     tm=bd, chunk=chunk, single_pos=True)
    prev_s = jnp.swapaxes(state_conv[0], 0, 1)
    y_s, a_s = _ffn(x1_s, prev_s, nf, wup, cw, cb, wdown, nfin, tm=bd, f_tile=d_ff // 2, single_pos=True)

    def rows6(a_t):
        n, _, npos = a_t.shape
        return jnp.transpose(a_t.reshape(n, 2, n_kv, hd, npos), (0, 4, 1, 2, 3))[None]

    win_keep = min(WINDOW, t)
    win_keep_s = min(WINDOW, cache_win.shape[2] + tn)
    win_s = jnp.concatenate([win_rows, kv_win_s[:, :, None]], axis=2)[:, :, cache_win.shape[2] + tn - win_keep_s:]
    conv_s = jnp.concatenate([state_conv[0][:, 1:], a_s[0][:, None, :]], axis=1)
    return (y_p, y_s.reshape(bd, tn, d),
            rows6(cmp_t), rows6(slc_t), rows6(win_t[:, :, t - win_keep:]),
            v_p[None], conv_p[None],
            rows6(kv_cmp_s[:, :, None]), rows6(kv_slc_s[:, :, None]), rows6(win_s),
            v_s.reshape(1, bd, tn, width), conv_s[None])
```

```python
import functools
from typing import NamedTuple

import jax
import jax.numpy as jnp
from jax import lax
from jax.experimental import pallas as pl
from jax.experimental.pallas import tpu as pltpu

F32 = jnp.float32
BF16 = jnp.bfloat16
I32 = jnp.int32

CMP_STRIDE = 16
SEG_PITCH = 24
PAGE_LOOKAHEAD = 2
PAGE_SLOTS = PAGE_LOOKAHEAD + 1
CMP_PAGE_LOOKAHEAD = 3
CMP_PAGE_SLOTS = CMP_PAGE_LOOKAHEAD + 1
SEL_BLOCK = 64
N_SEL = 16
N_LOCAL_SEL = 2
WINDOW = 512
Q_BLOCK = 256
KEY_TILE = 128
SLC_TILES = 8
V_PAD = 16
LOG2_E = 1.4426950408889634
GMLP_GROUPS = 4
EPS = 1e-6
NEG = -1e30
BELOW_NEG = -3e38
SEL_BONUS = 1e6

V7X_VMEM_BYTES = 64 * 1024 * 1024
VMEM_REQUEST_BYTES = 56 * 1024 * 1024
LANES = 128


class _Tiles(NamedTuple):
    front_rows: int
    mix_rows: int
    ffn_rows: int
    ffn_cols: int
    pages_per_step: int


def _tiles(t, d_ff, n_pages):
    return _Tiles(front_rows=min(t, 1024), mix_rows=min(t, 512), ffn_rows=min(t, 512), ffn_cols=d_ff // 2,
                  pages_per_step=min(n_pages, 32))


def _cparams(n_grid):
    return pltpu.CompilerParams(
        dimension_semantics=("arbitrary",) * n_grid, vmem_limit_bytes=VMEM_REQUEST_BYTES)


def _rmsnorm(x, g):
    ms = jnp.mean(x * x, axis=-1, keepdims=True)
    return x * lax.rsqrt(ms + EPS) * g


def _dot(a, b):
    return jnp.dot(a, b, preferred_element_type=F32)


def _dot_nt(a, b):
    return lax.dot_general(a, b, (((1,), (1,)), ((), ())), preferred_element_type=F32)


def _dot_exact(a, b):
    return jnp.dot(a, b, precision=lax.Precision.HIGHEST, preferred_element_type=F32)


def _shr(x, n):
    assert n & (n - 1) == 0
    return x >> (n.bit_length() - 1)


def _const_spec(shape):
    nd = len(shape)
    return pl.BlockSpec(shape, lambda *_: (0,) * nd)


def _resident_spec(shape):
    nd = len(shape)
    return pl.BlockSpec(shape, lambda *_: (0,) * nd, pipeline_mode=pl.Buffered(1))


def _front_kernel(x_ref, nw_ref, wkvt_ref, wqt_ref, wgt_ref, cmp_ref, slc_ref, win_ref, qt_ref, gt_ref,
                  *, kv_cols, scale):
    h = _rmsnorm(x_ref[0], nw_ref[...]).astype(BF16)
    kvt = _dot_nt(wkvt_ref[...], h)
    cmp_ref[0] = kvt[0:kv_cols]
    slc_ref[0] = kvt[kv_cols:2 * kv_cols]
    win_ref[0] = kvt[2 * kv_cols:3 * kv_cols]
    qt_ref[0] = (_dot_nt(wqt_ref[...], h) * scale).astype(BF16)
    gt_ref[0] = jax.nn.sigmoid(_dot_nt(wgt_ref[...], h))


def _front(x, nw, wkvt, wqt, wgt, *, kv_cols, scale, tm):
    b, t, d = x.shape
    q_cols, g_rows = wqt.shape[0], wgt.shape[0]
    kv_shape = jax.ShapeDtypeStruct((b, kv_cols, t), F32)
    kv_spec = pl.BlockSpec((1, kv_cols, tm), lambda i, j: (i, 0, j))
    return pl.pallas_call(
        functools.partial(_front_kernel, kv_cols=kv_cols, scale=scale),
        grid=(b, t // tm),
        in_specs=[pl.BlockSpec((1, tm, d), lambda i, j: (i, j, 0)), _const_spec(nw.shape),
                  _const_spec(wkvt.shape), _const_spec(wqt.shape), _const_spec(wgt.shape)],
        out_specs=[kv_spec, kv_spec, kv_spec,
                   pl.BlockSpec((1, q_cols, tm), lambda i, j: (i, 0, j)),
                   pl.BlockSpec((1, g_rows, tm), lambda i, j: (i, 0, j))],
        out_shape=[kv_shape, kv_shape, kv_shape,
                   jax.ShapeDtypeStruct((b, q_cols, t), BF16),
                   jax.ShapeDtypeStruct((b, g_rows, t), F32)],
        compiler_params=_cparams(2), name="front",
    )(x, nw, wkvt, wqt, wgt)


def _front_dec_kernel(x_ref, nw_ref, w_ref, q_ref, kv_ref, g_ref, *, q_cols, kv_cols3, scale):
    h = _rmsnorm(x_ref[...], nw_ref[...]).astype(BF16)
    z = _dot_nt(h, w_ref[...])
    q_ref[...] = z[:, 0:q_cols] * scale
    kv_ref[...] = z[:, q_cols:q_cols + kv_cols3]
    g_ref[...] = jax.nn.sigmoid(z[:, q_cols + kv_cols3:])


def _front_dec(x, nw, w, *, q_cols, kv_cols3, scale):
    n = x.shape[0]
    g_cols = w.shape[0] - q_cols - kv_cols3
    return pl.pallas_call(
        functools.partial(_front_dec_kernel, q_cols=q_cols, kv_cols3=kv_cols3, scale=scale),
        grid=(1,),
        in_specs=[_const_spec(x.shape), _const_spec(nw.shape), _const_spec(w.shape)],
        out_specs=[_const_spec((n, q_cols)), _const_spec((n, kv_cols3)), _const_spec((n, g_cols))],
        out_shape=[jax.ShapeDtypeStruct((n, q_cols), F32), jax.ShapeDtypeStruct((n, kv_cols3), F32),
                   jax.ShapeDtypeStruct((n, g_cols), F32)],
        compiler_params=_cparams(1), name="front_dec",
    )(x, nw, w)


def _segments_onto_rows(tile_of, n_tiles, pos_ref, kv, stride):
    seg_per_tile = LANES // stride
    for ti in range(n_tiles):
        t = tile_of(ti).T
        for n in range(seg_per_tile):
            p0 = (ti * seg_per_tile + n) * SEG_PITCH
            pos_ref[kv, p0:p0 + stride, :] = t[n * stride:(n + 1) * stride, :]


def _compress_rows(kv, pe_ref, w1_ref, b1_ref, w2_ref, out_ref, carry_ref, pos_ref, *, stride, half, hid2):
    rows = out_ref.shape[1]
    xkv = jnp.concatenate(
        [pos_ref[kv, pl.ds(s, rows, stride=SEG_PITCH), :] for s in range(stride)],
        axis=1)
    parts = []
    for r in range(2):
        a = (xkv + pe_ref[kv, r]).astype(BF16)
        parts.append(_dot(a, w1_ref[kv, :, r * hid2:(r + 1) * hid2]))
    prev = carry_ref[kv, 0:1, :]
    row = lax.broadcasted_iota(I32, (rows, hid2), 0)
    shifted = jnp.where(row == 0, prev, pltpu.roll(parts[0], 1, 0))
    carry_ref[kv, 0:1, :] = parts[0][rows - 1:rows, :]
    hid = b1_ref[kv] + shifted + parts[1]
    out_ref[0, :, kv * half:(kv + 1) * half] = _dot(jax.nn.gelu(hid).astype(BF16), w2_ref[kv])


def _compress_kernel(x_ref, pe_ref, w1_ref, b1_ref, w2_ref, out_ref, carry_ref, pos_ref, *, stride, kv_cols, hid2):
    half = kv_cols // 2
    assert half == LANES

    @pl.when(pl.program_id(1) == 0)
    def _():
        carry_ref[...] = jnp.zeros_like(carry_ref)

    for kv in range(2):
        _segments_onto_rows(lambda ti: x_ref[0, kv * half:(kv + 1) * half, ti * LANES:(ti + 1) * LANES],
                            x_ref.shape[2] // LANES, pos_ref, kv, stride)
        _compress_rows(kv, pe_ref, w1_ref, b1_ref, w2_ref, out_ref, carry_ref, pos_ref,
                       stride=stride, half=half, hid2=hid2)


def _compress_paged_kernel(pt_ref, pages_ref, pe_ref, w1_ref, b1_ref, w2_ref, out_ref, carry_ref, pos_ref, buf_ref,
                           sem, *, stride, kv_cols, hid2, n_pages):
    half = kv_cols // 2
    assert half == LANES and buf_ref.shape[3] == LANES
    steps = pl.num_programs(1)
    total = pl.num_programs(0) * steps
    t = pl.program_id(0) * steps + pl.program_id(1)
    look, slots = CMP_PAGE_LOOKAHEAD, CMP_PAGE_SLOTS
    static = dict(stride=stride, half=half, hid2=hid2)

    def page_copy(src_step, ring_step, k):
        page = pt_ref[src_step // steps, (src_step % steps) * n_pages + k]
        slot = ring_step % slots
        return pltpu.make_async_copy(pages_ref.at[page], buf_ref.at[slot, k], sem.at[slot])

    def rows_of(ring_step, kv):
        slot = ring_step % slots
        _segments_onto_rows(lambda ti: buf_ref[slot, ti, kv * half:(kv + 1) * half, :], n_pages, pos_ref, kv, stride)

    @pl.when(t == 0)
    def _():
        for ahead in range(look):
            for k in range(n_pages):
                page_copy(jnp.minimum(ahead, total - 1), ahead, k).start()
        for k in range(n_pages):
            page_copy(0, 0, k).wait()
        rows_of(0, 0)

    @pl.when(pl.program_id(1) == 0)
    def _():
        carry_ref[...] = jnp.zeros_like(carry_ref)

    for k in range(n_pages):
        page_copy(t, t + 1, k).wait()
    _compress_rows(0, pe_ref, w1_ref, b1_ref, w2_ref, out_ref, carry_ref, pos_ref, **static)
    rows_of(t, 1)
    _compress_rows(1, pe_ref, w1_ref, b1_ref, w2_ref, out_ref, carry_ref, pos_ref, **static)
    rows_of(t + 1, 0)
    for k in range(n_pages):
        page_copy(jnp.minimum(t + look, total - 1), t + look, k).start()

    @pl.when(t == total - 1)
    def _():
        for ahead in range(2, look + 1):
            for k in range(n_pages):
                page_copy(t, t + ahead, k).wait()


def _compress(x, pe, w1, b1, w2, *, n_seq, steps, rows, page_table=None):
    stride, kv_cols = CMP_STRIDE, w2.shape[2] * 2
    hid2 = b1.shape[2]
    static = dict(stride=stride, kv_cols=kv_cols, hid2=hid2)
    out_shape = jax.ShapeDtypeStruct((n_seq, steps * rows, kv_cols), F32)
    scratch = [pltpu.VMEM((2, 8, hid2), F32), pltpu.VMEM((2, rows * SEG_PITCH, kv_cols // 2), F32)]
    weights = (pe, w1, b1, w2)
    if page_table is None:
        return pl.pallas_call(
            functools.partial(_compress_kernel, **static), grid=(n_seq, steps),
            in_specs=[pl.BlockSpec((1, kv_cols, rows * stride), lambda i, j: (i, 0, j))]
                     + [_const_spec(a.shape) for a in weights],
            out_specs=pl.BlockSpec((1, rows, kv_cols), lambda i, j: (i, j, 0)),
            out_shape=out_shape, scratch_shapes=scratch, compiler_params=_cparams(2), name="compress",
        )(x, *weights)
    page = x.shape[2]
    n_pages = rows * stride // page
    assert n_seq * steps > CMP_PAGE_LOOKAHEAD
    w_specs = [pl.BlockSpec(a.shape, functools.partial(lambda i, j, pt, nd: (0,) * nd, nd=a.ndim)) for a in weights]
    return pl.pallas_call(
        functools.partial(_compress_paged_kernel, n_pages=n_pages, **static),
        grid_spec=pltpu.PrefetchScalarGridSpec(
            num_scalar_prefetch=1, grid=(n_seq, steps),
            in_specs=[pl.BlockSpec(memory_space=pl.ANY)] + w_specs,
            out_specs=pl.BlockSpec((1, rows, kv_cols), lambda i, j, pt: (i, j, 0)),
            scratch_shapes=scratch + [pltpu.VMEM((CMP_PAGE_SLOTS, n_pages, kv_cols, page), F32),
                                      pltpu.SemaphoreType.DMA((CMP_PAGE_SLOTS,))]),
        out_shape=out_shape, compiler_params=_cparams(2), name="compress_paged",
    )(page_table, x, *weights)


def _block_scores(imp, blk, q_pos, n_blocks):
    cur = _shr(q_pos, SEL_BLOCK)
    valid = (blk * SEL_BLOCK <= q_pos) & (blk < n_blocks)
    forced = (blk == 0) | ((blk <= cur) & (blk > cur - N_LOCAL_SEL))
    score = jnp.where(valid, imp + jnp.where(forced, SEL_BONUS, 0.0), NEG)
    return jnp.where(blk < n_blocks, score, BELOW_NEG)


def _attn_kernel(qt_ref, gt_ref, kcv_ref, slc_ref, win_ref, o_ref,
                 kaug_ref, vts_ref, kwin_ref, vtw_ref, kc_ref, vct_ref, *, n_kv, n_rep, hd, n_blocks):
    i = pl.program_id(1)
    qb, kt = Q_BLOCK, KEY_TILE
    q_tiles = qb // kt
    n_tiles = slc_ref.shape[2] // kt
    n_cmp_rows = kcv_ref.shape[1]
    nq = n_rep * qb
    kd = n_kv * hd
    g_rows = gt_ref.shape[1] // n_kv
    assert kd == LANES and n_blocks == hd, "the selection bias rows ride in the key one-hot lanes"

    @pl.when(i == 0)
    def _():
        lane = lax.broadcasted_iota(I32, (kt, kd), 1)
        krow = lax.broadcasted_iota(I32, (kt, kd), 0)

        def group_lanes(x, gg):
            return x if gg == 0 else pltpu.roll(x, kd - gg * hd, 1)

        ones_row = (lax.broadcasted_iota(I32, (V_PAD, kt), 0) == 0).astype(BF16)
        for c in range(n_tiles):
            cols = slice(c * kt, (c + 1) * kt)
            onehot = (lane - hd == _shr(c * kt + krow, SEL_BLOCK)).astype(F32)
            kt_s, kt_w = slc_ref[0, 0:kd, cols].T, win_ref[0, 0:kd, cols].T
            for gg in range(n_kv):
                v_rows = slice(kd + gg * hd, kd + (gg + 1) * hd)
                kaug_ref[gg, cols, :] = jnp.where(lane < hd, group_lanes(kt_s, gg), onehot).astype(BF16)
                vts_ref[gg, c, 0:hd, :] = slc_ref[0, v_rows, cols].astype(BF16)
                vts_ref[gg, c, hd:, :] = ones_row
                kwin_ref[gg, cols, :] = group_lanes(kt_w, gg)[:, 0:hd].astype(BF16)
                vtw_ref[gg, c, 0:hd, :] = win_ref[0, v_rows, cols].astype(BF16)
                vtw_ref[gg, c, hd:, :] = ones_row
        for c in range(n_cmp_rows // kt):
            rows = slice(c * kt, (c + 1) * kt)
            blk = kcv_ref[0, rows, :]
            vt = blk[:, kd:2 * kd].T
            for gg in range(n_kv):
                kc_ref[gg, rows, :] = group_lanes(blk[:, 0:kd], gg)[:, 0:hd].astype(BF16)
                vct_ref[gg, :, rows] = vt[gg * hd:(gg + 1) * hd, :].astype(BF16)

    q_pos = i * qb + (lax.broadcasted_iota(I32, (1, nq), 1) & (qb - 1))

    groups = range(n_kv)
    heads = range(n_rep)

    q_minus_k = (lax.broadcasted_iota(I32, (kt, nq), 1) & (qb - 1)) - lax.broadcasted_iota(I32, (kt, nq), 0)

    def softmax_step(m, sc):
        m_new = jnp.maximum(m, jnp.max(sc, axis=0, keepdims=True))
        return m_new, jnp.exp2(m - m_new), jnp.exp2(sc - m_new).astype(BF16)

    def normalised(acc):
        return acc[0:hd, :] * (1.0 / acc[hd:hd + 1, :])

    qts = [qt_ref[0, gg * n_rep * hd:(gg + 1) * n_rep * hd, :] for gg in groups]
    qcats = [jnp.concatenate([qts[gg][h * hd:(h + 1) * hd, :] for h in heads], axis=1) for gg in groups]

    def attend(k_ref, v_ref, qs, tile_ids, state, masked):
        scores = [[masked(u, _dot(k_ref[gg, pl.ds(pl.multiple_of(t * kt, kt), kt), :], qs[gg])) for gg in groups]
                  for u, t in enumerate(tile_ids)]
        maxes, accs = list(state[0]), list(state[1])
        for u, t in enumerate(tile_ids):
            for gg in groups:
                maxes[gg], alpha, pb = softmax_step(maxes[gg], scores[u][gg])
                accs[gg] = alpha * accs[gg] + _dot(v_ref[gg, t], pb)
        return tuple(maxes), tuple(accs)

    state0 = ((jnp.full((1, nq), NEG, F32),) * n_kv, (jnp.zeros((hd + V_PAD, nq), F32),) * n_kv)

    s_cmp = [_dot(kc_ref[gg], qcats[gg]) for gg in groups]
    m_idx = lax.broadcasted_iota(I32, (n_cmp_rows, nq), 0)
    vis = (m_idx >= 1) & ((m_idx - 1) * CMP_STRIDE + 2 * CMP_STRIDE - 1 <= q_pos)
    ratio = SEL_BLOCK // CMP_STRIDE
    pj = lax.broadcasted_iota(I32, (n_blocks, n_cmp_rows), 0)
    pm = lax.broadcasted_iota(I32, (n_blocks, n_cmp_rows), 1)
    pool = ((pm >= 1) & (pm >= ratio * pj) & (pm <= ratio * pj + ratio)).astype(F32)
    o_cmp, imp = [], []
    for gg in groups:
        s = jnp.where(vis, s_cmp[gg], NEG)
        e = jnp.where(vis, jnp.exp2(s - jnp.max(s, axis=0, keepdims=True)), 0.0)
        den = jnp.sum(e, axis=0, keepdims=True)
        p = e * (1.0 / jnp.where(den > 0, den, 1.0))
        o_cmp.append(_dot(vct_ref[gg], p.astype(BF16)))
        p_grp = p[:, 0:qb]
        for h in range(1, n_rep):
            p_grp = p_grp + p[:, h * qb:(h + 1) * qb]
        imp.append(_dot_exact(pool, p_grp))

    n_band = WINDOW // kt
    first_t = i * q_tiles - n_band
    win_tiles = [jnp.maximum(first_t + u, 0) for u in range(n_band + q_tiles)]

    def in_window(u, sc):
        dlt = q_minus_k + (n_band - u) * kt
        if u < q_tiles:
            sc = jnp.where(dlt <= WINDOW, sc, NEG)
        if u >= n_band:
            sc = jnp.where(dlt >= 0, sc, NEG)
        if u < n_band:
            sc = sc + jnp.where(first_t + u < 0, NEG, 0.0)
        if u == 0:
            sc = sc + after_importance
        return sc

    after_importance = sum(imp[gg][0:1, 0:1] for gg in groups) * 0.0

    _, accs = attend(kwin_ref, vtw_ref, qcats, win_tiles, state0, in_window)
    o_win = [normalised(accs[gg]) for gg in groups]

    blk_id = lax.broadcasted_iota(I32, (n_blocks, qb), 0)
    blk_f = blk_id.astype(F32)
    qaug = []
    for gg in groups:
        score = _block_scores(imp[gg], blk_id, q_pos[:, 0:qb], n_blocks)
        work, sel = score, jnp.zeros((n_blocks, qb), F32)
        for _ in range(min(N_SEL, n_blocks)):
            mx = jnp.max(work, axis=0, keepdims=True)
            first = jnp.min(jnp.where(work == mx, blk_f, float(n_blocks)), axis=0, keepdims=True)
            pick = blk_f == first
            sel = jnp.where(pick, 1.0, sel)
            work = jnp.where(pick, BELOW_NEG, work)
        bias = jnp.where((sel > 0) & (score > NEG / 2), 0.0, NEG).astype(BF16)
        qaug.append(jnp.concatenate(
            [jnp.concatenate([qts[gg][h * hd:(h + 1) * hd, :], bias], axis=0) for h in heads], axis=1))

    assert n_tiles % SLC_TILES == 0 and SLC_TILES % q_tiles == 0
    n_before = i * q_tiles

    def unmasked(first_tile, n_t):
        return lambda j, st: attend(kaug_ref, vts_ref, qaug, [first_tile + j * n_t + u for u in range(n_t)], st,
                                    lambda u, sc: sc)

    state = lax.fori_loop(0, _shr(n_before, SLC_TILES), unmasked(0, SLC_TILES), state0)
    size = SLC_TILES // 2
    while size >= q_tiles:
        first = _shr(n_before, 2 * size) * (2 * size)
        state = lax.fori_loop(0, _shr(n_before, size) & 1, unmasked(first, size), state)
        size //= 2
    _, accs = attend(kaug_ref, vts_ref, qaug, [n_before + u for u in range(q_tiles)], state,
                     lambda u, sc: jnp.where(q_minus_k - u * kt >= 0, sc, NEG))

    outs = []
    for gg in groups:
        o_slc = normalised(accs[gg])
        for h in heads:
            cols = slice(h * qb, (h + 1) * qb)
            gate = [gt_ref[0, gg * g_rows + 3 * h + br:gg * g_rows + 3 * h + br + 1, :] for br in range(3)]
            outs.append(gate[0] * o_cmp[gg][:, cols] + gate[1] * o_slc[:, cols] + gate[2] * o_win[gg][:, cols])
    o_ref[0] = jnp.concatenate(outs, axis=0).T.astype(BF16)


def _attention(qt, gt, kcv, slc, win, *, n_rep, hd):
    b, q_cols, t = qt.shape
    n_kv = q_cols // (n_rep * hd)
    qb, kt = Q_BLOCK, KEY_TILE
    n_tiles = t // kt
    n_blocks = t // SEL_BLOCK
    kv_cols = slc.shape[1]
    cmp_rows = kcv.shape[1]
    tile_spec = pl.BlockSpec((1, kv_cols, t), lambda bi, i: (bi, 0, 0))
    return pl.pallas_call(
        functools.partial(_attn_kernel, n_kv=n_kv, n_rep=n_rep, hd=hd, n_blocks=n_blocks),
        grid=(b, t // qb),
        in_specs=[pl.BlockSpec((1, q_cols, qb), lambda bi, i: (bi, 0, i)),
                  pl.BlockSpec((1, gt.shape[1], qb), lambda bi, i: (bi, 0, i)),
                  pl.BlockSpec((1,) + kcv.shape[1:], lambda bi, i: (bi, 0, 0)),
                  tile_spec, tile_spec],
        out_specs=pl.BlockSpec((1, qb, q_cols), lambda bi, i: (bi, i, 0)),
        out_shape=jax.ShapeDtypeStruct((b, t, q_cols), BF16),
        scratch_shapes=[pltpu.VMEM((n_kv, t, 2 * hd), BF16), pltpu.VMEM((n_kv, n_tiles, hd + V_PAD, kt), BF16),
                        pltpu.VMEM((n_kv, t, hd), BF16), pltpu.VMEM((n_kv, n_tiles, hd + V_PAD, kt), BF16),
                        pltpu.VMEM((n_kv, cmp_rows, hd), BF16), pltpu.VMEM((n_kv, hd, cmp_rows), BF16)],
        compiler_params=_cparams(2), name="attention",
    )(qt, gt, kcv, slc, win)


def _spread_q(q, hd, n_rep):
    n_heads = q.shape[0]
    d = lax.broadcasted_iota(I32, (hd, 4 * hd), 0)
    c = lax.broadcasted_iota(I32, (hd, 4 * hd), 1)
    qb16 = q.astype(BF16)
    row = lax.broadcasted_iota(I32, (n_heads, 4 * hd), 0)
    out = jnp.zeros((n_heads, 4 * hd), F32)
    for gg in range(n_heads // n_rep):
        placed = _dot(qb16, (c == d + gg * hd).astype(BF16))
        out = jnp.where(_shr(row, n_rep) == gg, placed, out)
    return out.astype(BF16)


def _masked_softmax_rows(s, mask):
    s = jnp.where(mask, s, NEG)
    e = jnp.where(mask, jnp.exp(s - jnp.max(s, axis=-1, keepdims=True)), 0.0)
    den = jnp.sum(e, axis=-1, keepdims=True)
    return e * (1.0 / jnp.where(den > 0, den, 1.0))


def _group_value_lanes(o_full, hd, n_rep):
    row = lax.broadcasted_iota(I32, (o_full.shape[0], hd), 0)
    out = o_full[:, 2 * hd:3 * hd]
    for gg in range(1, o_full.shape[0] // n_rep):
        out = jnp.where(_shr(row, n_rep) == gg, o_full[:, (2 + gg) * hd:(3 + gg) * hd], out)
    return out


def _cmp_select_dec_kernel(q_ref, kcv_ref, o_ref, idx_ref, pool_ref, score_ref, *, hd, n_rep, q_pos, n_blocks,
                           blk_lanes):
    n_heads = q_ref.shape[1]
    n_rows = kcv_ref.shape[1]
    b = pl.program_id(0)
    n_seq = idx_ref.shape[0]

    @pl.when(b == 0)
    def _():
        ratio = SEL_BLOCK // CMP_STRIDE
        pm = lax.broadcasted_iota(I32, (n_rows, blk_lanes), 0)
        pj = lax.broadcasted_iota(I32, (n_rows, blk_lanes), 1)
        pool_ref[...] = ((pm >= 1) & (pm >= ratio * pj) & (pm <= ratio * pj + ratio)).astype(BF16)

    q2 = _spread_q(q_ref[0], hd, n_rep)
    kcv = kcv_ref[0].astype(BF16)
    m_idx = lax.broadcasted_iota(I32, (n_heads, n_rows), 1)
    vis = (m_idx >= 1) & ((m_idx - 1) * CMP_STRIDE + 2 * CMP_STRIDE - 1 <= q_pos)
    p = _masked_softmax_rows(_dot_nt(q2, kcv), vis)
    o_ref[0] = _group_value_lanes(_dot(p.astype(BF16), kcv), hd, n_rep)

    row = lax.broadcasted_iota(I32, (n_heads, n_rows), 0)
    grp = jnp.zeros((n_heads, n_rows), F32)
    for gg in range(n_heads // n_rep):
        tot = jnp.sum(jnp.where(_shr(row, n_rep) == gg, p, 0.0), axis=0, keepdims=True)
        grp = jnp.where(row == gg, tot, grp)
    hi = grp.astype(BF16).astype(F32)
    mid = (grp - hi).astype(BF16).astype(F32)
    lo = grp - hi - mid
    pieces = _dot(jnp.concatenate([hi, mid, lo], axis=0).astype(BF16), pool_ref[...])
    imp = pieces[0:n_heads] + pieces[n_heads:2 * n_heads] + pieces[2 * n_heads:]
    score_ref[b] = _block_scores(imp, lax.broadcasted_iota(I32, (n_heads, blk_lanes), 1), q_pos, n_blocks)

    @pl.when(b == n_seq - 1)
    def _():
        work = score_ref[...].reshape(n_seq * n_heads, blk_lanes)
        blk_f = lax.broadcasted_iota(I32, work.shape, 1).astype(F32)
        out_lane = lax.broadcasted_iota(I32, (n_seq * n_heads, LANES), 1)
        out = jnp.full((n_seq * n_heads, LANES), -1, I32)
        for it in range(min(N_SEL, n_blocks)):
            mx = jnp.max(work, axis=-1, keepdims=True)
            first = jnp.min(jnp.where(work == mx, blk_f, float(blk_lanes)), axis=-1, keepdims=True)
            out = jnp.where(out_lane == it, jnp.where(mx > NEG / 2, first.astype(I32), -1), out)
            work = jnp.where(blk_f == first, BELOW_NEG, work)
        idx_ref[...] = out.reshape(n_seq, n_heads, LANES)


def _cmp_select_dec(q3, kcv, *, hd, n_rep, q_pos, n_blocks):
    n, n_heads, _ = q3.shape
    blk_lanes = -(-n_blocks // LANES) * LANES
    return pl.pallas_call(
        functools.partial(_cmp_select_dec_kernel, hd=hd, n_rep=n_rep, q_pos=q_pos, n_blocks=n_blocks,
                          blk_lanes=blk_lanes),
        grid=(n,),
        in_specs=[pl.BlockSpec((1, n_heads, hd), lambda i: (i, 0, 0)),
                  pl.BlockSpec((1,) + kcv.shape[1:], lambda i: (i, 0, 0))],
        out_specs=[pl.BlockSpec((1, n_heads, hd), lambda i: (i, 0, 0)),
                   _const_spec((n, n_heads, LANES))],
        out_shape=[jax.ShapeDtypeStruct((n, n_heads, hd), F32), jax.ShapeDtypeStruct((n, n_heads, LANES), I32)],
        scratch_shapes=[pltpu.VMEM((kcv.shape[1], blk_lanes), BF16), pltpu.VMEM((n, n_heads, blk_lanes), F32)],
        compiler_params=_cparams(1), name="cmp_select_dec",
    )(q3, kcv)


def _attend_dec_kernel(pt_ref, sel_ref, pages_ref, q_ref, new_slc_ref, new_win_ref, cwin_ref, g_ref, ocmp_ref, o_ref,
                       buf_ref, sem, *, hd, n_rep, n_sel, n_past_blocks):
    b = pl.program_id(0)
    total = pl.num_programs(0)
    n_heads = q_ref.shape[1]
    n_kv = n_heads // n_rep
    n_pages = n_kv * n_sel
    page = buf_ref.shape[3]
    per = page // SEL_BLOCK

    def page_copy(src_seq, ring_step, k):
        idx = jnp.clip(sel_ref[src_seq, k], 0, n_past_blocks - 1)
        slot = ring_step % PAGE_SLOTS
        return pltpu.make_async_copy(pages_ref.at[pt_ref[src_seq, _shr(idx, per)]], buf_ref.at[slot, k], sem.at[slot])

    @pl.when(b == 0)
    def _():
        for ahead in range(PAGE_LOOKAHEAD):
            for k in range(n_pages):
                page_copy(jnp.minimum(ahead, total - 1), ahead, k).start()

    for k in range(n_pages):
        page_copy(b, b, k).wait()
    ring_slot = b % PAGE_SLOTS

    q2 = _spread_q(q_ref[0], hd, n_rep)
    q2f = q2.astype(F32)
    head_grp = _shr(lax.broadcasted_iota(I32, (n_heads, 1), 0), n_rep)

    def with_new_key(s, mask, keys_t, new_row, new_ok):
        nr = new_row.astype(BF16).astype(F32)
        s_new = jnp.sum(q2f * nr, axis=-1, keepdims=True)
        s = jnp.where(mask, s, NEG)
        s_new = jnp.where(new_ok, s_new, NEG)
        m = jnp.maximum(jnp.max(s, axis=-1, keepdims=True), s_new)
        e = jnp.where(mask, jnp.exp(s - m), 0.0)
        e_new = jnp.where(new_ok, jnp.exp(s_new - m), 0.0)
        den = jnp.sum(e, axis=-1, keepdims=True) + e_new
        inv = 1.0 / jnp.where(den > 0, den, 1.0)
        return _dot_nt((e * inv).astype(BF16), keys_t) + (e_new * inv) * nr

    keys_t = jnp.concatenate([buf_ref[ring_slot, k] for k in range(n_pages)], axis=1).astype(BF16)
    n_keys = keys_t.shape[1]
    col_slot = _shr(lax.broadcasted_iota(I32, (1, n_keys), 1), page)
    page_blk = _shr(lax.broadcasted_iota(I32, (1, page), 1), SEL_BLOCK)
    slot_ok = []
    new_ok = jnp.zeros((n_heads, 1), I32)
    for gg in range(n_kv):
        for k in range(n_sel):
            idx = sel_ref[b, gg * n_sel + k]
            cached = ((idx >= 0) & (idx < n_past_blocks)).astype(I32)
            slot_ok.append(jnp.where(page_blk == (idx & (per - 1)), cached, 0))
            new_ok = jnp.where(head_grp == gg, new_ok | (idx == n_past_blocks).astype(I32), new_ok)
    mask = (jnp.concatenate(slot_ok, axis=1) > 0) & (_shr(col_slot, n_sel) == head_grp)
    o_slc = _group_value_lanes(with_new_key(_dot(q2, keys_t), mask, keys_t, new_slc_ref[0], new_ok > 0), hd, n_rep)

    keys_t = cwin_ref[0].astype(BF16)
    all_ok = jnp.full((n_heads, keys_t.shape[1]), True)
    o_win = _group_value_lanes(
        with_new_key(_dot(q2, keys_t), all_ok, keys_t, new_win_ref[0], jnp.full((n_heads, 1), True)), hd, n_rep)

    gates = g_ref[0]
    o_ref[0] = gates[:, 0:1] * ocmp_ref[0] + gates[:, 1:2] * o_slc + gates[:, 2:3] * o_win

    for k in range(n_pages):
        page_copy(jnp.minimum(b + PAGE_LOOKAHEAD, total - 1), b + PAGE_LOOKAHEAD, k).start()

    @pl.when(b == total - 1)
    def _():
        for ahead in range(1, PAGE_LOOKAHEAD + 1):
            for k in range(n_pages):
                page_copy(b, b + ahead, k).wait()


def _attend_dec(page_table, sel, slc_pages, q3, new_slc, new_win, cache_win, gates3, o_cmp, *, hd, n_rep, n_sel,
                n_past_blocks):
    n, n_heads, _ = q3.shape
    n_kv = n_heads // n_rep
    assert n > PAGE_LOOKAHEAD and n_past_blocks * SEL_BLOCK == page_table.shape[1] * slc_pages.shape[2]

    def row_spec(shape):
        nd = len(shape)
        return pl.BlockSpec((1,) + tuple(shape[1:]), lambda i, pt, sl: (i,) + (0,) * (nd - 1))

    others = (q3, new_slc, new_win, cache_win, gates3, o_cmp)
    return pl.pallas_call(
        functools.partial(_attend_dec_kernel, hd=hd, n_rep=n_rep, n_sel=n_sel, n_past_blocks=n_past_blocks),
        grid_spec=pltpu.PrefetchScalarGridSpec(
            num_scalar_prefetch=2, grid=(n,),
            in_specs=[pl.BlockSpec(memory_space=pl.ANY)] + [row_spec(a.shape) for a in others],
            out_specs=row_spec(o_cmp.shape),
            scratch_shapes=[pltpu.VMEM((PAGE_SLOTS, n_kv * n_sel) + slc_pages.shape[1:], F32),
                            pltpu.SemaphoreType.DMA((PAGE_SLOTS,))]),
        out_shape=jax.ShapeDtypeStruct(o_cmp.shape, F32),
        compiler_params=_cparams(1), name="attend_dec",
    )(page_table, sel, slc_pages, *others)


def _mix_kernel(x_ref, o_ref, nw_ref, wuv_ref, wgate_ref, gn_ref, ws_ref, bs_ref, wpa_ref, wpb_ref, wout_ref,
                x1_ref, v_ref, *, width, chunk, single_pos):
    x = x_ref[0]
    d = x.shape[-1]
    h = _rmsnorm(x, nw_ref[...]).astype(BF16)
    uv = jax.nn.gelu(_dot_nt(h, wuv_ref[...]))
    u, vn = uv[:, 0:width], _rmsnorm(uv[:, width:2 * width], gn_ref[...])
    gates = jax.nn.sigmoid(_dot_nt(h, wgate_ref[...]))
    gw = width // GMLP_GROUPS
    if single_pos:
        v_ref[0] = vn
        mixed = u * (ws_ref[...] * vn + bs_ref[...])
    else:
        rows = x.shape[0]
        v_ref[0] = vn[rows - chunk:rows, :]
        tri = lax.broadcasted_iota(I32, (chunk, chunk), 0) >= lax.broadcasted_iota(I32, (chunk, chunk), 1)
        vb = vn.astype(BF16)
        pieces = []
        for c in range(rows // chunk):
            zs = []
            for gi in range(GMLP_GROUPS):
                wm = jnp.where(tri, ws_ref[gi], 0.0).astype(BF16)
                zs.append(_dot(wm, vb[c * chunk:(c + 1) * chunk, gi * gw:(gi + 1) * gw]) + bs_ref[:, gi:gi + 1])
            pieces.append(jnp.concatenate(zs, axis=1))
        mixed = u * jnp.concatenate(pieces, axis=0)
    br_a = _dot(o_ref[0].astype(BF16), wpa_ref[...])
    br_b = _dot(mixed.astype(BF16), wpb_ref[...])
    merged = gates[:, 0:d] * br_a + gates[:, d:2 * d] * br_b
    x1_ref[0] = x + _dot(merged.astype(BF16), wout_ref[...])


def _mix(x, o_nsa, nw, wuv, wgate, gn, ws, bs, wpa, wpb, wout, *, tm, chunk, single_pos):
    b, t, d = x.shape
    width = wuv.shape[0] // 2
    v_rows = tm if single_pos else chunk
    weights = (nw, wuv, wgate, gn, ws, bs, wpa, wpb, wout)
    return pl.pallas_call(
        functools.partial(_mix_kernel, width=width, chunk=chunk, single_pos=single_pos),
        grid=(b, t // tm),
        in_specs=[pl.BlockSpec((1, tm, d), lambda i, j: (i, j, 0)),
                  pl.BlockSpec((1, tm, o_nsa.shape[-1]), lambda i, j: (i, j, 0))]
                 + [_resident_spec(a.shape) for a in weights],
        out_specs=[pl.BlockSpec((1, tm, d), lambda i, j: (i, j, 0)),
                   pl.BlockSpec((1, v_rows, width), lambda i, j: (i, 0, 0))],
        out_shape=[jax.ShapeDtypeStruct((b, t, d), F32), jax.ShapeDtypeStruct((b, v_rows, width), F32)],
        compiler_params=_cparams(2), name="mix_dec" if single_pos else "mix",
    )(x, o_nsa, *weights)


def _ffn_kernel(x1_ref, prev_ref, nf_ref, wup_ref, cw_ref, cb_ref, wdown_ref, nfin_ref, y_ref, a_ref,
                *, d_ff, f_tile, halo, single_pos):
    x1 = x1_ref[0]
    rows = x1.shape[0]
    if single_pos:
        h = _rmsnorm(x1, nf_ref[...]).astype(BF16)
    else:
        h = _rmsnorm(jnp.concatenate([prev_ref[0], x1], axis=0), nf_ref[...]).astype(BF16)
        ext_row = lax.broadcasted_iota(I32, (rows + halo, f_tile), 0)
        first = pl.program_id(1) == 0
    y = jnp.zeros_like(x1)
    for f0 in range(0, d_ff, f_tile):
        cols = slice(f0, f0 + f_tile)
        a = _dot(h, wup_ref[:, cols])
        bgate = _dot(h, wup_ref[:, d_ff + f0:d_ff + f0 + f_tile])
        if single_pos:
            a_ref[0, :, cols] = a
            c = cb_ref[:, cols] + prev_ref[0, :, cols] * cw_ref[0:1, cols] + prev_ref[1, :, cols] * cw_ref[1:2, cols] \
                + a * cw_ref[2:3, cols]
        else:
            a = jnp.where((ext_row < halo) & first, 0.0, a)
            back2 = pltpu.roll(a, 2, 0)
            a_ref[0, :, cols] = back2[0:2, :]
            c = cb_ref[:, cols] + back2[halo:, :] * cw_ref[0:1, cols] \
                + pltpu.roll(a, 1, 0)[halo:, :] * cw_ref[1:2, cols] + a[halo:, :] * cw_ref[2:3, cols]
            bgate = bgate[halo:, :]
        y = y + _dot((jax.nn.gelu(c) * bgate).astype(BF16), wdown_ref[cols, :])
    y_ref[0] = _rmsnorm(x1 + y, nfin_ref[...])


def _ffn(x1, prev, nf, wup, cw, cb, wdown, nfin, *, tm, f_tile, single_pos):
    b, t, d = x1.shape
    d_ff = wdown.shape[0]
    halo = 8
    weights = (nf, wup, cw, cb, wdown, nfin)
    if single_pos:
        prev_spec = _const_spec(prev.shape)
        a_rows = tm
    else:
        per = tm // halo
        prev_spec = pl.BlockSpec((1, halo, d), lambda i, j: (i, jnp.maximum(j * per - 1, 0), 0))
        a_rows = 2
    return pl.pallas_call(
        functools.partial(_ffn_kernel, d_ff=d_ff, f_tile=f_tile, halo=halo, single_pos=single_pos),
        grid=(b, t // tm),
        in_specs=[pl.BlockSpec((1, tm, d), lambda i, j: (i, j, 0)), prev_spec]
                 + [_resident_spec(a.shape) for a in weights],
        out_specs=[pl.BlockSpec((1, tm, d), lambda i, j: (i, j, 0)),
                   pl.BlockSpec((1, a_rows, d_ff), lambda i, j: (i, 0, 0))],
        out_shape=[jax.ShapeDtypeStruct((b, t, d), F32), jax.ShapeDtypeStruct((b, a_rows, d_ff), F32)],
        compiler_params=_cparams(2), name="ffn_dec" if single_pos else "ffn",
    )(x1, prev, *weights)


def _compress_params(pe, w1, b1, w2, n_kv):
    cmp_len, hd = pe.shape[1], pe.shape[2]
    hid = w1.shape[2]
    halves = cmp_len // CMP_STRIDE
    eye = jnp.eye(n_kv, dtype=w1.dtype)
    pe_t = jnp.broadcast_to(pe.reshape(2, halves, CMP_STRIDE, 1, hd), (2, halves, CMP_STRIDE, n_kv, hd))
    pe_t = pe_t.reshape(2, halves, 1, CMP_STRIDE * n_kv * hd)
    w1h = w1.reshape(2, halves, CMP_STRIDE, hd, hid)
    w1b = jnp.einsum('krsdh,gf->ksgdrfh', w1h, eye).reshape(2, CMP_STRIDE * n_kv * hd, halves * n_kv * hid)
    b1t = jnp.tile(b1, (1, n_kv)).reshape(2, 1, n_kv * hid)
    w2b = jnp.einsum('khd,gf->kghfd', w2, eye).reshape(2, n_kv * hid, n_kv * hd)
    return pe_t, w1b.astype(BF16), b1t, w2b.astype(BF16)


def kernel(x_prompt, x_sample, cache_cmp, cache_slc, cache_win, state_conv, page_table, norm_mix, w_in, cmp_pe,
           cmp_w1, cmp_b1, cmp_w2, gmlp_norm, gmlp_ws, gmlp_bs, w_proj_a, w_proj_b, w_out, norm_ffn, w_up, conv_w,
           conv_b, w_down, norm_final):
    depth = w_in.shape[0]
    assert depth == 1, "single-layer step"
    bp, t, d = x_prompt.shape
    bd, tn, _ = x_sample.shape
    assert tn == 1
    n_kv, hd = cache_cmp.shape[4], cache_cmp.shape[5]
    page = cache_cmp.shape[2]
    q_cols = w_proj_a.shape[1]
    n_heads = q_cols // hd
    n_rep = n_heads // n_kv
    kv_cols = 2 * n_kv * hd
    width = w_proj_b.shape[1]
    chunk = gmlp_ws.shape[-1]
    d_ff = w_down.shape[1]
    n_pages = page_table.shape[1]
    past_len = n_pages * page
    scale = hd ** -0.5
    assert conv_w.shape[1] == 3 and cache_win.shape[2] <= WINDOW and past_len % SEL_BLOCK == 0

    w_in_t = w_in[0].T
    off_kv, off_g = q_cols, q_cols + 3 * kv_cols
    off_uv = off_g + 3 * n_heads
    off_gate = off_uv + 2 * width
    w_qt, w_kvt, w_g, w_uv, w_gate = (
        w_in_t[a:b].astype(BF16) for a, b in
        ((0, off_kv), (off_kv, off_g), (off_g, off_uv), (off_uv, off_gate), (off_gate, w_in_t.shape[0])))
    tiles = _tiles(t, d_ff, n_pages)
    g_rows = 16
    w_gt = jnp.pad(w_g.reshape(n_kv, 3 * n_rep, d), ((0, 0), (0, g_rows - 3 * n_rep), (0, 0))).reshape(n_kv * g_rows, d)
    w_dec = jnp.concatenate([w_qt, w_kvt, jnp.pad(w_g, ((0, LANES - 3 * n_heads), (0, 0)))], axis=0)
    nm, nf, nfin, gn = norm_mix[0][None], norm_ffn[0][None], norm_final[None], gmlp_norm[0][None]
    pe_t, w1b, b1t, w2b = _compress_params(cmp_pe[0], cmp_w1[0], cmp_b1[0], cmp_w2[0], n_kv)
    wpa, wpb, wout = w_proj_a[0].astype(BF16), w_proj_b[0].astype(BF16), w_out[0].astype(BF16)
    wup, wdown = w_up[0].astype(BF16), w_down[0].astype(BF16)
    cw, cb = conv_w[0], conv_b[0][None]
    ws, bs = gmlp_ws[0], gmlp_bs[0]

    cmp_t, slc_t, win_t, qt, gt = _front(x_prompt, nm, w_kvt, w_qt, w_gt, kv_cols=kv_cols, scale=scale * LOG2_E,
                                         tm=tiles.front_rows)
    kcv = _compress(cmp_t, pe_t, w1b, b1t, w2b, n_seq=bp, steps=1, rows=t // CMP_STRIDE)
    o_nsa = _attention(qt, gt, kcv, slc_t, win_t, n_rep=n_rep, hd=hd)
    x1, v_p = _mix(x_prompt, o_nsa, nm, w_uv, w_gate, gn, ws, bs.T, wpa, wpb, wout,
                   tm=tiles.mix_rows, chunk=chunk, single_pos=False)
    y_p, conv_p = _ffn(x1, x1, nf, wup, cw, cb, wdown, nfin, tm=tiles.ffn_rows, f_tile=tiles.ffn_cols,
                       single_pos=False)

    xs = x_sample.reshape(bd, d)
    q_s, kv_s, g_s = _front_dec(xs, nm, w_dec, q_cols=q_cols, kv_cols3=3 * kv_cols, scale=scale)
    kv_cmp_s, kv_slc_s, kv_win_s = kv_s[:, 0:kv_cols], kv_s[:, kv_cols:2 * kv_cols], kv_s[:, 2 * kv_cols:]
    def positions_last(c):
        return jnp.transpose(c, (0, 2, 3, 4, 1)).reshape(c.shape[0], kv_cols, c.shape[1])

    cmp_pages, slc_pages, win_rows = positions_last(cache_cmp[0]), positions_last(cache_slc[0]), positions_last(cache_win[0])
    pages_per_step = tiles.pages_per_step
    kcv_s = _compress(cmp_pages, pe_t, w1b, b1t, w2b, n_seq=bd, steps=n_pages // pages_per_step,
                      rows=pages_per_step * page // CMP_STRIDE, page_table=page_table)
    q3 = q_s.reshape(bd, n_heads, hd)
    n_blocks_s = -(-(past_len + tn) // SEL_BLOCK)
    o_cmp_s, idx_s = _cmp_select_dec(q3, kcv_s, hd=hd, n_rep=n_rep, q_pos=past_len, n_blocks=n_blocks_s)
    sel = idx_s[:, 0:n_kv, 0:N_SEL].reshape(bd, n_kv * N_SEL)
    o_nsa_s = _attend_dec(page_table, sel, slc_pages, q3, kv_slc_s[:, None, :], kv_win_s[:, None, :], win_rows,
                          g_s[:, 0:3 * n_heads].reshape(bd, n_heads, 3), o_cmp_s,
                          hd=hd, n_rep=n_rep, n_sel=N_SEL, n_past_blocks=past_len // SEL_BLOCK)
    gw = width // GMLP_GROUPS
    ws0 = jnp.repeat(ws[:, 0, 0], gw)[None]
    bs0 = jnp.repeat(bs[:, 0], gw)[None]
    x1_s, v_s = _mix(xs[None], o_nsa_s.reshape(1, bd, q_cols), nm, w_uv, w_gate, gn, ws0, bs0, wpa, wpb, wout,
                     tm=bd, chunk=chunk, single_pos=True)
    prev_s = jnp.swapaxes(state_conv[0], 0, 1)
    y_s, a_s = _ffn(x1_s, prev_s, nf, wup, cw, cb, wdown, nfin, tm=bd, f_tile=d_ff // 2, single_pos=True)

    def rows6(a_t):
        n, _, npos = a_t.shape
        return jnp.transpose(a_t.reshape(n, 2, n_kv, hd, npos), (0, 4, 1, 2, 3))[None]

    win_keep = min(WINDOW, t)
    win_keep_s = min(WINDOW, cache_win.shape[2] + tn)
    win_s = jnp.concatenate([win_rows, kv_win_s[:, :, None]], axis=2)[:, :, cache_win.shape[2] + tn - win_keep_s:]
    conv_s = jnp.concatenate([state_conv[0][:, 1:], a_s[0][:, None, :]], axis=1)
    return (y_p, y_s.reshape(bd, tn, d),
            rows6(cmp_t), rows6(slc_t), rows6(win_t[:, :, t - win_keep:]),
            v_p[None], conv_p[None],
            rows6(kv_cmp_s[:, :, None]), rows6(kv_slc_s[:, :, None]), rows6(win_s),
            v_s.reshape(1, bd, tn, width), conv_s[None])
```

```python
import functools
from typing import NamedTuple

import jax
import jax.numpy as jnp
from jax import lax
from jax.experimental import pallas as pl
from jax.experimental.pallas import tpu as pltpu

F32 = jnp.float32
BF16 = jnp.bfloat16
I32 = jnp.int32

CMP_STRIDE = 16
SEG_PITCH = 24
PAGE_LOOKAHEAD = 2
PAGE_SLOTS = PAGE_LOOKAHEAD + 1
CMP_PAGE_LOOKAHEAD = 3
CMP_PAGE_SLOTS = CMP_PAGE_LOOKAHEAD + 1
SEL_BLOCK = 64
N_SEL = 16
N_LOCAL_SEL = 2
WINDOW = 512
Q_BLOCK = 256
KEY_TILE = 128
SLC_TILES = 8
V_PAD = 16
LOG2_E = 1.4426950408889634
GMLP_GROUPS = 4
EPS = 1e-6
NEG = -1e30
BELOW_NEG = -3e38
SEL_BONUS = 1e6

V7X_VMEM_BYTES = 64 * 1024 * 1024
VMEM_REQUEST_BYTES = 56 * 1024 * 1024
LANES = 128


class _Tiles(NamedTuple):
    front_rows: int
    mix_rows: int
    ffn_rows: int
    ffn_cols: int
    pages_per_step: int


def _tiles(t, d_ff, n_pages):
    return _Tiles(front_rows=min(t, 1024), mix_rows=min(t, 512), ffn_rows=min(t, 512), ffn_cols=d_ff // 2,
                  pages_per_step=min(n_pages, 32))


def _cparams(n_grid):
    return pltpu.CompilerParams(
        dimension_semantics=("arbitrary",) * n_grid, vmem_limit_bytes=VMEM_REQUEST_BYTES)


def _rmsnorm(x, g):
    ms = jnp.mean(x * x, axis=-1, keepdims=True)
    return x * lax.rsqrt(ms + EPS) * g


def _dot(a, b):
    return jnp.dot(a, b, preferred_element_type=F32)


def _dot_nt(a, b):
    return lax.dot_general(a, b, (((1,), (1,)), ((), ())), preferred_element_type=F32)


def _bf16_pieces(x):
    hi = x.astype(BF16).astype(F32)
    mid = (x - hi).astype(BF16).astype(F32)
    return hi, mid, x - hi - mid


def _shr(x, n):
    assert n & (n - 1) == 0
    return x >> (n.bit_length() - 1)


def _const_spec(shape):
    nd = len(shape)
    return pl.BlockSpec(shape, lambda *_: (0,) * nd)


def _resident_spec(shape):
    nd = len(shape)
    return pl.BlockSpec(shape, lambda *_: (0,) * nd, pipeline_mode=pl.Buffered(1))


def _front_kernel(x_ref, nw_ref, wkvt_ref, wqt_ref, wgt_ref, cmp_ref, slc_ref, win_ref, qt_ref, gt_ref,
                  *, kv_cols, scale):
    h = _rmsnorm(x_ref[0], nw_ref[...]).astype(BF16)
    kvt = _dot_nt(wkvt_ref[...], h)
    cmp_ref[0] = kvt[0:kv_cols]
    slc_ref[0] = kvt[kv_cols:2 * kv_cols]
    win_ref[0] = kvt[2 * kv_cols:3 * kv_cols]
    qt_ref[0] = (_dot_nt(wqt_ref[...], h) * scale).astype(BF16)
    gt_ref[0] = jax.nn.sigmoid(_dot_nt(wgt_ref[...], h))


def _front(x, nw, wkvt, wqt, wgt, *, kv_cols, scale, tm):
    b, t, d = x.shape
    q_cols, g_rows = wqt.shape[0], wgt.shape[0]
    kv_shape = jax.ShapeDtypeStruct((b, kv_cols, t), F32)
    kv_spec = pl.BlockSpec((1, kv_cols, tm), lambda i, j: (i, 0, j))
    return pl.pallas_call(
        functools.partial(_front_kernel, kv_cols=kv_cols, scale=scale),
        grid=(b, t // tm),
        in_specs=[pl.BlockSpec((1, tm, d), lambda i, j: (i, j, 0)), _const_spec(nw.shape),
                  _const_spec(wkvt.shape), _const_spec(wqt.shape), _const_spec(wgt.shape)],
        out_specs=[kv_spec, kv_spec, kv_spec,
                   pl.BlockSpec((1, q_cols, tm), lambda i, j: (i, 0, j)),
                   pl.BlockSpec((1, g_rows, tm), lambda i, j: (i, 0, j))],
        out_shape=[kv_shape, kv_shape, kv_shape,
                   jax.ShapeDtypeStruct((b, q_cols, t), BF16),
                   jax.ShapeDtypeStruct((b, g_rows, t), F32)],
        compiler_params=_cparams(2), name="front",
    )(x, nw, wkvt, wqt, wgt)


def _front_dec_kernel(x_ref, nw_ref, w_ref, q_ref, kv_ref, g_ref, *, q_cols, kv_cols3, scale):
    h = _rmsnorm(x_ref[...], nw_ref[...]).astype(BF16)
    z = _dot_nt(h, w_ref[...])
    q_ref[...] = z[:, 0:q_cols] * scale
    kv_ref[...] = z[:, q_cols:q_cols + kv_cols3]
    g_ref[...] = jax.nn.sigmoid(z[:, q_cols + kv_cols3:])


def _front_dec(x, nw, w, *, q_cols, kv_cols3, scale):
    n = x.shape[0]
    g_cols = w.shape[0] - q_cols - kv_cols3
    return pl.pallas_call(
        functools.partial(_front_dec_kernel, q_cols=q_cols, kv_cols3=kv_cols3, scale=scale),
        grid=(1,),
        in_specs=[_const_spec(x.shape), _const_spec(nw.shape), _const_spec(w.shape)],
        out_specs=[_const_spec((n, q_cols)), _const_spec((n, kv_cols3)), _const_spec((n, g_cols))],
        out_shape=[jax.ShapeDtypeStruct((n, q_cols), F32), jax.ShapeDtypeStruct((n, kv_cols3), F32),
                   jax.ShapeDtypeStruct((n, g_cols), F32)],
        compiler_params=_cparams(1), name="front_dec",
    )(x, nw, w)


def _segments_onto_rows(tile_of, n_tiles, pos_ref, kv, stride):
    seg_per_tile = LANES // stride
    for ti in range(n_tiles):
        t = tile_of(ti).T
        for n in range(seg_per_tile):
            p0 = (ti * seg_per_tile + n) * SEG_PITCH
            pos_ref[kv, p0:p0 + stride, :] = t[n * stride:(n + 1) * stride, :]


def _compress_rows(kv, pe_ref, w1_ref, b1_ref, w2_ref, out_ref, carry_ref, pos_ref, *, stride, half, hid2):
    rows = out_ref.shape[1]
    xkv = jnp.concatenate(
        [pos_ref[kv, pl.ds(s, rows, stride=SEG_PITCH), :] for s in range(stride)],
        axis=1)
    parts = []
    for r in range(2):
        a = (xkv + pe_ref[kv, r]).astype(BF16)
        parts.append(_dot(a, w1_ref[kv, :, r * hid2:(r + 1) * hid2]))
    prev = carry_ref[kv, 0:1, :]
    row = lax.broadcasted_iota(I32, (rows, hid2), 0)
    shifted = jnp.where(row == 0, prev, pltpu.roll(parts[0], 1, 0))
    carry_ref[kv, 0:1, :] = parts[0][rows - 1:rows, :]
    hid = b1_ref[kv] + shifted + parts[1]
    out_ref[0, :, kv * half:(kv + 1) * half] = _dot(jax.nn.gelu(hid).astype(BF16), w2_ref[kv])


def _compress_kernel(x_ref, pe_ref, w1_ref, b1_ref, w2_ref, out_ref, carry_ref, pos_ref, *, stride, kv_cols, hid2):
    half = kv_cols // 2
    assert half == LANES

    @pl.when(pl.program_id(1) == 0)
    def _():
        carry_ref[...] = jnp.zeros_like(carry_ref)

    for kv in range(2):
        _segments_onto_rows(lambda ti: x_ref[0, kv * half:(kv + 1) * half, ti * LANES:(ti + 1) * LANES],
                            x_ref.shape[2] // LANES, pos_ref, kv, stride)
        _compress_rows(kv, pe_ref, w1_ref, b1_ref, w2_ref, out_ref, carry_ref, pos_ref,
                       stride=stride, half=half, hid2=hid2)


def _compress_paged_kernel(pt_ref, pages_ref, pe_ref, w1_ref, b1_ref, w2_ref, out_ref, carry_ref, pos_ref, buf_ref,
                           sem, *, stride, kv_cols, hid2, n_pages):
    half = kv_cols // 2
    assert half == LANES and buf_ref.shape[3] == LANES
    steps = pl.num_programs(1)
    total = pl.num_programs(0) * steps
    t = pl.program_id(0) * steps + pl.program_id(1)
    look, slots = CMP_PAGE_LOOKAHEAD, CMP_PAGE_SLOTS
    static = dict(stride=stride, half=half, hid2=hid2)

    def page_copy(src_step, ring_step, k):
        page = pt_ref[src_step // steps, (src_step % steps) * n_pages + k]
        slot = ring_step % slots
        return pltpu.make_async_copy(pages_ref.at[page], buf_ref.at[slot, k], sem.at[slot])

    def rows_of(ring_step, kv):
        slot = ring_step % slots
        _segments_onto_rows(lambda ti: buf_ref[slot, ti, kv * half:(kv + 1) * half, :], n_pages, pos_ref, kv, stride)

    @pl.when(t == 0)
    def _():
        for ahead in range(look):
            for k in range(n_pages):
                page_copy(jnp.minimum(ahead, total - 1), ahead, k).start()
        for k in range(n_pages):
            page_copy(0, 0, k).wait()
        rows_of(0, 0)

    @pl.when(pl.program_id(1) == 0)
    def _():
        carry_ref[...] = jnp.zeros_like(carry_ref)

    for k in range(n_pages):
        page_copy(t, t + 1, k).wait()
    _compress_rows(0, pe_ref, w1_ref, b1_ref, w2_ref, out_ref, carry_ref, pos_ref, **static)
    rows_of(t, 1)
    _compress_rows(1, pe_ref, w1_ref, b1_ref, w2_ref, out_ref, carry_ref, pos_ref, **static)
    rows_of(t + 1, 0)
    for k in range(n_pages):
        page_copy(jnp.minimum(t + look, total - 1), t + look, k).start()

    @pl.when(t == total - 1)
    def _():
        for ahead in range(2, look + 1):
            for k in range(n_pages):
                page_copy(t, t + ahead, k).wait()


def _compress(x, pe, w1, b1, w2, *, n_seq, steps, rows, page_table=None):
    stride, kv_cols = CMP_STRIDE, w2.shape[2] * 2
    hid2 = b1.shape[2]
    static = dict(stride=stride, kv_cols=kv_cols, hid2=hid2)
    out_shape = jax.ShapeDtypeStruct((n_seq, steps * rows, kv_cols), F32)
    scratch = [pltpu.VMEM((2, 8, hid2), F32), pltpu.VMEM((2, rows * SEG_PITCH, kv_cols // 2), F32)]
    weights = (pe, w1, b1, w2)
    if page_table is None:
        return pl.pallas_call(
            functools.partial(_compress_kernel, **static), grid=(n_seq, steps),
            in_specs=[pl.BlockSpec((1, kv_cols, rows * stride), lambda i, j: (i, 0, j))]
                     + [_const_spec(a.shape) for a in weights],
            out_specs=pl.BlockSpec((1, rows, kv_cols), lambda i, j: (i, j, 0)),
            out_shape=out_shape, scratch_shapes=scratch, compiler_params=_cparams(2), name="compress",
        )(x, *weights)
    page = x.shape[2]
    n_pages = rows * stride // page
    assert n_seq * steps > CMP_PAGE_LOOKAHEAD
    w_specs = [pl.BlockSpec(a.shape, functools.partial(lambda i, j, pt, nd: (0,) * nd, nd=a.ndim)) for a in weights]
    return pl.pallas_call(
        functools.partial(_compress_paged_kernel, n_pages=n_pages, **static),
        grid_spec=pltpu.PrefetchScalarGridSpec(
            num_scalar_prefetch=1, grid=(n_seq, steps),
            in_specs=[pl.BlockSpec(memory_space=pl.ANY)] + w_specs,
            out_specs=pl.BlockSpec((1, rows, kv_cols), lambda i, j, pt: (i, j, 0)),
            scratch_shapes=scratch + [pltpu.VMEM((CMP_PAGE_SLOTS, n_pages, kv_cols, page), F32),
                                      pltpu.SemaphoreType.DMA((CMP_PAGE_SLOTS,))]),
        out_shape=out_shape, compiler_params=_cparams(2), name="compress_paged",
    )(page_table, x, *weights)


def _block_scores(imp, blk, q_pos, n_blocks):
    cur = _shr(q_pos, SEL_BLOCK)
    valid = (blk * SEL_BLOCK <= q_pos) & (blk < n_blocks)
    forced = (blk == 0) | ((blk <= cur) & (blk > cur - N_LOCAL_SEL))
    score = jnp.where(valid, imp + jnp.where(forced, SEL_BONUS, 0.0), NEG)
    return jnp.where(blk < n_blocks, score, BELOW_NEG)


def _attn_kernel(qt_ref, gt_ref, kcv_ref, slc_ref, win_ref, o_ref,
                 kaug_ref, vts_ref, kwin_ref, vtw_ref, kc_ref, vct_ref, *, n_kv, n_rep, hd, n_blocks):
    i = pl.program_id(1)
    qb, kt = Q_BLOCK, KEY_TILE
    q_tiles = qb // kt
    n_tiles = slc_ref.shape[2] // kt
    n_cmp_rows = kcv_ref.shape[1]
    nq = n_rep * qb
    kd = n_kv * hd
    g_rows = gt_ref.shape[1] // n_kv
    assert kd == LANES and n_blocks == hd, "the selection bias rows ride in the key one-hot lanes"

    @pl.when(i == 0)
    def _():
        lane = lax.broadcasted_iota(I32, (kt, kd), 1)
        krow = lax.broadcasted_iota(I32, (kt, kd), 0)

        def group_lanes(x, gg):
            return x if gg == 0 else pltpu.roll(x, kd - gg * hd, 1)

        ones_row = (lax.broadcasted_iota(I32, (V_PAD, kt), 0) == 0).astype(BF16)
        for c in range(n_tiles):
            cols = slice(c * kt, (c + 1) * kt)
            onehot = (lane - hd == _shr(c * kt + krow, SEL_BLOCK)).astype(F32)
            kt_s, kt_w = slc_ref[0, 0:kd, cols].T, win_ref[0, 0:kd, cols].T
            for gg in range(n_kv):
                v_rows = slice(kd + gg * hd, kd + (gg + 1) * hd)
                kaug_ref[gg, cols, :] = jnp.where(lane < hd, group_lanes(kt_s, gg), onehot).astype(BF16)
                vts_ref[gg, c, 0:hd, :] = slc_ref[0, v_rows, cols].astype(BF16)
                vts_ref[gg, c, hd:, :] = ones_row
                kwin_ref[gg, cols, :] = group_lanes(kt_w, gg)[:, 0:hd].astype(BF16)
                vtw_ref[gg, c, 0:hd, :] = win_ref[0, v_rows, cols].astype(BF16)
                vtw_ref[gg, c, hd:, :] = ones_row
        for c in range(n_cmp_rows // kt):
            rows = slice(c * kt, (c + 1) * kt)
            blk = kcv_ref[0, rows, :]
            vt = blk[:, kd:2 * kd].T
            for gg in range(n_kv):
                kc_ref[gg, rows, :] = group_lanes(blk[:, 0:kd], gg)[:, 0:hd].astype(BF16)
                vct_ref[gg, :, rows] = vt[gg * hd:(gg + 1) * hd, :].astype(BF16)

    q_pos = i * qb + (lax.broadcasted_iota(I32, (1, nq), 1) & (qb - 1))

    groups = range(n_kv)
    heads = range(n_rep)

    q_minus_k = (lax.broadcasted_iota(I32, (kt, nq), 1) & (qb - 1)) - lax.broadcasted_iota(I32, (kt, nq), 0)

    def softmax_step(m, sc):
        m_new = jnp.maximum(m, jnp.max(sc, axis=0, keepdims=True))
        return m_new, jnp.exp2(m - m_new), jnp.exp2(sc - m_new).astype(BF16)

    def normalised(acc):
        return acc[0:hd, :] * (1.0 / acc[hd:hd + 1, :])

    qts = [qt_ref[0, gg * n_rep * hd:(gg + 1) * n_rep * hd, :] for gg in groups]
    qcats = [jnp.concatenate([qts[gg][h * hd:(h + 1) * hd, :] for h in heads], axis=1) for gg in groups]

    def attend(k_ref, v_ref, qs, tile_ids, state, masked):
        scores = [[masked(u, _dot(k_ref[gg, pl.ds(pl.multiple_of(t * kt, kt), kt), :], qs[gg])) for gg in groups]
                  for u, t in enumerate(tile_ids)]
        maxes, accs = list(state[0]), list(state[1])
        for u, t in enumerate(tile_ids):
            for gg in groups:
                maxes[gg], alpha, pb = softmax_step(maxes[gg], scores[u][gg])
                accs[gg] = alpha * accs[gg] + _dot(v_ref[gg, t], pb)
        return tuple(maxes), tuple(accs)

    state0 = ((jnp.full((1, nq), NEG, F32),) * n_kv, (jnp.zeros((hd + V_PAD, nq), F32),) * n_kv)

    s_cmp = [_dot(kc_ref[gg], qcats[gg]) for gg in groups]
    m_idx = lax.broadcasted_iota(I32, (n_cmp_rows, nq), 0)
    vis = (m_idx >= 1) & ((m_idx - 1) * CMP_STRIDE + 2 * CMP_STRIDE - 1 <= q_pos)
    ratio = SEL_BLOCK // CMP_STRIDE
    pj = lax.broadcasted_iota(I32, (n_blocks, n_cmp_rows), 0)
    pm = lax.broadcasted_iota(I32, (n_blocks, n_cmp_rows), 1)
    pool = ((pm >= 1) & (pm >= ratio * pj) & (pm <= ratio * pj + ratio)).astype(BF16)
    o_cmp, imp = [], []
    for gg in groups:
        s = jnp.where(vis, s_cmp[gg], NEG)
        e = jnp.where(vis, jnp.exp2(s - jnp.max(s, axis=0, keepdims=True)), 0.0)
        den = jnp.sum(e, axis=0, keepdims=True)
        p = e * (1.0 / jnp.where(den > 0, den, 1.0))
        o_cmp.append(_dot(vct_ref[gg], p.astype(BF16)))
        p_grp = p[:, 0:qb]
        for h in range(1, n_rep):
            p_grp = p_grp + p[:, h * qb:(h + 1) * qb]
        pieces = _dot(pool, jnp.concatenate(_bf16_pieces(p_grp), axis=1).astype(BF16))
        imp.append(pieces[:, 0:qb] + pieces[:, qb:2 * qb] + pieces[:, 2 * qb:])

    n_band = WINDOW // kt
    first_t = i * q_tiles - n_band
    win_tiles = [jnp.maximum(first_t + u, 0) for u in range(n_band + q_tiles)]

    def in_window(u, sc):
        dlt = q_minus_k + (n_band - u) * kt
        if u < q_tiles:
            sc = jnp.where(dlt <= WINDOW, sc, NEG)
        if u >= n_band:
            sc = jnp.where(dlt >= 0, sc, NEG)
        if u < n_band:
            sc = sc + jnp.where(first_t + u < 0, NEG, 0.0)
        if u == 0:
            sc = sc + after_importance
        return sc

    after_importance = sum(imp[gg][0:1, 0:1] for gg in groups) * 0.0

    _, accs = attend(kwin_ref, vtw_ref, qcats, win_tiles, state0, in_window)
    o_win = [normalised(accs[gg]) for gg in groups]

    blk_id = lax.broadcasted_iota(I32, (n_blocks, qb), 0)
    blk_f = blk_id.astype(F32)
    qaug = []
    for gg in groups:
        score = _block_scores(imp[gg], blk_id, q_pos[:, 0:qb], n_blocks)
        work, sel = score, jnp.zeros((n_blocks, qb), F32)
        for _ in range(min(N_SEL, n_blocks)):
            mx = jnp.max(work, axis=0, keepdims=True)
            first = jnp.min(jnp.where(work == mx, blk_f, float(n_blocks)), axis=0, keepdims=True)
            pick = blk_f == first
            sel = jnp.where(pick, 1.0, sel)
            work = jnp.where(pick, BELOW_NEG, work)
        bias = jnp.where((sel > 0) & (score > NEG / 2), 0.0, NEG).astype(BF16)
        qaug.append(jnp.concatenate(
            [jnp.concatenate([qts[gg][h * hd:(h + 1) * hd, :], bias], axis=0) for h in heads], axis=1))

    assert n_tiles % SLC_TILES == 0 and SLC_TILES % q_tiles == 0
    n_before = i * q_tiles

    def unmasked(first_tile, n_t):
        return lambda j, st: attend(kaug_ref, vts_ref, qaug, [first_tile + j * n_t + u for u in range(n_t)], st,
                                    lambda u, sc: sc)

    state = lax.fori_loop(0, _shr(n_before, SLC_TILES), unmasked(0, SLC_TILES), state0)
    size = SLC_TILES // 2
    while size >= q_tiles:
        first = _shr(n_before, 2 * size) * (2 * size)
        state = lax.fori_loop(0, _shr(n_before, size) & 1, unmasked(first, size), state)
        size //= 2
    _, accs = attend(kaug_ref, vts_ref, qaug, [n_before + u for u in range(q_tiles)], state,
                     lambda u, sc: jnp.where(q_minus_k - u * kt >= 0, sc, NEG))

    outs = []
    for gg in groups:
        o_slc = normalised(accs[gg])
        for h in heads:
            cols = slice(h * qb, (h + 1) * qb)
            gate = [gt_ref[0, gg * g_rows + 3 * h + br:gg * g_rows + 3 * h + br + 1, :] for br in range(3)]
            outs.append(gate[0] * o_cmp[gg][:, cols] + gate[1] * o_slc[:, cols] + gate[2] * o_win[gg][:, cols])
    o_ref[0] = jnp.concatenate(outs, axis=0).T.astype(BF16)


def _attention(qt, gt, kcv, slc, win, *, n_rep, hd):
    b, q_cols, t = qt.shape
    n_kv = q_cols // (n_rep * hd)
    qb, kt = Q_BLOCK, KEY_TILE
    n_tiles = t // kt
    n_blocks = t // SEL_BLOCK
    kv_cols = slc.shape[1]
    cmp_rows = kcv.shape[1]
    tile_spec = pl.BlockSpec((1, kv_cols, t), lambda bi, i: (bi, 0, 0))
    return pl.pallas_call(
        functools.partial(_attn_kernel, n_kv=n_kv, n_rep=n_rep, hd=hd, n_blocks=n_blocks),
        grid=(b, t // qb),
        in_specs=[pl.BlockSpec((1, q_cols, qb), lambda bi, i: (bi, 0, i)),
                  pl.BlockSpec((1, gt.shape[1], qb), lambda bi, i: (bi, 0, i)),
                  pl.BlockSpec((1,) + kcv.shape[1:], lambda bi, i: (bi, 0, 0)),
                  tile_spec, tile_spec],
        out_specs=pl.BlockSpec((1, qb, q_cols), lambda bi, i: (bi, i, 0)),
        out_shape=jax.ShapeDtypeStruct((b, t, q_cols), BF16),
        scratch_shapes=[pltpu.VMEM((n_kv, t, 2 * hd), BF16), pltpu.VMEM((n_kv, n_tiles, hd + V_PAD, kt), BF16),
                        pltpu.VMEM((n_kv, t, hd), BF16), pltpu.VMEM((n_kv, n_tiles, hd + V_PAD, kt), BF16),
                        pltpu.VMEM((n_kv, cmp_rows, hd), BF16), pltpu.VMEM((n_kv, hd, cmp_rows), BF16)],
        compiler_params=_cparams(2), name="attention",
    )(qt, gt, kcv, slc, win)


def _spread_q(q, hd, n_rep):
    n_heads = q.shape[0]
    kd = n_heads // n_rep * hd
    d = lax.broadcasted_iota(I32, (hd, kd), 0)
    c = lax.broadcasted_iota(I32, (hd, kd), 1)
    qb16 = q.astype(BF16)
    row = lax.broadcasted_iota(I32, (n_heads, kd), 0)
    out = jnp.zeros((n_heads, kd), F32)
    for gg in range(n_heads // n_rep):
        placed = _dot(qb16, (c == d + gg * hd).astype(BF16))
        out = jnp.where(_shr(row, n_rep) == gg, placed, out)
    return out.astype(BF16)


def _masked_softmax_rows(s, mask):
    s = jnp.where(mask, s, NEG)
    e = jnp.where(mask, jnp.exp(s - jnp.max(s, axis=-1, keepdims=True)), 0.0)
    den = jnp.sum(e, axis=-1, keepdims=True)
    return e * (1.0 / jnp.where(den > 0, den, 1.0))


def _group_value_lanes(o_v, hd, n_rep):
    row = lax.broadcasted_iota(I32, (o_v.shape[0], hd), 0)
    out = o_v[:, 0:hd]
    for gg in range(1, o_v.shape[0] // n_rep):
        out = jnp.where(_shr(row, n_rep) == gg, o_v[:, gg * hd:(gg + 1) * hd], out)
    return out


def _cmp_select_dec_kernel(q_ref, kcv_ref, o_ref, idx_ref, pool_ref, score_ref, *, hd, n_rep, q_pos, n_blocks,
                           blk_lanes):
    n_heads = q_ref.shape[1]
    n_rows = kcv_ref.shape[1]
    b = pl.program_id(0)
    n_seq = idx_ref.shape[0]

    @pl.when(b == 0)
    def _():
        ratio = SEL_BLOCK // CMP_STRIDE
        pm = lax.broadcasted_iota(I32, (n_rows, blk_lanes), 0)
        pj = lax.broadcasted_iota(I32, (n_rows, blk_lanes), 1)
        pool_ref[...] = ((pm >= 1) & (pm >= ratio * pj) & (pm <= ratio * pj + ratio)).astype(BF16)

    q2 = _spread_q(q_ref[0], hd, n_rep)
    kd = q2.shape[1]
    m_idx = lax.broadcasted_iota(I32, (n_heads, n_rows), 1)
    vis = (m_idx >= 1) & ((m_idx - 1) * CMP_STRIDE + 2 * CMP_STRIDE - 1 <= q_pos)
    p = _masked_softmax_rows(_dot_nt(q2, kcv_ref[0, :, 0:kd].astype(BF16)), vis)
    o_ref[0] = _group_value_lanes(_dot(p.astype(BF16), kcv_ref[0, :, kd:2 * kd].astype(BF16)), hd, n_rep)

    row = lax.broadcasted_iota(I32, (n_heads, n_rows), 0)
    grp = jnp.zeros((n_heads, n_rows), F32)
    for gg in range(n_heads // n_rep):
        tot = jnp.sum(jnp.where(_shr(row, n_rep) == gg, p, 0.0), axis=0, keepdims=True)
        grp = jnp.where(row == gg, tot, grp)
    pieces = _dot(jnp.concatenate(_bf16_pieces(grp), axis=0).astype(BF16), pool_ref[...])
    imp = pieces[0:n_heads] + pieces[n_heads:2 * n_heads] + pieces[2 * n_heads:]
    score_ref[b] = _block_scores(imp, lax.broadcasted_iota(I32, (n_heads, blk_lanes), 1), q_pos, n_blocks)

    @pl.when(b == n_seq - 1)
    def _():
        work = score_ref[...].reshape(n_seq * n_heads, blk_lanes)
        blk_f = lax.broadcasted_iota(I32, work.shape, 1).astype(F32)
        out_lane = lax.broadcasted_iota(I32, (n_seq * n_heads, LANES), 1)
        out = jnp.full((n_seq * n_heads, LANES), -1, I32)
        for it in range(min(N_SEL, n_blocks)):
            mx = jnp.max(work, axis=-1, keepdims=True)
            first = jnp.min(jnp.where(work == mx, blk_f, float(blk_lanes)), axis=-1, keepdims=True)
            out = jnp.where(out_lane == it, jnp.where(mx > NEG / 2, first.astype(I32), -1), out)
            work = jnp.where(blk_f == first, BELOW_NEG, work)
        idx_ref[...] = out.reshape(n_seq, n_heads, LANES)


def _cmp_select_dec(q3, kcv, *, hd, n_rep, q_pos, n_blocks):
    n, n_heads, _ = q3.shape
    blk_lanes = -(-n_blocks // LANES) * LANES
    return pl.pallas_call(
        functools.partial(_cmp_select_dec_kernel, hd=hd, n_rep=n_rep, q_pos=q_pos, n_blocks=n_blocks,
                          blk_lanes=blk_lanes),
        grid=(n,),
        in_specs=[pl.BlockSpec((1, n_heads, hd), lambda i: (i, 0, 0)),
                  pl.BlockSpec((1,) + kcv.shape[1:], lambda i: (i, 0, 0))],
        out_specs=[pl.BlockSpec((1, n_heads, hd), lambda i: (i, 0, 0)),
                   _const_spec((n, n_heads, LANES))],
        out_shape=[jax.ShapeDtypeStruct((n, n_heads, hd), F32), jax.ShapeDtypeStruct((n, n_heads, LANES), I32)],
        scratch_shapes=[pltpu.VMEM((kcv.shape[1], blk_lanes), BF16), pltpu.VMEM((n, n_heads, blk_lanes), F32)],
        compiler_params=_cparams(1), name="cmp_select_dec",
    )(q3, kcv)


def _attend_dec_kernel(pt_ref, sel_ref, pages_ref, q_ref, new_slc_ref, new_win_ref, cwin_ref, g_ref, ocmp_ref, o_ref,
                       buf_ref, sem, *, hd, n_rep, n_sel, n_past_blocks):
    b = pl.program_id(0)
    total = pl.num_programs(0)
    n_heads = q_ref.shape[1]
    n_kv = n_heads // n_rep
    n_pages = n_kv * n_sel
    page = buf_ref.shape[3]
    per = page // SEL_BLOCK

    def page_copy(src_seq, ring_step, k):
        idx = jnp.clip(sel_ref[src_seq, k], 0, n_past_blocks - 1)
        slot = ring_step % PAGE_SLOTS
        return pltpu.make_async_copy(pages_ref.at[pt_ref[src_seq, _shr(idx, per)]], buf_ref.at[slot, k], sem.at[slot])

    @pl.when(b == 0)
    def _():
        for ahead in range(PAGE_LOOKAHEAD):
            for k in range(n_pages):
                page_copy(jnp.minimum(ahead, total - 1), ahead, k).start()

    for k in range(n_pages):
        page_copy(b, b, k).wait()
    ring_slot = b % PAGE_SLOTS

    q2 = _spread_q(q_ref[0], hd, n_rep)
    q2f = q2.astype(F32)
    kd = q2.shape[1]
    head_grp = _shr(lax.broadcasted_iota(I32, (n_heads, 1), 0), n_rep)

    def attend_with_new_key(k_t, v_t, mask, new_row, new_ok):
        nr = new_row.astype(BF16).astype(F32)
        s = jnp.where(mask, _dot(q2, k_t), NEG)
        s_new = jnp.where(new_ok, jnp.sum(q2f * nr[:, 0:kd], axis=-1, keepdims=True), NEG)
        m = jnp.maximum(jnp.max(s, axis=-1, keepdims=True), s_new)
        e = jnp.where(mask, jnp.exp(s - m), 0.0)
        e_new = jnp.where(new_ok, jnp.exp(s_new - m), 0.0)
        den = jnp.sum(e, axis=-1, keepdims=True) + e_new
        inv = 1.0 / jnp.where(den > 0, den, 1.0)
        return _dot_nt((e * inv).astype(BF16), v_t) + (e_new * inv) * nr[:, kd:2 * kd]

    k_t = jnp.concatenate([buf_ref[ring_slot, k, 0:kd, :] for k in range(n_pages)], axis=1).astype(BF16)
    v_t = jnp.concatenate([buf_ref[ring_slot, k, kd:2 * kd, :] for k in range(n_pages)], axis=1).astype(BF16)
    n_keys = k_t.shape[1]
    col_slot = _shr(lax.broadcasted_iota(I32, (1, n_keys), 1), page)
    page_blk = _shr(lax.broadcasted_iota(I32, (1, page), 1), SEL_BLOCK)
    slot_ok = []
    new_ok = jnp.zeros((n_heads, 1), I32)
    for gg in range(n_kv):
        for k in range(n_sel):
            idx = sel_ref[b, gg * n_sel + k]
            cached = ((idx >= 0) & (idx < n_past_blocks)).astype(I32)
            slot_ok.append(jnp.where(page_blk == (idx & (per - 1)), cached, 0))
            new_ok = jnp.where(head_grp == gg, new_ok | (idx == n_past_blocks).astype(I32), new_ok)
    mask = (jnp.concatenate(slot_ok, axis=1) > 0) & (_shr(col_slot, n_sel) == head_grp)
    o_slc = _group_value_lanes(attend_with_new_key(k_t, v_t, mask, new_slc_ref[0], new_ok > 0), hd, n_rep)

    all_ok = jnp.full((n_heads, cwin_ref.shape[2]), True)
    o_win = _group_value_lanes(
        attend_with_new_key(cwin_ref[0, 0:kd, :].astype(BF16), cwin_ref[0, kd:2 * kd, :].astype(BF16), all_ok,
                            new_win_ref[0], jnp.full((n_heads, 1), True)), hd, n_rep)

    gates = g_ref[0]
    o_ref[0] = gates[:, 0:1] * ocmp_ref[0] + gates[:, 1:2] * o_slc + gates[:, 2:3] * o_win

    for k in range(n_pages):
        page_copy(jnp.minimum(b + PAGE_LOOKAHEAD, total - 1), b + PAGE_LOOKAHEAD, k).start()

    @pl.when(b == total - 1)
    def _():
        for ahead in range(1, PAGE_LOOKAHEAD + 1):
            for k in range(n_pages):
                page_copy(b, b + ahead, k).wait()


def _attend_dec(page_table, sel, slc_pages, q3, new_slc, new_win, cache_win, gates3, o_cmp, *, hd, n_rep, n_sel,
                n_past_blocks):
    n, n_heads, _ = q3.shape
    n_kv = n_heads // n_rep
    assert n > PAGE_LOOKAHEAD and n_past_blocks * SEL_BLOCK == page_table.shape[1] * slc_pages.shape[2]

    def row_spec(shape):
        nd = len(shape)
        return pl.BlockSpec((1,) + tuple(shape[1:]), lambda i, pt, sl: (i,) + (0,) * (nd - 1))

    others = (q3, new_slc, new_win, cache_win, gates3, o_cmp)
    return pl.pallas_call(
        functools.partial(_attend_dec_kernel, hd=hd, n_rep=n_rep, n_sel=n_sel, n_past_blocks=n_past_blocks),
        grid_spec=pltpu.PrefetchScalarGridSpec(
            num_scalar_prefetch=2, grid=(n,),
            in_specs=[pl.BlockSpec(memory_space=pl.ANY)] + [row_spec(a.shape) for a in others],
            out_specs=row_spec(o_cmp.shape),
            scratch_shapes=[pltpu.VMEM((PAGE_SLOTS, n_kv * n_sel) + slc_pages.shape[1:], F32),
                            pltpu.SemaphoreType.DMA((PAGE_SLOTS,))]),
        out_shape=jax.ShapeDtypeStruct(o_cmp.shape, F32),
        compiler_params=_cparams(1), name="attend_dec",
    )(page_table, sel, slc_pages, *others)


def _mix_kernel(x_ref, o_ref, nw_ref, wuv_ref, wgate_ref, gn_ref, ws_ref, bs_ref, wpa_ref, wpb_ref, wout_ref,
                x1_ref, v_ref, *, width, chunk, single_pos):
    x = x_ref[0]
    d = x.shape[-1]
    h = _rmsnorm(x, nw_ref[...]).astype(BF16)
    uv = jax.nn.gelu(_dot_nt(h, wuv_ref[...]))
    u, vn = uv[:, 0:width], _rmsnorm(uv[:, width:2 * width], gn_ref[...])
    gates = jax.nn.sigmoid(_dot_nt(h, wgate_ref[...]))
    gw = width // GMLP_GROUPS
    if single_pos:
        v_ref[0] = vn
        mixed = u * (ws_ref[...] * vn + bs_ref[...])
    else:
        rows = x.shape[0]
        v_ref[0] = vn[rows - chunk:rows, :]
        tri = lax.broadcasted_iota(I32, (chunk, chunk), 0) >= lax.broadcasted_iota(I32, (chunk, chunk), 1)
        vb = vn.astype(BF16)
        pieces = []
        for c in range(rows // chunk):
            zs = []
            for gi in range(GMLP_GROUPS):
                wm = jnp.where(tri, ws_ref[gi], 0.0).astype(BF16)
                zs.append(_dot(wm, vb[c * chunk:(c + 1) * chunk, gi * gw:(gi + 1) * gw]) + bs_ref[:, gi:gi + 1])
            pieces.append(jnp.concatenate(zs, axis=1))
        mixed = u * jnp.concatenate(pieces, axis=0)
    br_a = _dot(o_ref[0].astype(BF16), wpa_ref[...])
    br_b = _dot(mixed.astype(BF16), wpb_ref[...])
    merged = gates[:, 0:d] * br_a + gates[:, d:2 * d] * br_b
    x1_ref[0] = x + _dot(merged.astype(BF16), wout_ref[...])


def _mix(x, o_nsa, nw, wuv, wgate, gn, ws, bs, wpa, wpb, wout, *, tm, chunk, single_pos):
    b, t, d = x.shape
    width = wuv.shape[0] // 2
    v_rows = tm if single_pos else chunk
    weights = (nw, wuv, wgate, gn, ws, bs, wpa, wpb, wout)
    return pl.pallas_call(
        functools.partial(_mix_kernel, width=width, chunk=chunk, single_pos=single_pos),
        grid=(b, t // tm),
        in_specs=[pl.BlockSpec((1, tm, d), lambda i, j: (i, j, 0)),
                  pl.BlockSpec((1, tm, o_nsa.shape[-1]), lambda i, j: (i, j, 0))]
                 + [_resident_spec(a.shape) for a in weights],
        out_specs=[pl.BlockSpec((1, tm, d), lambda i, j: (i, j, 0)),
                   pl.BlockSpec((1, v_rows, width), lambda i, j: (i, 0, 0))],
        out_shape=[jax.ShapeDtypeStruct((b, t, d), F32), jax.ShapeDtypeStruct((b, v_rows, width), F32)],
        compiler_params=_cparams(2), name="mix_dec" if single_pos else "mix",
    )(x, o_nsa, *weights)


def _ffn_kernel(x1_ref, prev_ref, nf_ref, wup_ref, cw_ref, cb_ref, wdown_ref, nfin_ref, y_ref, a_ref,
                *, d_ff, f_tile, halo, single_pos):
    x1 = x1_ref[0]
    rows = x1.shape[0]
    if single_pos:
        h = _rmsnorm(x1, nf_ref[...]).astype(BF16)
    else:
        h = _rmsnorm(jnp.concatenate([prev_ref[0], x1], axis=0), nf_ref[...]).astype(BF16)
        ext_row = lax.broadcasted_iota(I32, (rows + halo, f_tile), 0)
        first = pl.program_id(1) == 0
    y = jnp.zeros_like(x1)
    for f0 in range(0, d_ff, f_tile):
        cols = slice(f0, f0 + f_tile)
        a = _dot(h, wup_ref[:, cols])
        bgate = _dot(h, wup_ref[:, d_ff + f0:d_ff + f0 + f_tile])
        if single_pos:
            a_ref[0, :, cols] = a
            c = cb_ref[:, cols] + prev_ref[0, :, cols] * cw_ref[0:1, cols] + prev_ref[1, :, cols] * cw_ref[1:2, cols] \
                + a * cw_ref[2:3, cols]
        else:
            a = jnp.where((ext_row < halo) & first, 0.0, a)
            back2 = pltpu.roll(a, 2, 0)
            a_ref[0, :, cols] = back2[0:2, :]
            c = cb_ref[:, cols] + back2[halo:, :] * cw_ref[0:1, cols] \
                + pltpu.roll(a, 1, 0)[halo:, :] * cw_ref[1:2, cols] + a[halo:, :] * cw_ref[2:3, cols]
            bgate = bgate[halo:, :]
        y = y + _dot((jax.nn.gelu(c) * bgate).astype(BF16), wdown_ref[cols, :])
    y_ref[0] = _rmsnorm(x1 + y, nfin_ref[...])


def _ffn(x1, prev, nf, wup, cw, cb, wdown, nfin, *, tm, f_tile, single_pos):
    b, t, d = x1.shape
    d_ff = wdown.shape[0]
    halo = 8
    weights = (nf, wup, cw, cb, wdown, nfin)
    if single_pos:
        prev_spec = _const_spec(prev.shape)
        a_rows = tm
    else:
        per = tm // halo
        prev_spec = pl.BlockSpec((1, halo, d), lambda i, j: (i, jnp.maximum(j * per - 1, 0), 0))
        a_rows = 2
    return pl.pallas_call(
        functools.partial(_ffn_kernel, d_ff=d_ff, f_tile=f_tile, halo=halo, single_pos=single_pos),
        grid=(b, t // tm),
        in_specs=[pl.BlockSpec((1, tm, d), lambda i, j: (i, j, 0)), prev_spec]
                 + [_resident_spec(a.shape) for a in weights],
        out_specs=[pl.BlockSpec((1, tm, d), lambda i, j: (i, j, 0)),
                   pl.BlockSpec((1, a_rows, d_ff), lambda i, j: (i, 0, 0))],
        out_shape=[jax.ShapeDtypeStruct((b, t, d), F32), jax.ShapeDtypeStruct((b, a_rows, d_ff), F32)],
        compiler_params=_cparams(2), name="ffn_dec" if single_pos else "ffn",
    )(x1, prev, *weights)


def _compress_params(pe, w1, b1, w2, n_kv):
    cmp_len, hd = pe.shape[1], pe.shape[2]
    hid = w1.shape[2]
    halves = cmp_len // CMP_STRIDE
    eye = jnp.eye(n_kv, dtype=w1.dtype)
    pe_t = jnp.broadcast_to(pe.reshape(2, halves, CMP_STRIDE, 1, hd), (2, halves, CMP_STRIDE, n_kv, hd))
    pe_t = pe_t.reshape(2, halves, 1, CMP_STRIDE * n_kv * hd)
    w1h = w1.reshape(2, halves, CMP_STRIDE, hd, hid)
    w1b = jnp.einsum('krsdh,gf->ksgdrfh', w1h, eye).reshape(2, CMP_STRIDE * n_kv * hd, halves * n_kv * hid)
    b1t = jnp.tile(b1, (1, n_kv)).reshape(2, 1, n_kv * hid)
    w2b = jnp.einsum('khd,gf->kghfd', w2, eye).reshape(2, n_kv * hid, n_kv * hd)
    return pe_t, w1b.astype(BF16), b1t, w2b.astype(BF16)


def kernel(x_prompt, x_sample, cache_cmp, cache_slc, cache_win, state_conv, page_table, norm_mix, w_in, cmp_pe,
           cmp_w1, cmp_b1, cmp_w2, gmlp_norm, gmlp_ws, gmlp_bs, w_proj_a, w_proj_b, w_out, norm_ffn, w_up, conv_w,
           conv_b, w_down, norm_final):
    depth = w_in.shape[0]
    assert depth == 1, "single-layer step"
    bp, t, d = x_prompt.shape
    bd, tn, _ = x_sample.shape
    assert tn == 1
    n_kv, hd = cache_cmp.shape[4], cache_cmp.shape[5]
    page = cache_cmp.shape[2]
    q_cols = w_proj_a.shape[1]
    n_heads = q_cols // hd
    n_rep = n_heads // n_kv
    kv_cols = 2 * n_kv * hd
    width = w_proj_b.shape[1]
    chunk = gmlp_ws.shape[-1]
    d_ff = w_down.shape[1]
    n_pages = page_table.shape[1]
    past_len = n_pages * page
    scale = hd ** -0.5
    assert conv_w.shape[1] == 3 and cache_win.shape[2] <= WINDOW and past_len % SEL_BLOCK == 0

    w_in_t = w_in[0].T
    off_kv, off_g = q_cols, q_cols + 3 * kv_cols
    off_uv = off_g + 3 * n_heads
    off_gate = off_uv + 2 * width
    w_qt, w_kvt, w_g, w_uv, w_gate = (
        w_in_t[a:b].astype(BF16) for a, b in
        ((0, off_kv), (off_kv, off_g), (off_g, off_uv), (off_uv, off_gate), (off_gate, w_in_t.shape[0])))
    tiles = _tiles(t, d_ff, n_pages)
    g_rows = 16
    w_gt = jnp.pad(w_g.reshape(n_kv, 3 * n_rep, d), ((0, 0), (0, g_rows - 3 * n_rep), (0, 0))).reshape(n_kv * g_rows, d)
    w_dec = jnp.concatenate([w_qt, w_kvt, jnp.pad(w_g, ((0, LANES - 3 * n_heads), (0, 0)))], axis=0)
    nm, nf, nfin, gn = norm_mix[0][None], norm_ffn[0][None], norm_final[None], gmlp_norm[0][None]
    pe_t, w1b, b1t, w2b = _compress_params(cmp_pe[0], cmp_w1[0], cmp_b1[0], cmp_w2[0], n_kv)
    wpa, wpb, wout = w_proj_a[0].astype(BF16), w_proj_b[0].astype(BF16), w_out[0].astype(BF16)
    wup, wdown = w_up[0].astype(BF16), w_down[0].astype(BF16)
    cw, cb = conv_w[0], conv_b[0][None]
    ws, bs = gmlp_ws[0], gmlp_bs[0]

    cmp_t, slc_t, win_t, qt, gt = _front(x_prompt, nm, w_kvt, w_qt, w_gt, kv_cols=kv_cols, scale=scale * LOG2_E,
                                         tm=tiles.front_rows)
    kcv = _compress(cmp_t, pe_t, w1b, b1t, w2b, n_seq=bp, steps=1, rows=t // CMP_STRIDE)
    o_nsa = _attention(qt, gt, kcv, slc_t, win_t, n_rep=n_rep, hd=hd)
    x1, v_p = _mix(x_prompt, o_nsa, nm, w_uv, w_gate, gn, ws, bs.T, wpa, wpb, wout,
                   tm=tiles.mix_rows, chunk=chunk, single_pos=False)
    y_p, conv_p = _ffn(x1, x1, nf, wup, cw, cb, wdown, nfin, tm=tiles.ffn_rows, f_tile=tiles.ffn_cols,
                       single_pos=False)

    xs = x_sample.reshape(bd, d)
    q_s, kv_s, g_s = _front_dec(xs, nm, w_dec, q_cols=q_cols, kv_cols3=3 * kv_cols, scale=scale)
    kv_cmp_s, kv_slc_s, kv_win_s = kv_s[:, 0:kv_cols], kv_s[:, kv_cols:2 * kv_cols], kv_s[:, 2 * kv_cols:]
    def positions_last(c):
        return jnp.transpose(c, (0, 2, 3, 4, 1)).reshape(c.shape[0], kv_cols, c.shape[1])

    cmp_pages, slc_pages, win_rows = positions_last(cache_cmp[0]), positions_last(cache_slc[0]), positions_last(cache_win[0])
    pages_per_step = tiles.pages_per_step
    kcv_s = _compress(cmp_pages, pe_t, w1b, b1t, w2b, n_seq=bd, steps=n_pages // pages_per_step,
                      rows=pages_per_step * page // CMP_STRIDE, page_table=page_table)
    q3 = q_s.reshape(bd, n_heads, hd)
    n_blocks_s = -(-(past_len + tn) // SEL_BLOCK)
    o_cmp_s, idx_s = _cmp_select_dec(q3, kcv_s, hd=hd, n_rep=n_rep, q_pos=past_len, n_blocks=n_blocks_s)
    sel = idx_s[:, 0:n_kv, 0:N_SEL].reshape(bd, n_kv * N_SEL)
    o_nsa_s = _attend_dec(page_table, sel, slc_pages, q3, kv_slc_s[:, None, :], kv_win_s[:, None, :], win_rows,
                          g_s[:, 0:3 * n_heads].reshape(bd, n_heads, 3), o_cmp_s,
                          hd=hd, n_rep=n_rep, n_sel=N_SEL, n_past_blocks=past_len // SEL_BLOCK)
    gw = width // GMLP_GROUPS
    ws0 = jnp.repeat(ws[:, 0, 0], gw)[None]
    bs0 = jnp.repeat(bs[:, 0], gw)[None]
    x1_s, v_s = _mix(xs[None], o_nsa_s.reshape(1, bd, q_cols), nm, w_uv, w_gate, gn, ws0, bs0, wpa, wpb, wout,
                     tm=bd, chunk=chunk, single_pos=True)
    prev_s = jnp.swapaxes(state_conv[0], 0, 1)
    y_s, a_s = _ffn(x1_s, prev_s, nf, wup, cw, cb, wdown, nfin, tm=bd, f_tile=d_ff // 2, single_pos=True)

    def rows6(a_t):
        n, _, npos = a_t.shape
        return jnp.transpose(a_t.reshape(n, 2, n_kv, hd, npos), (0, 4, 1, 2, 3))[None]

    win_keep = min(WINDOW, t)
    win_keep_s = min(WINDOW, cache_win.shape[2] + tn)
    win_s = jnp.concatenate([win_rows, kv_win_s[:, :, None]], axis=2)[:, :, cache_win.shape[2] + tn - win_keep_s:]
    conv_s = jnp.concatenate([state_conv[0][:, 1:], a_s[0][:, None, :]], axis=1)
    return (y_p, y_s.reshape(bd, tn, d),
            rows6(cmp_t), rows6(slc_t), rows6(win_t[:, :, t - win_keep:]),
            v_p[None], conv_p[None],
            rows6(kv_cmp_s[:, :, None]), rows6(kv_slc_s[:, :, None]), rows6(win_s),
            v_s.reshape(1, bd, tn, width), conv_s[None])
```

```python
import functools
from typing import NamedTuple

import jax
import jax.numpy as jnp
from jax import lax
from jax.experimental import pallas as pl
from jax.experimental.pallas import tpu as pltpu

F32 = jnp.float32
BF16 = jnp.bfloat16
I32 = jnp.int32

CMP_STRIDE = 16
SEG_PITCH = 20
PAGE_LOOKAHEAD = 2
PAGE_SLOTS = PAGE_LOOKAHEAD + 1
CMP_PAGE_LOOKAHEAD = 3
CMP_PAGE_SLOTS = CMP_PAGE_LOOKAHEAD + 1
SEL_BLOCK = 64
N_SEL = 16
N_LOCAL_SEL = 2
WINDOW = 512
Q_BLOCK = 256
KEY_TILE = 128
SLC_TILES = 8
V_PAD = 16
LOG2_E = 1.4426950408889634
GMLP_GROUPS = 4
EPS = 1e-6
NEG = -1e30
BELOW_NEG = -3e38
SEL_BONUS = 1e6

V7X_VMEM_BYTES = 64 * 1024 * 1024
VMEM_REQUEST_BYTES = 56 * 1024 * 1024
LANES = 128


class _Tiles(NamedTuple):
    front_rows: int
    mix_rows: int
    ffn_rows: int
    ffn_cols: int
    pages_per_step: int


def _tiles(t, d_ff, n_pages):
    return _Tiles(front_rows=min(t, 1024), mix_rows=min(t, 512), ffn_rows=min(t, 512), ffn_cols=d_ff // 2,
                  pages_per_step=min(n_pages, 32))


def _cparams(n_grid):
    return pltpu.CompilerParams(
        dimension_semantics=("arbitrary",) * n_grid, vmem_limit_bytes=VMEM_REQUEST_BYTES)


def _rmsnorm(x, g):
    ms = jnp.mean(x * x, axis=-1, keepdims=True)
    return x * lax.rsqrt(ms + EPS) * g


def _dot(a, b):
    return jnp.dot(a, b, preferred_element_type=F32)


def _dot_nt(a, b):
    return lax.dot_general(a, b, (((1,), (1,)), ((), ())), preferred_element_type=F32)


def _bf16_pieces(x):
    hi = x.astype(BF16).astype(F32)
    mid = (x - hi).astype(BF16).astype(F32)
    return hi, mid, x - hi - mid


def _shr(x, n):
    assert n & (n - 1) == 0
    return x >> (n.bit_length() - 1)


def _const_spec(shape):
    nd = len(shape)
    return pl.BlockSpec(shape, lambda *_: (0,) * nd)


def _resident_spec(shape):
    nd = len(shape)
    return pl.BlockSpec(shape, lambda *_: (0,) * nd, pipeline_mode=pl.Buffered(1))


def _front_kernel(x_ref, nw_ref, wkvt_ref, wqt_ref, wgt_ref, cmp_ref, slc_ref, win_ref, qt_ref, gt_ref,
                  *, kv_cols, scale):
    h = _rmsnorm(x_ref[0], nw_ref[...]).astype(BF16)
    kvt = _dot_nt(wkvt_ref[...], h)
    cmp_ref[0] = kvt[0:kv_cols]
    slc_ref[0] = kvt[kv_cols:2 * kv_cols]
    win_ref[0] = kvt[2 * kv_cols:3 * kv_cols]
    qt_ref[0] = (_dot_nt(wqt_ref[...], h) * scale).astype(BF16)
    gt_ref[0] = jax.nn.sigmoid(_dot_nt(wgt_ref[...], h))


def _front(x, nw, wkvt, wqt, wgt, *, kv_cols, scale, tm):
    b, t, d = x.shape
    q_cols, g_rows = wqt.shape[0], wgt.shape[0]
    kv_shape = jax.ShapeDtypeStruct((b, kv_cols, t), F32)
    kv_spec = pl.BlockSpec((1, kv_cols, tm), lambda i, j: (i, 0, j))
    return pl.pallas_call(
        functools.partial(_front_kernel, kv_cols=kv_cols, scale=scale),
        grid=(b, t // tm),
        in_specs=[pl.BlockSpec((1, tm, d), lambda i, j: (i, j, 0)), _const_spec(nw.shape),
                  _const_spec(wkvt.shape), _const_spec(wqt.shape), _const_spec(wgt.shape)],
        out_specs=[kv_spec, kv_spec, kv_spec,
                   pl.BlockSpec((1, q_cols, tm), lambda i, j: (i, 0, j)),
                   pl.BlockSpec((1, g_rows, tm), lambda i, j: (i, 0, j))],
        out_shape=[kv_shape, kv_shape, kv_shape,
                   jax.ShapeDtypeStruct((b, q_cols, t), BF16),
                   jax.ShapeDtypeStruct((b, g_rows, t), F32)],
        compiler_params=_cparams(2), name="front",
    )(x, nw, wkvt, wqt, wgt)


def _front_dec_kernel(x_ref, nw_ref, w_ref, q_ref, kv_ref, g_ref, *, q_cols, kv_cols3, scale):
    h = _rmsnorm(x_ref[...], nw_ref[...]).astype(BF16)
    z = _dot_nt(h, w_ref[...])
    q_ref[...] = z[:, 0:q_cols] * scale
    kv_ref[...] = z[:, q_cols:q_cols + kv_cols3]
    g_ref[...] = jax.nn.sigmoid(z[:, q_cols + kv_cols3:])


def _front_dec(x, nw, w, *, q_cols, kv_cols3, scale):
    n = x.shape[0]
    g_cols = w.shape[0] - q_cols - kv_cols3
    return pl.pallas_call(
        functools.partial(_front_dec_kernel, q_cols=q_cols, kv_cols3=kv_cols3, scale=scale),
        grid=(1,),
        in_specs=[_const_spec(x.shape), _const_spec(nw.shape), _const_spec(w.shape)],
        out_specs=[_const_spec((n, q_cols)), _const_spec((n, kv_cols3)), _const_spec((n, g_cols))],
        out_shape=[jax.ShapeDtypeStruct((n, q_cols), F32), jax.ShapeDtypeStruct((n, kv_cols3), F32),
                   jax.ShapeDtypeStruct((n, g_cols), F32)],
        compiler_params=_cparams(1), name="front_dec",
    )(x, nw, w)


def _segments_onto_rows(tile_of, n_tiles, pos_ref, kv, stride):
    seg_per_tile = LANES // stride
    for ti in range(n_tiles):
        t = tile_of(ti).T
        for n in range(seg_per_tile):
            p0 = (ti * seg_per_tile + n) * SEG_PITCH
            pos_ref[kv, p0:p0 + stride, :] = t[n * stride:(n + 1) * stride, :]


def _compress_rows(kv, pe_ref, w1_ref, b1_ref, w2_ref, out_ref, carry_ref, pos_ref, *, stride, half, hid2):
    rows = out_ref.shape[1]
    xkv = jnp.concatenate(
        [pos_ref[kv, pl.ds(s, rows, stride=SEG_PITCH), :] for s in range(stride)],
        axis=1)
    parts = []
    for r in range(2):
        a = (xkv + pe_ref[kv, r]).astype(BF16)
        parts.append(_dot(a, w1_ref[kv, :, r * hid2:(r + 1) * hid2]))
    prev = carry_ref[kv, 0:1, :]
    row = lax.broadcasted_iota(I32, (rows, hid2), 0)
    shifted = jnp.where(row == 0, prev, pltpu.roll(parts[0], 1, 0))
    carry_ref[kv, 0:1, :] = parts[0][rows - 1:rows, :]
    hid = b1_ref[kv] + shifted + parts[1]
    out_ref[0, :, kv * half:(kv + 1) * half] = _dot(jax.nn.gelu(hid).astype(BF16), w2_ref[kv])


def _compress_kernel(x_ref, pe_ref, w1_ref, b1_ref, w2_ref, out_ref, carry_ref, pos_ref, *, stride, kv_cols, hid2):
    half = kv_cols // 2
    assert half == LANES

    @pl.when(pl.program_id(1) == 0)
    def _():
        carry_ref[...] = jnp.zeros_like(carry_ref)

    for kv in range(2):
        _segments_onto_rows(lambda ti: x_ref[0, kv * half:(kv + 1) * half, ti * LANES:(ti + 1) * LANES],
                            x_ref.shape[2] // LANES, pos_ref, kv, stride)
        _compress_rows(kv, pe_ref, w1_ref, b1_ref, w2_ref, out_ref, carry_ref, pos_ref,
                       stride=stride, half=half, hid2=hid2)


def _compress_paged_kernel(pt_ref, pages_ref, pe_ref, w1_ref, b1_ref, w2_ref, out_ref, carry_ref, pos_ref, buf_ref,
                           sem, *, stride, kv_cols, hid2, n_pages):
    half = kv_cols // 2
    assert half == LANES and buf_ref.shape[3] == LANES
    steps = pl.num_programs(1)
    total = pl.num_programs(0) * steps
    t = pl.program_id(0) * steps + pl.program_id(1)
    look, slots = CMP_PAGE_LOOKAHEAD, CMP_PAGE_SLOTS
    static = dict(stride=stride, half=half, hid2=hid2)

    def page_copy(src_step, ring_step, k):
        page = pt_ref[src_step // steps, (src_step % steps) * n_pages + k]
        slot = ring_step % slots
        return pltpu.make_async_copy(pages_ref.at[page], buf_ref.at[slot, k], sem.at[slot])

    def rows_of(ring_step, kv):
        slot = ring_step % slots
        _segments_onto_rows(lambda ti: buf_ref[slot, ti, kv * half:(kv + 1) * half, :], n_pages, pos_ref, kv, stride)

    @pl.when(t == 0)
    def _():
        for ahead in range(look):
            for k in range(n_pages):
                page_copy(jnp.minimum(ahead, total - 1), ahead, k).start()
        for k in range(n_pages):
            page_copy(0, 0, k).wait()
        rows_of(0, 0)

    @pl.when(pl.program_id(1) == 0)
    def _():
        carry_ref[...] = jnp.zeros_like(carry_ref)

    for k in range(n_pages):
        page_copy(t, t + 1, k).wait()
    _compress_rows(0, pe_ref, w1_ref, b1_ref, w2_ref, out_ref, carry_ref, pos_ref, **static)
    rows_of(t, 1)
    _compress_rows(1, pe_ref, w1_ref, b1_ref, w2_ref, out_ref, carry_ref, pos_ref, **static)
    rows_of(t + 1, 0)
    for k in range(n_pages):
        page_copy(jnp.minimum(t + look, total - 1), t + look, k).start()

    @pl.when(t == total - 1)
    def _():
        for ahead in range(2, look + 1):
            for k in range(n_pages):
                page_copy(t, t + ahead, k).wait()


def _compress(x, pe, w1, b1, w2, *, n_seq, steps, rows, page_table=None):
    stride, kv_cols = CMP_STRIDE, w2.shape[2] * 2
    hid2 = b1.shape[2]
    static = dict(stride=stride, kv_cols=kv_cols, hid2=hid2)
    out_shape = jax.ShapeDtypeStruct((n_seq, steps * rows, kv_cols), F32)
    scratch = [pltpu.VMEM((2, 8, hid2), F32), pltpu.VMEM((2, rows * SEG_PITCH, kv_cols // 2), F32)]
    weights = (pe, w1, b1, w2)
    if page_table is None:
        return pl.pallas_call(
            functools.partial(_compress_kernel, **static), grid=(n_seq, steps),
            in_specs=[pl.BlockSpec((1, kv_cols, rows * stride), lambda i, j: (i, 0, j))]
                     + [_const_spec(a.shape) for a in weights],
            out_specs=pl.BlockSpec((1, rows, kv_cols), lambda i, j: (i, j, 0)),
            out_shape=out_shape, scratch_shapes=scratch, compiler_params=_cparams(2), name="compress",
        )(x, *weights)
    page = x.shape[2]
    n_pages = rows * stride // page
    assert n_seq * steps > CMP_PAGE_LOOKAHEAD
    w_specs = [pl.BlockSpec(a.shape, functools.partial(lambda i, j, pt, nd: (0,) * nd, nd=a.ndim)) for a in weights]
    return pl.pallas_call(
        functools.partial(_compress_paged_kernel, n_pages=n_pages, **static),
        grid_spec=pltpu.PrefetchScalarGridSpec(
            num_scalar_prefetch=1, grid=(n_seq, steps),
            in_specs=[pl.BlockSpec(memory_space=pl.ANY)] + w_specs,
            out_specs=pl.BlockSpec((1, rows, kv_cols), lambda i, j, pt: (i, j, 0)),
            scratch_shapes=scratch + [pltpu.VMEM((CMP_PAGE_SLOTS, n_pages, kv_cols, page), F32),
                                      pltpu.SemaphoreType.DMA((CMP_PAGE_SLOTS,))]),
        out_shape=out_shape, compiler_params=_cparams(2), name="compress_paged",
    )(page_table, x, *weights)


def _block_scores(imp, blk, q_pos, n_blocks):
    cur = _shr(q_pos, SEL_BLOCK)
    valid = (blk * SEL_BLOCK <= q_pos) & (blk < n_blocks)
    forced = (blk == 0) | ((blk <= cur) & (blk > cur - N_LOCAL_SEL))
    score = jnp.where(valid, imp + jnp.where(forced, SEL_BONUS, 0.0), NEG)
    return jnp.where(blk < n_blocks, score, BELOW_NEG)


def _attn_kernel(qt_ref, gt_ref, kcv_ref, slc_ref, win_ref, o_ref,
                 kaug_ref, vts_ref, kwin_ref, vtw_ref, kc_ref, vct_ref, *, n_kv, n_rep, hd, n_blocks):
    i = pl.program_id(1)
    qb, kt = Q_BLOCK, KEY_TILE
    q_tiles = qb // kt
    n_tiles = slc_ref.shape[2] // kt
    n_cmp_rows = kcv_ref.shape[1]
    nq = n_rep * qb
    kd = n_kv * hd
    g_rows = gt_ref.shape[1] // n_kv
    assert kd == LANES and n_blocks == hd, "the selection bias rows ride in the key one-hot lanes"

    @pl.when(i == 0)
    def _():
        lane = lax.broadcasted_iota(I32, (kt, kd), 1)
        krow = lax.broadcasted_iota(I32, (kt, kd), 0)

        def group_lanes(x, gg):
            return x if gg == 0 else pltpu.roll(x, kd - gg * hd, 1)

        ones_row = (lax.broadcasted_iota(I32, (V_PAD, kt), 0) == 0).astype(BF16)
        for c in range(n_tiles):
            cols = slice(c * kt, (c + 1) * kt)
            onehot = (lane - hd == _shr(c * kt + krow, SEL_BLOCK)).astype(F32)
            kt_s, kt_w = slc_ref[0, 0:kd, cols].T, win_ref[0, 0:kd, cols].T
            for gg in range(n_kv):
                v_rows = slice(kd + gg * hd, kd + (gg + 1) * hd)
                kaug_ref[gg, cols, :] = jnp.where(lane < hd, group_lanes(kt_s, gg), onehot).astype(BF16)
                vts_ref[gg, c, 0:hd, :] = slc_ref[0, v_rows, cols].astype(BF16)
                vts_ref[gg, c, hd:, :] = ones_row
                kwin_ref[gg, cols, :] = group_lanes(kt_w, gg)[:, 0:hd].astype(BF16)
                vtw_ref[gg, c, 0:hd, :] = win_ref[0, v_rows, cols].astype(BF16)
                vtw_ref[gg, c, hd:, :] = ones_row
        for c in range(n_cmp_rows // kt):
            rows = slice(c * kt, (c + 1) * kt)
            blk = kcv_ref[0, rows, :]
            vt = blk[:, kd:2 * kd].T
            for gg in range(n_kv):
                kc_ref[gg, rows, :] = group_lanes(blk[:, 0:kd], gg)[:, 0:hd].astype(BF16)
                vct_ref[gg, :, rows] = vt[gg * hd:(gg + 1) * hd, :].astype(BF16)

    q_pos = i * qb + (lax.broadcasted_iota(I32, (1, nq), 1) & (qb - 1))

    groups = range(n_kv)
    heads = range(n_rep)

    q_minus_k = (lax.broadcasted_iota(I32, (kt, nq), 1) & (qb - 1)) - lax.broadcasted_iota(I32, (kt, nq), 0)

    def softmax_step(m, sc):
        m_new = jnp.maximum(m, jnp.max(sc, axis=0, keepdims=True))
        return m_new, jnp.exp2(m - m_new), jnp.exp2(sc - m_new).astype(BF16)

    def normalised(acc):
        return acc[0:hd, :] * (1.0 / acc[hd:hd + 1, :])

    qts = [qt_ref[0, gg * n_rep * hd:(gg + 1) * n_rep * hd, :] for gg in groups]
    qcats = [jnp.concatenate([qts[gg][h * hd:(h + 1) * hd, :] for h in heads], axis=1) for gg in groups]

    def attend(k_ref, v_ref, qs, tile_ids, state, masked):
        scores = [[masked(u, _dot(k_ref[gg, pl.ds(pl.multiple_of(t * kt, kt), kt), :], qs[gg])) for gg in groups]
                  for u, t in enumerate(tile_ids)]
        maxes, accs = list(state[0]), list(state[1])
        for u, t in enumerate(tile_ids):
            for gg in groups:
                maxes[gg], alpha, pb = softmax_step(maxes[gg], scores[u][gg])
                accs[gg] = alpha * accs[gg] + _dot(v_ref[gg, t], pb)
        return tuple(maxes), tuple(accs)

    state0 = ((jnp.full((1, nq), NEG, F32),) * n_kv, (jnp.zeros((hd + V_PAD, nq), F32),) * n_kv)

    s_cmp = [_dot(kc_ref[gg], qcats[gg]) for gg in groups]
    m_idx = lax.broadcasted_iota(I32, (n_cmp_rows, nq), 0)
    vis = (m_idx >= 1) & ((m_idx - 1) * CMP_STRIDE + 2 * CMP_STRIDE - 1 <= q_pos)
    ratio = SEL_BLOCK // CMP_STRIDE
    pj = lax.broadcasted_iota(I32, (n_blocks, n_cmp_rows), 0)
    pm = lax.broadcasted_iota(I32, (n_blocks, n_cmp_rows), 1)
    pool = ((pm >= 1) & (pm >= ratio * pj) & (pm <= ratio * pj + ratio)).astype(BF16)
    o_cmp, imp = [], []
    for gg in groups:
        s = jnp.where(vis, s_cmp[gg], NEG)
        e = jnp.where(vis, jnp.exp2(s - jnp.max(s, axis=0, keepdims=True)), 0.0)
        den = jnp.sum(e, axis=0, keepdims=True)
        p = e * (1.0 / jnp.where(den > 0, den, 1.0))
        o_cmp.append(_dot(vct_ref[gg], p.astype(BF16)))
        p_grp = p[:, 0:qb]
        for h in range(1, n_rep):
            p_grp = p_grp + p[:, h * qb:(h + 1) * qb]
        pieces = _dot(pool, jnp.concatenate(_bf16_pieces(p_grp), axis=1).astype(BF16))
        imp.append(pieces[:, 0:qb] + pieces[:, qb:2 * qb] + pieces[:, 2 * qb:])

    n_band = WINDOW // kt
    first_t = i * q_tiles - n_band
    win_tiles = [jnp.maximum(first_t + u, 0) for u in range(n_band + q_tiles)]

    def in_window(u, sc):
        dlt = q_minus_k + (n_band - u) * kt
        if u < q_tiles:
            sc = jnp.where(dlt <= WINDOW, sc, NEG)
        if u >= n_band:
            sc = jnp.where(dlt >= 0, sc, NEG)
        if u < n_band:
            sc = sc + jnp.where(first_t + u < 0, NEG, 0.0)
        if u == 0:
            sc = sc + after_importance
        return sc

    after_importance = sum(imp[gg][0:1, 0:1] for gg in groups) * 0.0

    _, accs = attend(kwin_ref, vtw_ref, qcats, win_tiles, state0, in_window)
    o_win = [normalised(accs[gg]) for gg in groups]

    blk_id = lax.broadcasted_iota(I32, (n_blocks, qb), 0)
    blk_f = blk_id.astype(F32)
    qaug = []
    for gg in groups:
        score = _block_scores(imp[gg], blk_id, q_pos[:, 0:qb], n_blocks)
        work, sel = score, jnp.zeros((n_blocks, qb), F32)
        for _ in range(min(N_SEL, n_blocks)):
            mx = jnp.max(work, axis=0, keepdims=True)
            first = jnp.min(jnp.where(work == mx, blk_f, float(n_blocks)), axis=0, keepdims=True)
            pick = blk_f == first
            sel = jnp.where(pick, 1.0, sel)
            work = jnp.where(pick, BELOW_NEG, work)
        bias = jnp.where((sel > 0) & (score > NEG / 2), 0.0, NEG).astype(BF16)
        qaug.append(jnp.concatenate(
            [jnp.concatenate([qts[gg][h * hd:(h + 1) * hd, :], bias], axis=0) for h in heads], axis=1))

    assert n_tiles % SLC_TILES == 0 and SLC_TILES % q_tiles == 0
    n_before = i * q_tiles

    def unmasked(first_tile, n_t):
        return lambda j, st: attend(kaug_ref, vts_ref, qaug, [first_tile + j * n_t + u for u in range(n_t)], st,
                                    lambda u, sc: sc)

    state = lax.fori_loop(0, _shr(n_before, SLC_TILES), unmasked(0, SLC_TILES), state0)
    size = SLC_TILES // 2
    while size >= q_tiles:
        first = _shr(n_before, 2 * size) * (2 * size)
        state = lax.fori_loop(0, _shr(n_before, size) & 1, unmasked(first, size), state)
        size //= 2
    _, accs = attend(kaug_ref, vts_ref, qaug, [n_before + u for u in range(q_tiles)], state,
                     lambda u, sc: jnp.where(q_minus_k - u * kt >= 0, sc, NEG))

    outs = []
    for gg in groups:
        o_slc = normalised(accs[gg])
        for h in heads:
            cols = slice(h * qb, (h + 1) * qb)
            gate = [gt_ref[0, gg * g_rows + 3 * h + br:gg * g_rows + 3 * h + br + 1, :] for br in range(3)]
            outs.append(gate[0] * o_cmp[gg][:, cols] + gate[1] * o_slc[:, cols] + gate[2] * o_win[gg][:, cols])
    o_ref[0] = jnp.concatenate(outs, axis=0).T.astype(BF16)


def _attention(qt, gt, kcv, slc, win, *, n_rep, hd):
    b, q_cols, t = qt.shape
    n_kv = q_cols // (n_rep * hd)
    qb, kt = Q_BLOCK, KEY_TILE
    n_tiles = t // kt
    n_blocks = t // SEL_BLOCK
    kv_cols = slc.shape[1]
    cmp_rows = kcv.shape[1]
    tile_spec = pl.BlockSpec((1, kv_cols, t), lambda bi, i: (bi, 0, 0))
    return pl.pallas_call(
        functools.partial(_attn_kernel, n_kv=n_kv, n_rep=n_rep, hd=hd, n_blocks=n_blocks),
        grid=(b, t // qb),
        in_specs=[pl.BlockSpec((1, q_cols, qb), lambda bi, i: (bi, 0, i)),
                  pl.BlockSpec((1, gt.shape[1], qb), lambda bi, i: (bi, 0, i)),
                  pl.BlockSpec((1,) + kcv.shape[1:], lambda bi, i: (bi, 0, 0)),
                  tile_spec, tile_spec],
        out_specs=pl.BlockSpec((1, qb, q_cols), lambda bi, i: (bi, i, 0)),
        out_shape=jax.ShapeDtypeStruct((b, t, q_cols), BF16),
        scratch_shapes=[pltpu.VMEM((n_kv, t, 2 * hd), BF16), pltpu.VMEM((n_kv, n_tiles, hd + V_PAD, kt), BF16),
                        pltpu.VMEM((n_kv, t, hd), BF16), pltpu.VMEM((n_kv, n_tiles, hd + V_PAD, kt), BF16),
                        pltpu.VMEM((n_kv, cmp_rows, hd), BF16), pltpu.VMEM((n_kv, hd, cmp_rows), BF16)],
        compiler_params=_cparams(2), name="attention",
    )(qt, gt, kcv, slc, win)


def _spread_q(q, hd, n_rep):
    n_heads = q.shape[0]
    kd = n_heads // n_rep * hd
    d = lax.broadcasted_iota(I32, (hd, kd), 0)
    c = lax.broadcasted_iota(I32, (hd, kd), 1)
    qb16 = q.astype(BF16)
    row = lax.broadcasted_iota(I32, (n_heads, kd), 0)
    out = jnp.zeros((n_heads, kd), F32)
    for gg in range(n_heads // n_rep):
        placed = _dot(qb16, (c == d + gg * hd).astype(BF16))
        out = jnp.where(_shr(row, n_rep) == gg, placed, out)
    return out.astype(BF16)


def _masked_softmax_rows(s, mask):
    s = jnp.where(mask, s, NEG)
    e = jnp.where(mask, jnp.exp(s - jnp.max(s, axis=-1, keepdims=True)), 0.0)
    den = jnp.sum(e, axis=-1, keepdims=True)
    return e * (1.0 / jnp.where(den > 0, den, 1.0))


def _group_value_lanes(o_v, hd, n_rep):
    row = lax.broadcasted_iota(I32, (o_v.shape[0], hd), 0)
    out = o_v[:, 0:hd]
    for gg in range(1, o_v.shape[0] // n_rep):
        out = jnp.where(_shr(row, n_rep) == gg, o_v[:, gg * hd:(gg + 1) * hd], out)
    return out


def _cmp_select_dec_kernel(q_ref, kcv_ref, o_ref, idx_ref, pool_ref, score_ref, *, hd, n_rep, q_pos, n_blocks,
                           blk_lanes):
    n_heads = q_ref.shape[1]
    n_rows = kcv_ref.shape[1]
    b = pl.program_id(0)
    n_seq = idx_ref.shape[0]

    @pl.when(b == 0)
    def _():
        ratio = SEL_BLOCK // CMP_STRIDE
        pm = lax.broadcasted_iota(I32, (n_rows, blk_lanes), 0)
        pj = lax.broadcasted_iota(I32, (n_rows, blk_lanes), 1)
        pool_ref[...] = ((pm >= 1) & (pm >= ratio * pj) & (pm <= ratio * pj + ratio)).astype(BF16)

    q2 = _spread_q(q_ref[0], hd, n_rep)
    kd = q2.shape[1]
    m_idx = lax.broadcasted_iota(I32, (n_heads, n_rows), 1)
    vis = (m_idx >= 1) & ((m_idx - 1) * CMP_STRIDE + 2 * CMP_STRIDE - 1 <= q_pos)
    p = _masked_softmax_rows(_dot_nt(q2, kcv_ref[0, :, 0:kd].astype(BF16)), vis)
    o_ref[0] = _group_value_lanes(_dot(p.astype(BF16), kcv_ref[0, :, kd:2 * kd].astype(BF16)), hd, n_rep)

    row = lax.broadcasted_iota(I32, (n_heads, n_rows), 0)
    grp = jnp.zeros((n_heads, n_rows), F32)
    for gg in range(n_heads // n_rep):
        tot = jnp.sum(jnp.where(_shr(row, n_rep) == gg, p, 0.0), axis=0, keepdims=True)
        grp = jnp.where(row == gg, tot, grp)
    pieces = _dot(jnp.concatenate(_bf16_pieces(grp), axis=0).astype(BF16), pool_ref[...])
    imp = pieces[0:n_heads] + pieces[n_heads:2 * n_heads] + pieces[2 * n_heads:]
    score_ref[b] = _block_scores(imp, lax.broadcasted_iota(I32, (n_heads, blk_lanes), 1), q_pos, n_blocks)

    @pl.when(b == n_seq - 1)
    def _():
        work = score_ref[...].reshape(n_seq * n_heads, blk_lanes)
        blk_f = lax.broadcasted_iota(I32, work.shape, 1).astype(F32)
        out_lane = lax.broadcasted_iota(I32, (n_seq * n_heads, LANES), 1)
        out = jnp.full((n_seq * n_heads, LANES), -1, I32)
        for it in range(min(N_SEL, n_blocks)):
            mx = jnp.max(work, axis=-1, keepdims=True)
            first = jnp.min(jnp.where(work == mx, blk_f, float(blk_lanes)), axis=-1, keepdims=True)
            out = jnp.where(out_lane == it, jnp.where(mx > NEG / 2, first.astype(I32), -1), out)
            work = jnp.where(blk_f == first, BELOW_NEG, work)
        idx_ref[...] = out.reshape(n_seq, n_heads, LANES)


def _cmp_select_dec(q3, kcv, *, hd, n_rep, q_pos, n_blocks):
    n, n_heads, _ = q3.shape
    blk_lanes = -(-n_blocks // LANES) * LANES
    return pl.pallas_call(
        functools.partial(_cmp_select_dec_kernel, hd=hd, n_rep=n_rep, q_pos=q_pos, n_blocks=n_blocks,
                          blk_lanes=blk_lanes),
        grid=(n,),
        in_specs=[pl.BlockSpec((1, n_heads, hd), lambda i: (i, 0, 0)),
                  pl.BlockSpec((1,) + kcv.shape[1:], lambda i: (i, 0, 0))],
        out_specs=[pl.BlockSpec((1, n_heads, hd), lambda i: (i, 0, 0)),
                   _const_spec((n, n_heads, LANES))],
        out_shape=[jax.ShapeDtypeStruct((n, n_heads, hd), F32), jax.ShapeDtypeStruct((n, n_heads, LANES), I32)],
        scratch_shapes=[pltpu.VMEM((kcv.shape[1], blk_lanes), BF16), pltpu.VMEM((n, n_heads, blk_lanes), F32)],
        compiler_params=_cparams(1), name="cmp_select_dec",
    )(q3, kcv)


def _attend_dec_kernel(pt_ref, sel_ref, pages_ref, q_ref, new_slc_ref, new_win_ref, cwin_ref, g_ref, ocmp_ref, o_ref,
                       buf_ref, sem, *, hd, n_rep, n_sel, n_past_blocks):
    b = pl.program_id(0)
    total = pl.num_programs(0)
    n_heads = q_ref.shape[1]
    n_kv = n_heads // n_rep
    n_pages = n_kv * n_sel
    page = buf_ref.shape[3]
    per = page // SEL_BLOCK

    def page_copy(src_seq, ring_step, k):
        idx = jnp.clip(sel_ref[src_seq, k], 0, n_past_blocks - 1)
        slot = ring_step % PAGE_SLOTS
        return pltpu.make_async_copy(pages_ref.at[pt_ref[src_seq, _shr(idx, per)]], buf_ref.at[slot, k], sem.at[slot])

    @pl.when(b == 0)
    def _():
        for ahead in range(PAGE_LOOKAHEAD):
            for k in range(n_pages):
                page_copy(jnp.minimum(ahead, total - 1), ahead, k).start()

    for k in range(n_pages):
        page_copy(b, b, k).wait()
    ring_slot = b % PAGE_SLOTS

    q2 = _spread_q(q_ref[0], hd, n_rep)
    q2f = q2.astype(F32)
    kd = q2.shape[1]
    head_grp = _shr(lax.broadcasted_iota(I32, (n_heads, 1), 0), n_rep)

    def attend_with_new_key(k_t, v_t, mask, new_row, new_ok):
        nr = new_row.astype(BF16).astype(F32)
        s = jnp.where(mask, _dot(q2, k_t), NEG)
        s_new = jnp.where(new_ok, jnp.sum(q2f * nr[:, 0:kd], axis=-1, keepdims=True), NEG)
        m = jnp.maximum(jnp.max(s, axis=-1, keepdims=True), s_new)
        e = jnp.where(mask, jnp.exp(s - m), 0.0)
        e_new = jnp.where(new_ok, jnp.exp(s_new - m), 0.0)
        den = jnp.sum(e, axis=-1, keepdims=True) + e_new
        inv = 1.0 / jnp.where(den > 0, den, 1.0)
        return _dot_nt((e * inv).astype(BF16), v_t) + (e_new * inv) * nr[:, kd:2 * kd]

    k_t = jnp.concatenate([buf_ref[ring_slot, k, 0:kd, :] for k in range(n_pages)], axis=1).astype(BF16)
    v_t = jnp.concatenate([buf_ref[ring_slot, k, kd:2 * kd, :] for k in range(n_pages)], axis=1).astype(BF16)
    n_keys = k_t.shape[1]
    col_slot = _shr(lax.broadcasted_iota(I32, (1, n_keys), 1), page)
    page_blk = _shr(lax.broadcasted_iota(I32, (1, page), 1), SEL_BLOCK)
    slot_ok = []
    new_ok = jnp.zeros((n_heads, 1), I32)
    for gg in range(n_kv):
        for k in range(n_sel):
            idx = sel_ref[b, gg * n_sel + k]
            cached = ((idx >= 0) & (idx < n_past_blocks)).astype(I32)
            slot_ok.append(jnp.where(page_blk == (idx & (per - 1)), cached, 0))
            new_ok = jnp.where(head_grp == gg, new_ok | (idx == n_past_blocks).astype(I32), new_ok)
    mask = (jnp.concatenate(slot_ok, axis=1) > 0) & (_shr(col_slot, n_sel) == head_grp)
    o_slc = _group_value_lanes(attend_with_new_key(k_t, v_t, mask, new_slc_ref[0], new_ok > 0), hd, n_rep)

    all_ok = jnp.full((n_heads, cwin_ref.shape[2]), True)
    o_win = _group_value_lanes(
        attend_with_new_key(cwin_ref[0, 0:kd, :].astype(BF16), cwin_ref[0, kd:2 * kd, :].astype(BF16), all_ok,
                            new_win_ref[0], jnp.full((n_heads, 1), True)), hd, n_rep)

    gates = g_ref[0]
    o_ref[0] = gates[:, 0:1] * ocmp_ref[0] + gates[:, 1:2] * o_slc + gates[:, 2:3] * o_win

    for k in range(n_pages):
        page_copy(jnp.minimum(b + PAGE_LOOKAHEAD, total - 1), b + PAGE_LOOKAHEAD, k).start()

    @pl.when(b == total - 1)
    def _():
        for ahead in range(1, PAGE_LOOKAHEAD + 1):
            for k in range(n_pages):
                page_copy(b, b + ahead, k).wait()


def _attend_dec(page_table, sel, slc_pages, q3, new_slc, new_win, cache_win, gates3, o_cmp, *, hd, n_rep, n_sel,
                n_past_blocks):
    n, n_heads, _ = q3.shape
    n_kv = n_heads // n_rep
    assert n > PAGE_LOOKAHEAD and n_past_blocks * SEL_BLOCK == page_table.shape[1] * slc_pages.shape[2]

    def row_spec(shape):
        nd = len(shape)
        return pl.BlockSpec((1,) + tuple(shape[1:]), lambda i, pt, sl: (i,) + (0,) * (nd - 1))

    others = (q3, new_slc, new_win, cache_win, gates3, o_cmp)
    return pl.pallas_call(
        functools.partial(_attend_dec_kernel, hd=hd, n_rep=n_rep, n_sel=n_sel, n_past_blocks=n_past_blocks),
        grid_spec=pltpu.PrefetchScalarGridSpec(
            num_scalar_prefetch=2, grid=(n,),
            in_specs=[pl.BlockSpec(memory_space=pl.ANY)] + [row_spec(a.shape) for a in others],
            out_specs=row_spec(o_cmp.shape),
            scratch_shapes=[pltpu.VMEM((PAGE_SLOTS, n_kv * n_sel) + slc_pages.shape[1:], F32),
                            pltpu.SemaphoreType.DMA((PAGE_SLOTS,))]),
        out_shape=jax.ShapeDtypeStruct(o_cmp.shape, F32),
        compiler_params=_cparams(1), name="attend_dec",
    )(page_table, sel, slc_pages, *others)


def _mix_kernel(x_ref, o_ref, nw_ref, wuv_ref, wgate_ref, gn_ref, ws_ref, bs_ref, wpa_ref, wpb_ref, wout_ref,
                x1_ref, v_ref, *, width, chunk, single_pos):
    x = x_ref[0]
    d = x.shape[-1]
    h = _rmsnorm(x, nw_ref[...]).astype(BF16)
    uv = jax.nn.gelu(_dot_nt(h, wuv_ref[...]))
    u, vn = uv[:, 0:width], _rmsnorm(uv[:, width:2 * width], gn_ref[...])
    gates = jax.nn.sigmoid(_dot_nt(h, wgate_ref[...]))
    gw = width // GMLP_GROUPS
    if single_pos:
        v_ref[0] = vn
        mixed = u * (ws_ref[...] * vn + bs_ref[...])
    else:
        rows = x.shape[0]
        v_ref[0] = vn[rows - chunk:rows, :]
        tri = lax.broadcasted_iota(I32, (chunk, chunk), 0) >= lax.broadcasted_iota(I32, (chunk, chunk), 1)
        vb = vn.astype(BF16)
        pieces = []
        for c in range(rows // chunk):
            zs = []
            for gi in range(GMLP_GROUPS):
                wm = jnp.where(tri, ws_ref[gi], 0.0).astype(BF16)
                zs.append(_dot(wm, vb[c * chunk:(c + 1) * chunk, gi * gw:(gi + 1) * gw]) + bs_ref[:, gi:gi + 1])
            pieces.append(jnp.concatenate(zs, axis=1))
        mixed = u * jnp.concatenate(pieces, axis=0)
    br_a = _dot(o_ref[0].astype(BF16), wpa_ref[...])
    br_b = _dot(mixed.astype(BF16), wpb_ref[...])
    merged = gates[:, 0:d] * br_a + gates[:, d:2 * d] * br_b
    x1_ref[0] = x + _dot(merged.astype(BF16), wout_ref[...])


def _mix(x, o_nsa, nw, wuv, wgate, gn, ws, bs, wpa, wpb, wout, *, tm, chunk, single_pos):
    b, t, d = x.shape
    width = wuv.shape[0] // 2
    v_rows = tm if single_pos else chunk
    weights = (nw, wuv, wgate, gn, ws, bs, wpa, wpb, wout)
    return pl.pallas_call(
        functools.partial(_mix_kernel, width=width, chunk=chunk, single_pos=single_pos),
        grid=(b, t // tm),
        in_specs=[pl.BlockSpec((1, tm, d), lambda i, j: (i, j, 0)),
                  pl.BlockSpec((1, tm, o_nsa.shape[-1]), lambda i, j: (i, j, 0))]
                 + [_resident_spec(a.shape) for a in weights],
        out_specs=[pl.BlockSpec((1, tm, d), lambda i, j: (i, j, 0)),
                   pl.BlockSpec((1, v_rows, width), lambda i, j: (i, 0, 0))],
        out_shape=[jax.ShapeDtypeStruct((b, t, d), F32), jax.ShapeDtypeStruct((b, v_rows, width), F32)],
        compiler_params=_cparams(2), name="mix_dec" if single_pos else "mix",
    )(x, o_nsa, *weights)


def _ffn_kernel(x1_ref, prev_ref, nf_ref, wup_ref, cw_ref, cb_ref, wdown_ref, nfin_ref, y_ref, a_ref,
                *, d_ff, f_tile, halo, single_pos):
    x1 = x1_ref[0]
    rows = x1.shape[0]
    if single_pos:
        h = _rmsnorm(x1, nf_ref[...]).astype(BF16)
    else:
        h = _rmsnorm(jnp.concatenate([prev_ref[0], x1], axis=0), nf_ref[...]).astype(BF16)
        ext_row = lax.broadcasted_iota(I32, (rows + halo, f_tile), 0)
        first = pl.program_id(1) == 0
    y = jnp.zeros_like(x1)
    for f0 in range(0, d_ff, f_tile):
        cols = slice(f0, f0 + f_tile)
        a = _dot(h, wup_ref[:, cols])
        bgate = _dot(h, wup_ref[:, d_ff + f0:d_ff + f0 + f_tile])
        if single_pos:
            a_ref[0, :, cols] = a
            c = cb_ref[:, cols] + prev_ref[0, :, cols] * cw_ref[0:1, cols] + prev_ref[1, :, cols] * cw_ref[1:2, cols] \
                + a * cw_ref[2:3, cols]
        else:
            a = jnp.where((ext_row < halo) & first, 0.0, a)
            back2 = pltpu.roll(a, 2, 0)
            a_ref[0, :, cols] = back2[0:2, :]
            c = cb_ref[:, cols] + back2[halo:, :] * cw_ref[0:1, cols] \
                + pltpu.roll(a, 1, 0)[halo:, :] * cw_ref[1:2, cols] + a[halo:, :] * cw_ref[2:3, cols]
            bgate = bgate[halo:, :]
        y = y + _dot((jax.nn.gelu(c) * bgate).astype(BF16), wdown_ref[cols, :])
    y_ref[0] = _rmsnorm(x1 + y, nfin_ref[...])


def _ffn(x1, prev, nf, wup, cw, cb, wdown, nfin, *, tm, f_tile, single_pos):
    b, t, d = x1.shape
    d_ff = wdown.shape[0]
    halo = 8
    weights = (nf, wup, cw, cb, wdown, nfin)
    if single_pos:
        prev_spec = _const_spec(prev.shape)
        a_rows = tm
    else:
        per = tm // halo
        prev_spec = pl.BlockSpec((1, halo, d), lambda i, j: (i, jnp.maximum(j * per - 1, 0), 0))
        a_rows = 2
    return pl.pallas_call(
        functools.partial(_ffn_kernel, d_ff=d_ff, f_tile=f_tile, halo=halo, single_pos=single_pos),
        grid=(b, t // tm),
        in_specs=[pl.BlockSpec((1, tm, d), lambda i, j: (i, j, 0)), prev_spec]
                 + [_resident_spec(a.shape) for a in weights],
        out_specs=[pl.BlockSpec((1, tm, d), lambda i, j: (i, j, 0)),
                   pl.BlockSpec((1, a_rows, d_ff), lambda i, j: (i, 0, 0))],
        out_shape=[jax.ShapeDtypeStruct((b, t, d), F32), jax.ShapeDtypeStruct((b, a_rows, d_ff), F32)],
        compiler_params=_cparams(2), name="ffn_dec" if single_pos else "ffn",
    )(x1, prev, *weights)


def _compress_params(pe, w1, b1, w2, n_kv):
    cmp_len, hd = pe.shape[1], pe.shape[2]
    hid = w1.shape[2]
    halves = cmp_len // CMP_STRIDE
    eye = jnp.eye(n_kv, dtype=w1.dtype)
    pe_t = jnp.broadcast_to(pe.reshape(2, halves, CMP_STRIDE, 1, hd), (2, halves, CMP_STRIDE, n_kv, hd))
    pe_t = pe_t.reshape(2, halves, 1, CMP_STRIDE * n_kv * hd)
    w1h = w1.reshape(2, halves, CMP_STRIDE, hd, hid)
    w1b = jnp.einsum('krsdh,gf->ksgdrfh', w1h, eye).reshape(2, CMP_STRIDE * n_kv * hd, halves * n_kv * hid)
    b1t = jnp.tile(b1, (1, n_kv)).reshape(2, 1, n_kv * hid)
    w2b = jnp.einsum('khd,gf->kghfd', w2, eye).reshape(2, n_kv * hid, n_kv * hd)
    return pe_t, w1b.astype(BF16), b1t, w2b.astype(BF16)


def kernel(x_prompt, x_sample, cache_cmp, cache_slc, cache_win, state_conv, page_table, norm_mix, w_in, cmp_pe,
           cmp_w1, cmp_b1, cmp_w2, gmlp_norm, gmlp_ws, gmlp_bs, w_proj_a, w_proj_b, w_out, norm_ffn, w_up, conv_w,
           conv_b, w_down, norm_final):
    depth = w_in.shape[0]
    assert depth == 1, "single-layer step"
    bp, t, d = x_prompt.shape
    bd, tn, _ = x_sample.shape
    assert tn == 1
    n_kv, hd = cache_cmp.shape[4], cache_cmp.shape[5]
    page = cache_cmp.shape[2]
    q_cols = w_proj_a.shape[1]
    n_heads = q_cols // hd
    n_rep = n_heads // n_kv
    kv_cols = 2 * n_kv * hd
    width = w_proj_b.shape[1]
    chunk = gmlp_ws.shape[-1]
    d_ff = w_down.shape[1]
    n_pages = page_table.shape[1]
    past_len = n_pages * page
    scale = hd ** -0.5
    assert conv_w.shape[1] == 3 and cache_win.shape[2] <= WINDOW and past_len % SEL_BLOCK == 0

    w_in_t = w_in[0].T
    off_kv, off_g = q_cols, q_cols + 3 * kv_cols
    off_uv = off_g + 3 * n_heads
    off_gate = off_uv + 2 * width
    w_qt, w_kvt, w_g, w_uv, w_gate = (
        w_in_t[a:b].astype(BF16) for a, b in
        ((0, off_kv), (off_kv, off_g), (off_g, off_uv), (off_uv, off_gate), (off_gate, w_in_t.shape[0])))
    tiles = _tiles(t, d_ff, n_pages)
    g_rows = 16
    w_gt = jnp.pad(w_g.reshape(n_kv, 3 * n_rep, d), ((0, 0), (0, g_rows - 3 * n_rep), (0, 0))).reshape(n_kv * g_rows, d)
    w_dec = jnp.concatenate([w_qt, w_kvt, jnp.pad(w_g, ((0, LANES - 3 * n_heads), (0, 0)))], axis=0)
    nm, nf, nfin, gn = norm_mix[0][None], norm_ffn[0][None], norm_final[None], gmlp_norm[0][None]
    pe_t, w1b, b1t, w2b = _compress_params(cmp_pe[0], cmp_w1[0], cmp_b1[0], cmp_w2[0], n_kv)
    wpa, wpb, wout = w_proj_a[0].astype(BF16), w_proj_b[0].astype(BF16), w_out[0].astype(BF16)
    wup, wdown = w_up[0].astype(BF16), w_down[0].astype(BF16)
    cw, cb = conv_w[0], conv_b[0][None]
    ws, bs = gmlp_ws[0], gmlp_bs[0]

    cmp_t, slc_t, win_t, qt, gt = _front(x_prompt, nm, w_kvt, w_qt, w_gt, kv_cols=kv_cols, scale=scale * LOG2_E,
                                         tm=tiles.front_rows)
    kcv = _compress(cmp_t, pe_t, w1b, b1t, w2b, n_seq=bp, steps=1, rows=t // CMP_STRIDE)
    o_nsa = _attention(qt, gt, kcv, slc_t, win_t, n_rep=n_rep, hd=hd)
    x1, v_p = _mix(x_prompt, o_nsa, nm, w_uv, w_gate, gn, ws, bs.T, wpa, wpb, wout,
                   tm=tiles.mix_rows, chunk=chunk, single_pos=False)
    y_p, conv_p = _ffn(x1, x1, nf, wup, cw, cb, wdown, nfin, tm=tiles.ffn_rows, f_tile=tiles.ffn_cols,
                       single_pos=False)

    xs = x_sample.reshape(bd, d)
    q_s, kv_s, g_s = _front_dec(xs, nm, w_dec, q_cols=q_cols, kv_cols3=3 * kv_cols, scale=scale)
    kv_cmp_s, kv_slc_s, kv_win_s = kv_s[:, 0:kv_cols], kv_s[:, kv_cols:2 * kv_cols], kv_s[:, 2 * kv_cols:]
    def positions_last(c):
        return jnp.transpose(c, (0, 2, 3, 4, 1)).reshape(c.shape[0], kv_cols, c.shape[1])

    cmp_pages, slc_pages, win_rows = positions_last(cache_cmp[0]), positions_last(cache_slc[0]), positions_last(cache_win[0])
    pages_per_step = tiles.pages_per_step
    kcv_s = _compress(cmp_pages, pe_t, w1b, b1t, w2b, n_seq=bd, steps=n_pages // pages_per_step,
                      rows=pages_per_step * page // CMP_STRIDE, page_table=page_table)
    q3 = q_s.reshape(bd, n_heads, hd)
    n_blocks_s = -(-(past_len + tn) // SEL_BLOCK)
    o_cmp_s, idx_s = _cmp_select_dec(q3, kcv_s, hd=hd, n_rep=n_rep, q_pos=past_len, n_blocks=n_blocks_s)
    sel = idx_s[:, 0:n_kv, 0:N_SEL].reshape(bd, n_kv * N_SEL)
    o_nsa_s = _attend_dec(page_table, sel, slc_pages, q3, kv_slc_s[:, None, :], kv_win_s[:, None, :], win_rows,
                          g_s[:, 0:3 * n_heads].reshape(bd, n_heads, 3), o_cmp_s,
                          hd=hd, n_rep=n_rep, n_sel=N_SEL, n_past_blocks=past_len // SEL_BLOCK)
    gw = width // GMLP_GROUPS
    ws0 = jnp.repeat(ws[:, 0, 0], gw)[None]
    bs0 = jnp.repeat(bs[:, 0], gw)[None]
    x1_s, v_s = _mix(xs[None], o_nsa_s.reshape(1, bd, q_cols), nm, w_uv, w_gate, gn, ws0, bs0, wpa, wpb, wout,
                     tm=bd, chunk=chunk, single_pos=True)
    prev_s = jnp.swapaxes(state_conv[0], 0, 1)
    y_s, a_s = _ffn(x1_s, prev_s, nf, wup, cw, cb, wdown, nfin, tm=bd, f_tile=d_ff // 2, single_pos=True)

    def rows6(a_t):
        n, _, npos = a_t.shape
        return jnp.transpose(a_t.reshape(n, 2, n_kv, hd, npos), (0, 4, 1, 2, 3))[None]

    win_keep = min(WINDOW, t)
    win_keep_s = min(WINDOW, cache_win.shape[2] + tn)
    win_s = jnp.concatenate([win_rows, kv_win_s[:, :, None]], axis=2)[:, :, cache_win.shape[2] + tn - win_keep_s:]
    conv_s = jnp.concatenate([state_conv[0][:, 1:], a_s[0][:, None, :]], axis=1)
    return (y_p, y_s.reshape(bd, tn, d),
            rows6(cmp_t), rows6(slc_t), rows6(win_t[:, :, t - win_keep:]),
            v_p[None], conv_p[None],
            rows6(kv_cmp_s[:, :, None]), rows6(kv_slc_s[:, :, None]), rows6(win_s),
            v_s.reshape(1, bd, tn, width), conv_s[None])
```

```python
import functools
from typing import NamedTuple

import jax
import jax.numpy as jnp
from jax import lax
from jax.experimental import pallas as pl
from jax.experimental.pallas import tpu as pltpu

F32 = jnp.float32
BF16 = jnp.bfloat16
I32 = jnp.int32

CMP_STRIDE = 16
SEG_PITCH = 24
PAGE_LOOKAHEAD = 2
PAGE_SLOTS = PAGE_LOOKAHEAD + 1
CMP_PAGE_LOOKAHEAD = 3
CMP_PAGE_SLOTS = CMP_PAGE_LOOKAHEAD + 1
SEL_BLOCK = 64
N_SEL = 16
N_LOCAL_SEL = 2
WINDOW = 512
Q_BLOCK = 256
KEY_TILE = 128
SLC_TILES = 8
V_PAD = 16
LOG2_E = 1.4426950408889634
GMLP_GROUPS = 4
EPS = 1e-6
NEG = -1e30
BELOW_NEG = -3e38
SEL_BONUS = 1e6

V7X_VMEM_BYTES = 64 * 1024 * 1024
VMEM_REQUEST_BYTES = 56 * 1024 * 1024
LANES = 128


class _Tiles(NamedTuple):
    front_rows: int
    mix_rows: int
    ffn_rows: int
    ffn_cols: int
    pages_per_step: int


def _tiles(t, d_ff, n_pages):
    return _Tiles(front_rows=min(t, 1024), mix_rows=min(t, 1024), ffn_rows=min(t, 1024), ffn_cols=d_ff // 2,
                  pages_per_step=min(n_pages, 32))


def _cparams(n_grid):
    return pltpu.CompilerParams(
        dimension_semantics=("arbitrary",) * n_grid, vmem_limit_bytes=VMEM_REQUEST_BYTES)


def _rmsnorm(x, g):
    ms = jnp.mean(x * x, axis=-1, keepdims=True)
    return x * lax.rsqrt(ms + EPS) * g


def _dot(a, b):
    return jnp.dot(a, b, preferred_element_type=F32)


def _dot_nt(a, b):
    return lax.dot_general(a, b, (((1,), (1,)), ((), ())), preferred_element_type=F32)


def _bf16_pieces(x):
    hi = x.astype(BF16).astype(F32)
    mid = (x - hi).astype(BF16).astype(F32)
    return hi, mid, x - hi - mid


def _shr(x, n):
    assert n & (n - 1) == 0
    return x >> (n.bit_length() - 1)


def _const_spec(shape):
    nd = len(shape)
    return pl.BlockSpec(shape, lambda *_: (0,) * nd)


def _resident_spec(shape):
    nd = len(shape)
    return pl.BlockSpec(shape, lambda *_: (0,) * nd, pipeline_mode=pl.Buffered(1))


def _front_kernel(x_ref, nw_ref, wkvt_ref, wqt_ref, wgt_ref, cmp_ref, slc_ref, win_ref, qt_ref, gt_ref,
                  *, kv_cols, scale):
    h = _rmsnorm(x_ref[0], nw_ref[...]).astype(BF16)
    kvt = _dot_nt(wkvt_ref[...], h)
    cmp_ref[0] = kvt[0:kv_cols]
    slc_ref[0] = kvt[kv_cols:2 * kv_cols]
    win_ref[0] = kvt[2 * kv_cols:3 * kv_cols]
    qt_ref[0] = (_dot_nt(wqt_ref[...], h) * scale).astype(BF16)
    gt_ref[0] = jax.nn.sigmoid(_dot_nt(wgt_ref[...], h))


def _front(x, nw, wkvt, wqt, wgt, *, kv_cols, scale, tm):
    b, t, d = x.shape
    q_cols, g_rows = wqt.shape[0], wgt.shape[0]
    kv_shape = jax.ShapeDtypeStruct((b, kv_cols, t), F32)
    kv_spec = pl.BlockSpec((1, kv_cols, tm), lambda i, j: (i, 0, j))
    return pl.pallas_call(
        functools.partial(_front_kernel, kv_cols=kv_cols, scale=scale),
        grid=(b, t // tm),
        in_specs=[pl.BlockSpec((1, tm, d), lambda i, j: (i, j, 0)), _const_spec(nw.shape),
                  _const_spec(wkvt.shape), _const_spec(wqt.shape), _const_spec(wgt.shape)],
        out_specs=[kv_spec, kv_spec, kv_spec,
                   pl.BlockSpec((1, q_cols, tm), lambda i, j: (i, 0, j)),
                   pl.BlockSpec((1, g_rows, tm), lambda i, j: (i, 0, j))],
        out_shape=[kv_shape, kv_shape, kv_shape,
                   jax.ShapeDtypeStruct((b, q_cols, t), BF16),
                   jax.ShapeDtypeStruct((b, g_rows, t), F32)],
        compiler_params=_cparams(2), name="front",
    )(x, nw, wkvt, wqt, wgt)


def _front_dec_kernel(x_ref, nw_ref, w_ref, q_ref, kv_ref, g_ref, *, q_cols, kv_cols3, scale):
    h = _rmsnorm(x_ref[...], nw_ref[...]).astype(BF16)
    z = _dot_nt(h, w_ref[...])
    q_ref[...] = z[:, 0:q_cols] * scale
    kv_ref[...] = z[:, q_cols:q_cols + kv_cols3]
    g_ref[...] = jax.nn.sigmoid(z[:, q_cols + kv_cols3:])


def _front_dec(x, nw, w, *, q_cols, kv_cols3, scale):
    n = x.shape[0]
    g_cols = w.shape[0] - q_cols - kv_cols3
    return pl.pallas_call(
        functools.partial(_front_dec_kernel, q_cols=q_cols, kv_cols3=kv_cols3, scale=scale),
        grid=(1,),
        in_specs=[_const_spec(x.shape), _const_spec(nw.shape), _const_spec(w.shape)],
        out_specs=[_const_spec((n, q_cols)), _const_spec((n, kv_cols3)), _const_spec((n, g_cols))],
        out_shape=[jax.ShapeDtypeStruct((n, q_cols), F32), jax.ShapeDtypeStruct((n, kv_cols3), F32),
                   jax.ShapeDtypeStruct((n, g_cols), F32)],
        compiler_params=_cparams(1), name="front_dec",
    )(x, nw, w)


def _segments_onto_rows(tile_of, n_tiles, pos_ref, kv, stride):
    seg_per_tile = LANES // stride
    for ti in range(n_tiles):
        t = tile_of(ti).T
        for n in range(seg_per_tile):
            p0 = (ti * seg_per_tile + n) * SEG_PITCH
            pos_ref[kv, p0:p0 + stride, :] = t[n * stride:(n + 1) * stride, :]


def _compress_rows(kv, pe_ref, w1_ref, b1_ref, w2_ref, out_ref, carry_ref, pos_ref, *, stride, half, hid2):
    rows = out_ref.shape[1]
    xkv = jnp.concatenate(
        [pos_ref[kv, pl.ds(s, rows, stride=SEG_PITCH), :] for s in range(stride)],
        axis=1)
    parts = []
    for r in range(2):
        a = (xkv + pe_ref[kv, r]).astype(BF16)
        parts.append(_dot(a, w1_ref[kv, :, r * hid2:(r + 1) * hid2]))
    prev = carry_ref[kv, 0:1, :]
    row = lax.broadcasted_iota(I32, (rows, hid2), 0)
    shifted = jnp.where(row == 0, prev, pltpu.roll(parts[0], 1, 0))
    carry_ref[kv, 0:1, :] = parts[0][rows - 1:rows, :]
    hid = b1_ref[kv] + shifted + parts[1]
    out_ref[0, :, kv * half:(kv + 1) * half] = _dot(jax.nn.gelu(hid).astype(BF16), w2_ref[kv])


def _compress_kernel(x_ref, pe_ref, w1_ref, b1_ref, w2_ref, out_ref, carry_ref, pos_ref, *, stride, kv_cols, hid2):
    half = kv_cols // 2
    assert half == LANES

    @pl.when(pl.program_id(1) == 0)
    def _():
        carry_ref[...] = jnp.zeros_like(carry_ref)

    for kv in range(2):
        _segments_onto_rows(lambda ti: x_ref[0, kv * half:(kv + 1) * half, ti * LANES:(ti + 1) * LANES],
                            x_ref.shape[2] // LANES, pos_ref, kv, stride)
        _compress_rows(kv, pe_ref, w1_ref, b1_ref, w2_ref, out_ref, carry_ref, pos_ref,
                       stride=stride, half=half, hid2=hid2)


def _compress_paged_kernel(pt_ref, pages_ref, pe_ref, w1_ref, b1_ref, w2_ref, out_ref, carry_ref, pos_ref, buf_ref,
                           sem, *, stride, kv_cols, hid2, n_pages):
    half = kv_cols // 2
    assert half == LANES and buf_ref.shape[3] == LANES
    steps = pl.num_programs(1)
    total = pl.num_programs(0) * steps
    t = pl.program_id(0) * steps + pl.program_id(1)
    look, slots = CMP_PAGE_LOOKAHEAD, CMP_PAGE_SLOTS
    static = dict(stride=stride, half=half, hid2=hid2)

    def page_copy(src_step, ring_step, k):
        page = pt_ref[src_step // steps, (src_step % steps) * n_pages + k]
        slot = ring_step % slots
        return pltpu.make_async_copy(pages_ref.at[page], buf_ref.at[slot, k], sem.at[slot])

    def rows_of(ring_step, kv):
        slot = ring_step % slots
        _segments_onto_rows(lambda ti: buf_ref[slot, ti, kv * half:(kv + 1) * half, :], n_pages, pos_ref, kv, stride)

    @pl.when(t == 0)
    def _():
        for ahead in range(look):
            for k in range(n_pages):
                page_copy(jnp.minimum(ahead, total - 1), ahead, k).start()
        for k in range(n_pages):
            page_copy(0, 0, k).wait()
        rows_of(0, 0)

    @pl.when(pl.program_id(1) == 0)
    def _():
        carry_ref[...] = jnp.zeros_like(carry_ref)

    for k in range(n_pages):
        page_copy(t, t + 1, k).wait()
    _compress_rows(0, pe_ref, w1_ref, b1_ref, w2_ref, out_ref, carry_ref, pos_ref, **static)
    rows_of(t, 1)
    _compress_rows(1, pe_ref, w1_ref, b1_ref, w2_ref, out_ref, carry_ref, pos_ref, **static)
    rows_of(t + 1, 0)
    for k in range(n_pages):
        page_copy(jnp.minimum(t + look, total - 1), t + look, k).start()

    @pl.when(t == total - 1)
    def _():
        for ahead in range(2, look + 1):
            for k in range(n_pages):
                page_copy(t, t + ahead, k).wait()


def _compress(x, pe, w1, b1, w2, *, n_seq, steps, rows, page_table=None):
    stride, kv_cols = CMP_STRIDE, w2.shape[2] * 2
    hid2 = b1.shape[2]
    static = dict(stride=stride, kv_cols=kv_cols, hid2=hid2)
    out_shape = jax.ShapeDtypeStruct((n_seq, steps * rows, kv_cols), F32)
    scratch = [pltpu.VMEM((2, 8, hid2), F32), pltpu.VMEM((2, rows * SEG_PITCH, kv_cols // 2), F32)]
    weights = (pe, w1, b1, w2)
    if page_table is None:
        return pl.pallas_call(
            functools.partial(_compress_kernel, **static), grid=(n_seq, steps),
            in_specs=[pl.BlockSpec((1, kv_cols, rows * stride), lambda i, j: (i, 0, j))]
                     + [_const_spec(a.shape) for a in weights],
            out_specs=pl.BlockSpec((1, rows, kv_cols), lambda i, j: (i, j, 0)),
            out_shape=out_shape, scratch_shapes=scratch, compiler_params=_cparams(2), name="compress",
        )(x, *weights)
    page = x.shape[2]
    n_pages = rows * stride // page
    assert n_seq * steps > CMP_PAGE_LOOKAHEAD
    w_specs = [pl.BlockSpec(a.shape, functools.partial(lambda i, j, pt, nd: (0,) * nd, nd=a.ndim)) for a in weights]
    return pl.pallas_call(
        functools.partial(_compress_paged_kernel, n_pages=n_pages, **static),
        grid_spec=pltpu.PrefetchScalarGridSpec(
            num_scalar_prefetch=1, grid=(n_seq, steps),
            in_specs=[pl.BlockSpec(memory_space=pl.ANY)] + w_specs,
            out_specs=pl.BlockSpec((1, rows, kv_cols), lambda i, j, pt: (i, j, 0)),
            scratch_shapes=scratch + [pltpu.VMEM((CMP_PAGE_SLOTS, n_pages, kv_cols, page), F32),
                                      pltpu.SemaphoreType.DMA((CMP_PAGE_SLOTS,))]),
        out_shape=out_shape, compiler_params=_cparams(2), name="compress_paged",
    )(page_table, x, *weights)


def _block_scores(imp, blk, q_pos, n_blocks):
    cur = _shr(q_pos, SEL_BLOCK)
    valid = (blk * SEL_BLOCK <= q_pos) & (blk < n_blocks)
    forced = (blk == 0) | ((blk <= cur) & (blk > cur - N_LOCAL_SEL))
    score = jnp.where(valid, imp + jnp.where(forced, SEL_BONUS, 0.0), NEG)
    return jnp.where(blk < n_blocks, score, BELOW_NEG)


def _attn_kernel(qt_ref, gt_ref, kcv_ref, slc_ref, win_ref, o_ref,
                 kaug_ref, vts_ref, kwin_ref, vtw_ref, kc_ref, vct_ref, *, n_kv, n_rep, hd, n_blocks):
    i = pl.program_id(1)
    qb, kt = Q_BLOCK, KEY_TILE
    q_tiles = qb // kt
    n_tiles = slc_ref.shape[2] // kt
    n_cmp_rows = kcv_ref.shape[1]
    nq = n_rep * qb
    kd = n_kv * hd
    g_rows = gt_ref.shape[1] // n_kv
    assert kd == LANES and n_blocks == hd, "the selection bias rows ride in the key one-hot lanes"

    @pl.when(i == 0)
    def _():
        lane = lax.broadcasted_iota(I32, (kt, kd), 1)
        krow = lax.broadcasted_iota(I32, (kt, kd), 0)

        def group_lanes(x, gg):
            return x if gg == 0 else pltpu.roll(x, kd - gg * hd, 1)

        ones_row = (lax.broadcasted_iota(I32, (V_PAD, kt), 0) == 0).astype(BF16)
        for c in range(n_tiles):
            cols = slice(c * kt, (c + 1) * kt)
            onehot = (lane - hd == _shr(c * kt + krow, SEL_BLOCK)).astype(F32)
            kt_s, kt_w = slc_ref[0, 0:kd, cols].T, win_ref[0, 0:kd, cols].T
            for gg in range(n_kv):
                v_rows = slice(kd + gg * hd, kd + (gg + 1) * hd)
                kaug_ref[gg, cols, :] = jnp.where(lane < hd, group_lanes(kt_s, gg), onehot).astype(BF16)
                vts_ref[gg, c, 0:hd, :] = slc_ref[0, v_rows, cols].astype(BF16)
                vts_ref[gg, c, hd:, :] = ones_row
                kwin_ref[gg, cols, :] = group_lanes(kt_w, gg)[:, 0:hd].astype(BF16)
                vtw_ref[gg, c, 0:hd, :] = win_ref[0, v_rows, cols].astype(BF16)
                vtw_ref[gg, c, hd:, :] = ones_row
        for c in range(n_cmp_rows // kt):
            rows = slice(c * kt, (c + 1) * kt)
            blk = kcv_ref[0, rows, :]
            vt = blk[:, kd:2 * kd].T
            for gg in range(n_kv):
                kc_ref[gg, rows, :] = group_lanes(blk[:, 0:kd], gg)[:, 0:hd].astype(BF16)
                vct_ref[gg, :, rows] = vt[gg * hd:(gg + 1) * hd, :].astype(BF16)

    q_pos = i * qb + (lax.broadcasted_iota(I32, (1, nq), 1) & (qb - 1))

    groups = range(n_kv)
    heads = range(n_rep)

    q_minus_k = (lax.broadcasted_iota(I32, (kt, nq), 1) & (qb - 1)) - lax.broadcasted_iota(I32, (kt, nq), 0)

    def softmax_step(m, sc):
        m_new = jnp.maximum(m, jnp.max(sc, axis=0, keepdims=True))
        return m_new, jnp.exp2(m - m_new), jnp.exp2(sc - m_new).astype(BF16)

    def normalised(acc):
        return acc[0:hd, :] * (1.0 / acc[hd:hd + 1, :])

    qts = [qt_ref[0, gg * n_rep * hd:(gg + 1) * n_rep * hd, :] for gg in groups]
    qcats = [jnp.concatenate([qts[gg][h * hd:(h + 1) * hd, :] for h in heads], axis=1) for gg in groups]

    def attend(k_ref, v_ref, qs, tile_ids, state, masked):
        scores = [[masked(u, _dot(k_ref[gg, pl.ds(pl.multiple_of(t * kt, kt), kt), :], qs[gg])) for gg in groups]
                  for u, t in enumerate(tile_ids)]
        maxes, accs = list(state[0]), list(state[1])
        for u, t in enumerate(tile_ids):
            for gg in groups:
                maxes[gg], alpha, pb = softmax_step(maxes[gg], scores[u][gg])
                accs[gg] = alpha * accs[gg] + _dot(v_ref[gg, t], pb)
        return tuple(maxes), tuple(accs)

    state0 = ((jnp.full((1, nq), NEG, F32),) * n_kv, (jnp.zeros((hd + V_PAD, nq), F32),) * n_kv)

    s_cmp = [_dot(kc_ref[gg], qcats[gg]) for gg in groups]
    m_idx = lax.broadcasted_iota(I32, (n_cmp_rows, nq), 0)
    vis = (m_idx >= 1) & ((m_idx - 1) * CMP_STRIDE + 2 * CMP_STRIDE - 1 <= q_pos)
    ratio = SEL_BLOCK // CMP_STRIDE
    pj = lax.broadcasted_iota(I32, (n_blocks, n_cmp_rows), 0)
    pm = lax.broadcasted_iota(I32, (n_blocks, n_cmp_rows), 1)
    pool = ((pm >= 1) & (pm >= ratio * pj) & (pm <= ratio * pj + ratio)).astype(BF16)
    o_cmp, imp = [], []
    for gg in groups:
        s = jnp.where(vis, s_cmp[gg], NEG)
        e = jnp.where(vis, jnp.exp2(s - jnp.max(s, axis=0, keepdims=True)), 0.0)
        den = jnp.sum(e, axis=0, keepdims=True)
        p = e * (1.0 / jnp.where(den > 0, den, 1.0))
        o_cmp.append(_dot(vct_ref[gg], p.astype(BF16)))
        p_grp = p[:, 0:qb]
        for h in range(1, n_rep):
            p_grp = p_grp + p[:, h * qb:(h + 1) * qb]
        pieces = _dot(pool, jnp.concatenate(_bf16_pieces(p_grp), axis=1).astype(BF16))
        imp.append(pieces[:, 0:qb] + pieces[:, qb:2 * qb] + pieces[:, 2 * qb:])

    n_band = WINDOW // kt
    first_t = i * q_tiles - n_band
    win_tiles = [jnp.maximum(first_t + u, 0) for u in range(n_band + q_tiles)]

    def in_window(u, sc):
        dlt = q_minus_k + (n_band - u) * kt
        if u < q_tiles:
            sc = jnp.where(dlt <= WINDOW, sc, NEG)
        if u >= n_band:
            sc = jnp.where(dlt >= 0, sc, NEG)
        if u < n_band:
            sc = sc + jnp.where(first_t + u < 0, NEG, 0.0)
        if u == 0:
            sc = sc + after_importance
        return sc

    after_importance = sum(imp[gg][0:1, 0:1] for gg in groups) * 0.0

    _, accs = attend(kwin_ref, vtw_ref, qcats, win_tiles, state0, in_window)
    o_win = [normalised(accs[gg]) for gg in groups]

    blk_id = lax.broadcasted_iota(I32, (n_blocks, qb), 0)
    blk_f = blk_id.astype(F32)
    qaug = []
    for gg in groups:
        score = _block_scores(imp[gg], blk_id, q_pos[:, 0:qb], n_blocks)
        work, sel = score, jnp.zeros((n_blocks, qb), F32)
        for _ in range(min(N_SEL, n_blocks)):
            mx = jnp.max(work, axis=0, keepdims=True)
            first = jnp.min(jnp.where(work == mx, blk_f, float(n_blocks)), axis=0, keepdims=True)
            pick = blk_f == first
            sel = jnp.where(pick, 1.0, sel)
            work = jnp.where(pick, BELOW_NEG, work)
        bias = jnp.where((sel > 0) & (score > NEG / 2), 0.0, NEG).astype(BF16)
        qaug.append(jnp.concatenate(
            [jnp.concatenate([qts[gg][h * hd:(h + 1) * hd, :], bias], axis=0) for h in heads], axis=1))

    assert n_tiles % SLC_TILES == 0 and SLC_TILES % q_tiles == 0
    n_before = i * q_tiles

    def unmasked(first_tile, n_t):
        return lambda j, st: attend(kaug_ref, vts_ref, qaug, [first_tile + j * n_t + u for u in range(n_t)], st,
                                    lambda u, sc: sc)

    state = lax.fori_loop(0, _shr(n_before, SLC_TILES), unmasked(0, SLC_TILES), state0)
    size = SLC_TILES // 2
    while size >= q_tiles:
        first = _shr(n_before, 2 * size) * (2 * size)
        state = lax.fori_loop(0, _shr(n_before, size) & 1, unmasked(first, size), state)
        size //= 2
    _, accs = attend(kaug_ref, vts_ref, qaug, [n_before + u for u in range(q_tiles)], state,
                     lambda u, sc: jnp.where(q_minus_k - u * kt >= 0, sc, NEG))

    outs = []
    for gg in groups:
        o_slc = normalised(accs[gg])
        for h in heads:
            cols = slice(h * qb, (h + 1) * qb)
            gate = [gt_ref[0, gg * g_rows + 3 * h + br:gg * g_rows + 3 * h + br + 1, :] for br in range(3)]
            outs.append(gate[0] * o_cmp[gg][:, cols] + gate[1] * o_slc[:, cols] + gate[2] * o_win[gg][:, cols])
    o_ref[0] = jnp.concatenate(outs, axis=0).T.astype(BF16)


def _attention(qt, gt, kcv, slc, win, *, n_rep, hd):
    b, q_cols, t = qt.shape
    n_kv = q_cols // (n_rep * hd)
    qb, kt = Q_BLOCK, KEY_TILE
    n_tiles = t // kt
    n_blocks = t // SEL_BLOCK
    kv_cols = slc.shape[1]
    cmp_rows = kcv.shape[1]
    tile_spec = pl.BlockSpec((1, kv_cols, t), lambda bi, i: (bi, 0, 0))
    return pl.pallas_call(
        functools.partial(_attn_kernel, n_kv=n_kv, n_rep=n_rep, hd=hd, n_blocks=n_blocks),
        grid=(b, t // qb),
        in_specs=[pl.BlockSpec((1, q_cols, qb), lambda bi, i: (bi, 0, i)),
                  pl.BlockSpec((1, gt.shape[1], qb), lambda bi, i: (bi, 0, i)),
                  pl.BlockSpec((1,) + kcv.shape[1:], lambda bi, i: (bi, 0, 0)),
                  tile_spec, tile_spec],
        out_specs=pl.BlockSpec((1, qb, q_cols), lambda bi, i: (bi, i, 0)),
        out_shape=jax.ShapeDtypeStruct((b, t, q_cols), BF16),
        scratch_shapes=[pltpu.VMEM((n_kv, t, 2 * hd), BF16), pltpu.VMEM((n_kv, n_tiles, hd + V_PAD, kt), BF16),
                        pltpu.VMEM((n_kv, t, hd), BF16), pltpu.VMEM((n_kv, n_tiles, hd + V_PAD, kt), BF16),
                        pltpu.VMEM((n_kv, cmp_rows, hd), BF16), pltpu.VMEM((n_kv, hd, cmp_rows), BF16)],
        compiler_params=_cparams(2), name="attention",
    )(qt, gt, kcv, slc, win)


def _spread_q(q, hd, n_rep):
    n_heads = q.shape[0]
    kd = n_heads // n_rep * hd
    d = lax.broadcasted_iota(I32, (hd, kd), 0)
    c = lax.broadcasted_iota(I32, (hd, kd), 1)
    qb16 = q.astype(BF16)
    row = lax.broadcasted_iota(I32, (n_heads, kd), 0)
    out = jnp.zeros((n_heads, kd), F32)
    for gg in range(n_heads // n_rep):
        placed = _dot(qb16, (c == d + gg * hd).astype(BF16))
        out = jnp.where(_shr(row, n_rep) == gg, placed, out)
    return out.astype(BF16)


def _masked_softmax_rows(s, mask):
    s = jnp.where(mask, s, NEG)
    e = jnp.where(mask, jnp.exp(s - jnp.max(s, axis=-1, keepdims=True)), 0.0)
    den = jnp.sum(e, axis=-1, keepdims=True)
    return e * (1.0 / jnp.where(den > 0, den, 1.0))


def _group_value_lanes(o_v, hd, n_rep):
    row = lax.broadcasted_iota(I32, (o_v.shape[0], hd), 0)
    out = o_v[:, 0:hd]
    for gg in range(1, o_v.shape[0] // n_rep):
        out = jnp.where(_shr(row, n_rep) == gg, o_v[:, gg * hd:(gg + 1) * hd], out)
    return out


def _cmp_select_dec_kernel(q_ref, kcv_ref, o_ref, idx_ref, pool_ref, score_ref, *, hd, n_rep, q_pos, n_blocks,
                           blk_lanes):
    n_heads = q_ref.shape[1]
    n_rows = kcv_ref.shape[1]
    b = pl.program_id(0)
    n_seq = idx_ref.shape[0]

    @pl.when(b == 0)
    def _():
        ratio = SEL_BLOCK // CMP_STRIDE
        pm = lax.broadcasted_iota(I32, (n_rows, blk_lanes), 0)
        pj = lax.broadcasted_iota(I32, (n_rows, blk_lanes), 1)
        pool_ref[...] = ((pm >= 1) & (pm >= ratio * pj) & (pm <= ratio * pj + ratio)).astype(BF16)

    q2 = _spread_q(q_ref[0], hd, n_rep)
    kd = q2.shape[1]
    m_idx = lax.broadcasted_iota(I32, (n_heads, n_rows), 1)
    vis = (m_idx >= 1) & ((m_idx - 1) * CMP_STRIDE + 2 * CMP_STRIDE - 1 <= q_pos)
    p = _masked_softmax_rows(_dot_nt(q2, kcv_ref[0, :, 0:kd].astype(BF16)), vis)
    o_ref[0] = _group_value_lanes(_dot(p.astype(BF16), kcv_ref[0, :, kd:2 * kd].astype(BF16)), hd, n_rep)

    row = lax.broadcasted_iota(I32, (n_heads, n_rows), 0)
    grp = jnp.zeros((n_heads, n_rows), F32)
    for gg in range(n_heads // n_rep):
        tot = jnp.sum(jnp.where(_shr(row, n_rep) == gg, p, 0.0), axis=0, keepdims=True)
        grp = jnp.where(row == gg, tot, grp)
    pieces = _dot(jnp.concatenate(_bf16_pieces(grp), axis=0).astype(BF16), pool_ref[...])
    imp = pieces[0:n_heads] + pieces[n_heads:2 * n_heads] + pieces[2 * n_heads:]
    score_ref[b] = _block_scores(imp, lax.broadcasted_iota(I32, (n_heads, blk_lanes), 1), q_pos, n_blocks)

    @pl.when(b == n_seq - 1)
    def _():
        work = score_ref[...].reshape(n_seq * n_heads, blk_lanes)
        blk_f = lax.broadcasted_iota(I32, work.shape, 1).astype(F32)
        out_lane = lax.broadcasted_iota(I32, (n_seq * n_heads, LANES), 1)
        out = jnp.full((n_seq * n_heads, LANES), -1, I32)
        for it in range(min(N_SEL, n_blocks)):
            mx = jnp.max(work, axis=-1, keepdims=True)
            first = jnp.min(jnp.where(work == mx, blk_f, float(blk_lanes)), axis=-1, keepdims=True)
            out = jnp.where(out_lane == it, jnp.where(mx > NEG / 2, first.astype(I32), -1), out)
            work = jnp.where(blk_f == first, BELOW_NEG, work)
        idx_ref[...] = out.reshape(n_seq, n_heads, LANES)


def _cmp_select_dec(q3, kcv, *, hd, n_rep, q_pos, n_blocks):
    n, n_heads, _ = q3.shape
    blk_lanes = -(-n_blocks // LANES) * LANES
    return pl.pallas_call(
        functools.partial(_cmp_select_dec_kernel, hd=hd, n_rep=n_rep, q_pos=q_pos, n_blocks=n_blocks,
                          blk_lanes=blk_lanes),
        grid=(n,),
        in_specs=[pl.BlockSpec((1, n_heads, hd), lambda i: (i, 0, 0)),
                  pl.BlockSpec((1,) + kcv.shape[1:], lambda i: (i, 0, 0))],
        out_specs=[pl.BlockSpec((1, n_heads, hd), lambda i: (i, 0, 0)),
                   _const_spec((n, n_heads, LANES))],
        out_shape=[jax.ShapeDtypeStruct((n, n_heads, hd), F32), jax.ShapeDtypeStruct((n, n_heads, LANES), I32)],
        scratch_shapes=[pltpu.VMEM((kcv.shape[1], blk_lanes), BF16), pltpu.VMEM((n, n_heads, blk_lanes), F32)],
        compiler_params=_cparams(1), name="cmp_select_dec",
    )(q3, kcv)


def _attend_dec_kernel(pt_ref, sel_ref, pages_ref, q_ref, new_slc_ref, new_win_ref, cwin_ref, g_ref, ocmp_ref, o_ref,
                       buf_ref, sem, *, hd, n_rep, n_sel, n_past_blocks):
    b = pl.program_id(0)
    total = pl.num_programs(0)
    n_heads = q_ref.shape[1]
    n_kv = n_heads // n_rep
    n_pages = n_kv * n_sel
    page = buf_ref.shape[3]
    per = page // SEL_BLOCK

    def page_copy(src_seq, ring_step, k):
        idx = jnp.clip(sel_ref[src_seq, k], 0, n_past_blocks - 1)
        slot = ring_step % PAGE_SLOTS
        return pltpu.make_async_copy(pages_ref.at[pt_ref[src_seq, _shr(idx, per)]], buf_ref.at[slot, k], sem.at[slot])

    @pl.when(b == 0)
    def _():
        for ahead in range(PAGE_LOOKAHEAD):
            for k in range(n_pages):
                page_copy(jnp.minimum(ahead, total - 1), ahead, k).start()

    for k in range(n_pages):
        page_copy(b, b, k).wait()
    ring_slot = b % PAGE_SLOTS

    q2 = _spread_q(q_ref[0], hd, n_rep)
    q2f = q2.astype(F32)
    kd = q2.shape[1]
    head_grp = _shr(lax.broadcasted_iota(I32, (n_heads, 1), 0), n_rep)

    def attend_with_new_key(k_t, v_t, mask, new_row, new_ok):
        nr = new_row.astype(BF16).astype(F32)
        s = jnp.where(mask, _dot(q2, k_t), NEG)
        s_new = jnp.where(new_ok, jnp.sum(q2f * nr[:, 0:kd], axis=-1, keepdims=True), NEG)
        m = jnp.maximum(jnp.max(s, axis=-1, keepdims=True), s_new)
        e = jnp.where(mask, jnp.exp(s - m), 0.0)
        e_new = jnp.where(new_ok, jnp.exp(s_new - m), 0.0)
        den = jnp.sum(e, axis=-1, keepdims=True) + e_new
        inv = 1.0 / jnp.where(den > 0, den, 1.0)
        return _dot_nt((e * inv).astype(BF16), v_t) + (e_new * inv) * nr[:, kd:2 * kd]

    k_t = jnp.concatenate([buf_ref[ring_slot, k, 0:kd, :] for k in range(n_pages)], axis=1).astype(BF16)
    v_t = jnp.concatenate([buf_ref[ring_slot, k, kd:2 * kd, :] for k in range(n_pages)], axis=1).astype(BF16)
    n_keys = k_t.shape[1]
    col_slot = _shr(lax.broadcasted_iota(I32, (1, n_keys), 1), page)
    page_blk = _shr(lax.broadcasted_iota(I32, (1, page), 1), SEL_BLOCK)
    slot_ok = []
    new_ok = jnp.zeros((n_heads, 1), I32)
    for gg in range(n_kv):
        for k in range(n_sel):
            idx = sel_ref[b, gg * n_sel + k]
            cached = ((idx >= 0) & (idx < n_past_blocks)).astype(I32)
            slot_ok.append(jnp.where(page_blk == (idx & (per - 1)), cached, 0))
            new_ok = jnp.where(head_grp == gg, new_ok | (idx == n_past_blocks).astype(I32), new_ok)
    mask = (jnp.concatenate(slot_ok, axis=1) > 0) & (_shr(col_slot, n_sel) == head_grp)
    o_slc = _group_value_lanes(attend_with_new_key(k_t, v_t, mask, new_slc_ref[0], new_ok > 0), hd, n_rep)

    all_ok = jnp.full((n_heads, cwin_ref.shape[2]), True)
    o_win = _group_value_lanes(
        attend_with_new_key(cwin_ref[0, 0:kd, :].astype(BF16), cwin_ref[0, kd:2 * kd, :].astype(BF16), all_ok,
                            new_win_ref[0], jnp.full((n_heads, 1), True)), hd, n_rep)

    gates = g_ref[0]
    o_ref[0] = gates[:, 0:1] * ocmp_ref[0] + gates[:, 1:2] * o_slc + gates[:, 2:3] * o_win

    for k in range(n_pages):
        page_copy(jnp.minimum(b + PAGE_LOOKAHEAD, total - 1), b + PAGE_LOOKAHEAD, k).start()

    @pl.when(b == total - 1)
    def _():
        for ahead in range(1, PAGE_LOOKAHEAD + 1):
            for k in range(n_pages):
                page_copy(b, b + ahead, k).wait()


def _attend_dec(page_table, sel, slc_pages, q3, new_slc, new_win, cache_win, gates3, o_cmp, *, hd, n_rep, n_sel,
                n_past_blocks):
    n, n_heads, _ = q3.shape
    n_kv = n_heads // n_rep
    assert n > PAGE_LOOKAHEAD and n_past_blocks * SEL_BLOCK == page_table.shape[1] * slc_pages.shape[2]

    def row_spec(shape):
        nd = len(shape)
        return pl.BlockSpec((1,) + tuple(shape[1:]), lambda i, pt, sl: (i,) + (0,) * (nd - 1))

    others = (q3, new_slc, new_win, cache_win, gates3, o_cmp)
    return pl.pallas_call(
        functools.partial(_attend_dec_kernel, hd=hd, n_rep=n_rep, n_sel=n_sel, n_past_blocks=n_past_blocks),
        grid_spec=pltpu.PrefetchScalarGridSpec(
            num_scalar_prefetch=2, grid=(n,),
            in_specs=[pl.BlockSpec(memory_space=pl.ANY)] + [row_spec(a.shape) for a in others],
            out_specs=row_spec(o_cmp.shape),
            scratch_shapes=[pltpu.VMEM((PAGE_SLOTS, n_kv * n_sel) + slc_pages.shape[1:], F32),
                            pltpu.SemaphoreType.DMA((PAGE_SLOTS,))]),
        out_shape=jax.ShapeDtypeStruct(o_cmp.shape, F32),
        compiler_params=_cparams(1), name="attend_dec",
    )(page_table, sel, slc_pages, *others)


def _mix_kernel(x_ref, o_ref, nw_ref, wuv_ref, wgate_ref, gn_ref, ws_ref, bs_ref, wpa_ref, wpb_ref, wout_ref,
                x1_ref, v_ref, *, width, chunk, single_pos):
    x = x_ref[0]
    d = x.shape[-1]
    h = _rmsnorm(x, nw_ref[...]).astype(BF16)
    uv = jax.nn.gelu(_dot_nt(h, wuv_ref[...]))
    u, vn = uv[:, 0:width], _rmsnorm(uv[:, width:2 * width], gn_ref[...])
    gates = jax.nn.sigmoid(_dot_nt(h, wgate_ref[...]))
    gw = width // GMLP_GROUPS
    if single_pos:
        v_ref[0] = vn
        mixed = u * (ws_ref[...] * vn + bs_ref[...])
    else:
        rows = x.shape[0]
        v_ref[0] = vn[rows - chunk:rows, :]
        tri = lax.broadcasted_iota(I32, (chunk, chunk), 0) >= lax.broadcasted_iota(I32, (chunk, chunk), 1)
        vb = vn.astype(BF16)
        pieces = []
        for c in range(rows // chunk):
            zs = []
            for gi in range(GMLP_GROUPS):
                wm = jnp.where(tri, ws_ref[gi], 0.0).astype(BF16)
                zs.append(_dot(wm, vb[c * chunk:(c + 1) * chunk, gi * gw:(gi + 1) * gw]) + bs_ref[:, gi:gi + 1])
            pieces.append(jnp.concatenate(zs, axis=1))
        mixed = u * jnp.concatenate(pieces, axis=0)
    br_a = _dot(o_ref[0].astype(BF16), wpa_ref[...])
    br_b = _dot(mixed.astype(BF16), wpb_ref[...])
    merged = gates[:, 0:d] * br_a + gates[:, d:2 * d] * br_b
    x1_ref[0] = x + _dot(merged.astype(BF16), wout_ref[...])


def _mix(x, o_nsa, nw, wuv, wgate, gn, ws, bs, wpa, wpb, wout, *, tm, chunk, single_pos):
    b, t, d = x.shape
    width = wuv.shape[0] // 2
    v_rows = tm if single_pos else chunk
    weights = (nw, wuv, wgate, gn, ws, bs, wpa, wpb, wout)
    return pl.pallas_call(
        functools.partial(_mix_kernel, width=width, chunk=chunk, single_pos=single_pos),
        grid=(b, t // tm),
        in_specs=[pl.BlockSpec((1, tm, d), lambda i, j: (i, j, 0)),
                  pl.BlockSpec((1, tm, o_nsa.shape[-1]), lambda i, j: (i, j, 0))]
                 + [_resident_spec(a.shape) for a in weights],
        out_specs=[pl.BlockSpec((1, tm, d), lambda i, j: (i, j, 0)),
                   pl.BlockSpec((1, v_rows, width), lambda i, j: (i, 0, 0))],
        out_shape=[jax.ShapeDtypeStruct((b, t, d), F32), jax.ShapeDtypeStruct((b, v_rows, width), F32)],
        compiler_params=_cparams(2), name="mix_dec" if single_pos else "mix",
    )(x, o_nsa, *weights)


def _ffn_kernel(x1_ref, prev_ref, nf_ref, wup_ref, cw_ref, cb_ref, wdown_ref, nfin_ref, y_ref, a_ref,
                *, d_ff, f_tile, halo, single_pos):
    x1 = x1_ref[0]
    rows = x1.shape[0]
    if single_pos:
        h = _rmsnorm(x1, nf_ref[...]).astype(BF16)
    else:
        h = _rmsnorm(jnp.concatenate([prev_ref[0], x1], axis=0), nf_ref[...]).astype(BF16)
        ext_row = lax.broadcasted_iota(I32, (rows + halo, f_tile), 0)
        first = pl.program_id(1) == 0
    y = jnp.zeros_like(x1)
    for f0 in range(0, d_ff, f_tile):
        cols = slice(f0, f0 + f_tile)
        a = _dot(h, wup_ref[:, cols])
        bgate = _dot(h, wup_ref[:, d_ff + f0:d_ff + f0 + f_tile])
        if single_pos:
            a_ref[0, :, cols] = a
            c = cb_ref[:, cols] + prev_ref[0, :, cols] * cw_ref[0:1, cols] + prev_ref[1, :, cols] * cw_ref[1:2, cols] \
                + a * cw_ref[2:3, cols]
        else:
            a = jnp.where((ext_row < halo) & first, 0.0, a)
            back2 = pltpu.roll(a, 2, 0)
            a_ref[0, :, cols] = back2[0:2, :]
            c = cb_ref[:, cols] + back2[halo:, :] * cw_ref[0:1, cols] \
                + pltpu.roll(a, 1, 0)[halo:, :] * cw_ref[1:2, cols] + a[halo:, :] * cw_ref[2:3, cols]
            bgate = bgate[halo:, :]
        y = y + _dot((jax.nn.gelu(c) * bgate).astype(BF16), wdown_ref[cols, :])
    y_ref[0] = _rmsnorm(x1 + y, nfin_ref[...])


def _ffn(x1, prev, nf, wup, cw, cb, wdown, nfin, *, tm, f_tile, single_pos):
    b, t, d = x1.shape
    d_ff = wdown.shape[0]
    halo = 8
    weights = (nf, wup, cw, cb, wdown, nfin)
    if single_pos:
        prev_spec = _const_spec(prev.shape)
        a_rows = tm
    else:
        per = tm // halo
        prev_spec = pl.BlockSpec((1, halo, d), lambda i, j: (i, jnp.maximum(j * per - 1, 0), 0))
        a_rows = 2
    return pl.pallas_call(
        functools.partial(_ffn_kernel, d_ff=d_ff, f_tile=f_tile, halo=halo, single_pos=single_pos),
        grid=(b, t // tm),
        in_specs=[pl.BlockSpec((1, tm, d), lambda i, j: (i, j, 0)), prev_spec]
                 + [_resident_spec(a.shape) for a in weights],
        out_specs=[pl.BlockSpec((1, tm, d), lambda i, j: (i, j, 0)),
                   pl.BlockSpec((1, a_rows, d_ff), lambda i, j: (i, 0, 0))],
        out_shape=[jax.ShapeDtypeStruct((b, t, d), F32), jax.ShapeDtypeStruct((b, a_rows, d_ff), F32)],
        compiler_params=_cparams(2), name="ffn_dec" if single_pos else "ffn",
    )(x1, prev, *weights)


def _compress_params(pe, w1, b1, w2, n_kv):
    cmp_len, hd = pe.shape[1], pe.shape[2]
    hid = w1.shape[2]
    halves = cmp_len // CMP_STRIDE
    eye = jnp.eye(n_kv, dtype=w1.dtype)
    pe_t = jnp.broadcast_to(pe.reshape(2, halves, CMP_STRIDE, 1, hd), (2, halves, CMP_STRIDE, n_kv, hd))
    pe_t = pe_t.reshape(2, halves, 1, CMP_STRIDE * n_kv * hd)
    w1h = w1.reshape(2, halves, CMP_STRIDE, hd, hid)
    w1b = jnp.einsum('krsdh,gf->ksgdrfh', w1h, eye).reshape(2, CMP_STRIDE * n_kv * hd, halves * n_kv * hid)
    b1t = jnp.tile(b1, (1, n_kv)).reshape(2, 1, n_kv * hid)
    w2b = jnp.einsum('khd,gf->kghfd', w2, eye).reshape(2, n_kv * hid, n_kv * hd)
    return pe_t, w1b.astype(BF16), b1t, w2b.astype(BF16)


def kernel(x_prompt, x_sample, cache_cmp, cache_slc, cache_win, state_conv, page_table, norm_mix, w_in, cmp_pe,
           cmp_w1, cmp_b1, cmp_w2, gmlp_norm, gmlp_ws, gmlp_bs, w_proj_a, w_proj_b, w_out, norm_ffn, w_up, conv_w,
           conv_b, w_down, norm_final):
    depth = w_in.shape[0]
    assert depth == 1, "single-layer step"
    bp, t, d = x_prompt.shape
    bd, tn, _ = x_sample.shape
    assert tn == 1
    n_kv, hd = cache_cmp.shape[4], cache_cmp.shape[5]
    page = cache_cmp.shape[2]
    q_cols = w_proj_a.shape[1]
    n_heads = q_cols // hd
    n_rep = n_heads // n_kv
    kv_cols = 2 * n_kv * hd
    width = w_proj_b.shape[1]
    chunk = gmlp_ws.shape[-1]
    d_ff = w_down.shape[1]
    n_pages = page_table.shape[1]
    past_len = n_pages * page
    scale = hd ** -0.5
    assert conv_w.shape[1] == 3 and cache_win.shape[2] <= WINDOW and past_len % SEL_BLOCK == 0

    w_in_t = w_in[0].T
    off_kv, off_g = q_cols, q_cols + 3 * kv_cols
    off_uv = off_g + 3 * n_heads
    off_gate = off_uv + 2 * width
    w_qt, w_kvt, w_g, w_uv, w_gate = (
        w_in_t[a:b].astype(BF16) for a, b in
        ((0, off_kv), (off_kv, off_g), (off_g, off_uv), (off_uv, off_gate), (off_gate, w_in_t.shape[0])))
    tiles = _tiles(t, d_ff, n_pages)
    g_rows = 16
    w_gt = jnp.pad(w_g.reshape(n_kv, 3 * n_rep, d), ((0, 0), (0, g_rows - 3 * n_rep), (0, 0))).reshape(n_kv * g_rows, d)
    w_dec = jnp.concatenate([w_qt, w_kvt, jnp.pad(w_g, ((0, LANES - 3 * n_heads), (0, 0)))], axis=0)
    nm, nf, nfin, gn = norm_mix[0][None], norm_ffn[0][None], norm_final[None], gmlp_norm[0][None]
    pe_t, w1b, b1t, w2b = _compress_params(cmp_pe[0], cmp_w1[0], cmp_b1[0], cmp_w2[0], n_kv)
    wpa, wpb, wout = w_proj_a[0].astype(BF16), w_proj_b[0].astype(BF16), w_out[0].astype(BF16)
    wup, wdown = w_up[0].astype(BF16), w_down[0].astype(BF16)
    cw, cb = conv_w[0], conv_b[0][None]
    ws, bs = gmlp_ws[0], gmlp_bs[0]

    cmp_t, slc_t, win_t, qt, gt = _front(x_prompt, nm, w_kvt, w_qt, w_gt, kv_cols=kv_cols, scale=scale * LOG2_E,
                                         tm=tiles.front_rows)
    kcv = _compress(cmp_t, pe_t, w1b, b1t, w2b, n_seq=bp, steps=1, rows=t // CMP_STRIDE)
    o_nsa = _attention(qt, gt, kcv, slc_t, win_t, n_rep=n_rep, hd=hd)
    x1, v_p = _mix(x_prompt, o_nsa, nm, w_uv, w_gate, gn, ws, bs.T, wpa, wpb, wout,
                   tm=tiles.mix_rows, chunk=chunk, single_pos=False)
    y_p, conv_p = _ffn(x1, x1, nf, wup, cw, cb, wdown, nfin, tm=tiles.ffn_rows, f_tile=tiles.ffn_cols,
                       single_pos=False)

    xs = x_sample.reshape(bd, d)
    q_s, kv_s, g_s = _front_dec(xs, nm, w_dec, q_cols=q_cols, kv_cols3=3 * kv_cols, scale=scale)
    kv_cmp_s, kv_slc_s, kv_win_s = kv_s[:, 0:kv_cols], kv_s[:, kv_cols:2 * kv_cols], kv_s[:, 2 * kv_cols:]
    def positions_last(c):
        return jnp.transpose(c, (0, 2, 3, 4, 1)).reshape(c.shape[0], kv_cols, c.shape[1])

    cmp_pages, slc_pages, win_rows = positions_last(cache_cmp[0]), positions_last(cache_slc[0]), positions_last(cache_win[0])
    pages_per_step = tiles.pages_per_step
    kcv_s = _compress(cmp_pages, pe_t, w1b, b1t, w2b, n_seq=bd, steps=n_pages // pages_per_step,
                      rows=pages_per_step * page // CMP_STRIDE, page_table=page_table)
    q3 = q_s.reshape(bd, n_heads, hd)
    n_blocks_s = -(-(past_len + tn) // SEL_BLOCK)
    o_cmp_s, idx_s = _cmp_select_dec(q3, kcv_s, hd=hd, n_rep=n_rep, q_pos=past_len, n_blocks=n_blocks_s)
    sel = idx_s[:, 0:n_kv, 0:N_SEL].reshape(bd, n_kv * N_SEL)
    o_nsa_s = _attend_dec(page_table, sel, slc_pages, q3, kv_slc_s[:, None, :], kv_win_s[:, None, :], win_rows,
                          g_s[:, 0:3 * n_heads].reshape(bd, n_heads, 3), o_cmp_s,
                          hd=hd, n_rep=n_rep, n_sel=N_SEL, n_past_blocks=past_len // SEL_BLOCK)
    gw = width // GMLP_GROUPS
    ws0 = jnp.repeat(ws[:, 0, 0], gw)[None]
    bs0 = jnp.repeat(bs[:, 0], gw)[None]
    x1_s, v_s = _mix(xs[None], o_nsa_s.reshape(1, bd, q_cols), nm, w_uv, w_gate, gn, ws0, bs0, wpa, wpb, wout,
                     tm=bd, chunk=chunk, single_pos=True)
    prev_s = jnp.swapaxes(state_conv[0], 0, 1)
    y_s, a_s = _ffn(x1_s, prev_s, nf, wup, cw, cb, wdown, nfin, tm=bd, f_tile=d_ff // 2, single_pos=True)

    def rows6(a_t):
        n, _, npos = a_t.shape
        return jnp.transpose(a_t.reshape(n, 2, n_kv, hd, npos), (0, 4, 1, 2, 3))[None]

    win_keep = min(WINDOW, t)
    win_keep_s = min(WINDOW, cache_win.shape[2] + tn)
    win_s = jnp.concatenate([win_rows, kv_win_s[:, :, None]], axis=2)[:, :, cache_win.shape[2] + tn - win_keep_s:]
    conv_s = jnp.concatenate([state_conv[0][:, 1:], a_s[0][:, None, :]], axis=1)
    return (y_p, y_s.reshape(bd, tn, d),
            rows6(cmp_t), rows6(slc_t), rows6(win_t[:, :, t - win_keep:]),
            v_p[None], conv_p[None],
            rows6(kv_cmp_s[:, :, None]), rows6(kv_slc_s[:, :, None]), rows6(win_s),
            v_s.reshape(1, bd, tn, width), conv_s[None])
```

```python
import functools
from typing import NamedTuple

import jax
import jax.numpy as jnp
from jax import lax
from jax.experimental import pallas as pl
from jax.experimental.pallas import tpu as pltpu

F32 = jnp.float32
BF16 = jnp.bfloat16
I32 = jnp.int32

CMP_STRIDE = 16
SEG_PITCH = 20
PAGE_LOOKAHEAD = 2
PAGE_SLOTS = PAGE_LOOKAHEAD + 1
CMP_PAGE_LOOKAHEAD = 3
CMP_PAGE_SLOTS = CMP_PAGE_LOOKAHEAD + 1
SEL_BLOCK = 64
N_SEL = 16
N_LOCAL_SEL = 2
WINDOW = 512
Q_BLOCK = 256
KEY_TILE = 128
SLC_TILES = 8
V_PAD = 16
LOG2_E = 1.4426950408889634
GMLP_GROUPS = 4
EPS = 1e-6
NEG = -1e30
BELOW_NEG = -3e38
SEL_BONUS = 1e6

V7X_VMEM_BYTES = 64 * 1024 * 1024
VMEM_REQUEST_BYTES = 56 * 1024 * 1024
LANES = 128


class _Tiles(NamedTuple):
    front_rows: int
    mix_rows: int
    ffn_rows: int
    ffn_cols: int
    pages_per_step: int


def _tiles(t, d_ff, n_pages):
    return _Tiles(front_rows=min(t, 1024), mix_rows=min(t, 1024), ffn_rows=min(t, 1024), ffn_cols=d_ff // 2,
                  pages_per_step=min(n_pages, 32))


def _cparams(n_grid):
    return pltpu.CompilerParams(
        dimension_semantics=("arbitrary",) * n_grid, vmem_limit_bytes=VMEM_REQUEST_BYTES)


def _rmsnorm(x, g):
    ms = jnp.mean(x * x, axis=-1, keepdims=True)
    return x * lax.rsqrt(ms + EPS) * g


def _dot(a, b):
    return jnp.dot(a, b, preferred_element_type=F32)


def _dot_nt(a, b):
    return lax.dot_general(a, b, (((1,), (1,)), ((), ())), preferred_element_type=F32)


def _bf16_pieces(x):
    hi = x.astype(BF16).astype(F32)
    mid = (x - hi).astype(BF16).astype(F32)
    return hi, mid, x - hi - mid


def _shr(x, n):
    assert n & (n - 1) == 0
    return x >> (n.bit_length() - 1)


def _const_spec(shape):
    nd = len(shape)
    return pl.BlockSpec(shape, lambda *_: (0,) * nd)


def _resident_spec(shape):
    nd = len(shape)
    return pl.BlockSpec(shape, lambda *_: (0,) * nd, pipeline_mode=pl.Buffered(1))


def _front_kernel(x_ref, nw_ref, wkvt_ref, wqt_ref, wgt_ref, cmp_ref, slc_ref, win_ref, qt_ref, gt_ref,
                  *, kv_cols, scale):
    h = _rmsnorm(x_ref[0], nw_ref[...]).astype(BF16)
    kvt = _dot_nt(wkvt_ref[...], h)
    cmp_ref[0] = kvt[0:kv_cols]
    slc_ref[0] = kvt[kv_cols:2 * kv_cols]
    win_ref[0] = kvt[2 * kv_cols:3 * kv_cols]
    qt_ref[0] = (_dot_nt(wqt_ref[...], h) * scale).astype(BF16)
    gt_ref[0] = jax.nn.sigmoid(_dot_nt(wgt_ref[...], h))


def _front(x, nw, wkvt, wqt, wgt, *, kv_cols, scale, tm):
    b, t, d = x.shape
    q_cols, g_rows = wqt.shape[0], wgt.shape[0]
    kv_shape = jax.ShapeDtypeStruct((b, kv_cols, t), F32)
    kv_spec = pl.BlockSpec((1, kv_cols, tm), lambda i, j: (i, 0, j))
    return pl.pallas_call(
        functools.partial(_front_kernel, kv_cols=kv_cols, scale=scale),
        grid=(b, t // tm),
        in_specs=[pl.BlockSpec((1, tm, d), lambda i, j: (i, j, 0)), _const_spec(nw.shape),
                  _const_spec(wkvt.shape), _const_spec(wqt.shape), _const_spec(wgt.shape)],
        out_specs=[kv_spec, kv_spec, kv_spec,
                   pl.BlockSpec((1, q_cols, tm), lambda i, j: (i, 0, j)),
                   pl.BlockSpec((1, g_rows, tm), lambda i, j: (i, 0, j))],
        out_shape=[kv_shape, kv_shape, kv_shape,
                   jax.ShapeDtypeStruct((b, q_cols, t), BF16),
                   jax.ShapeDtypeStruct((b, g_rows, t), F32)],
        compiler_params=_cparams(2), name="front",
    )(x, nw, wkvt, wqt, wgt)


def _front_dec_kernel(x_ref, nw_ref, w_ref, q_ref, kv_ref, g_ref, *, q_cols, kv_cols3, scale):
    h = _rmsnorm(x_ref[...], nw_ref[...]).astype(BF16)
    z = _dot_nt(h, w_ref[...])
    q_ref[...] = z[:, 0:q_cols] * scale
    kv_ref[...] = z[:, q_cols:q_cols + kv_cols3]
    g_ref[...] = jax.nn.sigmoid(z[:, q_cols + kv_cols3:])


def _front_dec(x, nw, w, *, q_cols, kv_cols3, scale):
    n = x.shape[0]
    g_cols = w.shape[0] - q_cols - kv_cols3
    return pl.pallas_call(
        functools.partial(_front_dec_kernel, q_cols=q_cols, kv_cols3=kv_cols3, scale=scale),
        grid=(1,),
        in_specs=[_const_spec(x.shape), _const_spec(nw.shape), _const_spec(w.shape)],
        out_specs=[_const_spec((n, q_cols)), _const_spec((n, kv_cols3)), _const_spec((n, g_cols))],
        out_shape=[jax.ShapeDtypeStruct((n, q_cols), F32), jax.ShapeDtypeStruct((n, kv_cols3), F32),
                   jax.ShapeDtypeStruct((n, g_cols), F32)],
        compiler_params=_cparams(1), name="front_dec",
    )(x, nw, w)


def _segments_onto_rows(tile_of, n_tiles, pos_ref, kv, stride):
    seg_per_tile = LANES // stride
    for ti in range(n_tiles):
        t = tile_of(ti).T
        for n in range(seg_per_tile):
            p0 = (ti * seg_per_tile + n) * SEG_PITCH
            pos_ref[kv, p0:p0 + stride, :] = t[n * stride:(n + 1) * stride, :]


def _compress_rows(kv, pe_ref, w1_ref, b1_ref, w2_ref, out_ref, carry_ref, pos_ref, *, stride, half, hid2):
    rows = out_ref.shape[1]
    xkv = jnp.concatenate(
        [pos_ref[kv, pl.ds(s, rows, stride=SEG_PITCH), :] for s in range(stride)],
        axis=1)
    parts = []
    for r in range(2):
        a = (xkv + pe_ref[kv, r]).astype(BF16)
        parts.append(_dot(a, w1_ref[kv, :, r * hid2:(r + 1) * hid2]))
    prev = carry_ref[kv, 0:1, :]
    row = lax.broadcasted_iota(I32, (rows, hid2), 0)
    shifted = jnp.where(row == 0, prev, pltpu.roll(parts[0], 1, 0))
    carry_ref[kv, 0:1, :] = parts[0][rows - 1:rows, :]
    hid = b1_ref[kv] + shifted + parts[1]
    out_ref[0, :, kv * half:(kv + 1) * half] = _dot(jax.nn.gelu(hid).astype(BF16), w2_ref[kv])


def _compress_kernel(x_ref, pe_ref, w1_ref, b1_ref, w2_ref, out_ref, carry_ref, pos_ref, *, stride, kv_cols, hid2):
    half = kv_cols // 2
    assert half == LANES

    @pl.when(pl.program_id(1) == 0)
    def _():
        carry_ref[...] = jnp.zeros_like(carry_ref)

    for kv in range(2):
        _segments_onto_rows(lambda ti: x_ref[0, kv * half:(kv + 1) * half, ti * LANES:(ti + 1) * LANES],
                            x_ref.shape[2] // LANES, pos_ref, kv, stride)
        _compress_rows(kv, pe_ref, w1_ref, b1_ref, w2_ref, out_ref, carry_ref, pos_ref,
                       stride=stride, half=half, hid2=hid2)


def _compress_paged_kernel(pt_ref, pages_ref, pe_ref, w1_ref, b1_ref, w2_ref, out_ref, carry_ref, pos_ref, buf_ref,
                           sem, *, stride, kv_cols, hid2, n_pages):
    half = kv_cols // 2
    assert half == LANES and buf_ref.shape[3] == LANES
    steps = pl.num_programs(1)
    total = pl.num_programs(0) * steps
    t = pl.program_id(0) * steps + pl.program_id(1)
    look, slots = CMP_PAGE_LOOKAHEAD, CMP_PAGE_SLOTS
    static = dict(stride=stride, half=half, hid2=hid2)

    def page_copy(src_step, ring_step, k):
        page = pt_ref[src_step // steps, (src_step % steps) * n_pages + k]
        slot = ring_step % slots
        return pltpu.make_async_copy(pages_ref.at[page], buf_ref.at[slot, k], sem.at[slot])

    def rows_of(ring_step, kv):
        slot = ring_step % slots
        _segments_onto_rows(lambda ti: buf_ref[slot, ti, kv * half:(kv + 1) * half, :], n_pages, pos_ref, kv, stride)

    @pl.when(t == 0)
    def _():
        for ahead in range(look):
            for k in range(n_pages):
                page_copy(jnp.minimum(ahead, total - 1), ahead, k).start()
        for k in range(n_pages):
            page_copy(0, 0, k).wait()
        rows_of(0, 0)

    @pl.when(pl.program_id(1) == 0)
    def _():
        carry_ref[...] = jnp.zeros_like(carry_ref)

    for k in range(n_pages):
        page_copy(t, t + 1, k).wait()
    _compress_rows(0, pe_ref, w1_ref, b1_ref, w2_ref, out_ref, carry_ref, pos_ref, **static)
    rows_of(t, 1)
    _compress_rows(1, pe_ref, w1_ref, b1_ref, w2_ref, out_ref, carry_ref, pos_ref, **static)
    rows_of(t + 1, 0)
    for k in range(n_pages):
        page_copy(jnp.minimum(t + look, total - 1), t + look, k).start()

    @pl.when(t == total - 1)
    def _():
        for ahead in range(2, look + 1):
            for k in range(n_pages):
                page_copy(t, t + ahead, k).wait()


def _compress(x, pe, w1, b1, w2, *, n_seq, steps, rows, page_table=None):
    stride, kv_cols = CMP_STRIDE, w2.shape[2] * 2
    hid2 = b1.shape[2]
    static = dict(stride=stride, kv_cols=kv_cols, hid2=hid2)
    out_shape = jax.ShapeDtypeStruct((n_seq, steps * rows, kv_cols), F32)
    scratch = [pltpu.VMEM((2, 8, hid2), F32), pltpu.VMEM((2, rows * SEG_PITCH, kv_cols // 2), F32)]
    weights = (pe, w1, b1, w2)
    if page_table is None:
        return pl.pallas_call(
            functools.partial(_compress_kernel, **static), grid=(n_seq, steps),
            in_specs=[pl.BlockSpec((1, kv_cols, rows * stride), lambda i, j: (i, 0, j))]
                     + [_const_spec(a.shape) for a in weights],
            out_specs=pl.BlockSpec((1, rows, kv_cols), lambda i, j: (i, j, 0)),
            out_shape=out_shape, scratch_shapes=scratch, compiler_params=_cparams(2), name="compress",
        )(x, *weights)
    page = x.shape[2]
    n_pages = rows * stride // page
    assert n_seq * steps > CMP_PAGE_LOOKAHEAD
    w_specs = [pl.BlockSpec(a.shape, functools.partial(lambda i, j, pt, nd: (0,) * nd, nd=a.ndim)) for a in weights]
    return pl.pallas_call(
        functools.partial(_compress_paged_kernel, n_pages=n_pages, **static),
        grid_spec=pltpu.PrefetchScalarGridSpec(
            num_scalar_prefetch=1, grid=(n_seq, steps),
            in_specs=[pl.BlockSpec(memory_space=pl.ANY)] + w_specs,
            out_specs=pl.BlockSpec((1, rows, kv_cols), lambda i, j, pt: (i, j, 0)),
            scratch_shapes=scratch + [pltpu.VMEM((CMP_PAGE_SLOTS, n_pages, kv_cols, page), F32),
                                      pltpu.SemaphoreType.DMA((CMP_PAGE_SLOTS,))]),
        out_shape=out_shape, compiler_params=_cparams(2), name="compress_paged",
    )(page_table, x, *weights)


def _block_scores(imp, blk, q_pos, n_blocks):
    cur = _shr(q_pos, SEL_BLOCK)
    valid = (blk * SEL_BLOCK <= q_pos) & (blk < n_blocks)
    forced = (blk == 0) | ((blk <= cur) & (blk > cur - N_LOCAL_SEL))
    score = jnp.where(valid, imp + jnp.where(forced, SEL_BONUS, 0.0), NEG)
    return jnp.where(blk < n_blocks, score, BELOW_NEG)


def _attn_kernel(qt_ref, gt_ref, kcv_ref, slc_ref, win_ref, o_ref,
                 kaug_ref, vts_ref, kwin_ref, vtw_ref, kc_ref, vct_ref, *, n_kv, n_rep, hd, n_blocks):
    i = pl.program_id(1)
    qb, kt = Q_BLOCK, KEY_TILE
    q_tiles = qb // kt
    n_tiles = slc_ref.shape[2] // kt
    n_cmp_rows = kcv_ref.shape[1]
    nq = n_rep * qb
    kd = n_kv * hd
    g_rows = gt_ref.shape[1] // n_kv
    assert kd == LANES and n_blocks == hd, "the selection bias rows ride in the key one-hot lanes"

    @pl.when(i == 0)
    def _():
        lane = lax.broadcasted_iota(I32, (kt, kd), 1)
        krow = lax.broadcasted_iota(I32, (kt, kd), 0)

        def group_lanes(x, gg):
            return x if gg == 0 else pltpu.roll(x, kd - gg * hd, 1)

        ones_row = (lax.broadcasted_iota(I32, (V_PAD, kt), 0) == 0).astype(BF16)
        for c in range(n_tiles):
            cols = slice(c * kt, (c + 1) * kt)
            onehot = (lane - hd == _shr(c * kt + krow, SEL_BLOCK)).astype(F32)
            kt_s, kt_w = slc_ref[0, 0:kd, cols].T, win_ref[0, 0:kd, cols].T
            for gg in range(n_kv):
                v_rows = slice(kd + gg * hd, kd + (gg + 1) * hd)
                kaug_ref[gg, cols, :] = jnp.where(lane < hd, group_lanes(kt_s, gg), onehot).astype(BF16)
                vts_ref[gg, c, 0:hd, :] = slc_ref[0, v_rows, cols].astype(BF16)
                vts_ref[gg, c, hd:, :] = ones_row
                kwin_ref[gg, cols, :] = group_lanes(kt_w, gg)[:, 0:hd].astype(BF16)
                vtw_ref[gg, c, 0:hd, :] = win_ref[0, v_rows, cols].astype(BF16)
                vtw_ref[gg, c, hd:, :] = ones_row
        for c in range(n_cmp_rows // kt):
            rows = slice(c * kt, (c + 1) * kt)
            blk = kcv_ref[0, rows, :]
            vt = blk[:, kd:2 * kd].T
            for gg in range(n_kv):
                kc_ref[gg, rows, :] = group_lanes(blk[:, 0:kd], gg)[:, 0:hd].astype(BF16)
                vct_ref[gg, :, rows] = vt[gg * hd:(gg + 1) * hd, :].astype(BF16)

    q_pos = i * qb + (lax.broadcasted_iota(I32, (1, nq), 1) & (qb - 1))

    groups = range(n_kv)
    heads = range(n_rep)

    q_minus_k = (lax.broadcasted_iota(I32, (kt, nq), 1) & (qb - 1)) - lax.broadcasted_iota(I32, (kt, nq), 0)

    def softmax_step(m, sc):
        m_new = jnp.maximum(m, jnp.max(sc, axis=0, keepdims=True))
        return m_new, jnp.exp2(m - m_new), jnp.exp2(sc - m_new).astype(BF16)

    def normalised(acc):
        return acc[0:hd, :] * (1.0 / acc[hd:hd + 1, :])

    qts = [qt_ref[0, gg * n_rep * hd:(gg + 1) * n_rep * hd, :] for gg in groups]
    qcats = [jnp.concatenate([qts[gg][h * hd:(h + 1) * hd, :] for h in heads], axis=1) for gg in groups]

    def attend(k_ref, v_ref, qs, tile_ids, state, masked):
        scores = [[masked(u, _dot(k_ref[gg, pl.ds(pl.multiple_of(t * kt, kt), kt), :], qs[gg])) for gg in groups]
                  for u, t in enumerate(tile_ids)]
        maxes, accs = list(state[0]), list(state[1])
        for u, t in enumerate(tile_ids):
            for gg in groups:
                maxes[gg], alpha, pb = softmax_step(maxes[gg], scores[u][gg])
                accs[gg] = alpha * accs[gg] + _dot(v_ref[gg, t], pb)
        return tuple(maxes), tuple(accs)

    state0 = ((jnp.full((1, nq), NEG, F32),) * n_kv, (jnp.zeros((hd + V_PAD, nq), F32),) * n_kv)

    s_cmp = [_dot(kc_ref[gg], qcats[gg]) for gg in groups]
    m_idx = lax.broadcasted_iota(I32, (n_cmp_rows, nq), 0)
    vis = (m_idx >= 1) & ((m_idx - 1) * CMP_STRIDE + 2 * CMP_STRIDE - 1 <= q_pos)
    ratio = SEL_BLOCK // CMP_STRIDE
    pj = lax.broadcasted_iota(I32, (n_blocks, n_cmp_rows), 0)
    pm = lax.broadcasted_iota(I32, (n_blocks, n_cmp_rows), 1)
    pool = ((pm >= 1) & (pm >= ratio * pj) & (pm <= ratio * pj + ratio)).astype(BF16)
    o_cmp, imp = [], []
    for gg in groups:
        s = jnp.where(vis, s_cmp[gg], NEG)
        e = jnp.where(vis, jnp.exp2(s - jnp.max(s, axis=0, keepdims=True)), 0.0)
        den = jnp.sum(e, axis=0, keepdims=True)
        p = e * (1.0 / jnp.where(den > 0, den, 1.0))
        o_cmp.append(_dot(vct_ref[gg], p.astype(BF16)))
        p_grp = p[:, 0:qb]
        for h in range(1, n_rep):
            p_grp = p_grp + p[:, h * qb:(h + 1) * qb]
        pieces = _dot(pool, jnp.concatenate(_bf16_pieces(p_grp), axis=1).astype(BF16))
        imp.append(pieces[:, 0:qb] + pieces[:, qb:2 * qb] + pieces[:, 2 * qb:])

    n_band = WINDOW // kt
    first_t = i * q_tiles - n_band
    win_tiles = [jnp.maximum(first_t + u, 0) for u in range(n_band + q_tiles)]

    def in_window(u, sc):
        dlt = q_minus_k + (n_band - u) * kt
        if u < q_tiles:
            sc = jnp.where(dlt <= WINDOW, sc, NEG)
        if u >= n_band:
            sc = jnp.where(dlt >= 0, sc, NEG)
        if u < n_band:
            sc = sc + jnp.where(first_t + u < 0, NEG, 0.0)
        if u == 0:
            sc = sc + after_importance
        return sc

    after_importance = sum(imp[gg][0:1, 0:1] for gg in groups) * 0.0

    _, accs = attend(kwin_ref, vtw_ref, qcats, win_tiles, state0, in_window)
    o_win = [normalised(accs[gg]) for gg in groups]

    blk_id = lax.broadcasted_iota(I32, (n_blocks, qb), 0)
    blk_f = blk_id.astype(F32)
    qaug = []
    for gg in groups:
        score = _block_scores(imp[gg], blk_id, q_pos[:, 0:qb], n_blocks)
        work, sel = score, jnp.zeros((n_blocks, qb), F32)
        for _ in range(min(N_SEL, n_blocks)):
            mx = jnp.max(work, axis=0, keepdims=True)
            first = jnp.min(jnp.where(work == mx, blk_f, float(n_blocks)), axis=0, keepdims=True)
            pick = blk_f == first
            sel = jnp.where(pick, 1.0, sel)
            work = jnp.where(pick, BELOW_NEG, work)
        bias = jnp.where((sel > 0) & (score > NEG / 2), 0.0, NEG).astype(BF16)
        qaug.append(jnp.concatenate(
            [jnp.concatenate([qts[gg][h * hd:(h + 1) * hd, :], bias], axis=0) for h in heads], axis=1))

    assert n_tiles % SLC_TILES == 0 and SLC_TILES % q_tiles == 0
    n_before = i * q_tiles

    def unmasked(first_tile, n_t):
        return lambda j, st: attend(kaug_ref, vts_ref, qaug, [first_tile + j * n_t + u for u in range(n_t)], st,
                                    lambda u, sc: sc)

    state = lax.fori_loop(0, _shr(n_before, SLC_TILES), unmasked(0, SLC_TILES), state0)
    size = SLC_TILES // 2
    while size >= q_tiles:
        first = _shr(n_before, 2 * size) * (2 * size)
        state = lax.fori_loop(0, _shr(n_before, size) & 1, unmasked(first, size), state)
        size //= 2
    _, accs = attend(kaug_ref, vts_ref, qaug, [n_before + u for u in range(q_tiles)], state,
                     lambda u, sc: jnp.where(q_minus_k - u * kt >= 0, sc, NEG))

    outs = []
    for gg in groups:
        o_slc = normalised(accs[gg])
        for h in heads:
            cols = slice(h * qb, (h + 1) * qb)
            gate = [gt_ref[0, gg * g_rows + 3 * h + br:gg * g_rows + 3 * h + br + 1, :] for br in range(3)]
            outs.append(gate[0] * o_cmp[gg][:, cols] + gate[1] * o_slc[:, cols] + gate[2] * o_win[gg][:, cols])
    o_ref[0] = jnp.concatenate(outs, axis=0).T.astype(BF16)


def _attention(qt, gt, kcv, slc, win, *, n_rep, hd):
    b, q_cols, t = qt.shape
    n_kv = q_cols // (n_rep * hd)
    qb, kt = Q_BLOCK, KEY_TILE
    n_tiles = t // kt
    n_blocks = t // SEL_BLOCK
    kv_cols = slc.shape[1]
    cmp_rows = kcv.shape[1]
    tile_spec = pl.BlockSpec((1, kv_cols, t), lambda bi, i: (bi, 0, 0))
    return pl.pallas_call(
        functools.partial(_attn_kernel, n_kv=n_kv, n_rep=n_rep, hd=hd, n_blocks=n_blocks),
        grid=(b, t // qb),
        in_specs=[pl.BlockSpec((1, q_cols, qb), lambda bi, i: (bi, 0, i)),
                  pl.BlockSpec((1, gt.shape[1], qb), lambda bi, i: (bi, 0, i)),
                  pl.BlockSpec((1,) + kcv.shape[1:], lambda bi, i: (bi, 0, 0)),
                  tile_spec, tile_spec],
        out_specs=pl.BlockSpec((1, qb, q_cols), lambda bi, i: (bi, i, 0)),
        out_shape=jax.ShapeDtypeStruct((b, t, q_cols), BF16),
        scratch_shapes=[pltpu.VMEM((n_kv, t, 2 * hd), BF16), pltpu.VMEM((n_kv, n_tiles, hd + V_PAD, kt), BF16),
                        pltpu.VMEM((n_kv, t, hd), BF16), pltpu.VMEM((n_kv, n_tiles, hd + V_PAD, kt), BF16),
                        pltpu.VMEM((n_kv, cmp_rows, hd), BF16), pltpu.VMEM((n_kv, hd, cmp_rows), BF16)],
        compiler_params=_cparams(2), name="attention",
    )(qt, gt, kcv, slc, win)


def _spread_q(q, hd, n_rep):
    n_heads = q.shape[0]
    kd = n_heads // n_rep * hd
    d = lax.broadcasted_iota(I32, (hd, kd), 0)
    c = lax.broadcasted_iota(I32, (hd, kd), 1)
    qb16 = q.astype(BF16)
    row = lax.broadcasted_iota(I32, (n_heads, kd), 0)
    out = jnp.zeros((n_heads, kd), F32)
    for gg in range(n_heads // n_rep):
        placed = _dot(qb16, (c == d + gg * hd).astype(BF16))
        out = jnp.where(_shr(row, n_rep) == gg, placed, out)
    return out.astype(BF16)


def _masked_softmax_rows(s, mask):
    s = jnp.where(mask, s, NEG)
    e = jnp.where(mask, jnp.exp(s - jnp.max(s, axis=-1, keepdims=True)), 0.0)
    den = jnp.sum(e, axis=-1, keepdims=True)
    return e * (1.0 / jnp.where(den > 0, den, 1.0))


def _group_value_lanes(o_v, hd, n_rep):
    row = lax.broadcasted_iota(I32, (o_v.shape[0], hd), 0)
    out = o_v[:, 0:hd]
    for gg in range(1, o_v.shape[0] // n_rep):
        out = jnp.where(_shr(row, n_rep) == gg, o_v[:, gg * hd:(gg + 1) * hd], out)
    return out


def _cmp_select_dec_kernel(q_ref, kcv_ref, o_ref, idx_ref, pool_ref, score_ref, *, hd, n_rep, q_pos, n_blocks,
                           blk_lanes):
    n_heads = q_ref.shape[1]
    n_rows = kcv_ref.shape[1]
    b = pl.program_id(0)
    n_seq = idx_ref.shape[0]

    @pl.when(b == 0)
    def _():
        ratio = SEL_BLOCK // CMP_STRIDE
        pm = lax.broadcasted_iota(I32, (n_rows, blk_lanes), 0)
        pj = lax.broadcasted_iota(I32, (n_rows, blk_lanes), 1)
        pool_ref[...] = ((pm >= 1) & (pm >= ratio * pj) & (pm <= ratio * pj + ratio)).astype(BF16)

    q2 = _spread_q(q_ref[0], hd, n_rep)
    kd = q2.shape[1]
    m_idx = lax.broadcasted_iota(I32, (n_heads, n_rows), 1)
    vis = (m_idx >= 1) & ((m_idx - 1) * CMP_STRIDE + 2 * CMP_STRIDE - 1 <= q_pos)
    p = _masked_softmax_rows(_dot_nt(q2, kcv_ref[0, :, 0:kd].astype(BF16)), vis)
    o_ref[0] = _group_value_lanes(_dot(p.astype(BF16), kcv_ref[0, :, kd:2 * kd].astype(BF16)), hd, n_rep)

    row = lax.broadcasted_iota(I32, (n_heads, n_rows), 0)
    grp = jnp.zeros((n_heads, n_rows), F32)
    for gg in range(n_heads // n_rep):
        tot = jnp.sum(jnp.where(_shr(row, n_rep) == gg, p, 0.0), axis=0, keepdims=True)
        grp = jnp.where(row == gg, tot, grp)
    pieces = _dot(jnp.concatenate(_bf16_pieces(grp), axis=0).astype(BF16), pool_ref[...])
    imp = pieces[0:n_heads] + pieces[n_heads:2 * n_heads] + pieces[2 * n_heads:]
    score_ref[b] = _block_scores(imp, lax.broadcasted_iota(I32, (n_heads, blk_lanes), 1), q_pos, n_blocks)

    @pl.when(b == n_seq - 1)
    def _():
        work = score_ref[...].reshape(n_seq * n_heads, blk_lanes)
        blk_f = lax.broadcasted_iota(I32, work.shape, 1).astype(F32)
        out_lane = lax.broadcasted_iota(I32, (n_seq * n_heads, LANES), 1)
        out = jnp.full((n_seq * n_heads, LANES), -1, I32)
        for it in range(min(N_SEL, n_blocks)):
            mx = jnp.max(work, axis=-1, keepdims=True)
            first = jnp.min(jnp.where(work == mx, blk_f, float(blk_lanes)), axis=-1, keepdims=True)
            out = jnp.where(out_lane == it, jnp.where(mx > NEG / 2, first.astype(I32), -1), out)
            work = jnp.where(blk_f == first, BELOW_NEG, work)
        idx_ref[...] = out.reshape(n_seq, n_heads, LANES)


def _cmp_select_dec(q3, kcv, *, hd, n_rep, q_pos, n_blocks):
    n, n_heads, _ = q3.shape
    blk_lanes = -(-n_blocks // LANES) * LANES
    return pl.pallas_call(
        functools.partial(_cmp_select_dec_kernel, hd=hd, n_rep=n_rep, q_pos=q_pos, n_blocks=n_blocks,
                          blk_lanes=blk_lanes),
        grid=(n,),
        in_specs=[pl.BlockSpec((1, n_heads, hd), lambda i: (i, 0, 0)),
                  pl.BlockSpec((1,) + kcv.shape[1:], lambda i: (i, 0, 0))],
        out_specs=[pl.BlockSpec((1, n_heads, hd), lambda i: (i, 0, 0)),
                   _const_spec((n, n_heads, LANES))],
        out_shape=[jax.ShapeDtypeStruct((n, n_heads, hd), F32), jax.ShapeDtypeStruct((n, n_heads, LANES), I32)],
        scratch_shapes=[pltpu.VMEM((kcv.shape[1], blk_lanes), BF16), pltpu.VMEM((n, n_heads, blk_lanes), F32)],
        compiler_params=_cparams(1), name="cmp_select_dec",
    )(q3, kcv)


def _attend_dec_kernel(pt_ref, sel_ref, pages_ref, q_ref, new_slc_ref, new_win_ref, cwin_ref, g_ref, ocmp_ref, o_ref,
                       buf_ref, sem, *, hd, n_rep, n_sel, n_past_blocks):
    b = pl.program_id(0)
    total = pl.num_programs(0)
    n_heads = q_ref.shape[1]
    n_kv = n_heads // n_rep
    n_pages = n_kv * n_sel
    page = buf_ref.shape[3]
    per = page // SEL_BLOCK

    def page_copy(src_seq, ring_step, k):
        idx = jnp.clip(sel_ref[src_seq, k], 0, n_past_blocks - 1)
        slot = ring_step % PAGE_SLOTS
        return pltpu.make_async_copy(pages_ref.at[pt_ref[src_seq, _shr(idx, per)]], buf_ref.at[slot, k], sem.at[slot])

    @pl.when(b == 0)
    def _():
        for ahead in range(PAGE_LOOKAHEAD):
            for k in range(n_pages):
                page_copy(jnp.minimum(ahead, total - 1), ahead, k).start()

    for k in range(n_pages):
        page_copy(b, b, k).wait()
    ring_slot = b % PAGE_SLOTS

    q2 = _spread_q(q_ref[0], hd, n_rep)
    q2f = q2.astype(F32)
    kd = q2.shape[1]
    head_grp = _shr(lax.broadcasted_iota(I32, (n_heads, 1), 0), n_rep)

    def attend_with_new_key(k_t, v_t, mask, new_row, new_ok):
        nr = new_row.astype(BF16).astype(F32)
        s = jnp.where(mask, _dot(q2, k_t), NEG)
        s_new = jnp.where(new_ok, jnp.sum(q2f * nr[:, 0:kd], axis=-1, keepdims=True), NEG)
        m = jnp.maximum(jnp.max(s, axis=-1, keepdims=True), s_new)
        e = jnp.where(mask, jnp.exp(s - m), 0.0)
        e_new = jnp.where(new_ok, jnp.exp(s_new - m), 0.0)
        den = jnp.sum(e, axis=-1, keepdims=True) + e_new
        inv = 1.0 / jnp.where(den > 0, den, 1.0)
        return _dot_nt((e * inv).astype(BF16), v_t) + (e_new * inv) * nr[:, kd:2 * kd]

    k_t = jnp.concatenate([buf_ref[ring_slot, k, 0:kd, :] for k in range(n_pages)], axis=1).astype(BF16)
    v_t = jnp.concatenate([buf_ref[ring_slot, k, kd:2 * kd, :] for k in range(n_pages)], axis=1).astype(BF16)
    n_keys = k_t.shape[1]
    col_slot = _shr(lax.broadcasted_iota(I32, (1, n_keys), 1), page)
    page_blk = _shr(lax.broadcasted_iota(I32, (1, page), 1), SEL_BLOCK)
    slot_ok = []
    new_ok = jnp.zeros((n_heads, 1), I32)
    for gg in range(n_kv):
        for k in range(n_sel):
            idx = sel_ref[b, gg * n_sel + k]
            cached = ((idx >= 0) & (idx < n_past_blocks)).astype(I32)
            slot_ok.append(jnp.where(page_blk == (idx & (per - 1)), cached, 0))
            new_ok = jnp.where(head_grp == gg, new_ok | (idx == n_past_blocks).astype(I32), new_ok)
    mask = (jnp.concatenate(slot_ok, axis=1) > 0) & (_shr(col_slot, n_sel) == head_grp)
    o_slc = _group_value_lanes(attend_with_new_key(k_t, v_t, mask, new_slc_ref[0], new_ok > 0), hd, n_rep)

    all_ok = jnp.full((n_heads, cwin_ref.shape[2]), True)
    o_win = _group_value_lanes(
        attend_with_new_key(cwin_ref[0, 0:kd, :].astype(BF16), cwin_ref[0, kd:2 * kd, :].astype(BF16), all_ok,
                            new_win_ref[0], jnp.full((n_heads, 1), True)), hd, n_rep)

    gates = g_ref[0]
    o_ref[0] = gates[:, 0:1] * ocmp_ref[0] + gates[:, 1:2] * o_slc + gates[:, 2:3] * o_win

    for k in range(n_pages):
        page_copy(jnp.minimum(b + PAGE_LOOKAHEAD, total - 1), b + PAGE_LOOKAHEAD, k).start()

    @pl.when(b == total - 1)
    def _():
        for ahead in range(1, PAGE_LOOKAHEAD + 1):
            for k in range(n_pages):
                page_copy(b, b + ahead, k).wait()


def _attend_dec(page_table, sel, slc_pages, q3, new_slc, new_win, cache_win, gates3, o_cmp, *, hd, n_rep, n_sel,
                n_past_blocks):
    n, n_heads, _ = q3.shape
    n_kv = n_heads // n_rep
    assert n > PAGE_LOOKAHEAD and n_past_blocks * SEL_BLOCK == page_table.shape[1] * slc_pages.shape[2]

    def row_spec(shape):
        nd = len(shape)
        return pl.BlockSpec((1,) + tuple(shape[1:]), lambda i, pt, sl: (i,) + (0,) * (nd - 1))

    others = (q3, new_slc, new_win, cache_win, gates3, o_cmp)
    return pl.pallas_call(
        functools.partial(_attend_dec_kernel, hd=hd, n_rep=n_rep, n_sel=n_sel, n_past_blocks=n_past_blocks),
        grid_spec=pltpu.PrefetchScalarGridSpec(
            num_scalar_prefetch=2, grid=(n,),
            in_specs=[pl.BlockSpec(memory_space=pl.ANY)] + [row_spec(a.shape) for a in others],
            out_specs=row_spec(o_cmp.shape),
            scratch_shapes=[pltpu.VMEM((PAGE_SLOTS, n_kv * n_sel) + slc_pages.shape[1:], F32),
                            pltpu.SemaphoreType.DMA((PAGE_SLOTS,))]),
        out_shape=jax.ShapeDtypeStruct(o_cmp.shape, F32),
        compiler_params=_cparams(1), name="attend_dec",
    )(page_table, sel, slc_pages, *others)


def _mix_kernel(x_ref, o_ref, nw_ref, wuv_ref, wgate_ref, gn_ref, ws_ref, bs_ref, wpa_ref, wpb_ref, wout_ref,
                x1_ref, v_ref, *, width, chunk, single_pos):
    x = x_ref[0]
    d = x.shape[-1]
    h = _rmsnorm(x, nw_ref[...]).astype(BF16)
    uv = jax.nn.gelu(_dot_nt(h, wuv_ref[...]))
    u, vn = uv[:, 0:width], _rmsnorm(uv[:, width:2 * width], gn_ref[...])
    gates = jax.nn.sigmoid(_dot_nt(h, wgate_ref[...]))
    gw = width // GMLP_GROUPS
    if single_pos:
        v_ref[0] = vn
        mixed = u * (ws_ref[...] * vn + bs_ref[...])
    else:
        rows = x.shape[0]
        v_ref[0] = vn[rows - chunk:rows, :]
        tri = lax.broadcasted_iota(I32, (chunk, chunk), 0) >= lax.broadcasted_iota(I32, (chunk, chunk), 1)
        vb = vn.astype(BF16)
        pieces = []
        for c in range(rows // chunk):
            zs = []
            for gi in range(GMLP_GROUPS):
                wm = jnp.where(tri, ws_ref[gi], 0.0).astype(BF16)
                zs.append(_dot(wm, vb[c * chunk:(c + 1) * chunk, gi * gw:(gi + 1) * gw]) + bs_ref[:, gi:gi + 1])
            pieces.append(jnp.concatenate(zs, axis=1))
        mixed = u * jnp.concatenate(pieces, axis=0)
    br_a = _dot(o_ref[0].astype(BF16), wpa_ref[...])
    br_b = _dot(mixed.astype(BF16), wpb_ref[...])
    merged = gates[:, 0:d] * br_a + gates[:, d:2 * d] * br_b
    x1_ref[0] = x + _dot(merged.astype(BF16), wout_ref[...])


def _mix(x, o_nsa, nw, wuv, wgate, gn, ws, bs, wpa, wpb, wout, *, tm, chunk, single_pos):
    b, t, d = x.shape
    width = wuv.shape[0] // 2
    v_rows = tm if single_pos else chunk
    weights = (nw, wuv, wgate, gn, ws, bs, wpa, wpb, wout)
    return pl.pallas_call(
        functools.partial(_mix_kernel, width=width, chunk=chunk, single_pos=single_pos),
        grid=(b, t // tm),
        in_specs=[pl.BlockSpec((1, tm, d), lambda i, j: (i, j, 0)),
                  pl.BlockSpec((1, tm, o_nsa.shape[-1]), lambda i, j: (i, j, 0))]
                 + [_resident_spec(a.shape) for a in weights],
        out_specs=[pl.BlockSpec((1, tm, d), lambda i, j: (i, j, 0)),
                   pl.BlockSpec((1, v_rows, width), lambda i, j: (i, 0, 0))],
        out_shape=[jax.ShapeDtypeStruct((b, t, d), F32), jax.ShapeDtypeStruct((b, v_rows, width), F32)],
        compiler_params=_cparams(2), name="mix_dec" if single_pos else "mix",
    )(x, o_nsa, *weights)


def _ffn_kernel(x1_ref, prev_ref, nf_ref, wup_ref, cw_ref, cb_ref, wdown_ref, nfin_ref, y_ref, a_ref,
                *, d_ff, f_tile, halo, single_pos):
    x1 = x1_ref[0]
    rows = x1.shape[0]
    if single_pos:
        h = _rmsnorm(x1, nf_ref[...]).astype(BF16)
    else:
        h = _rmsnorm(jnp.concatenate([prev_ref[0], x1], axis=0), nf_ref[...]).astype(BF16)
        ext_row = lax.broadcasted_iota(I32, (rows + halo, f_tile), 0)
        first = pl.program_id(1) == 0
    y = jnp.zeros_like(x1)
    for f0 in range(0, d_ff, f_tile):
        cols = slice(f0, f0 + f_tile)
        a = _dot(h, wup_ref[:, cols])
        bgate = _dot(h, wup_ref[:, d_ff + f0:d_ff + f0 + f_tile])
        if single_pos:
            a_ref[0, :, cols] = a
            c = cb_ref[:, cols] + prev_ref[0, :, cols] * cw_ref[0:1, cols] + prev_ref[1, :, cols] * cw_ref[1:2, cols] \
                + a * cw_ref[2:3, cols]
        else:
            a = jnp.where((ext_row < halo) & first, 0.0, a)
            back2 = pltpu.roll(a, 2, 0)
            a_ref[0, :, cols] = back2[0:2, :]
            c = cb_ref[:, cols] + back2[halo:, :] * cw_ref[0:1, cols] \
                + pltpu.roll(a, 1, 0)[halo:, :] * cw_ref[1:2, cols] + a[halo:, :] * cw_ref[2:3, cols]
            bgate = bgate[halo:, :]
        y = y + _dot((jax.nn.gelu(c) * bgate).astype(BF16), wdown_ref[cols, :])
    y_ref[0] = _rmsnorm(x1 + y, nfin_ref[...])


def _ffn(x1, prev, nf, wup, cw, cb, wdown, nfin, *, tm, f_tile, single_pos):
    b, t, d = x1.shape
    d_ff = wdown.shape[0]
    halo = 8
    weights = (nf, wup, cw, cb, wdown, nfin)
    if single_pos:
        prev_spec = _const_spec(prev.shape)
        a_rows = tm
    else:
        per = tm // halo
        prev_spec = pl.BlockSpec((1, halo, d), lambda i, j: (i, jnp.maximum(j * per - 1, 0), 0))
        a_rows = 2
    return pl.pallas_call(
        functools.partial(_ffn_kernel, d_ff=d_ff, f_tile=f_tile, halo=halo, single_pos=single_pos),
        grid=(b, t // tm),
        in_specs=[pl.BlockSpec((1, tm, d), lambda i, j: (i, j, 0)), prev_spec]
                 + [_resident_spec(a.shape) for a in weights],
        out_specs=[pl.BlockSpec((1, tm, d), lambda i, j: (i, j, 0)),
                   pl.BlockSpec((1, a_rows, d_ff), lambda i, j: (i, 0, 0))],
        out_shape=[jax.ShapeDtypeStruct((b, t, d), F32), jax.ShapeDtypeStruct((b, a_rows, d_ff), F32)],
        compiler_params=_cparams(2), name="ffn_dec" if single_pos else "ffn",
    )(x1, prev, *weights)


def _compress_params(pe, w1, b1, w2, n_kv):
    cmp_len, hd = pe.shape[1], pe.shape[2]
    hid = w1.shape[2]
    halves = cmp_len // CMP_STRIDE
    eye = jnp.eye(n_kv, dtype=w1.dtype)
    pe_t = jnp.broadcast_to(pe.reshape(2, halves, CMP_STRIDE, 1, hd), (2, halves, CMP_STRIDE, n_kv, hd))
    pe_t = pe_t.reshape(2, halves, 1, CMP_STRIDE * n_kv * hd)
    w1h = w1.reshape(2, halves, CMP_STRIDE, hd, hid)
    w1b = jnp.einsum('krsdh,gf->ksgdrfh', w1h, eye).reshape(2, CMP_STRIDE * n_kv * hd, halves * n_kv * hid)
    b1t = jnp.tile(b1, (1, n_kv)).reshape(2, 1, n_kv * hid)
    w2b = jnp.einsum('khd,gf->kghfd', w2, eye).reshape(2, n_kv * hid, n_kv * hd)
    return pe_t, w1b.astype(BF16), b1t, w2b.astype(BF16)


def kernel(x_prompt, x_sample, cache_cmp, cache_slc, cache_win, state_conv, page_table, norm_mix, w_in, cmp_pe,
           cmp_w1, cmp_b1, cmp_w2, gmlp_norm, gmlp_ws, gmlp_bs, w_proj_a, w_proj_b, w_out, norm_ffn, w_up, conv_w,
           conv_b, w_down, norm_final):
    depth = w_in.shape[0]
    assert depth == 1, "single-layer step"
    bp, t, d = x_prompt.shape
    bd, tn, _ = x_sample.shape
    assert tn == 1
    n_kv, hd = cache_cmp.shape[4], cache_cmp.shape[5]
    page = cache_cmp.shape[2]
    q_cols = w_proj_a.shape[1]
    n_heads = q_cols // hd
    n_rep = n_heads // n_kv
    kv_cols = 2 * n_kv * hd
    width = w_proj_b.shape[1]
    chunk = gmlp_ws.shape[-1]
    d_ff = w_down.shape[1]
    n_pages = page_table.shape[1]
    past_len = n_pages * page
    scale = hd ** -0.5
    assert conv_w.shape[1] == 3 and cache_win.shape[2] <= WINDOW and past_len % SEL_BLOCK == 0

    w_in_t = w_in[0].T
    off_kv, off_g = q_cols, q_cols + 3 * kv_cols
    off_uv = off_g + 3 * n_heads
    off_gate = off_uv + 2 * width
    w_qt, w_kvt, w_g, w_uv, w_gate = (
        w_in_t[a:b].astype(BF16) for a, b in
        ((0, off_kv), (off_kv, off_g), (off_g, off_uv), (off_uv, off_gate), (off_gate, w_in_t.shape[0])))
    tiles = _tiles(t, d_ff, n_pages)
    g_rows = 16
    w_gt = jnp.pad(w_g.reshape(n_kv, 3 * n_rep, d), ((0, 0), (0, g_rows - 3 * n_rep), (0, 0))).reshape(n_kv * g_rows, d)
    w_dec = jnp.concatenate([w_qt, w_kvt, jnp.pad(w_g, ((0, LANES - 3 * n_heads), (0, 0)))], axis=0)
    nm, nf, nfin, gn = norm_mix[0][None], norm_ffn[0][None], norm_final[None], gmlp_norm[0][None]
    pe_t, w1b, b1t, w2b = _compress_params(cmp_pe[0], cmp_w1[0], cmp_b1[0], cmp_w2[0], n_kv)
    wpa, wpb, wout = w_proj_a[0].astype(BF16), w_proj_b[0].astype(BF16), w_out[0].astype(BF16)
    wup, wdown = w_up[0].astype(BF16), w_down[0].astype(BF16)
    cw, cb = conv_w[0], conv_b[0][None]
    ws, bs = gmlp_ws[0], gmlp_bs[0]

    cmp_t, slc_t, win_t, qt, gt = _front(x_prompt, nm, w_kvt, w_qt, w_gt, kv_cols=kv_cols, scale=scale * LOG2_E,
                                         tm=tiles.front_rows)
    kcv = _compress(cmp_t, pe_t, w1b, b1t, w2b, n_seq=bp, steps=1, rows=t // CMP_STRIDE)
    o_nsa = _attention(qt, gt, kcv, slc_t, win_t, n_rep=n_rep, hd=hd)
    x1, v_p = _mix(x_prompt, o_nsa, nm, w_uv, w_gate, gn, ws, bs.T, wpa, wpb, wout,
                   tm=tiles.mix_rows, chunk=chunk, single_pos=False)
    y_p, conv_p = _ffn(x1, x1, nf, wup, cw, cb, wdown, nfin, tm=tiles.ffn_rows, f_tile=tiles.ffn_cols,
                       single_pos=False)

    xs = x_sample.reshape(bd, d)
    q_s, kv_s, g_s = _front_dec(xs, nm, w_dec, q_cols=q_cols, kv_cols3=3 * kv_cols, scale=scale)
    kv_cmp_s, kv_slc_s, kv_win_s = kv_s[:, 0:kv_cols], kv_s[:, kv_cols:2 * kv_cols], kv_s[:, 2 * kv_cols:]
    def positions_last(c):
        return jnp.transpose(c, (0, 2, 3, 4, 1)).reshape(c.shape[0], kv_cols, c.shape[1])

    cmp_pages, slc_pages, win_rows = positions_last(cache_cmp[0]), positions_last(cache_slc[0]), positions_last(cache_win[0])
    pages_per_step = tiles.pages_per_step
    kcv_s = _compress(cmp_pages, pe_t, w1b, b1t, w2b, n_seq=bd, steps=n_pages // pages_per_step,
                      rows=pages_per_step * page // CMP_STRIDE, page_table=page_table)
    q3 = q_s.reshape(bd, n_heads, hd)
    n_blocks_s = -(-(past_len + tn) // SEL_BLOCK)
    o_cmp_s, idx_s = _cmp_select_dec(q3, kcv_s, hd=hd, n_rep=n_rep, q_pos=past_len, n_blocks=n_blocks_s)
    sel = idx_s[:, 0:n_kv, 0:N_SEL].reshape(bd, n_kv * N_SEL)
    o_nsa_s = _attend_dec(page_table, sel, slc_pages, q3, kv_slc_s[:, None, :], kv_win_s[:, None, :], win_rows,
                          g_s[:, 0:3 * n_heads].reshape(bd, n_heads, 3), o_cmp_s,
                          hd=hd, n_rep=n_rep, n_sel=N_SEL, n_past_blocks=past_len // SEL_BLOCK)
    gw = width // GMLP_GROUPS
    ws0 = jnp.repeat(ws[:, 0, 0], gw)[None]
    bs0 = jnp.repeat(bs[:, 0], gw)[None]
    x1_s, v_s = _mix(xs[None], o_nsa_s.reshape(1, bd, q_cols), nm, w_uv, w_gate, gn, ws0, bs0, wpa, wpb, wout,
                     tm=bd, chunk=chunk, single_pos=True)
    prev_s = jnp.swapaxes(state_conv[0], 0, 1)
    y_s, a_s = _ffn(x1_s, prev_s, nf, wup, cw, cb, wdown, nfin, tm=bd, f_tile=d_ff // 2, single_pos=True)

    def rows6(a_t):
        n, _, npos = a_t.shape
        return jnp.transpose(a_t.reshape(n, 2, n_kv, hd, npos), (0, 4, 1, 2, 3))[None]

    win_keep = min(WINDOW, t)
    win_keep_s = min(WINDOW, cache_win.shape[2] + tn)
    win_s = jnp.concatenate([win_rows, kv_win_s[:, :, None]], axis=2)[:, :, cache_win.shape[2] + tn - win_keep_s:]
    conv_s = jnp.concatenate([state_conv[0][:, 1:], a_s[0][:, None, :]], axis=1)
    return (y_p, y_s.reshape(bd, tn, d),
            rows6(cmp_t), rows6(slc_t), rows6(win_t[:, :, t - win_keep:]),
            v_p[None], conv_p[None],
            rows6(kv_cmp_s[:, :, None]), rows6(kv_slc_s[:, :, None]), rows6(win_s),
            v_s.reshape(1, bd, tn, width), conv_s[None])
```
